```python
import jax, jax.numpy as jnp
from jax import lax
import numpy as np

D_MODEL = 1024
BATCH = 8
SEQ = 2048
DEPTH = 1
DEC_BATCH = 128
DEC_SEQ = 1
PAST_LEN = 16384
PAGE_SIZE = 128

A_WIDTH = D_MODEL // 2
A_GROUPS = 8
A_GROUP_DIM = A_WIDTH // A_GROUPS
CHUNK = 128
B_WIDTH = D_MODEL - A_WIDTH
RWKV_HEAD = 64
RWKV_HEADS = B_WIDTH // RWKV_HEAD
DECAY_LORA = 64
AAA_LORA = 64
GATE_LORA = 160
B_PROJ = 3 * B_WIDTH + DECAY_LORA + AAA_LORA + GATE_LORA
IN_PROJ = 2 * A_WIDTH + B_PROJ
MIX_WIDTH = A_WIDTH + B_WIDTH
D_FF = 2816
N_MEM = 256
XA_HEADS = 4
XA_HEAD_DIM = D_MODEL // XA_HEADS
NORM_EPS = 1e-6
LN_EPS = 1e-5
GN_EPS = 64e-5

kernel_name = "hymba_gmlp_rwkv7_macaron_memxattn_step"


def rms_norm(x, g):
    xf = x.astype(jnp.float32)
    y = xf * lax.rsqrt(jnp.mean(xf * xf, axis=-1, keepdims=True) + NORM_EPS)
    return (y * g).astype(x.dtype)


def layer_norm(x, g, b):
    xf = x.astype(jnp.float32)
    mu = jnp.mean(xf, axis=-1, keepdims=True)
    var = jnp.mean(jnp.square(xf - mu), axis=-1, keepdims=True)
    return ((xf - mu) * lax.rsqrt(var + LN_EPS) * g + b).astype(x.dtype)


def swiglu(x, w_gate, w_up, w_down):
    return ((jax.nn.silu(x @ w_gate) * (x @ w_up)) @ w_down).astype(x.dtype)


def chunk_mix(v, w_s, b_s):
    bsz, t = v.shape[0], v.shape[1]
    n_c = -(-t // CHUNK)
    vp = jnp.pad(v, ((0, 0), (0, n_c * CHUNK - t), (0, 0), (0, 0)))
    vp = vp.reshape(bsz, n_c, CHUNK, A_GROUPS, A_GROUP_DIM)
    mask = jnp.tril(jnp.ones((CHUNK, CHUNK), dtype=bool))
    w = jnp.where(mask[None], w_s, 0).astype(v.dtype)
    mixed = jnp.einsum('gts,bcsgd->bctgd', w, vp) + jnp.swapaxes(b_s, 0, 1)[None, None, :, :, None]
    return mixed.reshape(bsz, n_c * CHUNK, A_GROUPS, A_GROUP_DIM)[:, :t]


def rwkv7_scan(state0, r, w, k, v, kk, a):
    def step(S, inp):
        r_t, w_t, k_t, v_t, kk_t, a_t = inp
        sa = jnp.einsum('bhvk,bhk->bhv', S, -kk_t)
        S = (S * w_t[:, :, None, :] + sa[..., None] * (kk_t * a_t)[:, :, None, :]
             + v_t[..., None] * k_t[:, :, None, :])
        return S, jnp.einsum('bhvk,bhk->bhv', S, r_t)
    xs = tuple(jnp.swapaxes(z, 0, 1) for z in (r, w, k, v, kk, a))
    S, out = lax.scan(step, state0, xs)
    return S, jnp.swapaxes(out, 0, 1)


def token_mix(h, shift_prev, rwkv_state, w_in, w_out, sgu_w, sgu_b, sgu_ln_g, sgu_ln_b,
              rwkv_mu, rwkv_w0, rwkv_w2, rwkv_a0, rwkv_a2, rwkv_g2, rwkv_k_k, rwkv_k_a,
              rwkv_r_k, rwkv_gn_g, rwkv_gn_b):
    f32 = jnp.float32
    bsz, t, _ = h.shape
    z = h @ w_in
    za = jax.nn.gelu(z[..., :2 * A_WIDTH])
    u, va = za[..., :A_WIDTH], za[..., A_WIDTH:]
    va = layer_norm(va, sgu_ln_g, sgu_ln_b)
    mixed = chunk_mix(va.reshape(bsz, t, A_GROUPS, A_GROUP_DIM), sgu_w, sgu_b)
    ya = (u * mixed.reshape(bsz, t, A_WIDTH)).astype(h.dtype)
    zb = z[..., 2 * A_WIDTH:]
    zb_prev = jnp.concatenate([shift_prev.astype(zb.dtype), zb[:, :-1]], axis=1)
    new_shift = zb[:, -1:]
    zs = (zb + (zb_prev - zb) * rwkv_mu).astype(f32)
    o1, o2, o3 = B_WIDTH, 2 * B_WIDTH, 3 * B_WIDTH
    o4, o5 = o3 + DECAY_LORA, o3 + DECAY_LORA + AAA_LORA
    r, k, v = zs[..., :o1], zs[..., o1:o2], zs[..., o2:o3]
    wd, ad, gd = zs[..., o3:o4], zs[..., o4:o5], zs[..., o5:]
    w_log = -jax.nn.softplus(-(rwkv_w0 + jnp.tanh(wd) @ rwkv_w2)) - 0.5
    decay = jnp.exp(-jnp.exp(w_log))
    a = jax.nn.sigmoid(rwkv_a0 + ad @ rwkv_a2)
    g = jax.nn.sigmoid(gd) @ rwkv_g2
    heads = lambda q: q.reshape(bsz, t, RWKV_HEADS, RWKV_HEAD)
    kk = heads(k * rwkv_k_k)
    kk = kk * lax.rsqrt(jnp.maximum(jnp.sum(kk * kk, axis=-1, keepdims=True), 1e-24))
    k = k * (1.0 + (a - 1.0) * rwkv_k_a)
    rh, kh, vh = heads(r), heads(k), heads(v)
    new_state, o = rwkv7_scan(rwkv_state.astype(f32), rh, heads(decay), kh, vh, kk, heads(a))
    mu = jnp.mean(o, axis=-1, keepdims=True)
    var = jnp.mean(jnp.square(o - mu), axis=-1, keepdims=True)
    o = ((o - mu) * lax.rsqrt(var + GN_EPS)).reshape(bsz, t, B_WIDTH) * rwkv_gn_g + rwkv_gn_b
    bonus = (jnp.sum(rh * kh * rwkv_r_k, axis=-1, keepdims=True) * vh).reshape(bsz, t, B_WIDTH)
    yb = ((o + bonus) * g).astype(h.dtype)
    y = jnp.concatenate([ya, yb], axis=-1) @ w_out
    return y.astype(h.dtype), new_state.astype(h.dtype), new_shift, va


def memory_kv(mem, mem_norm_g, w_k, w_v):
    m = rms_norm(mem, mem_norm_g)
    bsz = mem.shape[0]
    mk = (m @ w_k).reshape(bsz, N_MEM, XA_HEADS, XA_HEAD_DIM)
    mv = (m @ w_v).reshape(bsz, N_MEM, XA_HEADS, XA_HEAD_DIM)
    return mk, mv


def cross_attn(h, mk, mv, w_q, w_o):
    bsz, t, _ = h.shape
    q = (h @ w_q).reshape(bsz, t, XA_HEADS, XA_HEAD_DIM)
    s = jnp.einsum('bthd,bmhd->bhtm', q, mk).astype(jnp.float32) * (XA_HEAD_DIM ** -0.5)
    p = jax.nn.softmax(s, axis=-1).astype(h.dtype)
    o = jnp.einsum('bhtm,bmhd->bthd', p, mv.astype(h.dtype)).reshape(bsz, t, D_MODEL)
    return (o @ w_o).astype(h.dtype)


def decoder_layer(x, mem_k, mem_v, shift_prev, rwkv_state, *, ln_ffn1, ffn1_gate, ffn1_up,
                  ffn1_down, ln_mix, w_in, w_out, sgu_w, sgu_b, sgu_ln_g, sgu_ln_b, rwkv_mu,
                  rwkv_w0, rwkv_w2, rwkv_a0, rwkv_a2, rwkv_g2, rwkv_k_k, rwkv_k_a, rwkv_r_k,
                  rwkv_gn_g, rwkv_gn_b, ln_xattn, xa_q, xa_o, ln_ffn2, ffn2_gate, ffn2_up,
                  ffn2_down):
    x = x + 0.5 * swiglu(rms_norm(x, ln_ffn1), ffn1_gate, ffn1_up, ffn1_down)
    y, new_state, new_shift, va = token_mix(
        rms_norm(x, ln_mix), shift_prev, rwkv_state, w_in, w_out, sgu_w, sgu_b, sgu_ln_g,
        sgu_ln_b, rwkv_mu, rwkv_w0, rwkv_w2, rwkv_a0, rwkv_a2, rwkv_g2, rwkv_k_k, rwkv_k_a,
        rwkv_r_k, rwkv_gn_g, rwkv_gn_b)
    x = x + y
    x = x + cross_attn(rms_norm(x, ln_xattn), mem_k, mem_v, xa_q, xa_o)
    x = x + 0.5 * swiglu(rms_norm(x, ln_ffn2), ffn2_gate, ffn2_up, ffn2_down)
    return x, new_state, new_shift, va


def setup_inputs(seed: int = 0) -> dict:
    key = jax.random.key(seed)
    ks = iter(jax.random.split(key, 64))
    f32 = jnp.float32
    nrm = lambda shape, scale: jax.random.normal(next(ks), shape, f32) * scale
    gain = lambda shape: 1.0 + jax.random.normal(next(ks), shape, f32) * 0.05
    L = DEPTH
    return {
        "x_prompt": nrm((BATCH, SEQ, D_MODEL), 1.0),
        "x_sample": nrm((DEC_BATCH, DEC_SEQ, D_MODEL), 1.0),
        "state_rwkv": nrm((L, DEC_BATCH, RWKV_HEADS, RWKV_HEAD, RWKV_HEAD), 0.5),
        "state_shift": nrm((L, DEC_BATCH, 1, B_PROJ), 1.0),
        "cache_mem_k": nrm((L, DEC_BATCH, N_MEM, XA_HEADS, XA_HEAD_DIM), 1.0),
        "cache_mem_v": nrm((L, DEC_BATCH, N_MEM, XA_HEADS, XA_HEAD_DIM), 1.0),
        "mem_prompt": nrm((BATCH, N_MEM, D_MODEL), 1.0),
        "ln_ffn1": gain((L, D_MODEL)),
        "ffn1_gate": nrm((L, D_MODEL, D_FF), D_MODEL ** -0.5),
        "ffn1_up": nrm((L, D_MODEL, D_FF), D_MODEL ** -0.5),
        "ffn1_down": nrm((L, D_FF, D_MODEL), D_FF ** -0.5),
        "ln_mix": gain((L, D_MODEL)),
        "w_in": nrm((L, D_MODEL, IN_PROJ), D_MODEL ** -0.5),
        "w_out": nrm((L, MIX_WIDTH, D_MODEL), MIX_WIDTH ** -0.5),
        "sgu_w": nrm((L, A_GROUPS, CHUNK, CHUNK), 0.5 * CHUNK ** -0.5),
        "sgu_b": gain((L, A_GROUPS, CHUNK)),
        "sgu_ln_g": gain((L, A_WIDTH)),
        "sgu_ln_b": nrm((L, A_WIDTH), 0.01),
        "rwkv_mu": jax.random.uniform(next(ks), (L, B_PROJ), f32),
        "rwkv_w0": nrm((L, B_WIDTH), 0.5),
        "rwkv_w2": nrm((L, DECAY_LORA, B_WIDTH), DECAY_LORA ** -0.5),
        "rwkv_a0": nrm((L, B_WIDTH), 0.1),
        "rwkv_a2": nrm((L, AAA_LORA, B_WIDTH), AAA_LORA ** -0.5),
        "rwkv_g2": nrm((L, GATE_LORA, B_WIDTH), GATE_LORA ** -0.5),
        "rwkv_k_k": 0.85 + nrm((L, B_WIDTH), 0.05),
        "rwkv_k_a": gain((L, B_WIDTH)),
        "rwkv_r_k": nrm((L, RWKV_HEADS, RWKV_HEAD), 0.1),
        "rwkv_gn_g": gain((L, B_WIDTH)),
        "rwkv_gn_b": nrm((L, B_WIDTH), 0.01),
        "ln_xattn": gain((L, D_MODEL)),
        "mem_norm": gain((L, D_MODEL)),
        "xa_q": nrm((L, D_MODEL, D_MODEL), D_MODEL ** -0.5),
        "xa_k": nrm((L, D_MODEL, D_MODEL), D_MODEL ** -0.5),
        "xa_v": nrm((L, D_MODEL, D_MODEL), D_MODEL ** -0.5),
        "xa_o": nrm((L, D_MODEL, D_MODEL), D_MODEL ** -0.5),
        "ln_ffn2": gain((L, D_MODEL)),
        "ffn2_gate": nrm((L, D_MODEL, D_FF), D_MODEL ** -0.5),
        "ffn2_up": nrm((L, D_MODEL, D_FF), D_MODEL ** -0.5),
        "ffn2_down": nrm((L, D_FF, D_MODEL), D_FF ** -0.5),
        "final_norm": gain((D_MODEL,)),
    }


def reference(x_prompt, x_sample, state_rwkv, state_shift, cache_mem_k, cache_mem_v, mem_prompt,
              ln_ffn1, ffn1_gate, ffn1_up, ffn1_down, ln_mix, w_in, w_out, sgu_w, sgu_b,
              sgu_ln_g, sgu_ln_b, rwkv_mu, rwkv_w0, rwkv_w2, rwkv_a0, rwkv_a2, rwkv_g2,
              rwkv_k_k, rwkv_k_a, rwkv_r_k, rwkv_gn_g, rwkv_gn_b, ln_xattn, mem_norm, xa_q,
              xa_k, xa_v, xa_o, ln_ffn2, ffn2_gate, ffn2_up, ffn2_down, final_norm):
    hp, hs = x_prompt, x_sample
    bp = x_prompt.shape[0]
    sp_l, shp_l, mkp_l, mvp_l, ss_l, shs_l, vs_l = [], [], [], [], [], [], []
    for l in range(DEPTH):
        lp = dict(ln_ffn1=ln_ffn1[l], ffn1_gate=ffn1_gate[l], ffn1_up=ffn1_up[l],
                  ffn1_down=ffn1_down[l], ln_mix=ln_mix[l], w_in=w_in[l], w_out=w_out[l],
                  sgu_w=sgu_w[l], sgu_b=sgu_b[l], sgu_ln_g=sgu_ln_g[l], sgu_ln_b=sgu_ln_b[l],
                  rwkv_mu=rwkv_mu[l], rwkv_w0=rwkv_w0[l], rwkv_w2=rwkv_w2[l],
                  rwkv_a0=rwkv_a0[l], rwkv_a2=rwkv_a2[l], rwkv_g2=rwkv_g2[l],
                  rwkv_k_k=rwkv_k_k[l], rwkv_k_a=rwkv_k_a[l], rwkv_r_k=rwkv_r_k[l],
                  rwkv_gn_g=rwkv_gn_g[l], rwkv_gn_b=rwkv_gn_b[l], ln_xattn=ln_xattn[l],
                  xa_q=xa_q[l], xa_o=xa_o[l], ln_ffn2=ln_ffn2[l], ffn2_gate=ffn2_gate[l],
                  ffn2_up=ffn2_up[l], ffn2_down=ffn2_down[l])
        mk_p, mv_p = memory_kv(mem_prompt, mem_norm[l], xa_k[l], xa_v[l])
        zero_state = jnp.zeros((bp, RWKV_HEADS, RWKV_HEAD, RWKV_HEAD), jnp.float32)
        zero_shift = jnp.zeros((bp, 1, B_PROJ), hp.dtype)
        hp, s_p, sh_p, _ = decoder_layer(hp, mk_p, mv_p, zero_shift, zero_state, **lp)
        hs, s_s, sh_s, v_s = decoder_layer(hs, cache_mem_k[l], cache_mem_v[l], state_shift[l],
                                           state_rwkv[l], **lp)
        sp_l.append(s_p); shp_l.append(sh_p); mkp_l.append(mk_p); mvp_l.append(mv_p)
        ss_l.append(s_s); shs_l.append(sh_s); vs_l.append(v_s)
    y_prompt = rms_norm(hp, final_norm)
    y_sample = rms_norm(hs, final_norm)
    return (y_prompt, y_sample, jnp.stack(sp_l), jnp.stack(shp_l), jnp.stack(mkp_l),
            jnp.stack(mvp_l), jnp.stack(ss_l), jnp.stack(shs_l), jnp.stack(vs_l))
```

```python
import functools

import jax
import jax.numpy as jnp
from jax import lax
from jax.experimental import pallas as pl
from jax.experimental.pallas import tpu as pltpu

F32 = jnp.float32
BF16 = jnp.bfloat16

D_MODEL = 1024
SEQ = 2048
A_WIDTH = 512
A_GROUPS = 8
A_GROUP_DIM = 64
CHUNK = 128
B_WIDTH = 512
HEAD = 64
HEADS = 8
PAIRS = HEADS // 2
PAIR_W = 2 * HEAD
DECAY_LORA = 64
AAA_LORA = 64
GATE_LORA = 160
B_PROJ = 3 * B_WIDTH + DECAY_LORA + AAA_LORA + GATE_LORA
MAIN_W = 2 * A_WIDTH + 3 * B_WIDTH
RKV_W = 3 * B_WIDTH
LORA_W = 512
LORA_WD, LORA_AD, LORA_GD = 0, 128, 256
D_FF = 2816
N_MEM = 256
XA_HEADS = 4
XA_DIM = 256
NORM_EPS = 1e-6
LN_EPS = 1e-5
GN_EPS = 64e-5

VMEM_LIMIT = 56 * 1024 * 1024


def _params(n_axes=1):
    return pltpu.CompilerParams(dimension_semantics=("arbitrary",) * n_axes,
                                vmem_limit_bytes=VMEM_LIMIT)


def _const_spec(shape):
    nd = len(shape)
    return pl.BlockSpec(shape, lambda *_: (0,) * nd, pipeline_mode=pl.Buffered(1))


def _rows_spec(tm, width):
    return pl.BlockSpec((tm, width), lambda i: (i, 0))


def _rms(x, g):
    return x * lax.rsqrt(jnp.mean(x * x, axis=-1, keepdims=True) + NORM_EPS) * g


def _dot(a, b):
    return jnp.dot(a.astype(BF16), b, preferred_element_type=F32)


def _seg_sum(x, ones_bd):
    hi = x.astype(BF16)
    lo = (x - hi.astype(F32)).astype(BF16)
    return (jnp.dot(hi, ones_bd, preferred_element_type=F32)
            + jnp.dot(lo, ones_bd, preferred_element_type=F32))


def _ffn_kernel(*refs, pre, final):
    it = iter(refs)
    x_ref = next(it)
    if pre:
        attn_ref, wo_ref = next(it), next(it)
    ln_ref, wg_ref, wu_ref, wd_ref = next(it), next(it), next(it), next(it)
    if final:
        fn_ref = next(it)
    o_ref = next(it)
    x = x_ref[...]
    if pre:
        x = x + _dot(attn_ref[...], wo_ref[...])
    xb = _rms(x, ln_ref[...]).astype(BF16)
    g = jnp.dot(xb, wg_ref[...], preferred_element_type=F32)
    u = jnp.dot(xb, wu_ref[...], preferred_element_type=F32)
    h = (g * jax.nn.sigmoid(g) * u).astype(BF16)
    x = x + 0.5 * jnp.dot(h, wd_ref[...], preferred_element_type=F32)
    if final:
        x = _rms(x, fn_ref[...])
    o_ref[...] = x


def _ffn(x, ln, wg, wu, wd, *, tm, attn=None, wo=None, final_norm=None):
    rows = x.shape[0]
    pre = attn is not None
    final = final_norm is not None
    args, specs = [x], [_rows_spec(tm, D_MODEL)]
    if pre:
        args += [attn, wo]
        specs += [_rows_spec(tm, D_MODEL), _const_spec((D_MODEL, D_MODEL))]
    args += [ln, wg, wu, wd]
    specs += [_const_spec((1, D_MODEL)), _const_spec((D_MODEL, D_FF)),
              _const_spec((D_MODEL, D_FF)), _const_spec((D_FF, D_MODEL))]
    if final:
        args.append(final_norm)
        specs.append(_const_spec((1, D_MODEL)))
    return pl.pallas_call(
        functools.partial(_ffn_kernel, pre=pre, final=final),
        grid=(rows // tm,),
        in_specs=specs,
        out_specs=_rows_spec(tm, D_MODEL),
        out_shape=jax.ShapeDtypeStruct((rows, D_MODEL), F32),
        compiler_params=_params(),
        name="ffn",
    )(*args)


def _memkv_kernel(m_ref, g_ref, wk_ref, wv_ref, k_ref, v_ref, kb_ref, vb_ref):
    mb = _rms(m_ref[...], g_ref[...]).astype(BF16)
    k = jnp.dot(mb, wk_ref[...], preferred_element_type=F32)
    v = jnp.dot(mb, wv_ref[...], preferred_element_type=F32)
    k_ref[...] = k
    v_ref[...] = v
    kb_ref[...] = k.astype(BF16)
    vb_ref[...] = v.astype(BF16)


def _memkv(mem, g, wk, wv, *, tm):
    rows = mem.shape[0]
    out = jax.ShapeDtypeStruct((rows, D_MODEL), F32)
    outb = jax.ShapeDtypeStruct((rows, D_MODEL), BF16)
    return pl.pallas_call(
        _memkv_kernel,
        grid=(rows // tm,),
        in_specs=[_rows_spec(tm, D_MODEL), _const_spec((1, D_MODEL)),
                  _const_spec((D_MODEL, D_MODEL)), _const_spec((D_MODEL, D_MODEL))],
        out_specs=[_rows_spec(tm, D_MODEL)] * 4,
        out_shape=[out, out, outb, outb],
        compiler_params=_params(),
        name="memkv",
    )(mem, g, wk, wv)


def _mix_kernel(*refs, sample, tiles_per_seq):
    it = iter(refs)
    x_ref, ln_ref, wmain_ref, wlora_ref = next(it), next(it), next(it), next(it)
    if sample:
        w00_ref, b0_ref, spm_ref, spl_ref = next(it), next(it), next(it), next(it)
    else:
        wcat_ref, bias_ref = next(it), next(it)
    (lng_ref, lnb_ref, mum_ref, mul_ref, w0_ref, w2_ref, a0_ref, a2_ref, g2_ref,
     kk_ref, ka_ref, rk_ref, ones_ref) = [next(it) for _ in range(13)]
    (ya_ref, r_ref, w_ref, k_ref, v_ref, kn_ref, b_ref, g_ref, bonus_ref) = [
        next(it) for _ in range(9)]
    if sample:
        va_ref, zm_ref, zl_ref = next(it), next(it), next(it)
    else:
        zlast_ref, cm_ref, cl_ref = next(it), next(it), next(it)

    tm = x_ref.shape[0]
    xb = _rms(x_ref[...], ln_ref[...]).astype(BF16)
    zmain = jnp.dot(xb, wmain_ref[...], preferred_element_type=F32)
    zl = jnp.dot(xb, wlora_ref[...], preferred_element_type=F32)

    za = jax.nn.gelu(zmain[:, :2 * A_WIDTH])
    u = za[:, :A_WIDTH]
    vx = za[:, A_WIDTH:]
    mu = jnp.mean(vx, axis=-1, keepdims=True)
    var = jnp.mean(jnp.square(vx - mu), axis=-1, keepdims=True)
    va = (vx - mu) * lax.rsqrt(var + LN_EPS) * lng_ref[...] + lnb_ref[...]
    if sample:
        mixed = va * w00_ref[...] + b0_ref[...]
        ya_ref[...] = (u * mixed).astype(BF16)
        va_ref[...] = va
    else:
        vab = va.astype(BF16)
        group = lax.broadcasted_iota(jnp.int32, (CHUNK, A_WIDTH), 1) // A_GROUP_DIM
        for c in range(tm // CHUNK):
            vc = vab[c * CHUNK:(c + 1) * CHUNK]
            rhs = jnp.concatenate(
                [jnp.where(group == g, vc, jnp.zeros_like(vc)) for g in range(A_GROUPS)],
                axis=0)
            mixed = jnp.dot(wcat_ref[...], rhs, preferred_element_type=F32) + bias_ref[...]
            ya_ref[c * CHUNK:(c + 1) * CHUNK, :] = (
                u[c * CHUNK:(c + 1) * CHUNK] * mixed).astype(BF16)

    zbm = zmain[:, 2 * A_WIDTH:]
    if sample:
        zpm, zpl = spm_ref[...], spl_ref[...]
        zm_ref[...] = zbm
        zl_ref[...] = zl
    else:
        i = pl.program_id(0)

        @pl.when(i % tiles_per_seq == 0)
        def _():
            cm_ref[...] = jnp.zeros_like(cm_ref)
            cl_ref[...] = jnp.zeros_like(cl_ref)

        first_m = lax.broadcasted_iota(jnp.int32, zbm.shape, 0) == 0
        first_l = lax.broadcasted_iota(jnp.int32, zl.shape, 0) == 0
        zpm = jnp.where(first_m, cm_ref[0:1, :], pltpu.roll(zbm, 1, axis=0))
        zpl = jnp.where(first_l, cl_ref[0:1, :], pltpu.roll(zl, 1, axis=0))
        cm_ref[0:1, :] = zbm[tm - 1:tm, :]
        cl_ref[0:1, :] = zl[tm - 1:tm, :]
        zlast_ref[:, :RKV_W] = jnp.broadcast_to(zbm[tm - 1:tm, :], (8, RKV_W))
        zlast_ref[:, RKV_W:] = jnp.broadcast_to(zl[tm - 1:tm, :], (8, LORA_W))
    zsm = zbm + (zpm - zbm) * mum_ref[...]
    zsl = zl + (zpl - zl) * mul_ref[...]
    r = zsm[:, :B_WIDTH]
    k = zsm[:, B_WIDTH:2 * B_WIDTH]
    v = zsm[:, 2 * B_WIDTH:]
    wd = zsl[:, LORA_WD:LORA_AD]
    ad = zsl[:, LORA_AD:LORA_GD]
    gd = zsl[:, LORA_GD:]
    w_log = -jax.nn.softplus(-(w0_ref[...] + _dot(jnp.tanh(wd), w2_ref[...]))) - 0.5
    decay = jnp.exp(-jnp.exp(w_log))
    a = jax.nn.sigmoid(a0_ref[...] + _dot(ad, a2_ref[...]))
    gate = _dot(jax.nn.sigmoid(gd), g2_ref[...])
    ones_bd = ones_ref[...]
    kk = k * kk_ref[...]
    kk = kk * lax.rsqrt(jnp.maximum(_seg_sum(kk * kk, ones_bd), 1e-24))
    k2 = k * (1.0 + (a - 1.0) * ka_ref[...])
    r_ref[...] = r
    w_ref[...] = decay
    k_ref[...] = k2
    v_ref[...] = v
    kn_ref[...] = kk
    b_ref[...] = kk * a
    g_ref[...] = gate
    bonus_ref[...] = _seg_sum(r * k2 * rk_ref[...], ones_bd) * v


def _mix_in(x, p, *, tm, sample, shift_main=None, shift_lora=None):
    rows = x.shape[0]
    n_tiles = rows // tm
    args = [x, p["ln_mix"], p["w_main"], p["w_lora"]]
    specs = [_rows_spec(tm, D_MODEL), _const_spec((1, D_MODEL)),
             _const_spec((D_MODEL, MAIN_W)), _const_spec((D_MODEL, LORA_W))]
    if sample:
        args += [p["sgu_w00"], p["sgu_b0"], shift_main, shift_lora]
        specs += [_const_spec((1, A_WIDTH)), _const_spec((1, A_WIDTH)),
                  _rows_spec(tm, RKV_W), _rows_spec(tm, LORA_W)]
    else:
        args += [p["sgu_wcat"], p["sgu_bias"]]
        specs += [_const_spec((CHUNK, A_GROUPS * CHUNK)), _const_spec((CHUNK, A_WIDTH))]
    args += [p["sgu_ln_g"], p["sgu_ln_b"], p["mu_main"], p["mu_lora"], p["w0"], p["w2"],
             p["a0"], p["a2"], p["g2"], p["k_k"], p["k_a"], p["r_k"], p["ones_bd"]]
    specs += [_const_spec((1, A_WIDTH)), _const_spec((1, A_WIDTH)), _const_spec((1, RKV_W)),
              _const_spec((1, LORA_W)), _const_spec((1, B_WIDTH)),
              _const_spec((LORA_AD - LORA_WD, B_WIDTH)), _const_spec((1, B_WIDTH)),
              _const_spec((LORA_GD - LORA_AD, B_WIDTH)), _const_spec((LORA_W - LORA_GD, B_WIDTH)),
              _const_spec((1, B_WIDTH)), _const_spec((1, B_WIDTH)), _const_spec((1, B_WIDTH)),
              _const_spec((B_WIDTH, B_WIDTH))]
    wide = jax.ShapeDtypeStruct((rows, B_WIDTH), F32)
    out_shape = [jax.ShapeDtypeStruct((rows, A_WIDTH), BF16)] + [wide] * 8
    out_specs = [_rows_spec(tm, B_WIDTH)] * 9
    scratch = []
    if sample:
        out_shape += [wide, jax.ShapeDtypeStruct((rows, RKV_W), F32),
                      jax.ShapeDtypeStruct((rows, LORA_W), F32)]
        out_specs += [_rows_spec(tm, A_WIDTH), _rows_spec(tm, RKV_W), _rows_spec(tm, LORA_W)]
    else:
        out_shape += [jax.ShapeDtypeStruct((n_tiles * 8, RKV_W + LORA_W), F32)]
        out_specs += [pl.BlockSpec((8, RKV_W + LORA_W), lambda i: (i, 0))]
        scratch = [pltpu.VMEM((8, RKV_W), F32), pltpu.VMEM((8, LORA_W), F32)]
    return pl.pallas_call(
        functools.partial(_mix_kernel, sample=sample, tiles_per_seq=max(SEQ // tm, 1)),
        grid=(n_tiles,),
        in_specs=specs,
        out_specs=out_specs,
        out_shape=out_shape,
        scratch_shapes=scratch,
        compiler_params=_params(),
        name="mix_in",
    )(*args)


def _pair_halves(x, left):
    zero = jnp.zeros_like(x)
    s0 = jnp.sum(jnp.where(left, x, zero), axis=-1, keepdims=True)
    s1 = jnp.sum(jnp.where(left, zero, x), axis=-1, keepdims=True)
    return jnp.where(left, s0, s1)


def _scan_step(s, r, w, k, v, kk, b, eye2, left):
    sa = -_pair_halves(s * kk, left)
    vcol = _pair_halves(eye2 * v, left)
    s = s * w + sa * b + vcol * k
    ocol = _pair_halves(s * r, left)
    return s, jnp.sum(eye2 * ocol, axis=0, keepdims=True)


def _pair_consts():
    lane = lax.broadcasted_iota(jnp.int32, (HEAD, PAIR_W), 1)
    row = lax.broadcasted_iota(jnp.int32, (HEAD, PAIR_W), 0)
    left = lane < HEAD
    eye2 = jnp.where((lane % HEAD) == row, 1.0, 0.0).astype(F32)
    return eye2, left


def _scan_prompt_kernel(r_ref, w_ref, k_ref, v_ref, kk_ref, b_ref, o_ref, sout_ref, s_ref):
    t_blk = pl.program_id(1)
    tt = r_ref.shape[0]

    @pl.when(t_blk == 0)
    def _():
        s_ref[...] = jnp.zeros_like(s_ref)

    eye2, left = _pair_consts()

    def body(t8, carry):
        rows8 = pl.ds(pl.multiple_of(t8 * 8, 8), 8)
        for p in range(PAIRS):
            sl = slice(p * PAIR_W, (p + 1) * PAIR_W)
            blk = [ref[rows8, sl] for ref in (r_ref, w_ref, k_ref, v_ref, kk_ref, b_ref)]
            s = s_ref[p]
            outs = []
            for j in range(8):
                s, o = _scan_step(s, *[x[j:j + 1, :] for x in blk], eye2, left)
                outs.append(o)
            s_ref[p] = s
            o_ref[rows8, sl] = jnp.concatenate(outs, axis=0)
        return carry

    lax.fori_loop(0, tt // 8, body, 0)

    @pl.when(t_blk == pl.num_programs(1) - 1)
    def _():
        sout_ref[0] = s_ref[...]


def _scan_prompt(r, w, k, v, kk, b, *, batch, tt):
    n_t = SEQ // tt
    spec = pl.BlockSpec((tt, B_WIDTH), lambda bi, ti: (bi * n_t + ti, 0))
    return pl.pallas_call(
        _scan_prompt_kernel,
        grid=(batch, n_t),
        in_specs=[spec] * 6,
        out_specs=[spec, pl.BlockSpec((1, PAIRS, HEAD, PAIR_W), lambda bi, ti: (bi, 0, 0, 0))],
        out_shape=[jax.ShapeDtypeStruct((batch * SEQ, B_WIDTH), F32),
                   jax.ShapeDtypeStruct((batch, PAIRS, HEAD, PAIR_W), F32)],
        scratch_shapes=[pltpu.VMEM((PAIRS, HEAD, PAIR_W), F32)],
        compiler_params=_params(2),
        name="scan_prompt",
    )(r, w, k, v, kk, b)


def _scan_sample_kernel(s_ref, r_ref, w_ref, k_ref, v_ref, kk_ref, b_ref, o_ref, sout_ref):
    eye2, left = _pair_consts()
    for j in range(s_ref.shape[0]):
        for p in range(PAIRS):
            sl = slice(p * PAIR_W, (p + 1) * PAIR_W)
            row = lambda ref: ref[j:j + 1, sl]
            s, o = _scan_step(s_ref[j, p], row(r_ref), row(w_ref), row(k_ref), row(v_ref),
                              row(kk_ref), row(b_ref), eye2, left)
            sout_ref[j, p] = s
            o_ref[j:j + 1, sl] = o


def _scan_sample(state, r, w, k, v, kk, b, *, bb):
    rows = r.shape[0]
    sspec = pl.BlockSpec((bb, PAIRS, HEAD, PAIR_W), lambda i: (i, 0, 0, 0))
    spec = _rows_spec(bb, B_WIDTH)
    return pl.pallas_call(
        _scan_sample_kernel,
        grid=(rows // bb,),
        in_specs=[sspec] + [spec] * 6,
        out_specs=[spec, sspec],
        out_shape=[jax.ShapeDtypeStruct((rows, B_WIDTH), F32),
                   jax.ShapeDtypeStruct(state.shape, F32)],
        compiler_params=_params(),
        name="scan_sample",
    )(state, r, w, k, v, kk, b)


def _softmax_rows(s):
    e = jnp.exp(s - jnp.max(s, axis=-1, keepdims=True))
    return e / jnp.sum(e, axis=-1, keepdims=True)


def _post_kernel(*refs, attend):
    it = iter(refs)
    (x_ref, ya_ref, o_ref, g_ref, bonus_ref, gng_ref, gnb_ref, ones_ref, woa_ref, wob_ref,
     lnx_ref, wq_ref) = [next(it) for _ in range(12)]
    if attend:
        mk_ref, mv_ref = next(it), next(it)
    x2_ref, out_ref = next(it), next(it)

    ones_bd = ones_ref[...]
    o = o_ref[...]
    mu = _seg_sum(o, ones_bd) * (1.0 / HEAD)
    d = o - mu
    var = _seg_sum(d * d, ones_bd) * (1.0 / HEAD)
    on = d * lax.rsqrt(var + GN_EPS) * gng_ref[...] + gnb_ref[...]
    yb = (on + bonus_ref[...]) * g_ref[...]
    x2 = x_ref[...] + jnp.dot(ya_ref[...], woa_ref[...], preferred_element_type=F32) \
        + _dot(yb, wob_ref[...])
    x2_ref[...] = x2
    q = _dot(_rms(x2, lnx_ref[...]), wq_ref[...])
    if not attend:
        out_ref[...] = q
        return
    qb = q.astype(BF16)
    for h in range(XA_HEADS):
        sl = slice(h * XA_DIM, (h + 1) * XA_DIM)
        s = lax.dot_general(qb[:, sl], mk_ref[0, :, sl], (((1,), (1,)), ((), ())),
                            preferred_element_type=F32) * (XA_DIM ** -0.5)
        p = _softmax_rows(s)
        out_ref[:, sl] = _dot(p, mv_ref[0, :, sl]).astype(BF16)


def _post_mix(x, ya, o, g, bonus, p, *, tm, mk=None, mv=None):
    rows = x.shape[0]
    attend = mk is not None
    args = [x, ya, o, g, bonus, p["gn_g"], p["gn_b"], p["ones_bd"], p["w_out_a"], p["w_out_b"],
            p["ln_xattn"], p["xa_q"]]
    specs = [_rows_spec(tm, D_MODEL)] + [_rows_spec(tm, B_WIDTH)] * 4 + [
        _const_spec((1, B_WIDTH)), _const_spec((1, B_WIDTH)), _const_spec((B_WIDTH, B_WIDTH)),
        _const_spec((A_WIDTH, D_MODEL)), _const_spec((B_WIDTH, D_MODEL)),
        _const_spec((1, D_MODEL)), _const_spec((D_MODEL, D_MODEL))]
    if attend:
        tiles_per_seq = SEQ // tm
        mspec = pl.BlockSpec((1, N_MEM, D_MODEL), lambda i: (i // tiles_per_seq, 0, 0))
        args += [mk, mv]
        specs += [mspec, mspec]
    return pl.pallas_call(
        functools.partial(_post_kernel, attend=attend),
        grid=(rows // tm,),
        in_specs=specs,
        out_specs=[_rows_spec(tm, D_MODEL)] * 2,
        out_shape=[jax.ShapeDtypeStruct((rows, D_MODEL), F32),
                   jax.ShapeDtypeStruct((rows, D_MODEL), BF16 if attend else F32)],
        compiler_params=_params(),
        name="post_mix",
    )(*args)


def _xa_sample_kernel(q_ref, k_ref, v_ref, o_ref):
    for j in range(q_ref.shape[0]):
        prod = k_ref[j] * q_ref[j:j + 1, :]
        for h in range(XA_HEADS):
            sl = slice(h * XA_DIM, (h + 1) * XA_DIM)
            s = jnp.sum(prod[:, sl], axis=-1, keepdims=True) * (XA_DIM ** -0.5)
            e = jnp.exp(s - jnp.max(s, axis=0, keepdims=True))
            p = e / jnp.sum(e, axis=0, keepdims=True)
            o_ref[j:j + 1, sl] = jnp.sum(p * v_ref[j, :, sl], axis=0, keepdims=True)


def _xa_sample(q, mk, mv, *, bb):
    rows = q.shape[0]
    mspec = pl.BlockSpec((bb, N_MEM, D_MODEL), lambda i: (i, 0, 0))
    return pl.pallas_call(
        _xa_sample_kernel,
        grid=(rows // bb,),
        in_specs=[_rows_spec(bb, D_MODEL), mspec, mspec],
        out_specs=_rows_spec(bb, D_MODEL),
        out_shape=jax.ShapeDtypeStruct((rows, D_MODEL), F32),
        compiler_params=_params(),
        name="xa_sample",
    )(q, mk, mv)


def _pad_lora(x):
    wd = x[..., :DECAY_LORA]
    ad = x[..., DECAY_LORA:DECAY_LORA + AAA_LORA]
    gd = x[..., DECAY_LORA + AAA_LORA:]
    z = lambda n: jnp.zeros(x.shape[:-1] + (n,), x.dtype)
    return jnp.concatenate([wd, z(LORA_AD - DECAY_LORA), ad, z(LORA_GD - LORA_AD - AAA_LORA),
                            gd, z(LORA_W - LORA_GD - GATE_LORA)], axis=-1)


def _unpad_shift(zm, zl):
    return jnp.concatenate([zm, zl[..., LORA_WD:LORA_WD + DECAY_LORA],
                            zl[..., LORA_AD:LORA_AD + AAA_LORA],
                            zl[..., LORA_GD:LORA_GD + GATE_LORA]], axis=-1)


def _pad_rows(w, n):
    return jnp.pad(w, ((0, n - w.shape[0]), (0, 0)))


def _to_pairs(s):
    b = s.shape[0]
    return s.reshape(b, PAIRS, 2, HEAD, HEAD).transpose(0, 1, 3, 2, 4).reshape(
        b, PAIRS, HEAD, PAIR_W)


def _from_pairs(s):
    b = s.shape[0]
    return s.reshape(b, PAIRS, HEAD, 2, HEAD).transpose(0, 1, 3, 2, 4).reshape(
        b, HEADS, HEAD, HEAD)


def kernel(x_prompt, x_sample, state_rwkv, state_shift, cache_mem_k, cache_mem_v, mem_prompt, ln_ffn1, ffn1_gate, ffn1_up, ffn1_down, ln_mix, w_in, w_out, sgu_w, sgu_b, sgu_ln_g, sgu_ln_b, rwkv_mu, rwkv_w0, rwkv_w2, rwkv_a0, rwkv_a2, rwkv_g2, rwkv_k_k, rwkv_k_a, rwkv_r_k, rwkv_gn_g, rwkv_gn_b, ln_xattn, mem_norm, xa_q, xa_k, xa_v, xa_o, ln_ffn2, ffn2_gate, ffn2_up, ffn2_down, final_norm):
    assert ln_ffn1.shape[0] == 1, "single layer"
    bp, seq, _ = x_prompt.shape
    bs = x_sample.shape[0]
    row = lambda a: a.reshape(1, -1).astype(F32)
    bf = lambda a: a.astype(BF16)
    l = 0
    head_id = jnp.arange(B_WIDTH) // HEAD
    tril = jnp.tril(jnp.ones((CHUNK, CHUNK), dtype=bool))
    wmask = jnp.where(tril[None], sgu_w[l], 0)
    p = {
        "ln_mix": row(ln_mix[l]),
        "w_main": bf(w_in[l][:, :MAIN_W]),
        "w_lora": bf(_pad_lora(w_in[l][:, MAIN_W:])),
        "sgu_wcat": bf(wmask.transpose(1, 0, 2).reshape(CHUNK, A_GROUPS * CHUNK)),
        "sgu_bias": jnp.repeat(sgu_b[l].T, A_GROUP_DIM, axis=1),
        "sgu_w00": row(jnp.repeat(sgu_w[l][:, 0, 0], A_GROUP_DIM)),
        "sgu_b0": row(jnp.repeat(sgu_b[l][:, 0], A_GROUP_DIM)),
        "sgu_ln_g": row(sgu_ln_g[l]), "sgu_ln_b": row(sgu_ln_b[l]),
        "mu_main": row(rwkv_mu[l][:RKV_W]),
        "mu_lora": row(_pad_lora(rwkv_mu[l][RKV_W:])),
        "w0": row(rwkv_w0[l]), "w2": bf(_pad_rows(rwkv_w2[l], LORA_AD - LORA_WD)),
        "a0": row(rwkv_a0[l]), "a2": bf(_pad_rows(rwkv_a2[l], LORA_GD - LORA_AD)),
        "g2": bf(_pad_rows(rwkv_g2[l], LORA_W - LORA_GD)),
        "k_k": row(rwkv_k_k[l]), "k_a": row(rwkv_k_a[l]), "r_k": row(rwkv_r_k[l]),
        "ones_bd": (head_id[:, None] == head_id[None, :]).astype(BF16),
        "gn_g": row(rwkv_gn_g[l]), "gn_b": row(rwkv_gn_b[l]),
        "w_out_a": bf(w_out[l][:A_WIDTH]), "w_out_b": bf(w_out[l][A_WIDTH:]),
        "ln_xattn": row(ln_xattn[l]), "xa_q": bf(xa_q[l]),
    }
    ffn1 = (row(ln_ffn1[l]), bf(ffn1_gate[l]), bf(ffn1_up[l]), bf(ffn1_down[l]))
    ffn2 = (row(ln_ffn2[l]), bf(ffn2_gate[l]), bf(ffn2_up[l]), bf(ffn2_down[l]))
    xa_o_b = bf(xa_o[l])
    fnorm = row(final_norm)

    tm = 512
    xp = x_prompt.reshape(bp * seq, D_MODEL)
    mk, mv, mkb, mvb = _memkv(mem_prompt.reshape(bp * N_MEM, D_MODEL), row(mem_norm[l]),
                              bf(xa_k[l]), bf(xa_v[l]), tm=tm)
    x1 = _ffn(xp, *ffn1, tm=tm)
    ya, r, w, k, v, kk, b, g, bonus, zlast = _mix_in(x1, p, tm=tm, sample=False)
    o, s_pairs = _scan_prompt(r, w, k, v, kk, b, batch=bp, tt=256)
    x2, attn = _post_mix(x1, ya, o, g, bonus, p, tm=tm,
                         mk=mkb.reshape(bp, N_MEM, D_MODEL), mv=mvb.reshape(bp, N_MEM, D_MODEL))
    y_prompt = _ffn(x2, *ffn2, tm=tm, attn=attn, wo=xa_o_b, final_norm=fnorm)
    tiles_per_seq = seq // tm
    zl_rows = zlast.reshape(bp, tiles_per_seq, 8, RKV_W + LORA_W)[:, -1, 0]
    shift_p = _unpad_shift(zl_rows[:, :RKV_W], zl_rows[:, RKV_W:])

    xs = x_sample.reshape(bs, D_MODEL)
    sh = state_shift[l].reshape(bs, B_PROJ)
    x1s = _ffn(xs, *ffn1, tm=bs)
    (ya_s, r_s, w_s, k_s, v_s, kk_s, b_s, g_s, bonus_s, va_s, zm_s, zl_s) = _mix_in(
        x1s, p, tm=bs, sample=True, shift_main=sh[:, :RKV_W], shift_lora=_pad_lora(sh[:, RKV_W:]))
    o_s, s_pairs_s = _scan_sample(_to_pairs(state_rwkv[l]), r_s, w_s, k_s, v_s, kk_s, b_s, bb=8)
    x2s, q_s = _post_mix(x1s, ya_s, o_s, g_s, bonus_s, p, tm=bs)
    attn_s = _xa_sample(q_s, cache_mem_k[l].reshape(bs, N_MEM, D_MODEL),
                        cache_mem_v[l].reshape(bs, N_MEM, D_MODEL), bb=8)
    y_sample = _ffn(x2s, *ffn2, tm=bs, attn=attn_s, wo=xa_o_b, final_norm=fnorm)

    return (y_prompt.reshape(bp, seq, D_MODEL),
            y_sample.reshape(bs, 1, D_MODEL),
            _from_pairs(s_pairs)[None],
            shift_p.reshape(1, bp, 1, B_PROJ),
            mk.reshape(1, bp, N_MEM, XA_HEADS, XA_DIM),
            mv.reshape(1, bp, N_MEM, XA_HEADS, XA_DIM),
            _from_pairs(s_pairs_s)[None],
            _unpad_shift(zm_s, zl_s).reshape(1, bs, 1, B_PROJ),
            va_s.reshape(1, bs, 1, A_WIDTH))
```

```python
import functools

import jax
import jax.numpy as jnp
from jax import lax
from jax.experimental import pallas as pl
from jax.experimental.pallas import tpu as pltpu

F32 = jnp.float32
BF16 = jnp.bfloat16

D_MODEL = 1024
SEQ = 2048
A_WIDTH = 512
A_GROUPS = 8
A_GROUP_DIM = 64
CHUNK = 128
B_WIDTH = 512
HEAD = 64
HEADS = 8
PAIRS = HEADS // 2
PAIR_W = 2 * HEAD
DECAY_LORA = 64
AAA_LORA = 64
GATE_LORA = 160
B_PROJ = 3 * B_WIDTH + DECAY_LORA + AAA_LORA + GATE_LORA
MAIN_W = 2 * A_WIDTH + 3 * B_WIDTH
RKV_W = 3 * B_WIDTH
LORA_W = 512
LORA_WD, LORA_AD, LORA_GD = 0, 128, 256
D_FF = 2816
N_MEM = 256
XA_HEADS = 4
XA_DIM = 256
NORM_EPS = 1e-6
LN_EPS = 1e-5
GN_EPS = 64e-5

VMEM_LIMIT = 56 * 1024 * 1024


def _params(n_axes=1):
    return pltpu.CompilerParams(dimension_semantics=("arbitrary",) * n_axes,
                                vmem_limit_bytes=VMEM_LIMIT)


def _const_spec(shape):
    nd = len(shape)
    return pl.BlockSpec(shape, lambda *_: (0,) * nd, pipeline_mode=pl.Buffered(1))


def _rows_spec(tm, width):
    return pl.BlockSpec((tm, width), lambda i: (i, 0))


def _rms(x, g):
    return x * lax.rsqrt(jnp.mean(x * x, axis=-1, keepdims=True) + NORM_EPS) * g


def _dot(a, b):
    return jnp.dot(a.astype(BF16), b, preferred_element_type=F32)


def _seg_sum(x, ones_bd):
    hi = x.astype(BF16)
    lo = (x - hi.astype(F32)).astype(BF16)
    return (jnp.dot(hi, ones_bd, preferred_element_type=F32)
            + jnp.dot(lo, ones_bd, preferred_element_type=F32))


def _ffn_kernel(*refs, pre, final):
    it = iter(refs)
    x_ref = next(it)
    if pre:
        attn_ref, wo_ref = next(it), next(it)
    ln_ref, wg_ref, wu_ref, wd_ref = next(it), next(it), next(it), next(it)
    if final:
        fn_ref = next(it)
    o_ref = next(it)
    x = x_ref[...]
    if pre:
        x = x + _dot(attn_ref[...], wo_ref[...])
    xb = _rms(x, ln_ref[...]).astype(BF16)
    g = jnp.dot(xb, wg_ref[...], preferred_element_type=F32)
    u = jnp.dot(xb, wu_ref[...], preferred_element_type=F32)
    h = (g * jax.nn.sigmoid(g) * u).astype(BF16)
    x = x + 0.5 * jnp.dot(h, wd_ref[...], preferred_element_type=F32)
    if final:
        x = _rms(x, fn_ref[...])
    o_ref[...] = x


def _ffn(x, ln, wg, wu, wd, *, tm, attn=None, wo=None, final_norm=None):
    rows = x.shape[0]
    pre = attn is not None
    final = final_norm is not None
    args, specs = [x], [_rows_spec(tm, D_MODEL)]
    if pre:
        args += [attn, wo]
        specs += [_rows_spec(tm, D_MODEL), _const_spec((D_MODEL, D_MODEL))]
    args += [ln, wg, wu, wd]
    specs += [_const_spec((1, D_MODEL)), _const_spec((D_MODEL, D_FF)),
              _const_spec((D_MODEL, D_FF)), _const_spec((D_FF, D_MODEL))]
    if final:
        args.append(final_norm)
        specs.append(_const_spec((1, D_MODEL)))
    return pl.pallas_call(
        functools.partial(_ffn_kernel, pre=pre, final=final),
        grid=(rows // tm,),
        in_specs=specs,
        out_specs=_rows_spec(tm, D_MODEL),
        out_shape=jax.ShapeDtypeStruct((rows, D_MODEL), F32),
        compiler_params=_params(),
        name="ffn",
    )(*args)


def _memkv_kernel(m_ref, g_ref, wk_ref, wv_ref, k_ref, v_ref, kb_ref, vb_ref):
    mb = _rms(m_ref[...], g_ref[...]).astype(BF16)
    k = jnp.dot(mb, wk_ref[...], preferred_element_type=F32)
    v = jnp.dot(mb, wv_ref[...], preferred_element_type=F32)
    k_ref[...] = k
    v_ref[...] = v
    kb_ref[...] = k.astype(BF16)
    vb_ref[...] = v.astype(BF16)


def _memkv(mem, g, wk, wv, *, tm):
    rows = mem.shape[0]
    out = jax.ShapeDtypeStruct((rows, D_MODEL), F32)
    outb = jax.ShapeDtypeStruct((rows, D_MODEL), BF16)
    return pl.pallas_call(
        _memkv_kernel,
        grid=(rows // tm,),
        in_specs=[_rows_spec(tm, D_MODEL), _const_spec((1, D_MODEL)),
                  _const_spec((D_MODEL, D_MODEL)), _const_spec((D_MODEL, D_MODEL))],
        out_specs=[_rows_spec(tm, D_MODEL)] * 4,
        out_shape=[out, out, outb, outb],
        compiler_params=_params(),
        name="memkv",
    )(mem, g, wk, wv)


def _mix_kernel(*refs, sample, tiles_per_seq):
    it = iter(refs)
    x_ref, ln_ref, wmain_ref, wlora_ref = next(it), next(it), next(it), next(it)
    if sample:
        w00_ref, b0_ref, spm_ref, spl_ref = next(it), next(it), next(it), next(it)
    else:
        wcat_ref, bias_ref = next(it), next(it)
    (lng_ref, lnb_ref, mum_ref, mul_ref, w0_ref, w2_ref, a0_ref, a2_ref, g2_ref,
     kk_ref, ka_ref, rk_ref, ones_ref) = [next(it) for _ in range(13)]
    (ya_ref, r_ref, w_ref, k_ref, v_ref, kn_ref, b_ref, g_ref, bonus_ref) = [
        next(it) for _ in range(9)]
    if sample:
        va_ref, zm_ref, zl_ref = next(it), next(it), next(it)
    else:
        zlast_ref, cm_ref, cl_ref = next(it), next(it), next(it)

    tm = x_ref.shape[0]
    xb = _rms(x_ref[...], ln_ref[...]).astype(BF16)
    zmain = jnp.dot(xb, wmain_ref[...], preferred_element_type=F32)
    zl = jnp.dot(xb, wlora_ref[...], preferred_element_type=F32)

    za = jax.nn.gelu(zmain[:, :2 * A_WIDTH])
    u = za[:, :A_WIDTH]
    vx = za[:, A_WIDTH:]
    mu = jnp.mean(vx, axis=-1, keepdims=True)
    var = jnp.mean(jnp.square(vx - mu), axis=-1, keepdims=True)
    va = (vx - mu) * lax.rsqrt(var + LN_EPS) * lng_ref[...] + lnb_ref[...]
    if sample:
        mixed = va * w00_ref[...] + b0_ref[...]
        ya_ref[...] = (u * mixed).astype(BF16)
        va_ref[...] = va
    else:
        vab = va.astype(BF16)
        group = lax.broadcasted_iota(jnp.int32, (CHUNK, A_WIDTH), 1) // A_GROUP_DIM
        for c in range(tm // CHUNK):
            vc = vab[c * CHUNK:(c + 1) * CHUNK]
            rhs = jnp.concatenate(
                [jnp.where(group == g, vc, jnp.zeros_like(vc)) for g in range(A_GROUPS)],
                axis=0)
            mixed = jnp.dot(wcat_ref[...], rhs, preferred_element_type=F32) + bias_ref[...]
            ya_ref[c * CHUNK:(c + 1) * CHUNK, :] = (
                u[c * CHUNK:(c + 1) * CHUNK] * mixed).astype(BF16)

    zbm = zmain[:, 2 * A_WIDTH:]
    if sample:
        zpm, zpl = spm_ref[...], spl_ref[...]
        zm_ref[...] = zbm
        zl_ref[...] = zl
    else:
        i = pl.program_id(0)

        @pl.when(i % tiles_per_seq == 0)
        def _():
            cm_ref[...] = jnp.zeros_like(cm_ref)
            cl_ref[...] = jnp.zeros_like(cl_ref)

        first_m = lax.broadcasted_iota(jnp.int32, zbm.shape, 0) == 0
        first_l = lax.broadcasted_iota(jnp.int32, zl.shape, 0) == 0
        zpm = jnp.where(first_m, cm_ref[0:1, :], pltpu.roll(zbm, 1, axis=0))
        zpl = jnp.where(first_l, cl_ref[0:1, :], pltpu.roll(zl, 1, axis=0))
        cm_ref[0:1, :] = zbm[tm - 1:tm, :]
        cl_ref[0:1, :] = zl[tm - 1:tm, :]
        zlast_ref[:, :RKV_W] = jnp.broadcast_to(zbm[tm - 1:tm, :], (8, RKV_W))
        zlast_ref[:, RKV_W:] = jnp.broadcast_to(zl[tm - 1:tm, :], (8, LORA_W))
    zsm = zbm + (zpm - zbm) * mum_ref[...]
    zsl = zl + (zpl - zl) * mul_ref[...]
    r = zsm[:, :B_WIDTH]
    k = zsm[:, B_WIDTH:2 * B_WIDTH]
    v = zsm[:, 2 * B_WIDTH:]
    wd = zsl[:, LORA_WD:LORA_AD]
    ad = zsl[:, LORA_AD:LORA_GD]
    gd = zsl[:, LORA_GD:]
    w_log = -jax.nn.softplus(-(w0_ref[...] + _dot(jnp.tanh(wd), w2_ref[...]))) - 0.5
    log_decay = -jnp.exp(w_log)
    a = jax.nn.sigmoid(a0_ref[...] + _dot(ad, a2_ref[...]))
    gate = _dot(jax.nn.sigmoid(gd), g2_ref[...])
    ones_bd = ones_ref[...]
    kk = k * kk_ref[...]
    kk = kk * lax.rsqrt(jnp.maximum(_seg_sum(kk * kk, ones_bd), 1e-24))
    k2 = k * (1.0 + (a - 1.0) * ka_ref[...])
    r_ref[...] = r
    w_ref[...] = jnp.exp(log_decay) if sample else log_decay
    k_ref[...] = k2
    v_ref[...] = v
    kn_ref[...] = kk
    b_ref[...] = kk * a
    g_ref[...] = gate
    bonus_ref[...] = _seg_sum(r * k2 * rk_ref[...], ones_bd) * v


def _mix_in(x, p, *, tm, sample, shift_main=None, shift_lora=None):
    rows = x.shape[0]
    n_tiles = rows // tm
    args = [x, p["ln_mix"], p["w_main"], p["w_lora"]]
    specs = [_rows_spec(tm, D_MODEL), _const_spec((1, D_MODEL)),
             _const_spec((D_MODEL, MAIN_W)), _const_spec((D_MODEL, LORA_W))]
    if sample:
        args += [p["sgu_w00"], p["sgu_b0"], shift_main, shift_lora]
        specs += [_const_spec((1, A_WIDTH)), _const_spec((1, A_WIDTH)),
                  _rows_spec(tm, RKV_W), _rows_spec(tm, LORA_W)]
    else:
        args += [p["sgu_wcat"], p["sgu_bias"]]
        specs += [_const_spec((CHUNK, A_GROUPS * CHUNK)), _const_spec((CHUNK, A_WIDTH))]
    args += [p["sgu_ln_g"], p["sgu_ln_b"], p["mu_main"], p["mu_lora"], p["w0"], p["w2"],
             p["a0"], p["a2"], p["g2"], p["k_k"], p["k_a"], p["r_k"], p["ones_bd"]]
    specs += [_const_spec((1, A_WIDTH)), _const_spec((1, A_WIDTH)), _const_spec((1, RKV_W)),
              _const_spec((1, LORA_W)), _const_spec((1, B_WIDTH)),
              _const_spec((LORA_AD - LORA_WD, B_WIDTH)), _const_spec((1, B_WIDTH)),
              _const_spec((LORA_GD - LORA_AD, B_WIDTH)), _const_spec((LORA_W - LORA_GD, B_WIDTH)),
              _const_spec((1, B_WIDTH)), _const_spec((1, B_WIDTH)), _const_spec((1, B_WIDTH)),
              _const_spec((B_WIDTH, B_WIDTH))]
    wide = jax.ShapeDtypeStruct((rows, B_WIDTH), F32)
    out_shape = [jax.ShapeDtypeStruct((rows, A_WIDTH), BF16)] + [wide] * 8
    out_specs = [_rows_spec(tm, B_WIDTH)] * 9
    scratch = []
    if sample:
        out_shape += [wide, jax.ShapeDtypeStruct((rows, RKV_W), F32),
                      jax.ShapeDtypeStruct((rows, LORA_W), F32)]
        out_specs += [_rows_spec(tm, A_WIDTH), _rows_spec(tm, RKV_W), _rows_spec(tm, LORA_W)]
    else:
        out_shape += [jax.ShapeDtypeStruct((n_tiles * 8, RKV_W + LORA_W), F32)]
        out_specs += [pl.BlockSpec((8, RKV_W + LORA_W), lambda i: (i, 0))]
        scratch = [pltpu.VMEM((8, RKV_W), F32), pltpu.VMEM((8, LORA_W), F32)]
    return pl.pallas_call(
        functools.partial(_mix_kernel, sample=sample, tiles_per_seq=max(SEQ // tm, 1)),
        grid=(n_tiles,),
        in_specs=specs,
        out_specs=out_specs,
        out_shape=out_shape,
        scratch_shapes=scratch,
        compiler_params=_params(),
        name="mix_in",
    )(*args)


def _pair_halves(x, left):
    zero = jnp.zeros_like(x)
    s0 = jnp.sum(jnp.where(left, x, zero), axis=-1, keepdims=True)
    s1 = jnp.sum(jnp.where(left, zero, x), axis=-1, keepdims=True)
    return jnp.where(left, s0, s1)


def _scan_step(s, r, w, k, v, kk, b, eye2, left):
    sa = -_pair_halves(s * kk, left)
    vcol = _pair_halves(eye2 * v, left)
    s = s * w + sa * b + vcol * k
    ocol = _pair_halves(s * r, left)
    return s, jnp.sum(eye2 * ocol, axis=0, keepdims=True)


def _pair_consts():
    lane = lax.broadcasted_iota(jnp.int32, (HEAD, PAIR_W), 1)
    row = lax.broadcasted_iota(jnp.int32, (HEAD, PAIR_W), 0)
    left = lane < HEAD
    eye2 = jnp.where((lane % HEAD) == row, 1.0, 0.0).astype(F32)
    return eye2, left


SCAN_C = 64


def _mm(a, b):
    return jnp.dot(a.astype(BF16), b.astype(BF16), preferred_element_type=F32)


def _mm_nt(a, b):
    return lax.dot_general(a.astype(BF16), b.astype(BF16), (((1,), (1,)), ((), ())),
                           preferred_element_type=F32)


def _mm_tn(a, b):
    return lax.dot_general(a.astype(BF16), b.astype(BF16), (((0,), (0,)), ((), ())),
                           preferred_element_type=F32)


def _cumsum_rows(x):
    n = x.shape[0]
    row = lax.broadcasted_iota(jnp.int32, x.shape, 0)
    s = 1
    while s < n:
        x = x + jnp.where(row >= s, pltpu.roll(x, s, axis=0), 0.0)
        s *= 2
    return x


INV_BASE = 8


def _unit_lower_inverse(n, row, col):
    f0 = jnp.zeros((), F32)
    same = lambda s: (row // s) == (col // s)
    nb = jnp.where(same(INV_BASE), n, f0)
    t = jnp.where(row == col, 1.0, f0) + nb
    p, s = nb, 2
    while s < INV_BASE:
        p = _mm(p, p)
        t = t + _mm(t, p)
        s *= 2
    s = INV_BASE
    while s < SCAN_C:
        off = jnp.where(same(2 * s) & jnp.logical_not(same(s)), n, f0)
        t = t + _mm(t, _mm(off, t))
        s *= 2
    return t


def _chunk_pair(s0, r, lw, k, v, kk, b):
    c = SCAN_C
    f0 = jnp.zeros((), F32)
    row = lax.broadcasted_iota(jnp.int32, (2 * c, PAIR_W), 0)
    col = lax.broadcasted_iota(jnp.int32, (2 * c, PAIR_W), 1)
    top, lft = row < c, col < HEAD
    strict = (row % c) > (col % HEAD)
    row_c = lax.broadcasted_iota(jnp.int32, (c, PAIR_W), 0)
    col_c = lax.broadcasted_iota(jnp.int32, (c, PAIR_W), 1)
    lft_c = col_c < HEAD
    strict_c = row_c > (col_c % HEAD)
    incl_c = row_c >= (col_c % HEAD)

    cum = _cumsum_rows(lw)
    end = cum[c - 1:c, :]
    a_t = -kk * jnp.exp(cum - lw)
    r_t = r * jnp.exp(cum)
    einv = jnp.exp(-cum)
    b_t, k_t = b * einv, k * einv
    eend = jnp.exp(end - cum)
    b_e, k_e = b * eend, k * eend

    x0 = jnp.concatenate([a_t, r_t], axis=0)
    x1 = jnp.concatenate([r_t, a_t], axis=0)
    g0 = _mm_nt(jnp.where(lft, x0, f0), jnp.concatenate([b_t, k_t], axis=0))
    g1 = _mm_nt(jnp.where(lft, f0, x1), jnp.concatenate([k_t, b_t], axis=0))
    pq = _mm_nt(x0, s0)
    v_l, v_r = jnp.where(lft_c, v, f0), jnp.where(lft_c, f0, v)

    ak = jnp.where(strict_c, jnp.where(lft_c, g1[c:], g0[:c]), f0)
    rhs = pq[:c] + _mm(ak, jnp.concatenate([v_r, v_l], axis=0))
    y = jnp.concatenate([jnp.where(lft_c, rhs, f0), jnp.where(lft_c, f0, rhs)], axis=0)
    n = jnp.where(strict & (top == lft), jnp.where(top, g0, g1), f0)
    y = _mm(_unit_lower_inverse(n, row, col), y)
    u = y[:c] + y[c:]

    lhs = jnp.concatenate([jnp.where(incl_c, g0[c:], f0), jnp.where(incl_c, g1[:c], f0)], axis=1)
    o = pq[c:] + _mm(lhs, jnp.concatenate([y[:c], v_l, v_r, y[c:]], axis=0))

    upd = _mm_tn(jnp.concatenate([u, v], axis=0), jnp.concatenate([b_e, k_e], axis=0))
    s_new = s0 * jnp.exp(end) + jnp.where(top == lft, upd, f0)
    return o, s_new


def _scan_prompt_kernel(r_ref, w_ref, k_ref, v_ref, kk_ref, b_ref, o_ref, sout_ref, s_ref):
    t_blk = pl.program_id(1)

    @pl.when(t_blk == 0)
    def _():
        s_ref[...] = jnp.zeros_like(s_ref)

    for p in range(PAIRS):
        sl = slice(p * PAIR_W, (p + 1) * PAIR_W)
        o, s_new = _chunk_pair(s_ref[p], r_ref[:, sl], w_ref[:, sl], k_ref[:, sl], v_ref[:, sl],
                               kk_ref[:, sl], b_ref[:, sl])
        o_ref[:, sl] = o
        s_ref[p] = s_new

    @pl.when(t_blk == pl.num_programs(1) - 1)
    def _():
        sout_ref[0] = s_ref[...]


def _scan_prompt(r, lw, k, v, kk, b, *, batch, seq):
    n_t = seq // SCAN_C
    spec = pl.BlockSpec((SCAN_C, B_WIDTH), lambda bi, ti: (bi * n_t + ti, 0))
    return pl.pallas_call(
        _scan_prompt_kernel,
        grid=(batch, n_t),
        in_specs=[spec] * 6,
        out_specs=[spec, pl.BlockSpec((1, PAIRS, PAIR_W, PAIR_W), lambda bi, ti: (bi, 0, 0, 0))],
        out_shape=[jax.ShapeDtypeStruct((batch * seq, B_WIDTH), F32),
                   jax.ShapeDtypeStruct((batch, PAIRS, PAIR_W, PAIR_W), F32)],
        scratch_shapes=[pltpu.VMEM((PAIRS, PAIR_W, PAIR_W), F32)],
        compiler_params=_params(2),
        name="scan_prompt",
    )(r, lw, k, v, kk, b)


def _scan_sample_kernel(s_ref, r_ref, w_ref, k_ref, v_ref, kk_ref, b_ref, o_ref, sout_ref):
    eye2, left = _pair_consts()
    for j in range(s_ref.shape[0]):
        for p in range(PAIRS):
            sl = slice(p * PAIR_W, (p + 1) * PAIR_W)
            row = lambda ref: ref[j:j + 1, sl]
            s, o = _scan_step(s_ref[j, p], row(r_ref), row(w_ref), row(k_ref), row(v_ref),
                              row(kk_ref), row(b_ref), eye2, left)
            sout_ref[j, p] = s
            o_ref[j:j + 1, sl] = o


def _scan_sample(state, r, w, k, v, kk, b, *, bb):
    rows = r.shape[0]
    sspec = pl.BlockSpec((bb, PAIRS, HEAD, PAIR_W), lambda i: (i, 0, 0, 0))
    spec = _rows_spec(bb, B_WIDTH)
    return pl.pallas_call(
        _scan_sample_kernel,
        grid=(rows // bb,),
        in_specs=[sspec] + [spec] * 6,
        out_specs=[spec, sspec],
        out_shape=[jax.ShapeDtypeStruct((rows, B_WIDTH), F32),
                   jax.ShapeDtypeStruct(state.shape, F32)],
        compiler_params=_params(),
        name="scan_sample",
    )(state, r, w, k, v, kk, b)


def _softmax_rows(s):
    e = jnp.exp(s - jnp.max(s, axis=-1, keepdims=True))
    return e / jnp.sum(e, axis=-1, keepdims=True)


def _post_kernel(*refs, attend):
    it = iter(refs)
    (x_ref, ya_ref, o_ref, g_ref, bonus_ref, gng_ref, gnb_ref, ones_ref, woa_ref, wob_ref,
     lnx_ref, wq_ref) = [next(it) for _ in range(12)]
    if attend:
        mk_ref, mv_ref = next(it), next(it)
    x2_ref, out_ref = next(it), next(it)

    ones_bd = ones_ref[...]
    o = o_ref[...]
    mu = _seg_sum(o, ones_bd) * (1.0 / HEAD)
    d = o - mu
    var = _seg_sum(d * d, ones_bd) * (1.0 / HEAD)
    on = d * lax.rsqrt(var + GN_EPS) * gng_ref[...] + gnb_ref[...]
    yb = (on + bonus_ref[...]) * g_ref[...]
    x2 = x_ref[...] + jnp.dot(ya_ref[...], woa_ref[...], preferred_element_type=F32) \
        + _dot(yb, wob_ref[...])
    x2_ref[...] = x2
    q = _dot(_rms(x2, lnx_ref[...]), wq_ref[...])
    if not attend:
        out_ref[...] = q
        return
    qb = q.astype(BF16)
    for h in range(XA_HEADS):
        sl = slice(h * XA_DIM, (h + 1) * XA_DIM)
        s = lax.dot_general(qb[:, sl], mk_ref[0, :, sl], (((1,), (1,)), ((), ())),
                            preferred_element_type=F32) * (XA_DIM ** -0.5)
        p = _softmax_rows(s)
        out_ref[:, sl] = _dot(p, mv_ref[0, :, sl]).astype(BF16)


def _post_mix(x, ya, o, g, bonus, p, *, tm, mk=None, mv=None):
    rows = x.shape[0]
    attend = mk is not None
    args = [x, ya, o, g, bonus, p["gn_g"], p["gn_b"], p["ones_bd"], p["w_out_a"], p["w_out_b"],
            p["ln_xattn"], p["xa_q"]]
    specs = [_rows_spec(tm, D_MODEL)] + [_rows_spec(tm, B_WIDTH)] * 4 + [
        _const_spec((1, B_WIDTH)), _const_spec((1, B_WIDTH)), _const_spec((B_WIDTH, B_WIDTH)),
        _const_spec((A_WIDTH, D_MODEL)), _const_spec((B_WIDTH, D_MODEL)),
        _const_spec((1, D_MODEL)), _const_spec((D_MODEL, D_MODEL))]
    if attend:
        tiles_per_seq = SEQ // tm
        mspec = pl.BlockSpec((1, N_MEM, D_MODEL), lambda i: (i // tiles_per_seq, 0, 0))
        args += [mk, mv]
        specs += [mspec, mspec]
    return pl.pallas_call(
        functools.partial(_post_kernel, attend=attend),
        grid=(rows // tm,),
        in_specs=specs,
        out_specs=[_rows_spec(tm, D_MODEL)] * 2,
        out_shape=[jax.ShapeDtypeStruct((rows, D_MODEL), F32),
                   jax.ShapeDtypeStruct((rows, D_MODEL), BF16 if attend else F32)],
        compiler_params=_params(),
        name="post_mix",
    )(*args)


def _xa_sample_kernel(q_ref, k_ref, v_ref, o_ref):
    for j in range(q_ref.shape[0]):
        prod = k_ref[j] * q_ref[j:j + 1, :]
        for h in range(XA_HEADS):
            sl = slice(h * XA_DIM, (h + 1) * XA_DIM)
            s = jnp.sum(prod[:, sl], axis=-1, keepdims=True) * (XA_DIM ** -0.5)
            e = jnp.exp(s - jnp.max(s, axis=0, keepdims=True))
            p = e / jnp.sum(e, axis=0, keepdims=True)
            o_ref[j:j + 1, sl] = jnp.sum(p * v_ref[j, :, sl], axis=0, keepdims=True)


def _xa_sample(q, mk, mv, *, bb):
    rows = q.shape[0]
    mspec = pl.BlockSpec((bb, N_MEM, D_MODEL), lambda i: (i, 0, 0))
    return pl.pallas_call(
        _xa_sample_kernel,
        grid=(rows // bb,),
        in_specs=[_rows_spec(bb, D_MODEL), mspec, mspec],
        out_specs=_rows_spec(bb, D_MODEL),
        out_shape=jax.ShapeDtypeStruct((rows, D_MODEL), F32),
        compiler_params=_params(),
        name="xa_sample",
    )(q, mk, mv)


def _pad_lora(x):
    wd = x[..., :DECAY_LORA]
    ad = x[..., DECAY_LORA:DECAY_LORA + AAA_LORA]
    gd = x[..., DECAY_LORA + AAA_LORA:]
    z = lambda n: jnp.zeros(x.shape[:-1] + (n,), x.dtype)
    return jnp.concatenate([wd, z(LORA_AD - DECAY_LORA), ad, z(LORA_GD - LORA_AD - AAA_LORA),
                            gd, z(LORA_W - LORA_GD - GATE_LORA)], axis=-1)


def _unpad_shift(zm, zl):
    return jnp.concatenate([zm, zl[..., LORA_WD:LORA_WD + DECAY_LORA],
                            zl[..., LORA_AD:LORA_AD + AAA_LORA],
                            zl[..., LORA_GD:LORA_GD + GATE_LORA]], axis=-1)


def _pad_rows(w, n):
    return jnp.pad(w, ((0, n - w.shape[0]), (0, 0)))


def _to_pairs(s):
    b = s.shape[0]
    return s.reshape(b, PAIRS, 2, HEAD, HEAD).transpose(0, 1, 3, 2, 4).reshape(
        b, PAIRS, HEAD, PAIR_W)


def _from_pairs(s):
    b = s.shape[0]
    return s.reshape(b, PAIRS, HEAD, 2, HEAD).transpose(0, 1, 3, 2, 4).reshape(
        b, HEADS, HEAD, HEAD)


def kernel(x_prompt, x_sample, state_rwkv, state_shift, cache_mem_k, cache_mem_v, mem_prompt, ln_ffn1, ffn1_gate, ffn1_up, ffn1_down, ln_mix, w_in, w_out, sgu_w, sgu_b, sgu_ln_g, sgu_ln_b, rwkv_mu, rwkv_w0, rwkv_w2, rwkv_a0, rwkv_a2, rwkv_g2, rwkv_k_k, rwkv_k_a, rwkv_r_k, rwkv_gn_g, rwkv_gn_b, ln_xattn, mem_norm, xa_q, xa_k, xa_v, xa_o, ln_ffn2, ffn2_gate, ffn2_up, ffn2_down, final_norm):
    assert ln_ffn1.shape[0] == 1, "single layer"
    bp, seq, _ = x_prompt.shape
    bs = x_sample.shape[0]
    row = lambda a: a.reshape(1, -1).astype(F32)
    bf = lambda a: a.astype(BF16)
    l = 0
    head_id = jnp.arange(B_WIDTH) // HEAD
    tril = jnp.tril(jnp.ones((CHUNK, CHUNK), dtype=bool))
    wmask = jnp.where(tril[None], sgu_w[l], 0)
    p = {
        "ln_mix": row(ln_mix[l]),
        "w_main": bf(w_in[l][:, :MAIN_W]),
        "w_lora": bf(_pad_lora(w_in[l][:, MAIN_W:])),
        "sgu_wcat": bf(wmask.transpose(1, 0, 2).reshape(CHUNK, A_GROUPS * CHUNK)),
        "sgu_bias": jnp.repeat(sgu_b[l].T, A_GROUP_DIM, axis=1),
        "sgu_w00": row(jnp.repeat(sgu_w[l][:, 0, 0], A_GROUP_DIM)),
        "sgu_b0": row(jnp.repeat(sgu_b[l][:, 0], A_GROUP_DIM)),
        "sgu_ln_g": row(sgu_ln_g[l]), "sgu_ln_b": row(sgu_ln_b[l]),
        "mu_main": row(rwkv_mu[l][:RKV_W]),
        "mu_lora": row(_pad_lora(rwkv_mu[l][RKV_W:])),
        "w0": row(rwkv_w0[l]), "w2": bf(_pad_rows(rwkv_w2[l], LORA_AD - LORA_WD)),
        "a0": row(rwkv_a0[l]), "a2": bf(_pad_rows(rwkv_a2[l], LORA_GD - LORA_AD)),
        "g2": bf(_pad_rows(rwkv_g2[l], LORA_W - LORA_GD)),
        "k_k": row(rwkv_k_k[l]), "k_a": row(rwkv_k_a[l]), "r_k": row(rwkv_r_k[l]),
        "ones_bd": (head_id[:, None] == head_id[None, :]).astype(BF16),
        "gn_g": row(rwkv_gn_g[l]), "gn_b": row(rwkv_gn_b[l]),
        "w_out_a": bf(w_out[l][:A_WIDTH]), "w_out_b": bf(w_out[l][A_WIDTH:]),
        "ln_xattn": row(ln_xattn[l]), "xa_q": bf(xa_q[l]),
    }
    ffn1 = (row(ln_ffn1[l]), bf(ffn1_gate[l]), bf(ffn1_up[l]), bf(ffn1_down[l]))
    ffn2 = (row(ln_ffn2[l]), bf(ffn2_gate[l]), bf(ffn2_up[l]), bf(ffn2_down[l]))
    xa_o_b = bf(xa_o[l])
    fnorm = row(final_norm)

    tm = 512
    xp = x_prompt.reshape(bp * seq, D_MODEL)
    mk, mv, mkb, mvb = _memkv(mem_prompt.reshape(bp * N_MEM, D_MODEL), row(mem_norm[l]),
                              bf(xa_k[l]), bf(xa_v[l]), tm=tm)
    x1 = _ffn(xp, *ffn1, tm=tm)
    ya, r, w, k, v, kk, b, g, bonus, zlast = _mix_in(x1, p, tm=tm, sample=False)
    o, s_bd = _scan_prompt(r, w, k, v, kk, b, batch=bp, seq=seq)
    state_p = jnp.stack([s_bd[:, :, :HEAD, :HEAD], s_bd[:, :, HEAD:, HEAD:]],
                        axis=2).reshape(bp, HEADS, HEAD, HEAD)
    x2, attn = _post_mix(x1, ya, o, g, bonus, p, tm=tm,
                         mk=mkb.reshape(bp, N_MEM, D_MODEL), mv=mvb.reshape(bp, N_MEM, D_MODEL))
    y_prompt = _ffn(x2, *ffn2, tm=tm, attn=attn, wo=xa_o_b, final_norm=fnorm)
    tiles_per_seq = seq // tm
    zl_rows = zlast.reshape(bp, tiles_per_seq, 8, RKV_W + LORA_W)[:, -1, 0]
    shift_p = _unpad_shift(zl_rows[:, :RKV_W], zl_rows[:, RKV_W:])

    xs = x_sample.reshape(bs, D_MODEL)
    sh = state_shift[l].reshape(bs, B_PROJ)
    x1s = _ffn(xs, *ffn1, tm=bs)
    (ya_s, r_s, w_s, k_s, v_s, kk_s, b_s, g_s, bonus_s, va_s, zm_s, zl_s) = _mix_in(
        x1s, p, tm=bs, sample=True, shift_main=sh[:, :RKV_W], shift_lora=_pad_lora(sh[:, RKV_W:]))
    o_s, s_pairs_s = _scan_sample(_to_pairs(state_rwkv[l]), r_s, w_s, k_s, v_s, kk_s, b_s, bb=8)
    x2s, q_s = _post_mix(x1s, ya_s, o_s, g_s, bonus_s, p, tm=bs)
    attn_s = _xa_sample(q_s, cache_mem_k[l].reshape(bs, N_MEM, D_MODEL),
                        cache_mem_v[l].reshape(bs, N_MEM, D_MODEL), bb=8)
    y_sample = _ffn(x2s, *ffn2, tm=bs, attn=attn_s, wo=xa_o_b, final_norm=fnorm)

    return (y_prompt.reshape(bp, seq, D_MODEL),
            y_sample.reshape(bs, 1, D_MODEL),
            state_p[None],
            shift_p.reshape(1, bp, 1, B_PROJ),
            mk.reshape(1, bp, N_MEM, XA_HEADS, XA_DIM),
            mv.reshape(1, bp, N_MEM, XA_HEADS, XA_DIM),
            _from_pairs(s_pairs_s)[None],
            _unpad_shift(zm_s, zl_s).reshape(1, bs, 1, B_PROJ),
            va_s.reshape(1, bs, 1, A_WIDTH))
```

```python
import functools

import jax
import jax.numpy as jnp
from jax import lax
from jax.experimental import pallas as pl
from jax.experimental.pallas import tpu as pltpu

F32 = jnp.float32
BF16 = jnp.bfloat16

D_MODEL = 1024
SEQ = 2048
A_WIDTH = 512
A_GROUPS = 8
A_GROUP_DIM = 64
CHUNK = 128
B_WIDTH = 512
HEAD = 64
HEADS = 8
PAIRS = HEADS // 2
PAIR_W = 2 * HEAD
DECAY_LORA = 64
AAA_LORA = 64
GATE_LORA = 160
B_PROJ = 3 * B_WIDTH + DECAY_LORA + AAA_LORA + GATE_LORA
MAIN_W = 2 * A_WIDTH + 3 * B_WIDTH
RKV_W = 3 * B_WIDTH
LORA_W = 512
LORA_WD, LORA_AD, LORA_GD = 0, 128, 256
D_FF = 2816
N_MEM = 256
XA_HEADS = 4
XA_DIM = 256
NORM_EPS = 1e-6
LN_EPS = 1e-5
GN_EPS = 64e-5

VMEM_LIMIT = 56 * 1024 * 1024


def _params(n_axes=1):
    return pltpu.CompilerParams(dimension_semantics=("arbitrary",) * n_axes,
                                vmem_limit_bytes=VMEM_LIMIT)


def _const_spec(shape):
    nd = len(shape)
    return pl.BlockSpec(shape, lambda *_: (0,) * nd, pipeline_mode=pl.Buffered(1))


def _rows_spec(tm, width):
    return pl.BlockSpec((tm, width), lambda i: (i, 0))


def _rms(x, g):
    return x * lax.rsqrt(jnp.mean(x * x, axis=-1, keepdims=True) + NORM_EPS) * g


def _dot(a, b):
    return jnp.dot(a.astype(BF16), b, preferred_element_type=F32)


def _seg_sum(x, ones_bd):
    hi = x.astype(BF16)
    lo = (x - hi.astype(F32)).astype(BF16)
    return (jnp.dot(hi, ones_bd, preferred_element_type=F32)
            + jnp.dot(lo, ones_bd, preferred_element_type=F32))


def _ffn_kernel(*refs, pre, final):
    it = iter(refs)
    x_ref = next(it)
    if pre:
        attn_ref, wo_ref = next(it), next(it)
    ln_ref, wg_ref, wu_ref, wd_ref = next(it), next(it), next(it), next(it)
    if final:
        fn_ref = next(it)
    o_ref = next(it)
    x = x_ref[...]
    if pre:
        x = x + _dot(attn_ref[...], wo_ref[...])
    xb = _rms(x, ln_ref[...]).astype(BF16)
    g = jnp.dot(xb, wg_ref[...], preferred_element_type=F32)
    u = jnp.dot(xb, wu_ref[...], preferred_element_type=F32)
    h = (g * jax.nn.sigmoid(g) * u).astype(BF16)
    x = x + 0.5 * jnp.dot(h, wd_ref[...], preferred_element_type=F32)
    if final:
        x = _rms(x, fn_ref[...])
    o_ref[...] = x


def _ffn(x, ln, wg, wu, wd, *, tm, attn=None, wo=None, final_norm=None):
    rows = x.shape[0]
    pre = attn is not None
    final = final_norm is not None
    args, specs = [x], [_rows_spec(tm, D_MODEL)]
    if pre:
        args += [attn, wo]
        specs += [_rows_spec(tm, D_MODEL), _const_spec((D_MODEL, D_MODEL))]
    args += [ln, wg, wu, wd]
    specs += [_const_spec((1, D_MODEL)), _const_spec((D_MODEL, D_FF)),
              _const_spec((D_MODEL, D_FF)), _const_spec((D_FF, D_MODEL))]
    if final:
        args.append(final_norm)
        specs.append(_const_spec((1, D_MODEL)))
    return pl.pallas_call(
        functools.partial(_ffn_kernel, pre=pre, final=final),
        grid=(rows // tm,),
        in_specs=specs,
        out_specs=_rows_spec(tm, D_MODEL),
        out_shape=jax.ShapeDtypeStruct((rows, D_MODEL), F32),
        compiler_params=_params(),
        name="ffn",
    )(*args)


def _memkv_kernel(m_ref, g_ref, wk_ref, wv_ref, k_ref, v_ref, kb_ref, vb_ref):
    mb = _rms(m_ref[...], g_ref[...]).astype(BF16)
    k = jnp.dot(mb, wk_ref[...], preferred_element_type=F32)
    v = jnp.dot(mb, wv_ref[...], preferred_element_type=F32)
    k_ref[...] = k
    v_ref[...] = v
    kb_ref[...] = k.astype(BF16)
    vb_ref[...] = v.astype(BF16)


def _memkv(mem, g, wk, wv, *, tm):
    rows = mem.shape[0]
    out = jax.ShapeDtypeStruct((rows, D_MODEL), F32)
    outb = jax.ShapeDtypeStruct((rows, D_MODEL), BF16)
    return pl.pallas_call(
        _memkv_kernel,
        grid=(rows // tm,),
        in_specs=[_rows_spec(tm, D_MODEL), _const_spec((1, D_MODEL)),
                  _const_spec((D_MODEL, D_MODEL)), _const_spec((D_MODEL, D_MODEL))],
        out_specs=[_rows_spec(tm, D_MODEL)] * 4,
        out_shape=[out, out, outb, outb],
        compiler_params=_params(),
        name="memkv",
    )(mem, g, wk, wv)


def _mix_kernel(*refs, sample, tiles_per_seq):
    it = iter(refs)
    x_ref, ln_ref, wmain_ref, wlora_ref = next(it), next(it), next(it), next(it)
    if sample:
        w00_ref, b0_ref, spm_ref, spl_ref = next(it), next(it), next(it), next(it)
    else:
        wcat_ref, bias_ref = next(it), next(it)
    (lng_ref, lnb_ref, mum_ref, mul_ref, w0_ref, w2_ref, a0_ref, a2_ref, g2_ref,
     kk_ref, ka_ref, rk_ref, ones_ref) = [next(it) for _ in range(13)]
    (ya_ref, r_ref, w_ref, k_ref, v_ref, kn_ref, b_ref, g_ref, bonus_ref) = [
        next(it) for _ in range(9)]
    if sample:
        va_ref, zm_ref, zl_ref = next(it), next(it), next(it)
    else:
        zlast_ref, cm_ref, cl_ref = next(it), next(it), next(it)

    tm = x_ref.shape[0]
    xb = _rms(x_ref[...], ln_ref[...]).astype(BF16)
    zmain = jnp.dot(xb, wmain_ref[...], preferred_element_type=F32)
    zl = jnp.dot(xb, wlora_ref[...], preferred_element_type=F32)

    za = jax.nn.gelu(zmain[:, :2 * A_WIDTH])
    u = za[:, :A_WIDTH]
    vx = za[:, A_WIDTH:]
    mu = jnp.mean(vx, axis=-1, keepdims=True)
    var = jnp.mean(jnp.square(vx - mu), axis=-1, keepdims=True)
    va = (vx - mu) * lax.rsqrt(var + LN_EPS) * lng_ref[...] + lnb_ref[...]
    if sample:
        mixed = va * w00_ref[...] + b0_ref[...]
        ya_ref[...] = (u * mixed).astype(BF16)
        va_ref[...] = va
    else:
        vab = va.astype(BF16)
        group = lax.broadcasted_iota(jnp.int32, (CHUNK, A_WIDTH), 1) // A_GROUP_DIM
        for c in range(tm // CHUNK):
            vc = vab[c * CHUNK:(c + 1) * CHUNK]
            rhs = jnp.concatenate(
                [jnp.where(group == g, vc, jnp.zeros_like(vc)) for g in range(A_GROUPS)],
                axis=0)
            mixed = jnp.dot(wcat_ref[...], rhs, preferred_element_type=F32) + bias_ref[...]
            ya_ref[c * CHUNK:(c + 1) * CHUNK, :] = (
                u[c * CHUNK:(c + 1) * CHUNK] * mixed).astype(BF16)

    zbm = zmain[:, 2 * A_WIDTH:]
    if sample:
        zpm, zpl = spm_ref[...], spl_ref[...]
        zm_ref[...] = zbm
        zl_ref[...] = zl
    else:
        i = pl.program_id(0)

        @pl.when(i % tiles_per_seq == 0)
        def _():
            cm_ref[...] = jnp.zeros_like(cm_ref)
            cl_ref[...] = jnp.zeros_like(cl_ref)

        first_m = lax.broadcasted_iota(jnp.int32, zbm.shape, 0) == 0
        first_l = lax.broadcasted_iota(jnp.int32, zl.shape, 0) == 0
        zpm = jnp.where(first_m, cm_ref[0:1, :], pltpu.roll(zbm, 1, axis=0))
        zpl = jnp.where(first_l, cl_ref[0:1, :], pltpu.roll(zl, 1, axis=0))
        cm_ref[0:1, :] = zbm[tm - 1:tm, :]
        cl_ref[0:1, :] = zl[tm - 1:tm, :]
        zlast_ref[:, :RKV_W] = jnp.broadcast_to(zbm[tm - 1:tm, :], (8, RKV_W))
        zlast_ref[:, RKV_W:] = jnp.broadcast_to(zl[tm - 1:tm, :], (8, LORA_W))
    zsm = zbm + (zpm - zbm) * mum_ref[...]
    zsl = zl + (zpl - zl) * mul_ref[...]
    r = zsm[:, :B_WIDTH]
    k = zsm[:, B_WIDTH:2 * B_WIDTH]
    v = zsm[:, 2 * B_WIDTH:]
    wd = zsl[:, LORA_WD:LORA_AD]
    ad = zsl[:, LORA_AD:LORA_GD]
    gd = zsl[:, LORA_GD:]
    w_log = -jax.nn.softplus(-(w0_ref[...] + _dot(jnp.tanh(wd), w2_ref[...]))) - 0.5
    log_decay = -jnp.exp(w_log)
    a = jax.nn.sigmoid(a0_ref[...] + _dot(ad, a2_ref[...]))
    gate = _dot(jax.nn.sigmoid(gd), g2_ref[...])
    ones_bd = ones_ref[...]
    kk = k * kk_ref[...]
    kk = kk * lax.rsqrt(jnp.maximum(_seg_sum(kk * kk, ones_bd), 1e-24))
    k2 = k * (1.0 + (a - 1.0) * ka_ref[...])
    r_ref[...] = r
    w_ref[...] = jnp.exp(log_decay) if sample else log_decay
    k_ref[...] = k2
    v_ref[...] = v
    kn_ref[...] = kk
    b_ref[...] = kk * a
    g_ref[...] = gate
    bonus_ref[...] = _seg_sum(r * k2 * rk_ref[...], ones_bd) * v


def _mix_in(x, p, *, tm, sample, shift_main=None, shift_lora=None):
    rows = x.shape[0]
    n_tiles = rows // tm
    args = [x, p["ln_mix"], p["w_main"], p["w_lora"]]
    specs = [_rows_spec(tm, D_MODEL), _const_spec((1, D_MODEL)),
             _const_spec((D_MODEL, MAIN_W)), _const_spec((D_MODEL, LORA_W))]
    if sample:
        args += [p["sgu_w00"], p["sgu_b0"], shift_main, shift_lora]
        specs += [_const_spec((1, A_WIDTH)), _const_spec((1, A_WIDTH)),
                  _rows_spec(tm, RKV_W), _rows_spec(tm, LORA_W)]
    else:
        args += [p["sgu_wcat"], p["sgu_bias"]]
        specs += [_const_spec((CHUNK, A_GROUPS * CHUNK)), _const_spec((CHUNK, A_WIDTH))]
    args += [p["sgu_ln_g"], p["sgu_ln_b"], p["mu_main"], p["mu_lora"], p["w0"], p["w2"],
             p["a0"], p["a2"], p["g2"], p["k_k"], p["k_a"], p["r_k"], p["ones_bd"]]
    specs += [_const_spec((1, A_WIDTH)), _const_spec((1, A_WIDTH)), _const_spec((1, RKV_W)),
              _const_spec((1, LORA_W)), _const_spec((1, B_WIDTH)),
              _const_spec((LORA_AD - LORA_WD, B_WIDTH)), _const_spec((1, B_WIDTH)),
              _const_spec((LORA_GD - LORA_AD, B_WIDTH)), _const_spec((LORA_W - LORA_GD, B_WIDTH)),
              _const_spec((1, B_WIDTH)), _const_spec((1, B_WIDTH)), _const_spec((1, B_WIDTH)),
              _const_spec((B_WIDTH, B_WIDTH))]
    wide = jax.ShapeDtypeStruct((rows, B_WIDTH), F32)
    out_shape = [jax.ShapeDtypeStruct((rows, A_WIDTH), BF16)] + [wide] * 8
    out_specs = [_rows_spec(tm, B_WIDTH)] * 9
    scratch = []
    if sample:
        out_shape += [wide, jax.ShapeDtypeStruct((rows, RKV_W), F32),
                      jax.ShapeDtypeStruct((rows, LORA_W), F32)]
        out_specs += [_rows_spec(tm, A_WIDTH), _rows_spec(tm, RKV_W), _rows_spec(tm, LORA_W)]
    else:
        out_shape += [jax.ShapeDtypeStruct((n_tiles * 8, RKV_W + LORA_W), F32)]
        out_specs += [pl.BlockSpec((8, RKV_W + LORA_W), lambda i: (i, 0))]
        scratch = [pltpu.VMEM((8, RKV_W), F32), pltpu.VMEM((8, LORA_W), F32)]
    return pl.pallas_call(
        functools.partial(_mix_kernel, sample=sample, tiles_per_seq=max(SEQ // tm, 1)),
        grid=(n_tiles,),
        in_specs=specs,
        out_specs=out_specs,
        out_shape=out_shape,
        scratch_shapes=scratch,
        compiler_params=_params(),
        name="mix_in",
    )(*args)


def _pair_halves(x, left):
    zero = jnp.zeros_like(x)
    s0 = jnp.sum(jnp.where(left, x, zero), axis=-1, keepdims=True)
    s1 = jnp.sum(jnp.where(left, zero, x), axis=-1, keepdims=True)
    return jnp.where(left, s0, s1)


def _scan_step(s, r, w, k, v, kk, b, eye2, left):
    sa = -_pair_halves(s * kk, left)
    vcol = _pair_halves(eye2 * v, left)
    s = s * w + sa * b + vcol * k
    ocol = _pair_halves(s * r, left)
    return s, jnp.sum(eye2 * ocol, axis=0, keepdims=True)


def _pair_consts():
    lane = lax.broadcasted_iota(jnp.int32, (HEAD, PAIR_W), 1)
    row = lax.broadcasted_iota(jnp.int32, (HEAD, PAIR_W), 0)
    left = lane < HEAD
    eye2 = jnp.where((lane % HEAD) == row, 1.0, 0.0).astype(F32)
    return eye2, left


SCAN_C = 64


def _mm(a, b):
    return jnp.dot(a.astype(BF16), b.astype(BF16), preferred_element_type=F32)


def _mm_nt(a, b):
    return lax.dot_general(a.astype(BF16), b.astype(BF16), (((1,), (1,)), ((), ())),
                           preferred_element_type=F32)


def _mm_tn(a, b):
    return lax.dot_general(a.astype(BF16), b.astype(BF16), (((0,), (0,)), ((), ())),
                           preferred_element_type=F32)


def _cumsum_rows(x):
    n = x.shape[0]
    row = lax.broadcasted_iota(jnp.int32, x.shape, 0)
    s = 1
    while s < n:
        x = x + jnp.where(row >= s, pltpu.roll(x, s, axis=0), 0.0)
        s *= 2
    return x


INV_BASE = 8


def _each(f, *lists):
    return [f(*xs) for xs in zip(*lists)]


def _unit_lower_inverse(ns, row, col):
    f0 = jnp.zeros((), F32)
    same = lambda s: (row // s) == (col // s)
    eye = jnp.where(row == col, 1.0, f0)
    ps = _each(lambda n: jnp.where(same(INV_BASE), n, f0), ns)
    ts = _each(lambda p: eye + p, ps)
    s = 2
    while s < INV_BASE:
        ps = _each(lambda p: _mm(p, p), ps)
        ts = _each(lambda t, p: t + _mm(t, p), ts, ps)
        s *= 2
    s = INV_BASE
    while s < SCAN_C:
        level = same(2 * s) & jnp.logical_not(same(s))
        ws = _each(lambda n, t: _mm(jnp.where(level, n, f0), t), ns, ts)
        ts = _each(lambda t, w: t + _mm(t, w), ts, ws)
        s *= 2
    return ts


def _chunk_pairs(s0s, rs, lws, ks, vs, kks, bs):
    c = SCAN_C
    f0 = jnp.zeros((), F32)
    row = lax.broadcasted_iota(jnp.int32, (2 * c, PAIR_W), 0)
    col = lax.broadcasted_iota(jnp.int32, (2 * c, PAIR_W), 1)
    top, lft = row < c, col < HEAD
    same_head = top == lft
    strict = (row % c) > (col % HEAD)
    row_c = lax.broadcasted_iota(jnp.int32, (c, PAIR_W), 0)
    col_c = lax.broadcasted_iota(jnp.int32, (c, PAIR_W), 1)
    lft_c = col_c < HEAD
    strict_c = row_c > (col_c % HEAD)
    incl_c = row_c >= (col_c % HEAD)

    def prep(r, lw, k, v, kk, b):
        cum = _cumsum_rows(lw)
        end = cum[c - 1:c, :]
        a_t = -kk * jnp.exp(cum - lw)
        r_t = r * jnp.exp(cum)
        einv = jnp.exp(-cum)
        eend = jnp.exp(end - cum)
        return dict(
            x0=jnp.concatenate([a_t, r_t], axis=0), x1=jnp.concatenate([r_t, a_t], axis=0),
            bk=jnp.concatenate([b * einv, k * einv], axis=0),
            kb=jnp.concatenate([k * einv, b * einv], axis=0),
            bk_e=jnp.concatenate([b * eend, k * eend], axis=0),
            w_end=jnp.exp(end), v=v,
            v_l=jnp.where(lft_c, v, f0), v_r=jnp.where(lft_c, f0, v))

    fs = _each(prep, rs, lws, ks, vs, kks, bs)
    g0s = _each(lambda f: _mm_nt(jnp.where(lft, f["x0"], f0), f["bk"]), fs)
    g1s = _each(lambda f: _mm_nt(jnp.where(lft, f0, f["x1"]), f["kb"]), fs)
    pqs = _each(lambda f, s0: _mm_nt(f["x0"], s0), fs, s0s)

    def rhs(f, g0, g1, pq):
        ak = jnp.where(strict_c, jnp.where(lft_c, g1[c:], g0[:c]), f0)
        x = pq[:c] + _mm(ak, jnp.concatenate([f["v_r"], f["v_l"]], axis=0))
        return jnp.concatenate([jnp.where(lft_c, x, f0), jnp.where(lft_c, f0, x)], axis=0)

    ys = _each(rhs, fs, g0s, g1s, pqs)
    ns = _each(lambda g0, g1: jnp.where(strict & same_head, jnp.where(top, g0, g1), f0),
               g0s, g1s)
    ts = _unit_lower_inverse(ns, row, col)
    ys = _each(_mm, ts, ys)

    def out(f, g0, g1, pq, y):
        lhs = jnp.concatenate([jnp.where(incl_c, g0[c:], f0), jnp.where(incl_c, g1[:c], f0)],
                              axis=1)
        return pq[c:] + _mm(lhs, jnp.concatenate([y[:c], f["v_l"], f["v_r"], y[c:]], axis=0))

    def state(f, s0, y):
        upd = _mm_tn(jnp.concatenate([y[:c] + y[c:], f["v"]], axis=0), f["bk_e"])
        return s0 * f["w_end"] + jnp.where(same_head, upd, f0)

    return _each(out, fs, g0s, g1s, pqs, ys), _each(state, fs, s0s, ys)


SCAN_BATCHES = 2


def _scan_prompt_kernel(r_ref, w_ref, k_ref, v_ref, kk_ref, b_ref, o_ref, sout_ref, s_ref):
    t_blk = pl.program_id(1)

    @pl.when(t_blk == 0)
    def _():
        s_ref[...] = jnp.zeros_like(s_ref)

    chains = [(j, p) for j in range(SCAN_BATCHES) for p in range(PAIRS)]
    lanes = lambda p: slice(p * PAIR_W, (p + 1) * PAIR_W)
    take = lambda ref: [ref[j, :, lanes(p)] for j, p in chains]
    os_, ss = _chunk_pairs([s_ref[j, p] for j, p in chains], take(r_ref), take(w_ref),
                           take(k_ref), take(v_ref), take(kk_ref), take(b_ref))
    for (j, p), o, s_new in zip(chains, os_, ss):
        o_ref[j, :, lanes(p)] = o
        s_ref[j, p] = s_new

    @pl.when(t_blk == pl.num_programs(1) - 1)
    def _():
        sout_ref[...] = s_ref[...]


def _scan_prompt(r, lw, k, v, kk, b, *, batch, seq):
    n_t = seq // SCAN_C
    nb = SCAN_BATCHES
    spec = pl.BlockSpec((nb, SCAN_C, B_WIDTH), lambda bi, ti: (bi, ti, 0))
    sspec = pl.BlockSpec((nb, PAIRS, PAIR_W, PAIR_W), lambda bi, ti: (bi, 0, 0, 0))
    o, s = pl.pallas_call(
        _scan_prompt_kernel,
        grid=(batch // nb, n_t),
        in_specs=[spec] * 6,
        out_specs=[spec, sspec],
        out_shape=[jax.ShapeDtypeStruct((batch, seq, B_WIDTH), F32),
                   jax.ShapeDtypeStruct((batch, PAIRS, PAIR_W, PAIR_W), F32)],
        scratch_shapes=[pltpu.VMEM((nb, PAIRS, PAIR_W, PAIR_W), F32)],
        compiler_params=_params(2),
        name="scan_prompt",
    )(*[x.reshape(batch, seq, B_WIDTH) for x in (r, lw, k, v, kk, b)])
    return o.reshape(batch * seq, B_WIDTH), s


def _scan_sample_kernel(s_ref, r_ref, w_ref, k_ref, v_ref, kk_ref, b_ref, o_ref, sout_ref):
    eye2, left = _pair_consts()
    for j in range(s_ref.shape[0]):
        for p in range(PAIRS):
            sl = slice(p * PAIR_W, (p + 1) * PAIR_W)
            row = lambda ref: ref[j:j + 1, sl]
            s, o = _scan_step(s_ref[j, p], row(r_ref), row(w_ref), row(k_ref), row(v_ref),
                              row(kk_ref), row(b_ref), eye2, left)
            sout_ref[j, p] = s
            o_ref[j:j + 1, sl] = o


def _scan_sample(state, r, w, k, v, kk, b, *, bb):
    rows = r.shape[0]
    sspec = pl.BlockSpec((bb, PAIRS, HEAD, PAIR_W), lambda i: (i, 0, 0, 0))
    spec = _rows_spec(bb, B_WIDTH)
    return pl.pallas_call(
        _scan_sample_kernel,
        grid=(rows // bb,),
        in_specs=[sspec] + [spec] * 6,
        out_specs=[spec, sspec],
        out_shape=[jax.ShapeDtypeStruct((rows, B_WIDTH), F32),
                   jax.ShapeDtypeStruct(state.shape, F32)],
        compiler_params=_params(),
        name="scan_sample",
    )(state, r, w, k, v, kk, b)


def _softmax_rows(s):
    e = jnp.exp(s - jnp.max(s, axis=-1, keepdims=True))
    return e / jnp.sum(e, axis=-1, keepdims=True)


def _post_kernel(*refs, attend):
    it = iter(refs)
    (x_ref, ya_ref, o_ref, g_ref, bonus_ref, gng_ref, gnb_ref, ones_ref, woa_ref, wob_ref,
     lnx_ref, wq_ref) = [next(it) for _ in range(12)]
    if attend:
        mk_ref, mv_ref = next(it), next(it)
    x2_ref, out_ref = next(it), next(it)

    ones_bd = ones_ref[...]
    o = o_ref[...]
    mu = _seg_sum(o, ones_bd) * (1.0 / HEAD)
    d = o - mu
    var = _seg_sum(d * d, ones_bd) * (1.0 / HEAD)
    on = d * lax.rsqrt(var + GN_EPS) * gng_ref[...] + gnb_ref[...]
    yb = (on + bonus_ref[...]) * g_ref[...]
    x2 = x_ref[...] + jnp.dot(ya_ref[...], woa_ref[...], preferred_element_type=F32) \
        + _dot(yb, wob_ref[...])
    x2_ref[...] = x2
    q = _dot(_rms(x2, lnx_ref[...]), wq_ref[...])
    if not attend:
        out_ref[...] = q
        return
    qb = q.astype(BF16)
    for h in range(XA_HEADS):
        sl = slice(h * XA_DIM, (h + 1) * XA_DIM)
        s = lax.dot_general(qb[:, sl], mk_ref[0, :, sl], (((1,), (1,)), ((), ())),
                            preferred_element_type=F32) * (XA_DIM ** -0.5)
        p = _softmax_rows(s)
        out_ref[:, sl] = _dot(p, mv_ref[0, :, sl]).astype(BF16)


def _post_mix(x, ya, o, g, bonus, p, *, tm, mk=None, mv=None):
    rows = x.shape[0]
    attend = mk is not None
    args = [x, ya, o, g, bonus, p["gn_g"], p["gn_b"], p["ones_bd"], p["w_out_a"], p["w_out_b"],
            p["ln_xattn"], p["xa_q"]]
    specs = [_rows_spec(tm, D_MODEL)] + [_rows_spec(tm, B_WIDTH)] * 4 + [
        _const_spec((1, B_WIDTH)), _const_spec((1, B_WIDTH)), _const_spec((B_WIDTH, B_WIDTH)),
        _const_spec((A_WIDTH, D_MODEL)), _const_spec((B_WIDTH, D_MODEL)),
        _const_spec((1, D_MODEL)), _const_spec((D_MODEL, D_MODEL))]
    if attend:
        tiles_per_seq = SEQ // tm
        mspec = pl.BlockSpec((1, N_MEM, D_MODEL), lambda i: (i // tiles_per_seq, 0, 0))
        args += [mk, mv]
        specs += [mspec, mspec]
    return pl.pallas_call(
        functools.partial(_post_kernel, attend=attend),
        grid=(rows // tm,),
        in_specs=specs,
        out_specs=[_rows_spec(tm, D_MODEL)] * 2,
        out_shape=[jax.ShapeDtypeStruct((rows, D_MODEL), F32),
                   jax.ShapeDtypeStruct((rows, D_MODEL), BF16 if attend else F32)],
        compiler_params=_params(),
        name="post_mix",
    )(*args)


def _xa_sample_kernel(q_ref, k_ref, v_ref, o_ref):
    for j in range(q_ref.shape[0]):
        for h in range(XA_HEADS):
            sl = slice(h * XA_DIM, (h + 1) * XA_DIM)
            prod = k_ref[j, :, h, :] * q_ref[j:j + 1, sl]
            s = jnp.sum(prod, axis=-1, keepdims=True) * (XA_DIM ** -0.5)
            e = jnp.exp(s - jnp.max(s, axis=0, keepdims=True))
            p = e / jnp.sum(e, axis=0, keepdims=True)
            o_ref[j:j + 1, sl] = jnp.sum(p * v_ref[j, :, h, :], axis=0, keepdims=True)


def _xa_sample(q, mk, mv, *, bb):
    rows = q.shape[0]
    mspec = pl.BlockSpec((bb, N_MEM, XA_HEADS, XA_DIM), lambda i: (i, 0, 0, 0))
    return pl.pallas_call(
        _xa_sample_kernel,
        grid=(rows // bb,),
        in_specs=[_rows_spec(bb, D_MODEL), mspec, mspec],
        out_specs=_rows_spec(bb, D_MODEL),
        out_shape=jax.ShapeDtypeStruct((rows, D_MODEL), F32),
        compiler_params=_params(),
        name="xa_sample",
    )(q, mk, mv)


def _pad_lora(x):
    wd = x[..., :DECAY_LORA]
    ad = x[..., DECAY_LORA:DECAY_LORA + AAA_LORA]
    gd = x[..., DECAY_LORA + AAA_LORA:]
    z = lambda n: jnp.zeros(x.shape[:-1] + (n,), x.dtype)
    return jnp.concatenate([wd, z(LORA_AD - DECAY_LORA), ad, z(LORA_GD - LORA_AD - AAA_LORA),
                            gd, z(LORA_W - LORA_GD - GATE_LORA)], axis=-1)


def _unpad_shift(zm, zl):
    return jnp.concatenate([zm, zl[..., LORA_WD:LORA_WD + DECAY_LORA],
                            zl[..., LORA_AD:LORA_AD + AAA_LORA],
                            zl[..., LORA_GD:LORA_GD + GATE_LORA]], axis=-1)


def _pad_rows(w, n):
    return jnp.pad(w, ((0, n - w.shape[0]), (0, 0)))


def _to_pairs(s):
    b = s.shape[0]
    return s.reshape(b, PAIRS, 2, HEAD, HEAD).transpose(0, 1, 3, 2, 4).reshape(
        b, PAIRS, HEAD, PAIR_W)


def _from_pairs(s):
    b = s.shape[0]
    return s.reshape(b, PAIRS, HEAD, 2, HEAD).transpose(0, 1, 3, 2, 4).reshape(
        b, HEADS, HEAD, HEAD)


def kernel(x_prompt, x_sample, state_rwkv, state_shift, cache_mem_k, cache_mem_v, mem_prompt, ln_ffn1, ffn1_gate, ffn1_up, ffn1_down, ln_mix, w_in, w_out, sgu_w, sgu_b, sgu_ln_g, sgu_ln_b, rwkv_mu, rwkv_w0, rwkv_w2, rwkv_a0, rwkv_a2, rwkv_g2, rwkv_k_k, rwkv_k_a, rwkv_r_k, rwkv_gn_g, rwkv_gn_b, ln_xattn, mem_norm, xa_q, xa_k, xa_v, xa_o, ln_ffn2, ffn2_gate, ffn2_up, ffn2_down, final_norm):
    assert ln_ffn1.shape[0] == 1, "single layer"
    bp, seq, _ = x_prompt.shape
    bs = x_sample.shape[0]
    row = lambda a: a.reshape(1, -1).astype(F32)
    bf = lambda a: a.astype(BF16)
    l = 0
    head_id = jnp.arange(B_WIDTH) // HEAD
    tril = jnp.tril(jnp.ones((CHUNK, CHUNK), dtype=bool))
    wmask = jnp.where(tril[None], sgu_w[l], 0)
    p = {
        "ln_mix": row(ln_mix[l]),
        "w_main": bf(w_in[l][:, :MAIN_W]),
        "w_lora": bf(_pad_lora(w_in[l][:, MAIN_W:])),
        "sgu_wcat": bf(wmask.transpose(1, 0, 2).reshape(CHUNK, A_GROUPS * CHUNK)),
        "sgu_bias": jnp.repeat(sgu_b[l].T, A_GROUP_DIM, axis=1),
        "sgu_w00": row(jnp.repeat(sgu_w[l][:, 0, 0], A_GROUP_DIM)),
        "sgu_b0": row(jnp.repeat(sgu_b[l][:, 0], A_GROUP_DIM)),
        "sgu_ln_g": row(sgu_ln_g[l]), "sgu_ln_b": row(sgu_ln_b[l]),
        "mu_main": row(rwkv_mu[l][:RKV_W]),
        "mu_lora": row(_pad_lora(rwkv_mu[l][RKV_W:])),
        "w0": row(rwkv_w0[l]), "w2": bf(_pad_rows(rwkv_w2[l], LORA_AD - LORA_WD)),
        "a0": row(rwkv_a0[l]), "a2": bf(_pad_rows(rwkv_a2[l], LORA_GD - LORA_AD)),
        "g2": bf(_pad_rows(rwkv_g2[l], LORA_W - LORA_GD)),
        "k_k": row(rwkv_k_k[l]), "k_a": row(rwkv_k_a[l]), "r_k": row(rwkv_r_k[l]),
        "ones_bd": (head_id[:, None] == head_id[None, :]).astype(BF16),
        "gn_g": row(rwkv_gn_g[l]), "gn_b": row(rwkv_gn_b[l]),
        "w_out_a": bf(w_out[l][:A_WIDTH]), "w_out_b": bf(w_out[l][A_WIDTH:]),
        "ln_xattn": row(ln_xattn[l]), "xa_q": bf(xa_q[l]),
    }
    ffn1 = (row(ln_ffn1[l]), bf(ffn1_gate[l]), bf(ffn1_up[l]), bf(ffn1_down[l]))
    ffn2 = (row(ln_ffn2[l]), bf(ffn2_gate[l]), bf(ffn2_up[l]), bf(ffn2_down[l]))
    xa_o_b = bf(xa_o[l])
    fnorm = row(final_norm)

    tm = 512
    xp = x_prompt.reshape(bp * seq, D_MODEL)
    mk, mv, mkb, mvb = _memkv(mem_prompt.reshape(bp * N_MEM, D_MODEL), row(mem_norm[l]),
                              bf(xa_k[l]), bf(xa_v[l]), tm=tm)
    x1 = _ffn(xp, *ffn1, tm=tm)
    ya, r, w, k, v, kk, b, g, bonus, zlast = _mix_in(x1, p, tm=tm, sample=False)
    o, s_bd = _scan_prompt(r, w, k, v, kk, b, batch=bp, seq=seq)
    state_p = jnp.stack([s_bd[:, :, :HEAD, :HEAD], s_bd[:, :, HEAD:, HEAD:]],
                        axis=2).reshape(bp, HEADS, HEAD, HEAD)
    x2, attn = _post_mix(x1, ya, o, g, bonus, p, tm=tm,
                         mk=mkb.reshape(bp, N_MEM, D_MODEL), mv=mvb.reshape(bp, N_MEM, D_MODEL))
    y_prompt = _ffn(x2, *ffn2, tm=tm, attn=attn, wo=xa_o_b, final_norm=fnorm)
    tiles_per_seq = seq // tm
    zl_rows = zlast.reshape(bp, tiles_per_seq, 8, RKV_W + LORA_W)[:, -1, 0]
    shift_p = _unpad_shift(zl_rows[:, :RKV_W], zl_rows[:, RKV_W:])

    xs = x_sample.reshape(bs, D_MODEL)
    sh = state_shift[l].reshape(bs, B_PROJ)
    x1s = _ffn(xs, *ffn1, tm=bs)
    (ya_s, r_s, w_s, k_s, v_s, kk_s, b_s, g_s, bonus_s, va_s, zm_s, zl_s) = _mix_in(
        x1s, p, tm=bs, sample=True, shift_main=sh[:, :RKV_W], shift_lora=_pad_lora(sh[:, RKV_W:]))
    o_s, s_pairs_s = _scan_sample(_to_pairs(state_rwkv[l]), r_s, w_s, k_s, v_s, kk_s, b_s, bb=8)
    x2s, q_s = _post_mix(x1s, ya_s, o_s, g_s, bonus_s, p, tm=bs)
    attn_s = _xa_sample(q_s, cache_mem_k[l], cache_mem_v[l], bb=8)
    y_sample = _ffn(x2s, *ffn2, tm=bs, attn=attn_s, wo=xa_o_b, final_norm=fnorm)

    return (y_prompt.reshape(bp, seq, D_MODEL),
            y_sample.reshape(bs, 1, D_MODEL),
            state_p[None],
            shift_p.reshape(1, bp, 1, B_PROJ),
            mk.reshape(1, bp, N_MEM, XA_HEADS, XA_DIM),
            mv.reshape(1, bp, N_MEM, XA_HEADS, XA_DIM),
            _from_pairs(s_pairs_s)[None],
            _unpad_shift(zm_s, zl_s).reshape(1, bs, 1, B_PROJ),
            va_s.reshape(1, bs, 1, A_WIDTH))
```

```python
import functools

import jax
import jax.numpy as jnp
from jax import lax
from jax.experimental import pallas as pl
from jax.experimental.pallas import tpu as pltpu

F32 = jnp.float32
BF16 = jnp.bfloat16

D_MODEL = 1024
SEQ = 2048
A_WIDTH = 512
A_GROUPS = 8
A_GROUP_DIM = 64
CHUNK = 128
B_WIDTH = 512
HEAD = 64
HEADS = 8
PAIRS = HEADS // 2
PAIR_W = 2 * HEAD
DECAY_LORA = 64
AAA_LORA = 64
GATE_LORA = 160
B_PROJ = 3 * B_WIDTH + DECAY_LORA + AAA_LORA + GATE_LORA
MAIN_W = 2 * A_WIDTH + 3 * B_WIDTH
RKV_W = 3 * B_WIDTH
LORA_W = 512
LORA_WD, LORA_AD, LORA_GD = 0, 128, 256
D_FF = 2816
N_MEM = 256
XA_HEADS = 4
XA_DIM = 256
NORM_EPS = 1e-6
LN_EPS = 1e-5
GN_EPS = 64e-5

VMEM_LIMIT = 56 * 1024 * 1024


def _params(n_axes=1):
    return pltpu.CompilerParams(dimension_semantics=("arbitrary",) * n_axes,
                                vmem_limit_bytes=VMEM_LIMIT)


def _const_spec(shape):
    nd = len(shape)
    return pl.BlockSpec(shape, lambda *_: (0,) * nd, pipeline_mode=pl.Buffered(1))


def _rows_spec(tm, width):
    return pl.BlockSpec((tm, width), lambda i: (i, 0))


def _rms(x, g):
    return x * lax.rsqrt(jnp.mean(x * x, axis=-1, keepdims=True) + NORM_EPS) * g


def _dot(a, b):
    return jnp.dot(a.astype(BF16), b, preferred_element_type=F32)


def _seg_sum(x, ones_bd):
    return jnp.dot(x.astype(BF16), ones_bd, preferred_element_type=F32)


def _ffn_kernel(*refs, pre, final):
    it = iter(refs)
    x_ref = next(it)
    if pre:
        attn_ref, wo_ref = next(it), next(it)
    ln_ref, wg_ref, wu_ref, wd_ref = next(it), next(it), next(it), next(it)
    if final:
        fn_ref = next(it)
    o_ref = next(it)
    x = x_ref[...]
    if pre:
        x = x + _dot(attn_ref[...], wo_ref[...])
    xb = _rms(x, ln_ref[...]).astype(BF16)
    g = jnp.dot(xb, wg_ref[...], preferred_element_type=F32)
    u = jnp.dot(xb, wu_ref[...], preferred_element_type=F32)
    h = (g * jax.nn.sigmoid(g) * u).astype(BF16)
    x = x + 0.5 * jnp.dot(h, wd_ref[...], preferred_element_type=F32)
    if final:
        x = _rms(x, fn_ref[...])
    o_ref[...] = x


def _ffn(x, ln, wg, wu, wd, *, tm, attn=None, wo=None, final_norm=None):
    rows = x.shape[0]
    pre = attn is not None
    final = final_norm is not None
    args, specs = [x], [_rows_spec(tm, D_MODEL)]
    if pre:
        args += [attn, wo]
        specs += [_rows_spec(tm, D_MODEL), _const_spec((D_MODEL, D_MODEL))]
    args += [ln, wg, wu, wd]
    specs += [_const_spec((1, D_MODEL)), _const_spec((D_MODEL, D_FF)),
              _const_spec((D_MODEL, D_FF)), _const_spec((D_FF, D_MODEL))]
    if final:
        args.append(final_norm)
        specs.append(_const_spec((1, D_MODEL)))
    return pl.pallas_call(
        functools.partial(_ffn_kernel, pre=pre, final=final),
        grid=(rows // tm,),
        in_specs=specs,
        out_specs=_rows_spec(tm, D_MODEL),
        out_shape=jax.ShapeDtypeStruct((rows, D_MODEL), F32),
        compiler_params=_params(),
        name="ffn",
    )(*args)


def _memkv_kernel(m_ref, g_ref, wk_ref, wv_ref, k_ref, v_ref, kb_ref, vb_ref):
    mb = _rms(m_ref[...], g_ref[...]).astype(BF16)
    k = jnp.dot(mb, wk_ref[...], preferred_element_type=F32)
    v = jnp.dot(mb, wv_ref[...], preferred_element_type=F32)
    k_ref[...] = k
    v_ref[...] = v
    kb_ref[...] = k.astype(BF16)
    vb_ref[...] = v.astype(BF16)


def _memkv(mem, g, wk, wv, *, tm):
    rows = mem.shape[0]
    out = jax.ShapeDtypeStruct((rows, D_MODEL), F32)
    outb = jax.ShapeDtypeStruct((rows, D_MODEL), BF16)
    return pl.pallas_call(
        _memkv_kernel,
        grid=(rows // tm,),
        in_specs=[_rows_spec(tm, D_MODEL), _const_spec((1, D_MODEL)),
                  _const_spec((D_MODEL, D_MODEL)), _const_spec((D_MODEL, D_MODEL))],
        out_specs=[_rows_spec(tm, D_MODEL)] * 4,
        out_shape=[out, out, outb, outb],
        compiler_params=_params(),
        name="memkv",
    )(mem, g, wk, wv)


def _mix_kernel(*refs, sample, tiles_per_seq):
    it = iter(refs)
    x_ref, ln_ref, wmain_ref, wlora_ref = next(it), next(it), next(it), next(it)
    if sample:
        w00_ref, b0_ref, spm_ref, spl_ref = next(it), next(it), next(it), next(it)
    else:
        wcat_ref, bias_ref = next(it), next(it)
    (lng_ref, lnb_ref, mum_ref, mul_ref, w0_ref, w2_ref, a0_ref, a2_ref, g2_ref,
     kk_ref, ka_ref, rk_ref, ones_ref) = [next(it) for _ in range(13)]
    (ya_ref, r_ref, w_ref, k_ref, v_ref, kn_ref, b_ref, g_ref, bonus_ref) = [
        next(it) for _ in range(9)]
    if sample:
        va_ref, zm_ref, zl_ref = next(it), next(it), next(it)
    else:
        zlast_ref, cm_ref, cl_ref = next(it), next(it), next(it)

    tm = x_ref.shape[0]
    xb = _rms(x_ref[...], ln_ref[...]).astype(BF16)
    zmain = jnp.dot(xb, wmain_ref[...], preferred_element_type=F32)
    zl = jnp.dot(xb, wlora_ref[...], preferred_element_type=F32)

    za = jax.nn.gelu(zmain[:, :2 * A_WIDTH])
    u = za[:, :A_WIDTH]
    vx = za[:, A_WIDTH:]
    mu = jnp.mean(vx, axis=-1, keepdims=True)
    var = jnp.mean(jnp.square(vx - mu), axis=-1, keepdims=True)
    va = (vx - mu) * lax.rsqrt(var + LN_EPS) * lng_ref[...] + lnb_ref[...]
    if sample:
        mixed = va * w00_ref[...] + b0_ref[...]
        ya_ref[...] = (u * mixed).astype(BF16)
        va_ref[...] = va
    else:
        vab = va.astype(BF16)
        first = lax.broadcasted_iota(jnp.int32, (CHUNK, 2 * A_GROUP_DIM), 1) < A_GROUP_DIM
        for c in range(tm // CHUNK):
            rows = slice(c * CHUNK, (c + 1) * CHUNK)
            for gp in range(A_GROUPS // 2):
                lanes = slice(gp * 2 * A_GROUP_DIM, (gp + 1) * 2 * A_GROUP_DIM)
                vc = vab[rows, lanes]
                zero = jnp.zeros_like(vc)
                rhs = jnp.concatenate([jnp.where(first, vc, zero), jnp.where(first, zero, vc)],
                                      axis=0)
                mixed = jnp.dot(wcat_ref[:, gp * 2 * CHUNK:(gp + 1) * 2 * CHUNK], rhs,
                                preferred_element_type=F32) + bias_ref[:, lanes]
                ya_ref[rows, lanes] = (u[rows, lanes] * mixed).astype(BF16)

    zbm = zmain[:, 2 * A_WIDTH:]
    if sample:
        zpm, zpl = spm_ref[...], spl_ref[...]
        zm_ref[...] = zbm
        zl_ref[...] = zl
    else:
        i = pl.program_id(0)

        @pl.when(i % tiles_per_seq == 0)
        def _():
            cm_ref[...] = jnp.zeros_like(cm_ref)
            cl_ref[...] = jnp.zeros_like(cl_ref)

        first_m = lax.broadcasted_iota(jnp.int32, zbm.shape, 0) == 0
        first_l = lax.broadcasted_iota(jnp.int32, zl.shape, 0) == 0
        zpm = jnp.where(first_m, cm_ref[0:1, :], pltpu.roll(zbm, 1, axis=0))
        zpl = jnp.where(first_l, cl_ref[0:1, :], pltpu.roll(zl, 1, axis=0))
        cm_ref[0:1, :] = zbm[tm - 1:tm, :]
        cl_ref[0:1, :] = zl[tm - 1:tm, :]
        zlast_ref[:, :RKV_W] = jnp.broadcast_to(zbm[tm - 1:tm, :], (8, RKV_W))
        zlast_ref[:, RKV_W:] = jnp.broadcast_to(zl[tm - 1:tm, :], (8, LORA_W))
    zsm = zbm + (zpm - zbm) * mum_ref[...]
    zsl = zl + (zpl - zl) * mul_ref[...]
    r = zsm[:, :B_WIDTH]
    k = zsm[:, B_WIDTH:2 * B_WIDTH]
    v = zsm[:, 2 * B_WIDTH:]
    wd = zsl[:, LORA_WD:LORA_AD]
    ad = zsl[:, LORA_AD:LORA_GD]
    gd = zsl[:, LORA_GD:]
    w_log = -jax.nn.softplus(-(w0_ref[...] + _dot(jnp.tanh(wd), w2_ref[...]))) - 0.5
    log_decay = -jnp.exp(w_log)
    a = jax.nn.sigmoid(a0_ref[...] + _dot(ad, a2_ref[...]))
    gate = _dot(jax.nn.sigmoid(gd), g2_ref[...])
    ones_bd = ones_ref[...]
    kk = k * kk_ref[...]
    kk = kk * lax.rsqrt(jnp.maximum(_seg_sum(kk * kk, ones_bd), 1e-24))
    k2 = k * (1.0 + (a - 1.0) * ka_ref[...])
    r_ref[...] = r
    w_ref[...] = jnp.exp(log_decay) if sample else log_decay
    k_ref[...] = k2
    v_ref[...] = v
    kn_ref[...] = kk
    b_ref[...] = kk * a
    g_ref[...] = gate
    bonus_ref[...] = _seg_sum(r * k2 * rk_ref[...], ones_bd) * v


def _mix_in(x, p, *, tm, sample, shift_main=None, shift_lora=None):
    rows = x.shape[0]
    n_tiles = rows // tm
    args = [x, p["ln_mix"], p["w_main"], p["w_lora"]]
    specs = [_rows_spec(tm, D_MODEL), _const_spec((1, D_MODEL)),
             _const_spec((D_MODEL, MAIN_W)), _const_spec((D_MODEL, LORA_W))]
    if sample:
        args += [p["sgu_w00"], p["sgu_b0"], shift_main, shift_lora]
        specs += [_const_spec((1, A_WIDTH)), _const_spec((1, A_WIDTH)),
                  _rows_spec(tm, RKV_W), _rows_spec(tm, LORA_W)]
    else:
        args += [p["sgu_wcat"], p["sgu_bias"]]
        specs += [_const_spec((CHUNK, A_GROUPS * CHUNK)), _const_spec((CHUNK, A_WIDTH))]
    args += [p["sgu_ln_g"], p["sgu_ln_b"], p["mu_main"], p["mu_lora"], p["w0"], p["w2"],
             p["a0"], p["a2"], p["g2"], p["k_k"], p["k_a"], p["r_k"], p["ones_bd"]]
    specs += [_const_spec((1, A_WIDTH)), _const_spec((1, A_WIDTH)), _const_spec((1, RKV_W)),
              _const_spec((1, LORA_W)), _const_spec((1, B_WIDTH)),
              _const_spec((LORA_AD - LORA_WD, B_WIDTH)), _const_spec((1, B_WIDTH)),
              _const_spec((LORA_GD - LORA_AD, B_WIDTH)), _const_spec((LORA_W - LORA_GD, B_WIDTH)),
              _const_spec((1, B_WIDTH)), _const_spec((1, B_WIDTH)), _const_spec((1, B_WIDTH)),
              _const_spec((B_WIDTH, B_WIDTH))]
    wide = jax.ShapeDtypeStruct((rows, B_WIDTH), F32)
    out_shape = [jax.ShapeDtypeStruct((rows, A_WIDTH), BF16)] + [wide] * 8
    out_specs = [_rows_spec(tm, B_WIDTH)] * 9
    scratch = []
    if sample:
        out_shape += [wide, jax.ShapeDtypeStruct((rows, RKV_W), F32),
                      jax.ShapeDtypeStruct((rows, LORA_W), F32)]
        out_specs += [_rows_spec(tm, A_WIDTH), _rows_spec(tm, RKV_W), _rows_spec(tm, LORA_W)]
    else:
        out_shape += [jax.ShapeDtypeStruct((n_tiles * 8, RKV_W + LORA_W), F32)]
        out_specs += [pl.BlockSpec((8, RKV_W + LORA_W), lambda i: (i, 0))]
        scratch = [pltpu.VMEM((8, RKV_W), F32), pltpu.VMEM((8, LORA_W), F32)]
    return pl.pallas_call(
        functools.partial(_mix_kernel, sample=sample, tiles_per_seq=max(SEQ // tm, 1)),
        grid=(n_tiles,),
        in_specs=specs,
        out_specs=out_specs,
        out_shape=out_shape,
        scratch_shapes=scratch,
        compiler_params=_params(),
        name="mix_in",
    )(*args)


def _each(f, *lists):
    return [f(*xs) for xs in zip(*lists)]


def _pair_halves(x, left):
    zero = jnp.zeros_like(x)
    s0 = jnp.sum(jnp.where(left, x, zero), axis=-1, keepdims=True)
    s1 = jnp.sum(jnp.where(left, zero, x), axis=-1, keepdims=True)
    return jnp.where(left, s0, s1)


def _scan_step(s, r, w, k, v, kk, b, eye2, left):
    sa = -_pair_halves(s * kk, left)
    vcol = _pair_halves(eye2 * v, left)
    s = s * w + sa * b + vcol * k
    ocol = _pair_halves(s * r, left)
    return s, jnp.sum(eye2 * ocol, axis=0, keepdims=True)


SCAN_C = 64


def _mm(a, b):
    return jnp.dot(a.astype(BF16), b.astype(BF16), preferred_element_type=F32)


def _mm_nt(a, b):
    return lax.dot_general(a.astype(BF16), b.astype(BF16), (((1,), (1,)), ((), ())),
                           preferred_element_type=F32)


def _mm_tn(a, b):
    return lax.dot_general(a.astype(BF16), b.astype(BF16), (((0,), (0,)), ((), ())),
                           preferred_element_type=F32)


def _cumsum_rows(x):
    n = x.shape[0]
    row = lax.broadcasted_iota(jnp.int32, x.shape, 0)
    s = 1
    while s < n:
        x = x + jnp.where(row >= s, pltpu.roll(x, s, axis=0), 0.0)
        s *= 2
    return x


INV_BASE = 8


def _unit_lower_inverse(ns, row, col):
    f0 = jnp.zeros((), F32)
    same = lambda s: (row // s) == (col // s)
    eye = jnp.where(row == col, 1.0, f0)
    ps = _each(lambda n: jnp.where(same(INV_BASE), n, f0), ns)
    ts = _each(lambda p: eye + p, ps)
    s = 2
    while s < INV_BASE:
        ps = _each(lambda p: _mm(p, p), ps)
        ts = _each(lambda t, p: t + _mm(t, p), ts, ps)
        s *= 2
    s = INV_BASE
    while s < SCAN_C:
        level = same(2 * s) & jnp.logical_not(same(s))
        ws = _each(lambda n, t: _mm(jnp.where(level, n, f0), t), ns, ts)
        ts = _each(lambda t, w: t + _mm(t, w), ts, ws)
        s *= 2
    return ts


def _chunk_pairs(s0s, rs, lws, ks, vs, kks, bs):
    c = SCAN_C
    f0 = jnp.zeros((), F32)
    row = lax.broadcasted_iota(jnp.int32, (2 * c, PAIR_W), 0)
    col = lax.broadcasted_iota(jnp.int32, (2 * c, PAIR_W), 1)
    top, lft = row < c, col < HEAD
    same_head = top == lft
    strict = (row % c) > (col % HEAD)
    row_c = lax.broadcasted_iota(jnp.int32, (c, PAIR_W), 0)
    col_c = lax.broadcasted_iota(jnp.int32, (c, PAIR_W), 1)
    lft_c = col_c < HEAD
    strict_c = row_c > (col_c % HEAD)
    incl_c = row_c >= (col_c % HEAD)

    def prep(r, lw, k, v, kk, b):
        cum = _cumsum_rows(lw)
        end = cum[c - 1:c, :]
        a_t = -kk * jnp.exp(cum - lw)
        r_t = r * jnp.exp(cum)
        einv = jnp.exp(-cum)
        eend = jnp.exp(end - cum)
        return dict(
            x0=jnp.concatenate([a_t, r_t], axis=0), x1=jnp.concatenate([r_t, a_t], axis=0),
            bk=jnp.concatenate([b * einv, k * einv], axis=0),
            kb=jnp.concatenate([k * einv, b * einv], axis=0),
            bk_e=jnp.concatenate([b * eend, k * eend], axis=0),
            w_end=jnp.exp(end), v=v,
            v_l=jnp.where(lft_c, v, f0), v_r=jnp.where(lft_c, f0, v))

    fs = _each(prep, rs, lws, ks, vs, kks, bs)
    g0s = _each(lambda f: _mm_nt(jnp.where(lft, f["x0"], f0), f["bk"]), fs)
    g1s = _each(lambda f: _mm_nt(jnp.where(lft, f0, f["x1"]), f["kb"]), fs)
    pqs = _each(lambda f, s0: _mm_nt(f["x0"], s0), fs, s0s)

    def rhs(f, g0, g1, pq):
        ak = jnp.where(strict_c, jnp.where(lft_c, g1[c:], g0[:c]), f0)
        x = pq[:c] + _mm(ak, jnp.concatenate([f["v_r"], f["v_l"]], axis=0))
        return jnp.concatenate([jnp.where(lft_c, x, f0), jnp.where(lft_c, f0, x)], axis=0)

    ys = _each(rhs, fs, g0s, g1s, pqs)
    ns = _each(lambda g0, g1: jnp.where(strict & same_head, jnp.where(top, g0, g1), f0),
               g0s, g1s)
    ts = _unit_lower_inverse(ns, row, col)
    ys = _each(_mm, ts, ys)

    def out(f, g0, g1, pq, y):
        lhs = jnp.concatenate([jnp.where(incl_c, g0[c:], f0), jnp.where(incl_c, g1[:c], f0)],
                              axis=1)
        return pq[c:] + _mm(lhs, jnp.concatenate([y[:c], f["v_l"], f["v_r"], y[c:]], axis=0))

    def state(f, s0, y):
        upd = _mm_tn(jnp.concatenate([y[:c] + y[c:], f["v"]], axis=0), f["bk_e"])
        return s0 * f["w_end"] + jnp.where(same_head, upd, f0)

    return _each(out, fs, g0s, g1s, pqs, ys), _each(state, fs, s0s, ys)


SCAN_BATCHES = 4


def _scan_prompt_kernel(r_ref, w_ref, k_ref, v_ref, kk_ref, b_ref, o_ref, sout_ref, s_ref):
    t_blk = pl.program_id(1)

    @pl.when(t_blk == 0)
    def _():
        s_ref[...] = jnp.zeros_like(s_ref)

    chains = [(j, p) for j in range(SCAN_BATCHES) for p in range(PAIRS)]
    lanes = lambda p: slice(p * PAIR_W, (p + 1) * PAIR_W)
    take = lambda ref: [ref[j, :, lanes(p)] for j, p in chains]
    os_, ss = _chunk_pairs([s_ref[j, p] for j, p in chains], take(r_ref), take(w_ref),
                           take(k_ref), take(v_ref), take(kk_ref), take(b_ref))
    for (j, p), o, s_new in zip(chains, os_, ss):
        o_ref[j, :, lanes(p)] = o
        s_ref[j, p] = s_new

    @pl.when(t_blk == pl.num_programs(1) - 1)
    def _():
        sout_ref[...] = s_ref[...]


def _scan_prompt(r, lw, k, v, kk, b, *, batch, seq):
    n_t = seq // SCAN_C
    nb = SCAN_BATCHES
    spec = pl.BlockSpec((nb, SCAN_C, B_WIDTH), lambda bi, ti: (bi, ti, 0))
    sspec = pl.BlockSpec((nb, PAIRS, PAIR_W, PAIR_W), lambda bi, ti: (bi, 0, 0, 0))
    o, s = pl.pallas_call(
        _scan_prompt_kernel,
        grid=(batch // nb, n_t),
        in_specs=[spec] * 6,
        out_specs=[spec, sspec],
        out_shape=[jax.ShapeDtypeStruct((batch, seq, B_WIDTH), F32),
                   jax.ShapeDtypeStruct((batch, PAIRS, PAIR_W, PAIR_W), F32)],
        scratch_shapes=[pltpu.VMEM((nb, PAIRS, PAIR_W, PAIR_W), F32)],
        compiler_params=_params(2),
        name="scan_prompt",
    )(*[x.reshape(batch, seq, B_WIDTH) for x in (r, lw, k, v, kk, b)])
    return o.reshape(batch * seq, B_WIDTH), s


def _scan_sample_kernel(s_ref, r_ref, w_ref, k_ref, v_ref, kk_ref, b_ref, o_ref, sout_ref):
    lane = lax.broadcasted_iota(jnp.int32, (HEAD, PAIR_W), 1)
    row = lax.broadcasted_iota(jnp.int32, (HEAD, PAIR_W), 0)
    left = lane < HEAD
    eye2 = jnp.where((lane % HEAD) == row, 1.0, 0.0).astype(F32)
    for j in range(s_ref.shape[0]):
        for p in range(PAIRS):
            sl = slice(p * PAIR_W, (p + 1) * PAIR_W)
            row_of = lambda ref: ref[j:j + 1, sl]
            s = jnp.concatenate([s_ref[j, 2 * p], s_ref[j, 2 * p + 1]], axis=1)
            s, o = _scan_step(s, row_of(r_ref), row_of(w_ref), row_of(k_ref), row_of(v_ref),
                              row_of(kk_ref), row_of(b_ref), eye2, left)
            sout_ref[j, 2 * p] = s[:, :HEAD]
            sout_ref[j, 2 * p + 1] = s[:, HEAD:]
            o_ref[j:j + 1, sl] = o


def _scan_sample(state, r, w, k, v, kk, b, *, bb):
    rows = r.shape[0]
    sspec = pl.BlockSpec((bb, HEADS, HEAD, HEAD), lambda i: (i, 0, 0, 0))
    spec = _rows_spec(bb, B_WIDTH)
    return pl.pallas_call(
        _scan_sample_kernel,
        grid=(rows // bb,),
        in_specs=[sspec] + [spec] * 6,
        out_specs=[spec, sspec],
        out_shape=[jax.ShapeDtypeStruct((rows, B_WIDTH), F32),
                   jax.ShapeDtypeStruct(state.shape, F32)],
        compiler_params=_params(),
        name="scan_sample",
    )(state, r, w, k, v, kk, b)


def _softmax_rows(s):
    e = jnp.exp(s - jnp.max(s, axis=-1, keepdims=True))
    return e / jnp.sum(e, axis=-1, keepdims=True)


def _post_kernel(*refs, attend):
    it = iter(refs)
    (x_ref, ya_ref, o_ref, g_ref, bonus_ref, gng_ref, gnb_ref, ones_ref, woa_ref, wob_ref,
     lnx_ref, wq_ref) = [next(it) for _ in range(12)]
    if attend:
        mk_ref, mv_ref = next(it), next(it)
    x2_ref, out_ref = next(it), next(it)

    ones_bd = ones_ref[...]
    o = o_ref[...]
    mu = _seg_sum(o, ones_bd) * (1.0 / HEAD)
    d = o - mu
    var = _seg_sum(d * d, ones_bd) * (1.0 / HEAD)
    on = d * lax.rsqrt(var + GN_EPS) * gng_ref[...] + gnb_ref[...]
    yb = (on + bonus_ref[...]) * g_ref[...]
    x2 = x_ref[...] + jnp.dot(ya_ref[...], woa_ref[...], preferred_element_type=F32) \
        + _dot(yb, wob_ref[...])
    x2_ref[...] = x2
    q = _dot(_rms(x2, lnx_ref[...]), wq_ref[...])
    if not attend:
        out_ref[...] = q
        return
    qb = q.astype(BF16)
    for h in range(XA_HEADS):
        sl = slice(h * XA_DIM, (h + 1) * XA_DIM)
        s = lax.dot_general(qb[:, sl], mk_ref[0, :, sl], (((1,), (1,)), ((), ())),
                            preferred_element_type=F32) * (XA_DIM ** -0.5)
        p = _softmax_rows(s)
        out_ref[:, sl] = _dot(p, mv_ref[0, :, sl]).astype(BF16)


def _post_mix(x, ya, o, g, bonus, p, *, tm, mk=None, mv=None):
    rows = x.shape[0]
    attend = mk is not None
    args = [x, ya, o, g, bonus, p["gn_g"], p["gn_b"], p["ones_bd"], p["w_out_a"], p["w_out_b"],
            p["ln_xattn"], p["xa_q"]]
    specs = [_rows_spec(tm, D_MODEL)] + [_rows_spec(tm, B_WIDTH)] * 4 + [
        _const_spec((1, B_WIDTH)), _const_spec((1, B_WIDTH)), _const_spec((B_WIDTH, B_WIDTH)),
        _const_spec((A_WIDTH, D_MODEL)), _const_spec((B_WIDTH, D_MODEL)),
        _const_spec((1, D_MODEL)), _const_spec((D_MODEL, D_MODEL))]
    if attend:
        tiles_per_seq = SEQ // tm
        mspec = pl.BlockSpec((1, N_MEM, D_MODEL), lambda i: (i // tiles_per_seq, 0, 0))
        args += [mk, mv]
        specs += [mspec, mspec]
    return pl.pallas_call(
        functools.partial(_post_kernel, attend=attend),
        grid=(rows // tm,),
        in_specs=specs,
        out_specs=[_rows_spec(tm, D_MODEL)] * 2,
        out_shape=[jax.ShapeDtypeStruct((rows, D_MODEL), F32),
                   jax.ShapeDtypeStruct((rows, D_MODEL), BF16 if attend else F32)],
        compiler_params=_params(),
        name="post_mix",
    )(*args)


MEM_ROWS = XA_HEADS * (XA_DIM // 128)


def _lane_allreduce(x, op):
    shift = MEM_ROWS
    while shift < 128:
        x = op(x, pltpu.roll(x, shift, axis=1))
        shift *= 2
    return x


def _xa_sample_kernel(q_ref, k_ref, v_ref, o_ref):
    f0 = jnp.zeros((), F32)
    n_blk = N_MEM * MEM_ROWS // 128
    sub = lax.broadcasted_iota(jnp.int32, (MEM_ROWS, 128), 0)
    lane = lax.broadcasted_iota(jnp.int32, (MEM_ROWS, 128), 1)
    diag = sub == (lane % MEM_ROWS)
    li = lax.broadcasted_iota(jnp.int32, (128, 128), 0)
    lj = lax.broadcasted_iota(jnp.int32, (128, 128), 1)
    comb = jnp.where((li // MEM_ROWS == lj // MEM_ROWS) & (li % XA_HEADS == lj % XA_HEADS),
                     1.0, 0.0).astype(BF16)
    samples = list(range(q_ref.shape[0]))
    scs = [_mm_nt(q_ref[j], k_ref[j]) for j in samples]

    def partial(sc):
        return jnp.concatenate(
            [jnp.sum(jnp.where(diag, sc[:, t * 128:(t + 1) * 128], f0), axis=0, keepdims=True)
             for t in range(n_blk)], axis=0)

    def scores(part):
        hi = part.astype(BF16)
        lo = (part - hi.astype(F32)).astype(BF16)
        return (jnp.dot(hi, comb, preferred_element_type=F32)
                + jnp.dot(lo, comb, preferred_element_type=F32)) * (XA_DIM ** -0.5)

    def softmax(s):
        mx = _lane_allreduce(jnp.broadcast_to(jnp.max(s, axis=0, keepdims=True), (MEM_ROWS, 128)),
                             jnp.maximum)
        e = jnp.exp(s - mx[0:1, :])
        den = _lane_allreduce(jnp.broadcast_to(jnp.sum(e, axis=0, keepdims=True), (MEM_ROWS, 128)),
                              jnp.add)
        p = e / den[0:1, :]
        return jnp.concatenate(
            [jnp.where(diag, jnp.broadcast_to(p[t:t + 1, :], (MEM_ROWS, 128)), f0)
             for t in range(n_blk)], axis=1)

    p_rows = _each(softmax, _each(scores, _each(partial, scs)))
    for j, p in zip(samples, p_rows):
        o_ref[j] = _mm(p, v_ref[j])


def _xa_sample(q, mk, mv, *, bb):
    rows = q.shape[0]
    qspec = pl.BlockSpec((bb, MEM_ROWS, 128), lambda i: (i, 0, 0))
    mspec = pl.BlockSpec((bb, N_MEM * MEM_ROWS, 128), lambda i: (i, 0, 0))
    return pl.pallas_call(
        _xa_sample_kernel,
        grid=(rows // bb,),
        in_specs=[qspec, mspec, mspec],
        out_specs=qspec,
        out_shape=jax.ShapeDtypeStruct((rows, MEM_ROWS, 128), F32),
        compiler_params=_params(),
        name="xa_sample",
    )(q, mk, mv)


def _pad_lora(x):
    wd = x[..., :DECAY_LORA]
    ad = x[..., DECAY_LORA:DECAY_LORA + AAA_LORA]
    gd = x[..., DECAY_LORA + AAA_LORA:]
    z = lambda n: jnp.zeros(x.shape[:-1] + (n,), x.dtype)
    return jnp.concatenate([wd, z(LORA_AD - DECAY_LORA), ad, z(LORA_GD - LORA_AD - AAA_LORA),
                            gd, z(LORA_W - LORA_GD - GATE_LORA)], axis=-1)


def _unpad_shift(zm, zl):
    return jnp.concatenate([zm, zl[..., LORA_WD:LORA_WD + DECAY_LORA],
                            zl[..., LORA_AD:LORA_AD + AAA_LORA],
                            zl[..., LORA_GD:LORA_GD + GATE_LORA]], axis=-1)


def _pad_rows(w, n):
    return jnp.pad(w, ((0, n - w.shape[0]), (0, 0)))


def _mem_rows(x):
    b = x.shape[0]
    return x.reshape(b, N_MEM, XA_HEADS, XA_DIM // 128, 128).transpose(0, 1, 3, 2, 4).reshape(
        b, N_MEM * MEM_ROWS, 128)


def _head_rows(x):
    b = x.shape[0]
    return x.reshape(b, XA_HEADS, XA_DIM // 128, 128).transpose(0, 2, 1, 3).reshape(b, MEM_ROWS, 128)


def _from_head_rows(x):
    b = x.shape[0]
    return x.reshape(b, XA_DIM // 128, XA_HEADS, 128).transpose(0, 2, 1, 3).reshape(b, D_MODEL)


def kernel(x_prompt, x_sample, state_rwkv, state_shift, cache_mem_k, cache_mem_v, mem_prompt, ln_ffn1, ffn1_gate, ffn1_up, ffn1_down, ln_mix, w_in, w_out, sgu_w, sgu_b, sgu_ln_g, sgu_ln_b, rwkv_mu, rwkv_w0, rwkv_w2, rwkv_a0, rwkv_a2, rwkv_g2, rwkv_k_k, rwkv_k_a, rwkv_r_k, rwkv_gn_g, rwkv_gn_b, ln_xattn, mem_norm, xa_q, xa_k, xa_v, xa_o, ln_ffn2, ffn2_gate, ffn2_up, ffn2_down, final_norm):
    assert ln_ffn1.shape[0] == 1, "single layer"
    bp, seq, _ = x_prompt.shape
    bs = x_sample.shape[0]
    row = lambda a: a.reshape(1, -1).astype(F32)
    bf = lambda a: a.astype(BF16)
    l = 0
    head_id = jnp.arange(B_WIDTH) // HEAD
    tril = jnp.tril(jnp.ones((CHUNK, CHUNK), dtype=bool))
    wmask = jnp.where(tril[None], sgu_w[l], 0)
    p = {
        "ln_mix": row(ln_mix[l]),
        "w_main": bf(w_in[l][:, :MAIN_W]),
        "w_lora": bf(_pad_lora(w_in[l][:, MAIN_W:])),
        "sgu_wcat": bf(wmask.transpose(1, 0, 2).reshape(CHUNK, A_GROUPS * CHUNK)),
        "sgu_bias": jnp.repeat(sgu_b[l].T, A_GROUP_DIM, axis=1),
        "sgu_w00": row(jnp.repeat(sgu_w[l][:, 0, 0], A_GROUP_DIM)),
        "sgu_b0": row(jnp.repeat(sgu_b[l][:, 0], A_GROUP_DIM)),
        "sgu_ln_g": row(sgu_ln_g[l]), "sgu_ln_b": row(sgu_ln_b[l]),
        "mu_main": row(rwkv_mu[l][:RKV_W]),
        "mu_lora": row(_pad_lora(rwkv_mu[l][RKV_W:])),
        "w0": row(rwkv_w0[l]), "w2": bf(_pad_rows(rwkv_w2[l], LORA_AD - LORA_WD)),
        "a0": row(rwkv_a0[l]), "a2": bf(_pad_rows(rwkv_a2[l], LORA_GD - LORA_AD)),
        "g2": bf(_pad_rows(rwkv_g2[l], LORA_W - LORA_GD)),
        "k_k": row(rwkv_k_k[l]), "k_a": row(rwkv_k_a[l]), "r_k": row(rwkv_r_k[l]),
        "ones_bd": (head_id[:, None] == head_id[None, :]).astype(BF16),
        "gn_g": row(rwkv_gn_g[l]), "gn_b": row(rwkv_gn_b[l]),
        "w_out_a": bf(w_out[l][:A_WIDTH]), "w_out_b": bf(w_out[l][A_WIDTH:]),
        "ln_xattn": row(ln_xattn[l]), "xa_q": bf(xa_q[l]),
    }
    ffn1 = (row(ln_ffn1[l]), bf(ffn1_gate[l]), bf(ffn1_up[l]), bf(ffn1_down[l]))
    ffn2 = (row(ln_ffn2[l]), bf(ffn2_gate[l]), bf(ffn2_up[l]), bf(ffn2_down[l]))
    xa_o_b = bf(xa_o[l])
    fnorm = row(final_norm)

    tm = 512
    xp = x_prompt.reshape(bp * seq, D_MODEL)
    mk, mv, mkb, mvb = _memkv(mem_prompt.reshape(bp * N_MEM, D_MODEL), row(mem_norm[l]),
                              bf(xa_k[l]), bf(xa_v[l]), tm=tm)
    x1 = _ffn(xp, *ffn1, tm=tm)
    ya, r, w, k, v, kk, b, g, bonus, zlast = _mix_in(x1, p, tm=tm, sample=False)
    o, s_bd = _scan_prompt(r, w, k, v, kk, b, batch=bp, seq=seq)
    state_p = jnp.stack([s_bd[:, :, :HEAD, :HEAD], s_bd[:, :, HEAD:, HEAD:]],
                        axis=2).reshape(bp, HEADS, HEAD, HEAD)
    x2, attn = _post_mix(x1, ya, o, g, bonus, p, tm=tm,
                         mk=mkb.reshape(bp, N_MEM, D_MODEL), mv=mvb.reshape(bp, N_MEM, D_MODEL))
    y_prompt = _ffn(x2, *ffn2, tm=tm, attn=attn, wo=xa_o_b, final_norm=fnorm)
    tiles_per_seq = seq // tm
    zl_rows = zlast.reshape(bp, tiles_per_seq, 8, RKV_W + LORA_W)[:, -1, 0]
    shift_p = _unpad_shift(zl_rows[:, :RKV_W], zl_rows[:, RKV_W:])

    xs = x_sample.reshape(bs, D_MODEL)
    sh = state_shift[l].reshape(bs, B_PROJ)
    x1s = _ffn(xs, *ffn1, tm=bs)
    (ya_s, r_s, w_s, k_s, v_s, kk_s, b_s, g_s, bonus_s, va_s, zm_s, zl_s) = _mix_in(
        x1s, p, tm=bs, sample=True, shift_main=sh[:, :RKV_W], shift_lora=_pad_lora(sh[:, RKV_W:]))
    o_s, state_s = _scan_sample(state_rwkv[l], r_s, w_s, k_s, v_s, kk_s, b_s, bb=8)
    x2s, q_s = _post_mix(x1s, ya_s, o_s, g_s, bonus_s, p, tm=bs)
    attn_s = _from_head_rows(_xa_sample(_head_rows(q_s), _mem_rows(cache_mem_k[l]),
                                        _mem_rows(cache_mem_v[l]), bb=8))
    y_sample = _ffn(x2s, *ffn2, tm=bs, attn=attn_s, wo=xa_o_b, final_norm=fnorm)

    return (y_prompt.reshape(bp, seq, D_MODEL),
            y_sample.reshape(bs, 1, D_MODEL),
            state_p[None],
            shift_p.reshape(1, bp, 1, B_PROJ),
            mk.reshape(1, bp, N_MEM, XA_HEADS, XA_DIM),
            mv.reshape(1, bp, N_MEM, XA_HEADS, XA_DIM),
            state_s[None],
            _unpad_shift(zm_s, zl_s).reshape(1, bs, 1, B_PROJ),
            va_s.reshape(1, bs, 1, A_WIDTH))
```

```python
import functools

import jax
import jax.numpy as jnp
from jax import lax
from jax.experimental import pallas as pl
from jax.experimental.pallas import tpu as pltpu

F32 = jnp.float32
BF16 = jnp.bfloat16

D_MODEL = 1024
SEQ = 2048
A_WIDTH = 512
A_GROUPS = 8
A_GROUP_DIM = 64
CHUNK = 128
B_WIDTH = 512
HEAD = 64
HEADS = 8
PAIRS = HEADS // 2
PAIR_W = 2 * HEAD
DECAY_LORA = 64
AAA_LORA = 64
GATE_LORA = 160
B_PROJ = 3 * B_WIDTH + DECAY_LORA + AAA_LORA + GATE_LORA
MAIN_W = 2 * A_WIDTH + 3 * B_WIDTH
RKV_W = 3 * B_WIDTH
LORA_W = 512
LORA_WD, LORA_AD, LORA_GD = 0, 128, 256
D_FF = 2816
N_MEM = 256
XA_HEADS = 4
XA_DIM = 256
NORM_EPS = 1e-6
LN_EPS = 1e-5
GN_EPS = 64e-5

VMEM_LIMIT = 56 * 1024 * 1024


def _params(n_axes=1):
    return pltpu.CompilerParams(dimension_semantics=("arbitrary",) * n_axes,
                                vmem_limit_bytes=VMEM_LIMIT)


def _const_spec(shape):
    nd = len(shape)
    return pl.BlockSpec(shape, lambda *_: (0,) * nd, pipeline_mode=pl.Buffered(1))


def _rows_spec(tm, width):
    return pl.BlockSpec((tm, width), lambda i: (i, 0))


def _rms(x, g):
    return x * lax.rsqrt(jnp.mean(x * x, axis=-1, keepdims=True) + NORM_EPS) * g


def _dot(a, b):
    return jnp.dot(a.astype(BF16), b, preferred_element_type=F32)


def _seg_sum(x, ones_bd):
    return jnp.dot(x.astype(BF16), ones_bd, preferred_element_type=F32)


def _ffn_kernel(*refs, pre, final):
    it = iter(refs)
    x_ref = next(it)
    if pre:
        attn_ref, wo_ref = next(it), next(it)
    ln_ref, wg_ref, wu_ref, wd_ref = next(it), next(it), next(it), next(it)
    if final:
        fn_ref = next(it)
    o_ref = next(it)
    x = x_ref[...]
    if pre:
        x = x + _dot(attn_ref[...], wo_ref[...])
    xb = _rms(x, ln_ref[...]).astype(BF16)
    g = jnp.dot(xb, wg_ref[...], preferred_element_type=F32)
    u = jnp.dot(xb, wu_ref[...], preferred_element_type=F32)
    h = (g * jax.nn.sigmoid(g) * u).astype(BF16)
    x = x + 0.5 * jnp.dot(h, wd_ref[...], preferred_element_type=F32)
    if final:
        x = _rms(x, fn_ref[...])
    o_ref[...] = x


def _ffn(x, ln, wg, wu, wd, *, tm, attn=None, wo=None, final_norm=None):
    rows = x.shape[0]
    pre = attn is not None
    final = final_norm is not None
    args, specs = [x], [_rows_spec(tm, D_MODEL)]
    if pre:
        args += [attn, wo]
        specs += [_rows_spec(tm, D_MODEL), _const_spec((D_MODEL, D_MODEL))]
    args += [ln, wg, wu, wd]
    specs += [_const_spec((1, D_MODEL)), _const_spec((D_MODEL, D_FF)),
              _const_spec((D_MODEL, D_FF)), _const_spec((D_FF, D_MODEL))]
    if final:
        args.append(final_norm)
        specs.append(_const_spec((1, D_MODEL)))
    return pl.pallas_call(
        functools.partial(_ffn_kernel, pre=pre, final=final),
        grid=(rows // tm,),
        in_specs=specs,
        out_specs=_rows_spec(tm, D_MODEL),
        out_shape=jax.ShapeDtypeStruct((rows, D_MODEL), F32),
        compiler_params=_params(),
        name="ffn",
    )(*args)


def _memkv_kernel(m_ref, g_ref, wk_ref, wv_ref, k_ref, v_ref, kb_ref, vb_ref):
    mb = _rms(m_ref[...], g_ref[...]).astype(BF16)
    k = jnp.dot(mb, wk_ref[...], preferred_element_type=F32)
    v = jnp.dot(mb, wv_ref[...], preferred_element_type=F32)
    k_ref[...] = k
    v_ref[...] = v
    kb_ref[...] = k.astype(BF16)
    vb_ref[...] = v.astype(BF16)


def _memkv(mem, g, wk, wv, *, tm):
    rows = mem.shape[0]
    out = jax.ShapeDtypeStruct((rows, D_MODEL), F32)
    outb = jax.ShapeDtypeStruct((rows, D_MODEL), BF16)
    return pl.pallas_call(
        _memkv_kernel,
        grid=(rows // tm,),
        in_specs=[_rows_spec(tm, D_MODEL), _const_spec((1, D_MODEL)),
                  _const_spec((D_MODEL, D_MODEL)), _const_spec((D_MODEL, D_MODEL))],
        out_specs=[_rows_spec(tm, D_MODEL)] * 4,
        out_shape=[out, out, outb, outb],
        compiler_params=_params(),
        name="memkv",
    )(mem, g, wk, wv)


def _mix_kernel(*refs, sample, tiles_per_seq):
    it = iter(refs)
    x_ref, ln_ref, wmain_ref, wlora_ref = next(it), next(it), next(it), next(it)
    if sample:
        w00_ref, b0_ref, spm_ref, spl_ref = next(it), next(it), next(it), next(it)
    else:
        wcat_ref, bias_ref = next(it), next(it)
    (lng_ref, lnb_ref, mum_ref, mul_ref, w0_ref, w2_ref, a0_ref, a2_ref, g2_ref,
     kk_ref, ka_ref, rk_ref, ones_ref) = [next(it) for _ in range(13)]
    (ya_ref, r_ref, w_ref, k_ref, v_ref, kn_ref, b_ref, g_ref, bonus_ref) = [
        next(it) for _ in range(9)]
    if sample:
        va_ref, zm_ref, zl_ref = next(it), next(it), next(it)
    else:
        zlast_ref, cm_ref, cl_ref = next(it), next(it), next(it)

    xb = _rms(x_ref[...], ln_ref[...]).astype(BF16)
    _mix_rest(jnp.dot(xb, wmain_ref[...], preferred_element_type=F32),
              jnp.dot(xb, wlora_ref[...], preferred_element_type=F32), dict(locals()))


def _mix_rest(zmain, zl, names):
    sample, tiles_per_seq = names["sample"], names["tiles_per_seq"]
    (lng_ref, lnb_ref, mum_ref, mul_ref, w0_ref, w2_ref, a0_ref, a2_ref, g2_ref, kk_ref, ka_ref,
     rk_ref, ones_ref, ya_ref, r_ref, w_ref, k_ref, v_ref, kn_ref, b_ref, g_ref, bonus_ref) = [
        names[n] for n in (
            "lng_ref", "lnb_ref", "mum_ref", "mul_ref", "w0_ref", "w2_ref", "a0_ref", "a2_ref",
            "g2_ref", "kk_ref", "ka_ref", "rk_ref", "ones_ref", "ya_ref", "r_ref", "w_ref",
            "k_ref", "v_ref", "kn_ref", "b_ref", "g_ref", "bonus_ref")]
    if sample:
        w00_ref, b0_ref, spm_ref, spl_ref, va_ref, zm_ref, zl_ref = [
            names[n] for n in ("w00_ref", "b0_ref", "spm_ref", "spl_ref", "va_ref", "zm_ref",
                               "zl_ref")]
    else:
        wcat_ref, bias_ref, zlast_ref, cm_ref, cl_ref = [
            names[n] for n in ("wcat_ref", "bias_ref", "zlast_ref", "cm_ref", "cl_ref")]
    tm = zl.shape[0]

    u = jax.nn.gelu(zmain[:, :A_WIDTH])
    vx = jax.nn.gelu(zmain[:, A_WIDTH:2 * A_WIDTH])
    mu = jnp.mean(vx, axis=-1, keepdims=True)
    var = jnp.mean(jnp.square(vx - mu), axis=-1, keepdims=True)
    va = (vx - mu) * lax.rsqrt(var + LN_EPS) * lng_ref[...] + lnb_ref[...]
    if sample:
        mixed = va * w00_ref[...] + b0_ref[...]
        ya_ref[...] = (u * mixed).astype(BF16)
        va_ref[...] = va
    else:
        vab = va.astype(BF16)
        first = lax.broadcasted_iota(jnp.int32, (CHUNK, 2 * A_GROUP_DIM), 1) < A_GROUP_DIM
        for c in range(tm // CHUNK):
            rows = slice(c * CHUNK, (c + 1) * CHUNK)
            for gp in range(A_GROUPS // 2):
                lanes = slice(gp * 2 * A_GROUP_DIM, (gp + 1) * 2 * A_GROUP_DIM)
                vc = vab[rows, lanes]
                zero = jnp.zeros_like(vc)
                rhs = jnp.concatenate([jnp.where(first, vc, zero), jnp.where(first, zero, vc)],
                                      axis=0)
                mixed = jnp.dot(wcat_ref[:, gp * 2 * CHUNK:(gp + 1) * 2 * CHUNK], rhs,
                                preferred_element_type=F32) + bias_ref[:, lanes]
                ya_ref[rows, lanes] = (u[rows, lanes] * mixed).astype(BF16)

    zbm = zmain[:, 2 * A_WIDTH:]
    if sample:
        zpm, zpl = spm_ref[...], spl_ref[...]
        zm_ref[...] = zbm
        zl_ref[...] = zl
    else:
        tile = pl.program_id(0)

        @pl.when(tile % tiles_per_seq == 0)
        def _():
            cm_ref[...] = jnp.zeros_like(cm_ref)
            cl_ref[...] = jnp.zeros_like(cl_ref)

        first_m = lax.broadcasted_iota(jnp.int32, zbm.shape, 0) == 0
        first_l = lax.broadcasted_iota(jnp.int32, zl.shape, 0) == 0
        zpm = jnp.where(first_m, cm_ref[0:1, :], pltpu.roll(zbm, 1, axis=0))
        zpl = jnp.where(first_l, cl_ref[0:1, :], pltpu.roll(zl, 1, axis=0))
        cm_ref[0:1, :] = zbm[tm - 1:tm, :]
        cl_ref[0:1, :] = zl[tm - 1:tm, :]
        zlast_ref[:, :RKV_W] = jnp.broadcast_to(zbm[tm - 1:tm, :], (8, RKV_W))
        zlast_ref[:, RKV_W:] = jnp.broadcast_to(zl[tm - 1:tm, :], (8, LORA_W))
    zsm = zbm + (zpm - zbm) * mum_ref[...]
    zsl = zl + (zpl - zl) * mul_ref[...]
    r = zsm[:, :B_WIDTH]
    k = zsm[:, B_WIDTH:2 * B_WIDTH]
    v = zsm[:, 2 * B_WIDTH:]
    wd = zsl[:, LORA_WD:LORA_AD]
    ad = zsl[:, LORA_AD:LORA_GD]
    gd = zsl[:, LORA_GD:]
    y = w0_ref[...] + _dot(jnp.tanh(wd), w2_ref[...])
    w_log = jnp.minimum(y, 0.0) - jnp.log1p(jnp.exp(-jnp.abs(y))) - 0.5
    log_decay = -jnp.exp(w_log)
    a = jax.nn.sigmoid(a0_ref[...] + _dot(ad, a2_ref[...]))
    gate = _dot(jax.nn.sigmoid(gd), g2_ref[...])
    ones_bd = ones_ref[...]
    kk = k * kk_ref[...]
    kk = kk * lax.rsqrt(jnp.maximum(_seg_sum(kk * kk, ones_bd), 1e-24))
    k2 = k * (1.0 + (a - 1.0) * ka_ref[...])
    r_ref[...] = r
    w_ref[...] = jnp.exp(log_decay) if sample else log_decay
    k_ref[...] = k2
    v_ref[...] = v
    kn_ref[...] = kk
    b_ref[...] = kk * a
    g_ref[...] = gate
    bonus_ref[...] = _seg_sum(r * k2 * rk_ref[...], ones_bd) * v


def _mix_in(x, p, *, tm, sample, shift_main=None, shift_lora=None):
    rows = x.shape[0]
    n_tiles = rows // tm
    args = [x, p["ln_mix"], p["w_main"], p["w_lora"]]
    specs = [_rows_spec(tm, D_MODEL), _const_spec((1, D_MODEL)),
             _const_spec((D_MODEL, MAIN_W)), _const_spec((D_MODEL, LORA_W))]
    if sample:
        args += [p["sgu_w00"], p["sgu_b0"], shift_main, shift_lora]
        specs += [_const_spec((1, A_WIDTH)), _const_spec((1, A_WIDTH)),
                  _rows_spec(tm, RKV_W), _rows_spec(tm, LORA_W)]
    else:
        args += [p["sgu_wcat"], p["sgu_bias"]]
        specs += [_const_spec((CHUNK, A_GROUPS * CHUNK)), _const_spec((CHUNK, A_WIDTH))]
    args += [p["sgu_ln_g"], p["sgu_ln_b"], p["mu_main"], p["mu_lora"], p["w0"], p["w2"],
             p["a0"], p["a2"], p["g2"], p["k_k"], p["k_a"], p["r_k"], p["ones_bd"]]
    specs += [_const_spec((1, A_WIDTH)), _const_spec((1, A_WIDTH)), _const_spec((1, RKV_W)),
              _const_spec((1, LORA_W)), _const_spec((1, B_WIDTH)),
              _const_spec((LORA_AD - LORA_WD, B_WIDTH)), _const_spec((1, B_WIDTH)),
              _const_spec((LORA_GD - LORA_AD, B_WIDTH)), _const_spec((LORA_W - LORA_GD, B_WIDTH)),
              _const_spec((1, B_WIDTH)), _const_spec((1, B_WIDTH)), _const_spec((1, B_WIDTH)),
              _const_spec((B_WIDTH, B_WIDTH))]
    wide = jax.ShapeDtypeStruct((rows, B_WIDTH), F32)
    out_shape = [jax.ShapeDtypeStruct((rows, A_WIDTH), BF16)] + [wide] * 8
    out_specs = [_rows_spec(tm, B_WIDTH)] * 9
    scratch = []
    if sample:
        out_shape += [wide, jax.ShapeDtypeStruct((rows, RKV_W), F32),
                      jax.ShapeDtypeStruct((rows, LORA_W), F32)]
        out_specs += [_rows_spec(tm, A_WIDTH), _rows_spec(tm, RKV_W), _rows_spec(tm, LORA_W)]
    else:
        out_shape += [jax.ShapeDtypeStruct((n_tiles * 8, RKV_W + LORA_W), F32)]
        out_specs += [pl.BlockSpec((8, RKV_W + LORA_W), lambda i: (i, 0))]
        scratch = [pltpu.VMEM((8, RKV_W), F32), pltpu.VMEM((8, LORA_W), F32)]
    return pl.pallas_call(
        functools.partial(_mix_kernel, sample=sample, tiles_per_seq=max(SEQ // tm, 1)),
        grid=(n_tiles,),
        in_specs=specs,
        out_specs=out_specs,
        out_shape=out_shape,
        scratch_shapes=scratch,
        compiler_params=_params(),
        name="mix_in",
    )(*args)


def _each(f, *lists):
    return [f(*xs) for xs in zip(*lists)]


def _pair_halves(tiles, half_ones):
    x = jnp.concatenate(tiles, axis=0)
    hi = x.astype(BF16)
    lo = (x - hi.astype(F32)).astype(BF16)
    y = (jnp.dot(hi, half_ones, preferred_element_type=F32)
         + jnp.dot(lo, half_ones, preferred_element_type=F32))
    return [y[i * HEAD:(i + 1) * HEAD] for i in range(len(tiles))]


def _scan_steps(ss, rs, ws, ks, vs, kks, bs):
    lane = lax.broadcasted_iota(jnp.int32, (HEAD, PAIR_W), 1)
    row = lax.broadcasted_iota(jnp.int32, (HEAD, PAIR_W), 0)
    eye2 = jnp.where((lane % HEAD) == row, 1.0, 0.0).astype(F32)
    li = lax.broadcasted_iota(jnp.int32, (PAIR_W, PAIR_W), 0)
    lj = lax.broadcasted_iota(jnp.int32, (PAIR_W, PAIR_W), 1)
    half_ones = jnp.where(li // HEAD == lj // HEAD, 1.0, 0.0).astype(BF16)
    n = len(ss)
    sums = _pair_halves(_each(lambda s, kk: s * kk, ss, kks) + _each(lambda v: eye2 * v, vs),
                        half_ones)
    sas, vcols = sums[:n], sums[n:]
    ss = _each(lambda s, w, sa, b, vcol, k: s * w - sa * b + vcol * k, ss, ws, sas, bs, vcols, ks)
    ocols = _pair_halves(_each(lambda s, r: s * r, ss, rs), half_ones)
    return ss, _each(lambda ocol: jnp.sum(eye2 * ocol, axis=0, keepdims=True), ocols)


SCAN_C = 64


def _mm(a, b):
    return jnp.dot(a.astype(BF16), b.astype(BF16), preferred_element_type=F32)


def _mm_nt(a, b):
    return lax.dot_general(a.astype(BF16), b.astype(BF16), (((1,), (1,)), ((), ())),
                           preferred_element_type=F32)


def _mm_tn(a, b):
    return lax.dot_general(a.astype(BF16), b.astype(BF16), (((0,), (0,)), ((), ())),
                           preferred_element_type=F32)


def _cumsum_rows(x):
    n = x.shape[0]
    row = lax.broadcasted_iota(jnp.int32, x.shape, 0)
    s = 1
    while s < n:
        x = x + jnp.where(row >= s, pltpu.roll(x, s, axis=0), 0.0)
        s *= 2
    return x


INV_BASE = 8


def _unit_lower_inverse(ns, row, col):
    f0 = jnp.zeros((), F32)
    same = lambda s: (row // s) == (col // s)
    eye = jnp.where(row == col, 1.0, f0)
    ps = _each(lambda n: jnp.where(same(INV_BASE), n, f0), ns)
    ts = _each(lambda p: eye + p, ps)
    s = 2
    while s < INV_BASE:
        ps = _each(lambda p: _mm(p, p), ps)
        ts = _each(lambda t, p: t + _mm(t, p), ts, ps)
        s *= 2
    s = INV_BASE
    while s < SCAN_C:
        level = same(2 * s) & jnp.logical_not(same(s))
        ws = _each(lambda n, t: _mm(jnp.where(level, n, f0), t), ns, ts)
        ts = _each(lambda t, w: t + _mm(t, w), ts, ws)
        s *= 2
    return ts


def _chunk_pairs(s0s, rs, lws, ks, vs, kks, bs):
    c = SCAN_C
    f0 = jnp.zeros((), F32)
    row = lax.broadcasted_iota(jnp.int32, (2 * c, PAIR_W), 0)
    col = lax.broadcasted_iota(jnp.int32, (2 * c, PAIR_W), 1)
    top, lft = row < c, col < HEAD
    same_head = top == lft
    strict = (row % c) > (col % HEAD)
    row_c = lax.broadcasted_iota(jnp.int32, (c, PAIR_W), 0)
    col_c = lax.broadcasted_iota(jnp.int32, (c, PAIR_W), 1)
    lft_c = col_c < HEAD
    strict_c = row_c > (col_c % HEAD)
    incl_c = row_c >= (col_c % HEAD)

    def prep(r, lw, k, v, kk, b):
        cum = _cumsum_rows(lw)
        end = cum[c - 1:c, :]
        a_t = -kk * jnp.exp(cum - lw)
        r_t = r * jnp.exp(cum)
        einv = jnp.exp(-cum)
        eend = jnp.exp(end - cum)
        return dict(
            x0=jnp.concatenate([a_t, r_t], axis=0), x1=jnp.concatenate([r_t, a_t], axis=0),
            bk=jnp.concatenate([b * einv, k * einv], axis=0),
            kb=jnp.concatenate([k * einv, b * einv], axis=0),
            bk_e=jnp.concatenate([b * eend, k * eend], axis=0),
            w_end=jnp.exp(end), v=v,
            v_l=jnp.where(lft_c, v, f0), v_r=jnp.where(lft_c, f0, v))

    fs = _each(prep, rs, lws, ks, vs, kks, bs)
    g0s = _each(lambda f: _mm_nt(jnp.where(lft, f["x0"], f0), f["bk"]), fs)
    g1s = _each(lambda f: _mm_nt(jnp.where(lft, f0, f["x1"]), f["kb"]), fs)
    pqs = _each(lambda f, s0: _mm_nt(f["x0"], s0), fs, s0s)

    def rhs(f, g0, g1, pq):
        ak = jnp.where(strict_c, jnp.where(lft_c, g1[c:], g0[:c]), f0)
        x = pq[:c] + _mm(ak, jnp.concatenate([f["v_r"], f["v_l"]], axis=0))
        return jnp.concatenate([jnp.where(lft_c, x, f0), jnp.where(lft_c, f0, x)], axis=0)

    ys = _each(rhs, fs, g0s, g1s, pqs)
    ns = _each(lambda g0, g1: jnp.where(strict & same_head, jnp.where(top, g0, g1), f0),
               g0s, g1s)
    ts = _unit_lower_inverse(ns, row, col)
    ys = _each(_mm, ts, ys)

    def out(f, g0, g1, pq, y):
        lhs = jnp.concatenate([jnp.where(incl_c, g0[c:], f0), jnp.where(incl_c, g1[:c], f0)],
                              axis=1)
        return pq[c:] + _mm(lhs, jnp.concatenate([y[:c], f["v_l"], f["v_r"], y[c:]], axis=0))

    def state(f, s0, y):
        upd = _mm_tn(jnp.concatenate([y[:c] + y[c:], f["v"]], axis=0), f["bk_e"])
        return s0 * f["w_end"] + jnp.where(same_head, upd, f0)

    return _each(out, fs, g0s, g1s, pqs, ys), _each(state, fs, s0s, ys)


SCAN_BATCHES = 4


def _scan_prompt_kernel(r_ref, w_ref, k_ref, v_ref, kk_ref, b_ref, o_ref, sout_ref, s_ref):
    t_blk = pl.program_id(1)

    @pl.when(t_blk == 0)
    def _():
        s_ref[...] = jnp.zeros_like(s_ref)

    chains = [(j, p) for j in range(SCAN_BATCHES) for p in range(PAIRS)]
    lanes = lambda p: slice(p * PAIR_W, (p + 1) * PAIR_W)
    take = lambda ref: [ref[j, :, lanes(p)] for j, p in chains]
    os_, ss = _chunk_pairs([s_ref[j, p] for j, p in chains], take(r_ref), take(w_ref),
                           take(k_ref), take(v_ref), take(kk_ref), take(b_ref))
    for (j, p), o, s_new in zip(chains, os_, ss):
        o_ref[j, :, lanes(p)] = o
        s_ref[j, p] = s_new

    @pl.when(t_blk == pl.num_programs(1) - 1)
    def _():
        sout_ref[...] = s_ref[...]


def _scan_prompt(r, lw, k, v, kk, b, *, batch, seq):
    n_t = seq // SCAN_C
    nb = SCAN_BATCHES
    spec = pl.BlockSpec((nb, SCAN_C, B_WIDTH), lambda bi, ti: (bi, ti, 0))
    sspec = pl.BlockSpec((nb, PAIRS, PAIR_W, PAIR_W), lambda bi, ti: (bi, 0, 0, 0))
    o, s = pl.pallas_call(
        _scan_prompt_kernel,
        grid=(batch // nb, n_t),
        in_specs=[spec] * 6,
        out_specs=[spec, sspec],
        out_shape=[jax.ShapeDtypeStruct((batch, seq, B_WIDTH), F32),
                   jax.ShapeDtypeStruct((batch, PAIRS, PAIR_W, PAIR_W), F32)],
        scratch_shapes=[pltpu.VMEM((nb, PAIRS, PAIR_W, PAIR_W), F32)],
        compiler_params=_params(2),
        name="scan_prompt",
    )(*[x.reshape(batch, seq, B_WIDTH) for x in (r, lw, k, v, kk, b)])
    return o.reshape(batch * seq, B_WIDTH), s


def _scan_sample_kernel(s_ref, r_ref, w_ref, k_ref, v_ref, kk_ref, b_ref, o_ref, sout_ref):
    chains = [(j, p) for j in range(s_ref.shape[0]) for p in range(PAIRS)]
    take = lambda ref: [ref[j:j + 1, p * PAIR_W:(p + 1) * PAIR_W] for j, p in chains]
    ss = [jnp.concatenate([s_ref[j, 2 * p], s_ref[j, 2 * p + 1]], axis=1) for j, p in chains]
    ss, outs = _scan_steps(ss, take(r_ref), take(w_ref), take(k_ref), take(v_ref), take(kk_ref),
                           take(b_ref))
    for (j, p), s, o in zip(chains, ss, outs):
        sout_ref[j, 2 * p] = s[:, :HEAD]
        sout_ref[j, 2 * p + 1] = s[:, HEAD:]
        o_ref[j:j + 1, p * PAIR_W:(p + 1) * PAIR_W] = o


def _scan_sample(state, r, w, k, v, kk, b, *, bb):
    rows = r.shape[0]
    sspec = pl.BlockSpec((bb, HEADS, HEAD, HEAD), lambda i: (i, 0, 0, 0))
    spec = _rows_spec(bb, B_WIDTH)
    return pl.pallas_call(
        _scan_sample_kernel,
        grid=(rows // bb,),
        in_specs=[sspec] + [spec] * 6,
        out_specs=[spec, sspec],
        out_shape=[jax.ShapeDtypeStruct((rows, B_WIDTH), F32),
                   jax.ShapeDtypeStruct(state.shape, F32)],
        compiler_params=_params(),
        name="scan_sample",
    )(state, r, w, k, v, kk, b)


def _softmax_rows(s):
    e = jnp.exp(s - jnp.max(s, axis=-1, keepdims=True))
    return e / jnp.sum(e, axis=-1, keepdims=True)


def _post_kernel(*refs, attend):
    it = iter(refs)
    (x_ref, ya_ref, o_ref, g_ref, bonus_ref, gng_ref, gnb_ref, ones_ref, woa_ref, wob_ref,
     lnx_ref, wq_ref) = [next(it) for _ in range(12)]
    if attend:
        mk_ref, mv_ref = next(it), next(it)
    x2_ref, out_ref = next(it), next(it)

    ones_bd = ones_ref[...]
    o = o_ref[...]
    mu = _seg_sum(o, ones_bd) * (1.0 / HEAD)
    d = o - mu
    var = _seg_sum(d * d, ones_bd) * (1.0 / HEAD)
    on = d * lax.rsqrt(var + GN_EPS) * gng_ref[...] + gnb_ref[...]
    yb = (on + bonus_ref[...]) * g_ref[...]
    x2 = x_ref[...] + jnp.dot(ya_ref[...], woa_ref[...], preferred_element_type=F32) \
        + _dot(yb, wob_ref[...])
    x2_ref[...] = x2
    q = _dot(_rms(x2, lnx_ref[...]), wq_ref[...])
    if not attend:
        out_ref[...] = q
        return
    qb = q.astype(BF16)
    for h in range(XA_HEADS):
        sl = slice(h * XA_DIM, (h + 1) * XA_DIM)
        s = lax.dot_general(qb[:, sl], mk_ref[0, :, sl], (((1,), (1,)), ((), ())),
                            preferred_element_type=F32) * (XA_DIM ** -0.5)
        p = _softmax_rows(s)
        out_ref[:, sl] = _dot(p, mv_ref[0, :, sl]).astype(BF16)


def _post_mix(x, ya, o, g, bonus, p, *, tm, mk=None, mv=None):
    rows = x.shape[0]
    attend = mk is not None
    args = [x, ya, o, g, bonus, p["gn_g"], p["gn_b"], p["ones_bd"], p["w_out_a"], p["w_out_b"],
            p["ln_xattn"], p["xa_q"]]
    specs = [_rows_spec(tm, D_MODEL)] + [_rows_spec(tm, B_WIDTH)] * 4 + [
        _const_spec((1, B_WIDTH)), _const_spec((1, B_WIDTH)), _const_spec((B_WIDTH, B_WIDTH)),
        _const_spec((A_WIDTH, D_MODEL)), _const_spec((B_WIDTH, D_MODEL)),
        _const_spec((1, D_MODEL)), _const_spec((D_MODEL, D_MODEL))]
    if attend:
        tiles_per_seq = SEQ // tm
        mspec = pl.BlockSpec((1, N_MEM, D_MODEL), lambda i: (i // tiles_per_seq, 0, 0))
        args += [mk, mv]
        specs += [mspec, mspec]
    return pl.pallas_call(
        functools.partial(_post_kernel, attend=attend),
        grid=(rows // tm,),
        in_specs=specs,
        out_specs=[_rows_spec(tm, D_MODEL)] * 2,
        out_shape=[jax.ShapeDtypeStruct((rows, D_MODEL), F32),
                   jax.ShapeDtypeStruct((rows, D_MODEL), BF16 if attend else F32)],
        compiler_params=_params(),
        name="post_mix",
    )(*args)


MEM_ROWS = XA_HEADS * (XA_DIM // 128)


def _lane_allreduce(x, op):
    shift = MEM_ROWS
    while shift < 128:
        x = op(x, pltpu.roll(x, shift, axis=1))
        shift *= 2
    return x


def _xa_sample_kernel(q_ref, k_ref, v_ref, o_ref):
    f0 = jnp.zeros((), F32)
    n_blk = N_MEM * MEM_ROWS // 128
    sub = lax.broadcasted_iota(jnp.int32, (MEM_ROWS, 128), 0)
    lane = lax.broadcasted_iota(jnp.int32, (MEM_ROWS, 128), 1)
    diag = sub == (lane % MEM_ROWS)
    li = lax.broadcasted_iota(jnp.int32, (128, 128), 0)
    lj = lax.broadcasted_iota(jnp.int32, (128, 128), 1)
    comb = jnp.where((li // MEM_ROWS == lj // MEM_ROWS) & (li % XA_HEADS == lj % XA_HEADS),
                     1.0, 0.0).astype(BF16)
    samples = list(range(q_ref.shape[0]))
    scs = [_mm_nt(q_ref[j], k_ref[j]) for j in samples]

    def partial(sc):
        return jnp.concatenate(
            [jnp.sum(jnp.where(diag, sc[:, t * 128:(t + 1) * 128], f0), axis=0, keepdims=True)
             for t in range(n_blk)], axis=0)

    def scores(part):
        hi = part.astype(BF16)
        lo = (part - hi.astype(F32)).astype(BF16)
        return (jnp.dot(hi, comb, preferred_element_type=F32)
                + jnp.dot(lo, comb, preferred_element_type=F32)) * (XA_DIM ** -0.5)

    def softmax(s):
        mx = _lane_allreduce(jnp.broadcast_to(jnp.max(s, axis=0, keepdims=True), (MEM_ROWS, 128)),
                             jnp.maximum)
        e = jnp.exp(s - mx[0:1, :])
        den = _lane_allreduce(jnp.broadcast_to(jnp.sum(e, axis=0, keepdims=True), (MEM_ROWS, 128)),
                              jnp.add)
        p = e / den[0:1, :]
        return jnp.concatenate(
            [jnp.where(diag, jnp.broadcast_to(p[t:t + 1, :], (MEM_ROWS, 128)), f0)
             for t in range(n_blk)], axis=1)

    p_rows = _each(softmax, _each(scores, _each(partial, scs)))
    for j, p in zip(samples, p_rows):
        o_ref[j] = _mm(p, v_ref[j])


def _xa_sample(q, mk, mv, *, bb):
    rows = q.shape[0]
    qspec = pl.BlockSpec((bb, MEM_ROWS, 128), lambda i: (i, 0, 0))
    mspec = pl.BlockSpec((bb, N_MEM * MEM_ROWS, 128), lambda i: (i, 0, 0))
    return pl.pallas_call(
        _xa_sample_kernel,
        grid=(rows // bb,),
        in_specs=[qspec, mspec, mspec],
        out_specs=qspec,
        out_shape=jax.ShapeDtypeStruct((rows, MEM_ROWS, 128), F32),
        compiler_params=_params(),
        name="xa_sample",
    )(q, mk, mv)


def _pad_lora(x):
    wd = x[..., :DECAY_LORA]
    ad = x[..., DECAY_LORA:DECAY_LORA + AAA_LORA]
    gd = x[..., DECAY_LORA + AAA_LORA:]
    z = lambda n: jnp.zeros(x.shape[:-1] + (n,), x.dtype)
    return jnp.concatenate([wd, z(LORA_AD - DECAY_LORA), ad, z(LORA_GD - LORA_AD - AAA_LORA),
                            gd, z(LORA_W - LORA_GD - GATE_LORA)], axis=-1)


def _unpad_shift(zm, zl):
    return jnp.concatenate([zm, zl[..., LORA_WD:LORA_WD + DECAY_LORA],
                            zl[..., LORA_AD:LORA_AD + AAA_LORA],
                            zl[..., LORA_GD:LORA_GD + GATE_LORA]], axis=-1)


def _pad_rows(w, n):
    return jnp.pad(w, ((0, n - w.shape[0]), (0, 0)))


def _mem_rows(x):
    b = x.shape[0]
    return x.reshape(b, N_MEM, XA_HEADS, XA_DIM // 128, 128).transpose(0, 1, 3, 2, 4).reshape(
        b, N_MEM * MEM_ROWS, 128)


def _head_rows(x):
    b = x.shape[0]
    return x.reshape(b, XA_HEADS, XA_DIM // 128, 128).transpose(0, 2, 1, 3).reshape(b, MEM_ROWS, 128)


def _from_head_rows(x):
    b = x.shape[0]
    return x.reshape(b, XA_DIM // 128, XA_HEADS, 128).transpose(0, 2, 1, 3).reshape(b, D_MODEL)


def kernel(x_prompt, x_sample, state_rwkv, state_shift, cache_mem_k, cache_mem_v, mem_prompt, ln_ffn1, ffn1_gate, ffn1_up, ffn1_down, ln_mix, w_in, w_out, sgu_w, sgu_b, sgu_ln_g, sgu_ln_b, rwkv_mu, rwkv_w0, rwkv_w2, rwkv_a0, rwkv_a2, rwkv_g2, rwkv_k_k, rwkv_k_a, rwkv_r_k, rwkv_gn_g, rwkv_gn_b, ln_xattn, mem_norm, xa_q, xa_k, xa_v, xa_o, ln_ffn2, ffn2_gate, ffn2_up, ffn2_down, final_norm):
    assert ln_ffn1.shape[0] == 1, "single layer"
    bp, seq, _ = x_prompt.shape
    bs = x_sample.shape[0]
    row = lambda a: a.reshape(1, -1).astype(F32)
    bf = lambda a: a.astype(BF16)
    l = 0
    head_id = jnp.arange(B_WIDTH) // HEAD
    tril = jnp.tril(jnp.ones((CHUNK, CHUNK), dtype=bool))
    wmask = jnp.where(tril[None], sgu_w[l], 0)
    p = {
        "ln_mix": row(ln_mix[l]),
        "w_main": bf(w_in[l][:, :MAIN_W]),
        "w_lora": bf(_pad_lora(w_in[l][:, MAIN_W:])),
        "sgu_wcat": bf(wmask.transpose(1, 0, 2).reshape(CHUNK, A_GROUPS * CHUNK)),
        "sgu_bias": jnp.repeat(sgu_b[l].T, A_GROUP_DIM, axis=1),
        "sgu_w00": row(jnp.repeat(sgu_w[l][:, 0, 0], A_GROUP_DIM)),
        "sgu_b0": row(jnp.repeat(sgu_b[l][:, 0], A_GROUP_DIM)),
        "sgu_ln_g": row(sgu_ln_g[l]), "sgu_ln_b": row(sgu_ln_b[l]),
        "mu_main": row(rwkv_mu[l][:RKV_W]),
        "mu_lora": row(_pad_lora(rwkv_mu[l][RKV_W:])),
        "w0": row(rwkv_w0[l]), "w2": bf(_pad_rows(rwkv_w2[l], LORA_AD - LORA_WD)),
        "a0": row(rwkv_a0[l]), "a2": bf(_pad_rows(rwkv_a2[l], LORA_GD - LORA_AD)),
        "g2": bf(_pad_rows(rwkv_g2[l], LORA_W - LORA_GD)),
        "k_k": row(rwkv_k_k[l]), "k_a": row(rwkv_k_a[l]), "r_k": row(rwkv_r_k[l]),
        "ones_bd": (head_id[:, None] == head_id[None, :]).astype(BF16),
        "gn_g": row(rwkv_gn_g[l]), "gn_b": row(rwkv_gn_b[l]),
        "w_out_a": bf(w_out[l][:A_WIDTH]), "w_out_b": bf(w_out[l][A_WIDTH:]),
        "ln_xattn": row(ln_xattn[l]), "xa_q": bf(xa_q[l]),
    }
    ffn1 = (row(ln_ffn1[l]), bf(ffn1_gate[l]), bf(ffn1_up[l]), bf(ffn1_down[l]))
    ffn2 = (row(ln_ffn2[l]), bf(ffn2_gate[l]), bf(ffn2_up[l]), bf(ffn2_down[l]))
    xa_o_b = bf(xa_o[l])
    fnorm = row(final_norm)

    tm = 512
    xp = x_prompt.reshape(bp * seq, D_MODEL)
    mk, mv, mkb, mvb = _memkv(mem_prompt.reshape(bp * N_MEM, D_MODEL), row(mem_norm[l]),
                              bf(xa_k[l]), bf(xa_v[l]), tm=tm)
    x1 = _ffn(xp, *ffn1, tm=tm)
    ya, r, w, k, v, kk, b, g, bonus, zlast = _mix_in(x1, p, tm=tm, sample=False)
    o, s_bd = _scan_prompt(r, w, k, v, kk, b, batch=bp, seq=seq)
    state_p = jnp.stack([s_bd[:, :, :HEAD, :HEAD], s_bd[:, :, HEAD:, HEAD:]],
                        axis=2).reshape(bp, HEADS, HEAD, HEAD)
    x2, attn = _post_mix(x1, ya, o, g, bonus, p, tm=tm,
                         mk=mkb.reshape(bp, N_MEM, D_MODEL), mv=mvb.reshape(bp, N_MEM, D_MODEL))
    y_prompt = _ffn(x2, *ffn2, tm=tm, attn=attn, wo=xa_o_b, final_norm=fnorm)
    tiles_per_seq = seq // tm
    zl_rows = zlast.reshape(bp, tiles_per_seq, 8, RKV_W + LORA_W)[:, -1, 0]
    shift_p = _unpad_shift(zl_rows[:, :RKV_W], zl_rows[:, RKV_W:])

    xs = x_sample.reshape(bs, D_MODEL)
    sh = state_shift[l].reshape(bs, B_PROJ)
    x1s = _ffn(xs, *ffn1, tm=bs)
    (ya_s, r_s, w_s, k_s, v_s, kk_s, b_s, g_s, bonus_s, va_s, zm_s, zl_s) = _mix_in(
        x1s, p, tm=bs, sample=True, shift_main=sh[:, :RKV_W], shift_lora=_pad_lora(sh[:, RKV_W:]))
    o_s, state_s = _scan_sample(state_rwkv[l], r_s, w_s, k_s, v_s, kk_s, b_s, bb=8)
    x2s, q_s = _post_mix(x1s, ya_s, o_s, g_s, bonus_s, p, tm=bs)
    attn_s = _from_head_rows(_xa_sample(_head_rows(q_s), _mem_rows(cache_mem_k[l]),
                                        _mem_rows(cache_mem_v[l]), bb=8))
    y_sample = _ffn(x2s, *ffn2, tm=bs, attn=attn_s, wo=xa_o_b, final_norm=fnorm)

    return (y_prompt.reshape(bp, seq, D_MODEL),
            y_sample.reshape(bs, 1, D_MODEL),
            state_p[None],
            shift_p.reshape(1, bp, 1, B_PROJ),
            mk.reshape(1, bp, N_MEM, XA_HEADS, XA_DIM),
            mv.reshape(1, bp, N_MEM, XA_HEADS, XA_DIM),
            state_s[None],
            _unpad_shift(zm_s, zl_s).reshape(1, bs, 1, B_PROJ),
            va_s.reshape(1, bs, 1, A_WIDTH))
```

```python
import functools

import jax
import jax.numpy as jnp
from jax import lax
from jax.experimental import pallas as pl
from jax.experimental.pallas import tpu as pltpu

F32 = jnp.float32
BF16 = jnp.bfloat16

D_MODEL = 1024
SEQ = 2048
A_WIDTH = 512
A_GROUPS = 8
A_GROUP_DIM = 64
CHUNK = 128
B_WIDTH = 512
HEAD = 64
HEADS = 8
PAIRS = HEADS // 2
PAIR_W = 2 * HEAD
DECAY_LORA = 64
AAA_LORA = 64
GATE_LORA = 160
B_PROJ = 3 * B_WIDTH + DECAY_LORA + AAA_LORA + GATE_LORA
MAIN_W = 2 * A_WIDTH + 3 * B_WIDTH
RKV_W = 3 * B_WIDTH
LORA_W = 512
LORA_WD, LORA_AD, LORA_GD = 0, 128, 256
D_FF = 2816
N_MEM = 256
XA_HEADS = 4
XA_DIM = 256
NORM_EPS = 1e-6
LN_EPS = 1e-5
GN_EPS = 64e-5

VMEM_LIMIT = 56 * 1024 * 1024


def _params(n_axes=1):
    return pltpu.CompilerParams(dimension_semantics=("arbitrary",) * n_axes,
                                vmem_limit_bytes=VMEM_LIMIT)


def _const_spec(shape):
    nd = len(shape)
    return pl.BlockSpec(shape, lambda *_: (0,) * nd, pipeline_mode=pl.Buffered(1))


def _rows_spec(tm, width):
    return pl.BlockSpec((tm, width), lambda i: (i, 0))


def _rms(x, g):
    return x * lax.rsqrt(jnp.mean(x * x, axis=-1, keepdims=True) + NORM_EPS) * g


def _dot(a, b):
    return jnp.dot(a.astype(BF16), b, preferred_element_type=F32)


def _seg_sum(x, ones_bd):
    return jnp.dot(x.astype(BF16), ones_bd, preferred_element_type=F32)


def _ffn_kernel(*refs, pre, final):
    it = iter(refs)
    x_ref = next(it)
    if pre:
        attn_ref, wo_ref = next(it), next(it)
    ln_ref, wg_ref, wu_ref, wd_ref = next(it), next(it), next(it), next(it)
    if final:
        fn_ref = next(it)
    o_ref = next(it)
    x = x_ref[...]
    if pre:
        x = x + _dot(attn_ref[...], wo_ref[...])
    xb = _rms(x, ln_ref[...]).astype(BF16)
    g = jnp.dot(xb, wg_ref[...], preferred_element_type=F32)
    u = jnp.dot(xb, wu_ref[...], preferred_element_type=F32)
    h = (g * jax.nn.sigmoid(g) * u).astype(BF16)
    x = x + 0.5 * jnp.dot(h, wd_ref[...], preferred_element_type=F32)
    if final:
        x = _rms(x, fn_ref[...])
    o_ref[...] = x


def _ffn(x, ln, wg, wu, wd, *, tm, attn=None, wo=None, final_norm=None):
    rows = x.shape[0]
    pre = attn is not None
    final = final_norm is not None
    args, specs = [x], [_rows_spec(tm, D_MODEL)]
    if pre:
        args += [attn, wo]
        specs += [_rows_spec(tm, D_MODEL), _const_spec((D_MODEL, D_MODEL))]
    args += [ln, wg, wu, wd]
    specs += [_const_spec((1, D_MODEL)), _const_spec((D_MODEL, D_FF)),
              _const_spec((D_MODEL, D_FF)), _const_spec((D_FF, D_MODEL))]
    if final:
        args.append(final_norm)
        specs.append(_const_spec((1, D_MODEL)))
    return pl.pallas_call(
        functools.partial(_ffn_kernel, pre=pre, final=final),
        grid=(rows // tm,),
        in_specs=specs,
        out_specs=_rows_spec(tm, D_MODEL),
        out_shape=jax.ShapeDtypeStruct((rows, D_MODEL), F32),
        compiler_params=_params(),
        name="ffn",
    )(*args)


def _memkv_kernel(m_ref, g_ref, wk_ref, wv_ref, k_ref, v_ref, kb_ref, vb_ref):
    mb = _rms(m_ref[...], g_ref[...]).astype(BF16)
    k = jnp.dot(mb, wk_ref[...], preferred_element_type=F32)
    v = jnp.dot(mb, wv_ref[...], preferred_element_type=F32)
    k_ref[...] = k
    v_ref[...] = v
    kb_ref[...] = k.astype(BF16)
    vb_ref[...] = v.astype(BF16)


def _memkv(mem, g, wk, wv, *, tm):
    rows = mem.shape[0]
    out = jax.ShapeDtypeStruct((rows, D_MODEL), F32)
    outb = jax.ShapeDtypeStruct((rows, D_MODEL), BF16)
    return pl.pallas_call(
        _memkv_kernel,
        grid=(rows // tm,),
        in_specs=[_rows_spec(tm, D_MODEL), _const_spec((1, D_MODEL)),
                  _const_spec((D_MODEL, D_MODEL)), _const_spec((D_MODEL, D_MODEL))],
        out_specs=[_rows_spec(tm, D_MODEL)] * 4,
        out_shape=[out, out, outb, outb],
        compiler_params=_params(),
        name="memkv",
    )(mem, g, wk, wv)


def _mix_kernel(*refs, sample, tiles_per_seq):
    it = iter(refs)
    x_ref, ln_ref, wmain_ref, wlora_ref = next(it), next(it), next(it), next(it)
    if sample:
        w00_ref, b0_ref, spm_ref, spl_ref = next(it), next(it), next(it), next(it)
    else:
        wcat_ref, bias_ref = next(it), next(it)
    (lng_ref, lnb_ref, mum_ref, mul_ref, w0_ref, w2_ref, a0_ref, a2_ref, g2_ref,
     kk_ref, ka_ref, rk_ref, ones_ref) = [next(it) for _ in range(13)]
    (ya_ref, r_ref, w_ref, k_ref, v_ref, kn_ref, b_ref, g_ref, bonus_ref) = [
        next(it) for _ in range(9)]
    if sample:
        va_ref, zm_ref, zl_ref = next(it), next(it), next(it)
    else:
        zlast_ref, cm_ref, cl_ref = next(it), next(it), next(it)

    xb = _rms(x_ref[...], ln_ref[...]).astype(BF16)
    _mix_rest(jnp.dot(xb, wmain_ref[...], preferred_element_type=F32),
              jnp.dot(xb, wlora_ref[...], preferred_element_type=F32), dict(locals()))


def _mix_rest(zmain, zl, names):
    sample, tiles_per_seq = names["sample"], names["tiles_per_seq"]
    (lng_ref, lnb_ref, mum_ref, mul_ref, w0_ref, w2_ref, a0_ref, a2_ref, g2_ref, kk_ref, ka_ref,
     rk_ref, ones_ref, ya_ref, r_ref, w_ref, k_ref, v_ref, kn_ref, b_ref, g_ref, bonus_ref) = [
        names[n] for n in (
            "lng_ref", "lnb_ref", "mum_ref", "mul_ref", "w0_ref", "w2_ref", "a0_ref", "a2_ref",
            "g2_ref", "kk_ref", "ka_ref", "rk_ref", "ones_ref", "ya_ref", "r_ref", "w_ref",
            "k_ref", "v_ref", "kn_ref", "b_ref", "g_ref", "bonus_ref")]
    if sample:
        w00_ref, b0_ref, spm_ref, spl_ref, va_ref, zm_ref, zl_ref = [
            names[n] for n in ("w00_ref", "b0_ref", "spm_ref", "spl_ref", "va_ref", "zm_ref",
                               "zl_ref")]
    else:
        wcat_ref, bias_ref, zlast_ref, cm_ref, cl_ref = [
            names[n] for n in ("wcat_ref", "bias_ref", "zlast_ref", "cm_ref", "cl_ref")]
    tm = zl.shape[0]

    u = jax.nn.gelu(zmain[:, :A_WIDTH])
    vx = jax.nn.gelu(zmain[:, A_WIDTH:2 * A_WIDTH])
    mu = jnp.mean(vx, axis=-1, keepdims=True)
    var = jnp.mean(jnp.square(vx - mu), axis=-1, keepdims=True)
    va = (vx - mu) * lax.rsqrt(var + LN_EPS) * lng_ref[...] + lnb_ref[...]
    if sample:
        mixed = va * w00_ref[...] + b0_ref[...]
        ya_ref[...] = (u * mixed).astype(BF16)
        va_ref[...] = va
    else:
        vab = va.astype(BF16)
        first = lax.broadcasted_iota(jnp.int32, (CHUNK, 2 * A_GROUP_DIM), 1) < A_GROUP_DIM
        for c in range(tm // CHUNK):
            rows = slice(c * CHUNK, (c + 1) * CHUNK)
            for gp in range(A_GROUPS // 2):
                lanes = slice(gp * 2 * A_GROUP_DIM, (gp + 1) * 2 * A_GROUP_DIM)
                vc = vab[rows, lanes]
                zero = jnp.zeros_like(vc)
                rhs = jnp.concatenate([jnp.where(first, vc, zero), jnp.where(first, zero, vc)],
                                      axis=0)
                mixed = jnp.dot(wcat_ref[:, gp * 2 * CHUNK:(gp + 1) * 2 * CHUNK], rhs,
                                preferred_element_type=F32) + bias_ref[:, lanes]
                ya_ref[rows, lanes] = (u[rows, lanes] * mixed).astype(BF16)

    zbm = zmain[:, 2 * A_WIDTH:]
    if sample:
        zpm, zpl = spm_ref[...], spl_ref[...]
        zm_ref[...] = zbm
        zl_ref[...] = zl
    else:
        tile = pl.program_id(0)

        @pl.when(tile % tiles_per_seq == 0)
        def _():
            cm_ref[...] = jnp.zeros_like(cm_ref)
            cl_ref[...] = jnp.zeros_like(cl_ref)

        first_m = lax.broadcasted_iota(jnp.int32, zbm.shape, 0) == 0
        first_l = lax.broadcasted_iota(jnp.int32, zl.shape, 0) == 0
        zpm = jnp.where(first_m, cm_ref[0:1, :], pltpu.roll(zbm, 1, axis=0))
        zpl = jnp.where(first_l, cl_ref[0:1, :], pltpu.roll(zl, 1, axis=0))
        cm_ref[0:1, :] = zbm[tm - 1:tm, :]
        cl_ref[0:1, :] = zl[tm - 1:tm, :]
        zlast_ref[:, :RKV_W] = jnp.broadcast_to(zbm[tm - 1:tm, :], (8, RKV_W))
        zlast_ref[:, RKV_W:] = jnp.broadcast_to(zl[tm - 1:tm, :], (8, LORA_W))
    zsm = zbm + (zpm - zbm) * mum_ref[...]
    zsl = zl + (zpl - zl) * mul_ref[...]
    r = zsm[:, :B_WIDTH]
    k = zsm[:, B_WIDTH:2 * B_WIDTH]
    v = zsm[:, 2 * B_WIDTH:]
    wd = zsl[:, LORA_WD:LORA_AD]
    ad = zsl[:, LORA_AD:LORA_GD]
    gd = zsl[:, LORA_GD:]
    y = w0_ref[...] + _dot(jnp.tanh(wd), w2_ref[...])
    w_log = jnp.minimum(y, 0.0) - jnp.log1p(jnp.exp(-jnp.abs(y))) - 0.5
    log_decay = -jnp.exp(w_log)
    a = jax.nn.sigmoid(a0_ref[...] + _dot(ad, a2_ref[...]))
    gate = _dot(jax.nn.sigmoid(gd), g2_ref[...])
    ones_bd = ones_ref[...]
    kk = k * kk_ref[...]
    kk = kk * lax.rsqrt(jnp.maximum(_seg_sum(kk * kk, ones_bd), 1e-24))
    k2 = k * (1.0 + (a - 1.0) * ka_ref[...])
    r_ref[...] = r
    w_ref[...] = jnp.exp(log_decay) if sample else log_decay
    k_ref[...] = k2
    v_ref[...] = v
    kn_ref[...] = kk
    b_ref[...] = kk * a
    g_ref[...] = gate
    bonus_ref[...] = _seg_sum(r * k2 * rk_ref[...], ones_bd) * v


def _mix_in(x, p, *, tm, sample, shift_main=None, shift_lora=None):
    rows = x.shape[0]
    n_tiles = rows // tm
    args = [x, p["ln_mix"], p["w_main"], p["w_lora"]]
    specs = [_rows_spec(tm, D_MODEL), _const_spec((1, D_MODEL)),
             _const_spec((D_MODEL, MAIN_W)), _const_spec((D_MODEL, LORA_W))]
    if sample:
        args += [p["sgu_w00"], p["sgu_b0"], shift_main, shift_lora]
        specs += [_const_spec((1, A_WIDTH)), _const_spec((1, A_WIDTH)),
                  _rows_spec(tm, RKV_W), _rows_spec(tm, LORA_W)]
    else:
        args += [p["sgu_wcat"], p["sgu_bias"]]
        specs += [_const_spec((CHUNK, A_GROUPS * CHUNK)), _const_spec((CHUNK, A_WIDTH))]
    args += [p["sgu_ln_g"], p["sgu_ln_b"], p["mu_main"], p["mu_lora"], p["w0"], p["w2"],
             p["a0"], p["a2"], p["g2"], p["k_k"], p["k_a"], p["r_k"], p["ones_bd"]]
    specs += [_const_spec((1, A_WIDTH)), _const_spec((1, A_WIDTH)), _const_spec((1, RKV_W)),
              _const_spec((1, LORA_W)), _const_spec((1, B_WIDTH)),
              _const_spec((LORA_AD - LORA_WD, B_WIDTH)), _const_spec((1, B_WIDTH)),
              _const_spec((LORA_GD - LORA_AD, B_WIDTH)), _const_spec((LORA_W - LORA_GD, B_WIDTH)),
              _const_spec((1, B_WIDTH)), _const_spec((1, B_WIDTH)), _const_spec((1, B_WIDTH)),
              _const_spec((B_WIDTH, B_WIDTH))]
    wide = jax.ShapeDtypeStruct((rows, B_WIDTH), F32)
    out_shape = [jax.ShapeDtypeStruct((rows, A_WIDTH), BF16)] + [wide] * 8
    out_specs = [_rows_spec(tm, B_WIDTH)] * 9
    scratch = []
    if sample:
        out_shape += [wide, jax.ShapeDtypeStruct((rows, RKV_W), F32),
                      jax.ShapeDtypeStruct((rows, LORA_W), F32)]
        out_specs += [_rows_spec(tm, A_WIDTH), _rows_spec(tm, RKV_W), _rows_spec(tm, LORA_W)]
    else:
        out_shape += [jax.ShapeDtypeStruct((n_tiles * 8, RKV_W + LORA_W), F32)]
        out_specs += [pl.BlockSpec((8, RKV_W + LORA_W), lambda i: (i, 0))]
        scratch = [pltpu.VMEM((8, RKV_W), F32), pltpu.VMEM((8, LORA_W), F32)]
    return pl.pallas_call(
        functools.partial(_mix_kernel, sample=sample, tiles_per_seq=max(SEQ // tm, 1)),
        grid=(n_tiles,),
        in_specs=specs,
        out_specs=out_specs,
        out_shape=out_shape,
        scratch_shapes=scratch,
        compiler_params=_params(),
        name="mix_in",
    )(*args)


def _each(f, *lists):
    return [f(*xs) for xs in zip(*lists)]


SCAN_C = 64


def _mm(a, b):
    return jnp.dot(a.astype(BF16), b.astype(BF16), preferred_element_type=F32)


def _mm_nt(a, b):
    return lax.dot_general(a.astype(BF16), b.astype(BF16), (((1,), (1,)), ((), ())),
                           preferred_element_type=F32)


def _mm_tn(a, b):
    return lax.dot_general(a.astype(BF16), b.astype(BF16), (((0,), (0,)), ((), ())),
                           preferred_element_type=F32)


def _cumsum_rows(x):
    n = x.shape[0]
    row = lax.broadcasted_iota(jnp.int32, x.shape, 0)
    s = 1
    while s < n:
        x = x + jnp.where(row >= s, pltpu.roll(x, s, axis=0), 0.0)
        s *= 2
    return x


INV_BASE = 8


def _unit_lower_inverse(ns, row, col):
    f0 = jnp.zeros((), F32)
    same = lambda s: (row // s) == (col // s)
    eye = jnp.where(row == col, 1.0, f0)
    ps = _each(lambda n: jnp.where(same(INV_BASE), n, f0), ns)
    ts = _each(lambda p: eye + p, ps)
    s = 2
    while s < INV_BASE:
        ps = _each(lambda p: _mm(p, p), ps)
        ts = _each(lambda t, p: t + _mm(t, p), ts, ps)
        s *= 2
    s = INV_BASE
    while s < SCAN_C:
        level = same(2 * s) & jnp.logical_not(same(s))
        ws = _each(lambda n, t: _mm(jnp.where(level, n, f0), t), ns, ts)
        ts = _each(lambda t, w: t + _mm(t, w), ts, ws)
        s *= 2
    return ts


def _chunk_pairs(s0s, rs, lws, ks, vs, kks, bs):
    c = SCAN_C
    f0 = jnp.zeros((), F32)
    row = lax.broadcasted_iota(jnp.int32, (2 * c, PAIR_W), 0)
    col = lax.broadcasted_iota(jnp.int32, (2 * c, PAIR_W), 1)
    top, lft = row < c, col < HEAD
    same_head = top == lft
    strict = (row % c) > (col % HEAD)
    row_c = lax.broadcasted_iota(jnp.int32, (c, PAIR_W), 0)
    col_c = lax.broadcasted_iota(jnp.int32, (c, PAIR_W), 1)
    lft_c = col_c < HEAD
    strict_c = row_c > (col_c % HEAD)
    incl_c = row_c >= (col_c % HEAD)

    def prep(r, lw, k, v, kk, b):
        cum = _cumsum_rows(lw)
        end = cum[c - 1:c, :]
        a_t = -kk * jnp.exp(cum - lw)
        r_t = r * jnp.exp(cum)
        einv = jnp.exp(-cum)
        eend = jnp.exp(end - cum)
        return dict(
            x0=jnp.concatenate([a_t, r_t], axis=0), x1=jnp.concatenate([r_t, a_t], axis=0),
            bk=jnp.concatenate([b * einv, k * einv], axis=0),
            kb=jnp.concatenate([k * einv, b * einv], axis=0),
            bk_e=jnp.concatenate([b * eend, k * eend], axis=0),
            w_end=jnp.exp(end), v=v,
            v_l=jnp.where(lft_c, v, f0), v_r=jnp.where(lft_c, f0, v))

    fs = _each(prep, rs, lws, ks, vs, kks, bs)
    g0s = _each(lambda f: _mm_nt(jnp.where(lft, f["x0"], f0), f["bk"]), fs)
    g1s = _each(lambda f: _mm_nt(jnp.where(lft, f0, f["x1"]), f["kb"]), fs)
    pqs = _each(lambda f, s0: _mm_nt(f["x0"], s0), fs, s0s)

    def rhs(f, g0, g1, pq):
        ak = jnp.where(strict_c, jnp.where(lft_c, g1[c:], g0[:c]), f0)
        x = pq[:c] + _mm(ak, jnp.concatenate([f["v_r"], f["v_l"]], axis=0))
        return jnp.concatenate([jnp.where(lft_c, x, f0), jnp.where(lft_c, f0, x)], axis=0)

    ys = _each(rhs, fs, g0s, g1s, pqs)
    ns = _each(lambda g0, g1: jnp.where(strict & same_head, jnp.where(top, g0, g1), f0),
               g0s, g1s)
    ts = _unit_lower_inverse(ns, row, col)
    ys = _each(_mm, ts, ys)

    def out(f, g0, g1, pq, y):
        lhs = jnp.concatenate([jnp.where(incl_c, g0[c:], f0), jnp.where(incl_c, g1[:c], f0)],
                              axis=1)
        return pq[c:] + _mm(lhs, jnp.concatenate([y[:c], f["v_l"], f["v_r"], y[c:]], axis=0))

    def state(f, s0, y):
        upd = _mm_tn(jnp.concatenate([y[:c] + y[c:], f["v"]], axis=0), f["bk_e"])
        return s0 * f["w_end"] + jnp.where(same_head, upd, f0)

    return _each(out, fs, g0s, g1s, pqs, ys), _each(state, fs, s0s, ys)


SCAN_BATCHES = 4


def _scan_prompt_kernel(r_ref, w_ref, k_ref, v_ref, kk_ref, b_ref, o_ref, sout_ref, s_ref):
    t_blk = pl.program_id(1)

    @pl.when(t_blk == 0)
    def _():
        s_ref[...] = jnp.zeros_like(s_ref)

    chains = [(j, p) for j in range(SCAN_BATCHES) for p in range(PAIRS)]
    lanes = lambda p: slice(p * PAIR_W, (p + 1) * PAIR_W)
    take = lambda ref: [ref[j, :, lanes(p)] for j, p in chains]
    os_, ss = _chunk_pairs([s_ref[j, p] for j, p in chains], take(r_ref), take(w_ref),
                           take(k_ref), take(v_ref), take(kk_ref), take(b_ref))
    for (j, p), o, s_new in zip(chains, os_, ss):
        o_ref[j, :, lanes(p)] = o
        s_ref[j, p] = s_new

    @pl.when(t_blk == pl.num_programs(1) - 1)
    def _():
        sout_ref[...] = s_ref[...]


def _scan_prompt(r, lw, k, v, kk, b, *, batch, seq):
    n_t = seq // SCAN_C
    nb = SCAN_BATCHES
    spec = pl.BlockSpec((nb, SCAN_C, B_WIDTH), lambda bi, ti: (bi, ti, 0))
    sspec = pl.BlockSpec((nb, PAIRS, PAIR_W, PAIR_W), lambda bi, ti: (bi, 0, 0, 0))
    o, s = pl.pallas_call(
        _scan_prompt_kernel,
        grid=(batch // nb, n_t),
        in_specs=[spec] * 6,
        out_specs=[spec, sspec],
        out_shape=[jax.ShapeDtypeStruct((batch, seq, B_WIDTH), F32),
                   jax.ShapeDtypeStruct((batch, PAIRS, PAIR_W, PAIR_W), F32)],
        scratch_shapes=[pltpu.VMEM((nb, PAIRS, PAIR_W, PAIR_W), F32)],
        compiler_params=_params(2),
        name="scan_prompt",
    )(*[x.reshape(batch, seq, B_WIDTH) for x in (r, lw, k, v, kk, b)])
    return o.reshape(batch * seq, B_WIDTH), s


def _scan_sample_kernel(s_ref, r_ref, w_ref, k_ref, v_ref, kk_ref, b_ref, o_ref, sout_ref,
                        t_ref, ot_ref):
    h = pl.program_id(0)

    @pl.when(h == 0)
    def _():
        for i, ref in enumerate((r_ref, w_ref, k_ref, v_ref, kk_ref, b_ref)):
            t_ref[i] = ref[...].T

    base = pl.multiple_of(h * HEAD, HEAD)
    keys = pl.ds(base, HEAD)
    r, w, k, kk, b = [t_ref[i, keys, :] for i in (0, 1, 2, 4, 5)]

    def body(v8, carry):
        rows = pl.ds(pl.multiple_of(base + v8 * 8, 8), 8)
        v_rows = t_ref[3, rows, :]
        outs = []
        for j in range(8):
            vi = v8 * 8 + j
            s = s_ref[0, vi]
            sa = jnp.sum(s * kk, axis=0, keepdims=True)
            s = s * w - sa * b + v_rows[j:j + 1, :] * k
            sout_ref[0, vi] = s
            outs.append(jnp.sum(s * r, axis=0, keepdims=True))
        ot_ref[rows, :] = jnp.concatenate(outs, axis=0)
        return carry

    lax.fori_loop(0, HEAD // 8, body, 0)

    @pl.when(h == pl.num_programs(0) - 1)
    def _():
        o_ref[...] = ot_ref[...].T


def _scan_sample(state_t, r, w, k, v, kk, b):
    rows = r.shape[0]
    sspec = pl.BlockSpec((1, HEAD, HEAD, rows), lambda h: (h, 0, 0, 0))
    spec = _const_spec((rows, B_WIDTH))
    return pl.pallas_call(
        _scan_sample_kernel,
        grid=(HEADS,),
        in_specs=[sspec] + [spec] * 6,
        out_specs=[pl.BlockSpec((rows, B_WIDTH), lambda h: (0, 0)), sspec],
        out_shape=[jax.ShapeDtypeStruct((rows, B_WIDTH), F32),
                   jax.ShapeDtypeStruct(state_t.shape, F32)],
        scratch_shapes=[pltpu.VMEM((6, B_WIDTH, rows), F32), pltpu.VMEM((B_WIDTH, rows), F32)],
        compiler_params=_params(),
        name="scan_sample",
    )(state_t, r, w, k, v, kk, b)


def _softmax_rows(s):
    e = jnp.exp(s - jnp.max(s, axis=-1, keepdims=True))
    return e / jnp.sum(e, axis=-1, keepdims=True)


def _post_kernel(*refs, attend):
    it = iter(refs)
    (x_ref, ya_ref, o_ref, g_ref, bonus_ref, gng_ref, gnb_ref, ones_ref, woa_ref, wob_ref,
     lnx_ref, wq_ref) = [next(it) for _ in range(12)]
    if attend:
        mk_ref, mv_ref = next(it), next(it)
    x2_ref, out_ref = next(it), next(it)

    ones_bd = ones_ref[...]
    o = o_ref[...]
    mu = _seg_sum(o, ones_bd) * (1.0 / HEAD)
    d = o - mu
    var = _seg_sum(d * d, ones_bd) * (1.0 / HEAD)
    on = d * lax.rsqrt(var + GN_EPS) * gng_ref[...] + gnb_ref[...]
    yb = (on + bonus_ref[...]) * g_ref[...]
    x2 = x_ref[...] + jnp.dot(ya_ref[...], woa_ref[...], preferred_element_type=F32) \
        + _dot(yb, wob_ref[...])
    x2_ref[...] = x2
    q = _dot(_rms(x2, lnx_ref[...]), wq_ref[...])
    if not attend:
        out_ref[...] = q
        return
    qb = q.astype(BF16)
    for h in range(XA_HEADS):
        sl = slice(h * XA_DIM, (h + 1) * XA_DIM)
        s = lax.dot_general(qb[:, sl], mk_ref[0, :, sl], (((1,), (1,)), ((), ())),
                            preferred_element_type=F32) * (XA_DIM ** -0.5)
        p = _softmax_rows(s)
        out_ref[:, sl] = _dot(p, mv_ref[0, :, sl]).astype(BF16)


def _post_mix(x, ya, o, g, bonus, p, *, tm, mk=None, mv=None):
    rows = x.shape[0]
    attend = mk is not None
    args = [x, ya, o, g, bonus, p["gn_g"], p["gn_b"], p["ones_bd"], p["w_out_a"], p["w_out_b"],
            p["ln_xattn"], p["xa_q"]]
    specs = [_rows_spec(tm, D_MODEL)] + [_rows_spec(tm, B_WIDTH)] * 4 + [
        _const_spec((1, B_WIDTH)), _const_spec((1, B_WIDTH)), _const_spec((B_WIDTH, B_WIDTH)),
        _const_spec((A_WIDTH, D_MODEL)), _const_spec((B_WIDTH, D_MODEL)),
        _const_spec((1, D_MODEL)), _const_spec((D_MODEL, D_MODEL))]
    if attend:
        tiles_per_seq = SEQ // tm
        mspec = pl.BlockSpec((1, N_MEM, D_MODEL), lambda i: (i // tiles_per_seq, 0, 0))
        args += [mk, mv]
        specs += [mspec, mspec]
    return pl.pallas_call(
        functools.partial(_post_kernel, attend=attend),
        grid=(rows // tm,),
        in_specs=specs,
        out_specs=[_rows_spec(tm, D_MODEL)] * 2,
        out_shape=[jax.ShapeDtypeStruct((rows, D_MODEL), F32),
                   jax.ShapeDtypeStruct((rows, D_MODEL), BF16 if attend else F32)],
        compiler_params=_params(),
        name="post_mix",
    )(*args)


MEM_ROWS = XA_HEADS * (XA_DIM // 128)


def _lane_allreduce(x, op):
    shift = MEM_ROWS
    while shift < 128:
        x = op(x, pltpu.roll(x, shift, axis=1))
        shift *= 2
    return x


def _xa_sample_kernel(q_ref, k_ref, v_ref, o_ref):
    f0 = jnp.zeros((), F32)
    n_blk = N_MEM * MEM_ROWS // 128
    sub = lax.broadcasted_iota(jnp.int32, (MEM_ROWS, 128), 0)
    lane = lax.broadcasted_iota(jnp.int32, (MEM_ROWS, 128), 1)
    diag = sub == (lane % MEM_ROWS)
    li = lax.broadcasted_iota(jnp.int32, (128, 128), 0)
    lj = lax.broadcasted_iota(jnp.int32, (128, 128), 1)
    comb = jnp.where((li // MEM_ROWS == lj // MEM_ROWS) & (li % XA_HEADS == lj % XA_HEADS),
                     1.0, 0.0).astype(BF16)
    samples = list(range(q_ref.shape[0]))
    scs = [_mm_nt(q_ref[j], k_ref[j]) for j in samples]

    def partial(sc):
        return jnp.concatenate(
            [jnp.sum(jnp.where(diag, sc[:, t * 128:(t + 1) * 128], f0), axis=0, keepdims=True)
             for t in range(n_blk)], axis=0)

    def scores(part):
        hi = part.astype(BF16)
        lo = (part - hi.astype(F32)).astype(BF16)
        return (jnp.dot(hi, comb, preferred_element_type=F32)
                + jnp.dot(lo, comb, preferred_element_type=F32)) * (XA_DIM ** -0.5)

    def softmax(s):
        mx = _lane_allreduce(jnp.broadcast_to(jnp.max(s, axis=0, keepdims=True), (MEM_ROWS, 128)),
                             jnp.maximum)
        e = jnp.exp(s - mx[0:1, :])
        den = _lane_allreduce(jnp.broadcast_to(jnp.sum(e, axis=0, keepdims=True), (MEM_ROWS, 128)),
                              jnp.add)
        p = e / den[0:1, :]
        return jnp.concatenate(
            [jnp.where(diag, jnp.broadcast_to(p[t:t + 1, :], (MEM_ROWS, 128)), f0)
             for t in range(n_blk)], axis=1)

    p_rows = _each(softmax, _each(scores, _each(partial, scs)))
    for j, p in zip(samples, p_rows):
        o_ref[j] = _mm(p, v_ref[j])


def _xa_sample(q, mk, mv, *, bb):
    rows = q.shape[0]
    qspec = pl.BlockSpec((bb, MEM_ROWS, 128), lambda i: (i, 0, 0))
    mspec = pl.BlockSpec((bb, N_MEM * MEM_ROWS, 128), lambda i: (i, 0, 0))
    return pl.pallas_call(
        _xa_sample_kernel,
        grid=(rows // bb,),
        in_specs=[qspec, mspec, mspec],
        out_specs=qspec,
        out_shape=jax.ShapeDtypeStruct((rows, MEM_ROWS, 128), F32),
        compiler_params=_params(),
        name="xa_sample",
    )(q, mk, mv)


def _pad_lora(x):
    wd = x[..., :DECAY_LORA]
    ad = x[..., DECAY_LORA:DECAY_LORA + AAA_LORA]
    gd = x[..., DECAY_LORA + AAA_LORA:]
    z = lambda n: jnp.zeros(x.shape[:-1] + (n,), x.dtype)
    return jnp.concatenate([wd, z(LORA_AD - DECAY_LORA), ad, z(LORA_GD - LORA_AD - AAA_LORA),
                            gd, z(LORA_W - LORA_GD - GATE_LORA)], axis=-1)


def _unpad_shift(zm, zl):
    return jnp.concatenate([zm, zl[..., LORA_WD:LORA_WD + DECAY_LORA],
                            zl[..., LORA_AD:LORA_AD + AAA_LORA],
                            zl[..., LORA_GD:LORA_GD + GATE_LORA]], axis=-1)


def _pad_rows(w, n):
    return jnp.pad(w, ((0, n - w.shape[0]), (0, 0)))


def _mem_rows(x):
    b = x.shape[0]
    return x.reshape(b, N_MEM, XA_HEADS, XA_DIM // 128, 128).transpose(0, 1, 3, 2, 4).reshape(
        b, N_MEM * MEM_ROWS, 128)


def _head_rows(x):
    b = x.shape[0]
    return x.reshape(b, XA_HEADS, XA_DIM // 128, 128).transpose(0, 2, 1, 3).reshape(b, MEM_ROWS, 128)


def _from_head_rows(x):
    b = x.shape[0]
    return x.reshape(b, XA_DIM // 128, XA_HEADS, 128).transpose(0, 2, 1, 3).reshape(b, D_MODEL)


def kernel(x_prompt, x_sample, state_rwkv, state_shift, cache_mem_k, cache_mem_v, mem_prompt, ln_ffn1, ffn1_gate, ffn1_up, ffn1_down, ln_mix, w_in, w_out, sgu_w, sgu_b, sgu_ln_g, sgu_ln_b, rwkv_mu, rwkv_w0, rwkv_w2, rwkv_a0, rwkv_a2, rwkv_g2, rwkv_k_k, rwkv_k_a, rwkv_r_k, rwkv_gn_g, rwkv_gn_b, ln_xattn, mem_norm, xa_q, xa_k, xa_v, xa_o, ln_ffn2, ffn2_gate, ffn2_up, ffn2_down, final_norm):
    assert ln_ffn1.shape[0] == 1, "single layer"
    bp, seq, _ = x_prompt.shape
    bs = x_sample.shape[0]
    row = lambda a: a.reshape(1, -1).astype(F32)
    bf = lambda a: a.astype(BF16)
    l = 0
    head_id = jnp.arange(B_WIDTH) // HEAD
    tril = jnp.tril(jnp.ones((CHUNK, CHUNK), dtype=bool))
    wmask = jnp.where(tril[None], sgu_w[l], 0)
    p = {
        "ln_mix": row(ln_mix[l]),
        "w_main": bf(w_in[l][:, :MAIN_W]),
        "w_lora": bf(_pad_lora(w_in[l][:, MAIN_W:])),
        "sgu_wcat": bf(wmask.transpose(1, 0, 2).reshape(CHUNK, A_GROUPS * CHUNK)),
        "sgu_bias": jnp.repeat(sgu_b[l].T, A_GROUP_DIM, axis=1),
        "sgu_w00": row(jnp.repeat(sgu_w[l][:, 0, 0], A_GROUP_DIM)),
        "sgu_b0": row(jnp.repeat(sgu_b[l][:, 0], A_GROUP_DIM)),
        "sgu_ln_g": row(sgu_ln_g[l]), "sgu_ln_b": row(sgu_ln_b[l]),
        "mu_main": row(rwkv_mu[l][:RKV_W]),
        "mu_lora": row(_pad_lora(rwkv_mu[l][RKV_W:])),
        "w0": row(rwkv_w0[l]), "w2": bf(_pad_rows(rwkv_w2[l], LORA_AD - LORA_WD)),
        "a0": row(rwkv_a0[l]), "a2": bf(_pad_rows(rwkv_a2[l], LORA_GD - LORA_AD)),
        "g2": bf(_pad_rows(rwkv_g2[l], LORA_W - LORA_GD)),
        "k_k": row(rwkv_k_k[l]), "k_a": row(rwkv_k_a[l]), "r_k": row(rwkv_r_k[l]),
        "ones_bd": (head_id[:, None] == head_id[None, :]).astype(BF16),
        "gn_g": row(rwkv_gn_g[l]), "gn_b": row(rwkv_gn_b[l]),
        "w_out_a": bf(w_out[l][:A_WIDTH]), "w_out_b": bf(w_out[l][A_WIDTH:]),
        "ln_xattn": row(ln_xattn[l]), "xa_q": bf(xa_q[l]),
    }
    ffn1 = (row(ln_ffn1[l]), bf(ffn1_gate[l]), bf(ffn1_up[l]), bf(ffn1_down[l]))
    ffn2 = (row(ln_ffn2[l]), bf(ffn2_gate[l]), bf(ffn2_up[l]), bf(ffn2_down[l]))
    xa_o_b = bf(xa_o[l])
    fnorm = row(final_norm)

    tm = 512
    xp = x_prompt.reshape(bp * seq, D_MODEL)
    mk, mv, mkb, mvb = _memkv(mem_prompt.reshape(bp * N_MEM, D_MODEL), row(mem_norm[l]),
                              bf(xa_k[l]), bf(xa_v[l]), tm=tm)
    x1 = _ffn(xp, *ffn1, tm=tm)
    ya, r, w, k, v, kk, b, g, bonus, zlast = _mix_in(x1, p, tm=tm, sample=False)
    o, s_bd = _scan_prompt(r, w, k, v, kk, b, batch=bp, seq=seq)
    state_p = jnp.stack([s_bd[:, :, :HEAD, :HEAD], s_bd[:, :, HEAD:, HEAD:]],
                        axis=2).reshape(bp, HEADS, HEAD, HEAD)
    x2, attn = _post_mix(x1, ya, o, g, bonus, p, tm=tm,
                         mk=mkb.reshape(bp, N_MEM, D_MODEL), mv=mvb.reshape(bp, N_MEM, D_MODEL))
    y_prompt = _ffn(x2, *ffn2, tm=tm, attn=attn, wo=xa_o_b, final_norm=fnorm)
    tiles_per_seq = seq // tm
    zl_rows = zlast.reshape(bp, tiles_per_seq, 8, RKV_W + LORA_W)[:, -1, 0]
    shift_p = _unpad_shift(zl_rows[:, :RKV_W], zl_rows[:, RKV_W:])

    xs = x_sample.reshape(bs, D_MODEL)
    sh = state_shift[l].reshape(bs, B_PROJ)
    x1s = _ffn(xs, *ffn1, tm=bs)
    (ya_s, r_s, w_s, k_s, v_s, kk_s, b_s, g_s, bonus_s, va_s, zm_s, zl_s) = _mix_in(
        x1s, p, tm=bs, sample=True, shift_main=sh[:, :RKV_W], shift_lora=_pad_lora(sh[:, RKV_W:]))
    o_s, state_t = _scan_sample(jnp.transpose(state_rwkv[l], (1, 2, 3, 0)),
                                r_s, w_s, k_s, v_s, kk_s, b_s)
    state_s = jnp.transpose(state_t, (3, 0, 1, 2))
    x2s, q_s = _post_mix(x1s, ya_s, o_s, g_s, bonus_s, p, tm=bs)
    attn_s = _from_head_rows(_xa_sample(_head_rows(q_s), _mem_rows(cache_mem_k[l]),
                                        _mem_rows(cache_mem_v[l]), bb=8))
    y_sample = _ffn(x2s, *ffn2, tm=bs, attn=attn_s, wo=xa_o_b, final_norm=fnorm)

    return (y_prompt.reshape(bp, seq, D_MODEL),
            y_sample.reshape(bs, 1, D_MODEL),
            state_p[None],
            shift_p.reshape(1, bp, 1, B_PROJ),
            mk.reshape(1, bp, N_MEM, XA_HEADS, XA_DIM),
            mv.reshape(1, bp, N_MEM, XA_HEADS, XA_DIM),
            state_s[None],
            _unpad_shift(zm_s, zl_s).reshape(1, bs, 1, B_PROJ),
            va_s.reshape(1, bs, 1, A_WIDTH))
```

```python
import functools

import jax
import jax.numpy as jnp
from jax import lax
from jax.experimental import pallas as pl
from jax.experimental.pallas import tpu as pltpu

F32 = jnp.float32
BF16 = jnp.bfloat16

D_MODEL = 1024
SEQ = 2048
A_WIDTH = 512
A_GROUPS = 8
A_GROUP_DIM = 64
CHUNK = 128
B_WIDTH = 512
HEAD = 64
HEADS = 8
PAIRS = HEADS // 2
PAIR_W = 2 * HEAD
DECAY_LORA = 64
AAA_LORA = 64
GATE_LORA = 160
B_PROJ = 3 * B_WIDTH + DECAY_LORA + AAA_LORA + GATE_LORA
MAIN_W = 2 * A_WIDTH + 3 * B_WIDTH
RKV_W = 3 * B_WIDTH
LORA_W = 512
LORA_WD, LORA_AD, LORA_GD = 0, 128, 256
D_FF = 2816
N_MEM = 256
XA_HEADS = 4
XA_DIM = 256
NORM_EPS = 1e-6
LN_EPS = 1e-5
GN_EPS = 64e-5

VMEM_LIMIT = 56 * 1024 * 1024


def _params(n_axes=1):
    return pltpu.CompilerParams(dimension_semantics=("arbitrary",) * n_axes,
                                vmem_limit_bytes=VMEM_LIMIT)


def _const_spec(shape):
    nd = len(shape)
    return pl.BlockSpec(shape, lambda *_: (0,) * nd, pipeline_mode=pl.Buffered(1))


def _rows_spec(tm, width):
    return pl.BlockSpec((tm, width), lambda i: (i, 0))


def _rms(x, g):
    return x * lax.rsqrt(jnp.mean(x * x, axis=-1, keepdims=True) + NORM_EPS) * g


def _dot(a, b):
    return jnp.dot(a.astype(BF16), b, preferred_element_type=F32)


def _seg_sum(x, ones_bd):
    return jnp.dot(x.astype(BF16), ones_bd, preferred_element_type=F32)


def _ffn_kernel(*refs, pre, final):
    it = iter(refs)
    x_ref = next(it)
    if pre:
        attn_ref, wo_ref = next(it), next(it)
    ln_ref, wg_ref, wu_ref, wd_ref = next(it), next(it), next(it), next(it)
    if final:
        fn_ref = next(it)
    o_ref = next(it)
    x = x_ref[...]
    if pre:
        x = x + _dot(attn_ref[...], wo_ref[...])
    xb = _rms(x, ln_ref[...]).astype(BF16)
    g = jnp.dot(xb, wg_ref[...], preferred_element_type=F32)
    u = jnp.dot(xb, wu_ref[...], preferred_element_type=F32)
    h = (g * jax.nn.sigmoid(g) * u).astype(BF16)
    x = x + 0.5 * jnp.dot(h, wd_ref[...], preferred_element_type=F32)
    if final:
        x = _rms(x, fn_ref[...])
    o_ref[...] = x


def _ffn(x, ln, wg, wu, wd, *, tm, attn=None, wo=None, final_norm=None):
    rows = x.shape[0]
    pre = attn is not None
    final = final_norm is not None
    args, specs = [x], [_rows_spec(tm, D_MODEL)]
    if pre:
        args += [attn, wo]
        specs += [_rows_spec(tm, D_MODEL), _const_spec((D_MODEL, D_MODEL))]
    args += [ln, wg, wu, wd]
    specs += [_const_spec((1, D_MODEL)), _const_spec((D_MODEL, D_FF)),
              _const_spec((D_MODEL, D_FF)), _const_spec((D_FF, D_MODEL))]
    if final:
        args.append(final_norm)
        specs.append(_const_spec((1, D_MODEL)))
    return pl.pallas_call(
        functools.partial(_ffn_kernel, pre=pre, final=final),
        grid=(rows // tm,),
        in_specs=specs,
        out_specs=_rows_spec(tm, D_MODEL),
        out_shape=jax.ShapeDtypeStruct((rows, D_MODEL), F32),
        compiler_params=_params(),
        name="ffn",
    )(*args)


def _memkv_kernel(m_ref, g_ref, wk_ref, wv_ref, k_ref, v_ref, kb_ref, vb_ref):
    mb = _rms(m_ref[...], g_ref[...]).astype(BF16)
    k = jnp.dot(mb, wk_ref[...], preferred_element_type=F32)
    v = jnp.dot(mb, wv_ref[...], preferred_element_type=F32)
    k_ref[...] = k
    v_ref[...] = v
    kb_ref[...] = k.astype(BF16)
    vb_ref[...] = v.astype(BF16)


def _memkv(mem, g, wk, wv, *, tm):
    rows = mem.shape[0]
    out = jax.ShapeDtypeStruct((rows, D_MODEL), F32)
    outb = jax.ShapeDtypeStruct((rows, D_MODEL), BF16)
    return pl.pallas_call(
        _memkv_kernel,
        grid=(rows // tm,),
        in_specs=[_rows_spec(tm, D_MODEL), _const_spec((1, D_MODEL)),
                  _const_spec((D_MODEL, D_MODEL)), _const_spec((D_MODEL, D_MODEL))],
        out_specs=[_rows_spec(tm, D_MODEL)] * 4,
        out_shape=[out, out, outb, outb],
        compiler_params=_params(),
        name="memkv",
    )(mem, g, wk, wv)


def _mix_kernel(*refs, sample, tiles_per_seq):
    it = iter(refs)
    x_ref, ln_ref, wmain_ref, wlora_ref = next(it), next(it), next(it), next(it)
    if sample:
        w00_ref, b0_ref, spm_ref, spl_ref = next(it), next(it), next(it), next(it)
    else:
        wcat_ref, bias_ref = next(it), next(it)
    (lng_ref, lnb_ref, mum_ref, mul_ref, w0_ref, w2_ref, a0_ref, a2_ref, g2_ref,
     kk_ref, ka_ref, rk_ref, ones_ref) = [next(it) for _ in range(13)]
    (ya_ref, r_ref, w_ref, k_ref, v_ref, kn_ref, b_ref, g_ref, bonus_ref) = [
        next(it) for _ in range(9)]
    if sample:
        va_ref, zm_ref, zl_ref = next(it), next(it), next(it)
    else:
        zlast_ref, cm_ref, cl_ref = next(it), next(it), next(it)

    xb = _rms(x_ref[...], ln_ref[...]).astype(BF16)
    _mix_rest(jnp.dot(xb, wmain_ref[...], preferred_element_type=F32),
              jnp.dot(xb, wlora_ref[...], preferred_element_type=F32), dict(locals()))


def _mix_rest(zmain, zl, names):
    sample, tiles_per_seq = names["sample"], names["tiles_per_seq"]
    (lng_ref, lnb_ref, mum_ref, mul_ref, w0_ref, w2_ref, a0_ref, a2_ref, g2_ref, kk_ref, ka_ref,
     rk_ref, ones_ref, ya_ref, r_ref, w_ref, k_ref, v_ref, kn_ref, b_ref, g_ref, bonus_ref) = [
        names[n] for n in (
            "lng_ref", "lnb_ref", "mum_ref", "mul_ref", "w0_ref", "w2_ref", "a0_ref", "a2_ref",
            "g2_ref", "kk_ref", "ka_ref", "rk_ref", "ones_ref", "ya_ref", "r_ref", "w_ref",
            "k_ref", "v_ref", "kn_ref", "b_ref", "g_ref", "bonus_ref")]
    if sample:
        w00_ref, b0_ref, spm_ref, spl_ref, va_ref, zm_ref, zl_ref = [
            names[n] for n in ("w00_ref", "b0_ref", "spm_ref", "spl_ref", "va_ref", "zm_ref",
                               "zl_ref")]
    else:
        wcat_ref, bias_ref, zlast_ref, cm_ref, cl_ref = [
            names[n] for n in ("wcat_ref", "bias_ref", "zlast_ref", "cm_ref", "cl_ref")]
    tm = zl.shape[0]

    u = jax.nn.gelu(zmain[:, :A_WIDTH])
    vx = jax.nn.gelu(zmain[:, A_WIDTH:2 * A_WIDTH])
    mu = jnp.mean(vx, axis=-1, keepdims=True)
    var = jnp.mean(jnp.square(vx - mu), axis=-1, keepdims=True)
    va = (vx - mu) * lax.rsqrt(var + LN_EPS) * lng_ref[...] + lnb_ref[...]
    if sample:
        mixed = va * w00_ref[...] + b0_ref[...]
        ya_ref[...] = (u * mixed).astype(BF16)
        va_ref[...] = va
    else:
        vab = va.astype(BF16)
        first = lax.broadcasted_iota(jnp.int32, (CHUNK, 2 * A_GROUP_DIM), 1) < A_GROUP_DIM
        for c in range(tm // CHUNK):
            rows = slice(c * CHUNK, (c + 1) * CHUNK)
            for gp in range(A_GROUPS // 2):
                lanes = slice(gp * 2 * A_GROUP_DIM, (gp + 1) * 2 * A_GROUP_DIM)
                vc = vab[rows, lanes]
                zero = jnp.zeros_like(vc)
                rhs = jnp.concatenate([jnp.where(first, vc, zero), jnp.where(first, zero, vc)],
                                      axis=0)
                mixed = jnp.dot(wcat_ref[:, gp * 2 * CHUNK:(gp + 1) * 2 * CHUNK], rhs,
                                preferred_element_type=F32) + bias_ref[:, lanes]
                ya_ref[rows, lanes] = (u[rows, lanes] * mixed).astype(BF16)

    zbm = zmain[:, 2 * A_WIDTH:]
    if sample:
        zpm, zpl = spm_ref[...], spl_ref[...]
        zm_ref[...] = zbm
        zl_ref[...] = zl
    else:
        tile = pl.program_id(0)

        @pl.when(tile % tiles_per_seq == 0)
        def _():
            cm_ref[...] = jnp.zeros_like(cm_ref)
            cl_ref[...] = jnp.zeros_like(cl_ref)

        first_m = lax.broadcasted_iota(jnp.int32, zbm.shape, 0) == 0
        first_l = lax.broadcasted_iota(jnp.int32, zl.shape, 0) == 0
        zpm = jnp.where(first_m, cm_ref[0:1, :], pltpu.roll(zbm, 1, axis=0))
        zpl = jnp.where(first_l, cl_ref[0:1, :], pltpu.roll(zl, 1, axis=0))
        cm_ref[0:1, :] = zbm[tm - 1:tm, :]
        cl_ref[0:1, :] = zl[tm - 1:tm, :]
        zlast_ref[:, :RKV_W] = jnp.broadcast_to(zbm[tm - 1:tm, :], (8, RKV_W))
        zlast_ref[:, RKV_W:] = jnp.broadcast_to(zl[tm - 1:tm, :], (8, LORA_W))
    zsm = zbm + (zpm - zbm) * mum_ref[...]
    zsl = zl + (zpl - zl) * mul_ref[...]
    r = zsm[:, :B_WIDTH]
    k = zsm[:, B_WIDTH:2 * B_WIDTH]
    v = zsm[:, 2 * B_WIDTH:]
    wd = zsl[:, LORA_WD:LORA_AD]
    ad = zsl[:, LORA_AD:LORA_GD]
    gd = zsl[:, LORA_GD:]
    y = w0_ref[...] + _dot(jnp.tanh(wd), w2_ref[...])
    w_log = jnp.minimum(y, 0.0) - jnp.log1p(jnp.exp(-jnp.abs(y))) - 0.5
    log_decay = -jnp.exp(w_log)
    a = jax.nn.sigmoid(a0_ref[...] + _dot(ad, a2_ref[...]))
    gate = _dot(jax.nn.sigmoid(gd), g2_ref[...])
    ones_bd = ones_ref[...]
    kk = k * kk_ref[...]
    kk = kk * lax.rsqrt(jnp.maximum(_seg_sum(kk * kk, ones_bd), 1e-24))
    k2 = k * (1.0 + (a - 1.0) * ka_ref[...])
    r_ref[...] = r
    w_ref[...] = jnp.exp(log_decay) if sample else log_decay
    k_ref[...] = k2
    v_ref[...] = v
    kn_ref[...] = kk
    b_ref[...] = kk * a
    g_ref[...] = gate
    bonus_ref[...] = _seg_sum(r * k2 * rk_ref[...], ones_bd) * v


def _mix_in(x, p, *, tm, sample, shift_main=None, shift_lora=None):
    rows = x.shape[0]
    n_tiles = rows // tm
    args = [x, p["ln_mix"], p["w_main"], p["w_lora"]]
    specs = [_rows_spec(tm, D_MODEL), _const_spec((1, D_MODEL)),
             _const_spec((D_MODEL, MAIN_W)), _const_spec((D_MODEL, LORA_W))]
    if sample:
        args += [p["sgu_w00"], p["sgu_b0"], shift_main, shift_lora]
        specs += [_const_spec((1, A_WIDTH)), _const_spec((1, A_WIDTH)),
                  _rows_spec(tm, RKV_W), _rows_spec(tm, LORA_W)]
    else:
        args += [p["sgu_wcat"], p["sgu_bias"]]
        specs += [_const_spec((CHUNK, A_GROUPS * CHUNK)), _const_spec((CHUNK, A_WIDTH))]
    args += [p["sgu_ln_g"], p["sgu_ln_b"], p["mu_main"], p["mu_lora"], p["w0"], p["w2"],
             p["a0"], p["a2"], p["g2"], p["k_k"], p["k_a"], p["r_k"], p["ones_bd"]]
    specs += [_const_spec((1, A_WIDTH)), _const_spec((1, A_WIDTH)), _const_spec((1, RKV_W)),
              _const_spec((1, LORA_W)), _const_spec((1, B_WIDTH)),
              _const_spec((LORA_AD - LORA_WD, B_WIDTH)), _const_spec((1, B_WIDTH)),
              _const_spec((LORA_GD - LORA_AD, B_WIDTH)), _const_spec((LORA_W - LORA_GD, B_WIDTH)),
              _const_spec((1, B_WIDTH)), _const_spec((1, B_WIDTH)), _const_spec((1, B_WIDTH)),
              _const_spec((B_WIDTH, B_WIDTH))]
    wide = jax.ShapeDtypeStruct((rows, B_WIDTH), F32)
    out_shape = [jax.ShapeDtypeStruct((rows, A_WIDTH), BF16)] + [wide] * 8
    out_specs = [_rows_spec(tm, B_WIDTH)] * 9
    scratch = []
    if sample:
        out_shape += [wide, jax.ShapeDtypeStruct((rows, RKV_W), F32),
                      jax.ShapeDtypeStruct((rows, LORA_W), F32)]
        out_specs += [_rows_spec(tm, A_WIDTH), _rows_spec(tm, RKV_W), _rows_spec(tm, LORA_W)]
    else:
        out_shape += [jax.ShapeDtypeStruct((n_tiles * 8, RKV_W + LORA_W), F32)]
        out_specs += [pl.BlockSpec((8, RKV_W + LORA_W), lambda i: (i, 0))]
        scratch = [pltpu.VMEM((8, RKV_W), F32), pltpu.VMEM((8, LORA_W), F32)]
    return pl.pallas_call(
        functools.partial(_mix_kernel, sample=sample, tiles_per_seq=max(SEQ // tm, 1)),
        grid=(n_tiles,),
        in_specs=specs,
        out_specs=out_specs,
        out_shape=out_shape,
        scratch_shapes=scratch,
        compiler_params=_params(),
        name="mix_in",
    )(*args)


def _each(f, *lists):
    return [f(*xs) for xs in zip(*lists)]


SCAN_C = 64


def _mm(a, b):
    return jnp.dot(a.astype(BF16), b.astype(BF16), preferred_element_type=F32)


def _mm_nt(a, b):
    return lax.dot_general(a.astype(BF16), b.astype(BF16), (((1,), (1,)), ((), ())),
                           preferred_element_type=F32)


def _mm_tn(a, b):
    return lax.dot_general(a.astype(BF16), b.astype(BF16), (((0,), (0,)), ((), ())),
                           preferred_element_type=F32)


def _cumsum_rows(x):
    n = x.shape[0]
    row = lax.broadcasted_iota(jnp.int32, x.shape, 0)
    s = 1
    while s < n:
        x = x + jnp.where(row >= s, pltpu.roll(x, s, axis=0), 0.0)
        s *= 2
    return x


INV_BASE = 8


def _unit_lower_inverse(ns, row, col):
    f0 = jnp.zeros((), F32)
    same = lambda s: (row // s) == (col // s)
    eye = jnp.where(row == col, 1.0, f0)
    ps = _each(lambda n: jnp.where(same(INV_BASE), n, f0), ns)
    ts = _each(lambda p: eye + p, ps)
    s = 2
    while s < INV_BASE:
        ps = _each(lambda p: _mm(p, p), ps)
        yield
        ts = _each(lambda t, p: t + _mm(t, p), ts, ps)
        yield
        s *= 2
    s = INV_BASE
    while s < SCAN_C:
        level = same(2 * s) & jnp.logical_not(same(s))
        ws = _each(lambda n, t: _mm(jnp.where(level, n, f0), t), ns, ts)
        yield
        ts = _each(lambda t, w: t + _mm(t, w), ts, ws)
        yield
        s *= 2
    return ts


def _chunk_pairs(s0s, rs, lws, ks, vs, kks, bs):
    c = SCAN_C
    f0 = jnp.zeros((), F32)
    row = lax.broadcasted_iota(jnp.int32, (2 * c, PAIR_W), 0)
    col = lax.broadcasted_iota(jnp.int32, (2 * c, PAIR_W), 1)
    top, lft = row < c, col < HEAD
    same_head = top == lft
    strict = (row % c) > (col % HEAD)
    row_c = lax.broadcasted_iota(jnp.int32, (c, PAIR_W), 0)
    col_c = lax.broadcasted_iota(jnp.int32, (c, PAIR_W), 1)
    lft_c = col_c < HEAD
    strict_c = row_c > (col_c % HEAD)
    incl_c = row_c >= (col_c % HEAD)

    def prep(r, lw, k, v, kk, b):
        cum = _cumsum_rows(lw)
        end = cum[c - 1:c, :]
        a_t = -kk * jnp.exp(cum - lw)
        r_t = r * jnp.exp(cum)
        einv = jnp.exp(-cum)
        eend = jnp.exp(end - cum)
        return dict(
            x0=jnp.concatenate([a_t, r_t], axis=0), x1=jnp.concatenate([r_t, a_t], axis=0),
            bk=jnp.concatenate([b * einv, k * einv], axis=0),
            kb=jnp.concatenate([k * einv, b * einv], axis=0),
            bk_e=jnp.concatenate([b * eend, k * eend], axis=0),
            w_end=jnp.exp(end), v=v,
            v_l=jnp.where(lft_c, v, f0), v_r=jnp.where(lft_c, f0, v))

    fs = _each(prep, rs, lws, ks, vs, kks, bs)
    yield
    g0s = _each(lambda f: _mm_nt(jnp.where(lft, f["x0"], f0), f["bk"]), fs)
    g1s = _each(lambda f: _mm_nt(jnp.where(lft, f0, f["x1"]), f["kb"]), fs)
    pqs = _each(lambda f, s0: _mm_nt(f["x0"], s0), fs, s0s)
    yield

    def rhs(f, g0, g1, pq):
        ak = jnp.where(strict_c, jnp.where(lft_c, g1[c:], g0[:c]), f0)
        x = pq[:c] + _mm(ak, jnp.concatenate([f["v_r"], f["v_l"]], axis=0))
        return jnp.concatenate([jnp.where(lft_c, x, f0), jnp.where(lft_c, f0, x)], axis=0)

    ys = _each(rhs, fs, g0s, g1s, pqs)
    yield
    ns = _each(lambda g0, g1: jnp.where(strict & same_head, jnp.where(top, g0, g1), f0),
               g0s, g1s)
    ts = yield from _unit_lower_inverse(ns, row, col)
    ys = _each(_mm, ts, ys)
    yield

    def out(f, g0, g1, pq, y):
        lhs = jnp.concatenate([jnp.where(incl_c, g0[c:], f0), jnp.where(incl_c, g1[:c], f0)],
                              axis=1)
        return pq[c:] + _mm(lhs, jnp.concatenate([y[:c], f["v_l"], f["v_r"], y[c:]], axis=0))

    def state(f, s0, y):
        upd = _mm_tn(jnp.concatenate([y[:c] + y[c:], f["v"]], axis=0), f["bk_e"])
        return s0 * f["w_end"] + jnp.where(same_head, upd, f0)

    outs = _each(out, fs, g0s, g1s, pqs, ys)
    yield
    return outs, _each(state, fs, s0s, ys)


SCAN_BATCHES = 4
XA_EVERY = 3


def _run_with(main, side, *, every):
    n = 0
    while True:
        if n % every == 0:
            next(side, None)
        n += 1
        try:
            next(main)
        except StopIteration as stop:
            for _ in side:
                pass
            return stop.value


def _scan_prompt_kernel(r_ref, w_ref, k_ref, v_ref, kk_ref, b_ref, xq_ref, xk_ref, xv_ref,
                        o_ref, sout_ref, xo_ref, s_ref):
    t_blk = pl.program_id(1)

    @pl.when(t_blk == 0)
    def _():
        s_ref[...] = jnp.zeros_like(s_ref)

    chains = [(j, p) for j in range(SCAN_BATCHES) for p in range(PAIRS)]
    lanes = lambda p: slice(p * PAIR_W, (p + 1) * PAIR_W)
    take = lambda ref: [ref[j, :, lanes(p)] for j, p in chains]
    os_, ss = _run_with(
        _chunk_pairs([s_ref[j, p] for j, p in chains], take(r_ref), take(w_ref), take(k_ref),
                     take(v_ref), take(kk_ref), take(b_ref)),
        _xa_attend(xq_ref, xk_ref, xv_ref, xo_ref), every=XA_EVERY)
    for (j, p), o, s_new in zip(chains, os_, ss):
        o_ref[j, :, lanes(p)] = o
        s_ref[j, p] = s_new

    @pl.when(t_blk == pl.num_programs(1) - 1)
    def _():
        sout_ref[...] = s_ref[...]


def _scan_prompt(r, lw, k, v, kk, b, xq, xk, xv, *, batch, seq):
    n_t = seq // SCAN_C
    nb = SCAN_BATCHES
    steps = (batch // nb) * n_t
    n_s = xq.shape[0]
    assert n_s % steps == 0, (n_s, steps)
    per = n_s // steps
    spec = pl.BlockSpec((nb, SCAN_C, B_WIDTH), lambda bi, ti: (bi, ti, 0))
    sspec = pl.BlockSpec((nb, PAIRS, PAIR_W, PAIR_W), lambda bi, ti: (bi, 0, 0, 0))
    step = lambda bi, ti: (bi * n_t + ti, 0, 0)
    qspec = pl.BlockSpec((per, MEM_ROWS, 128), step)
    mspec = pl.BlockSpec((per, N_MEM * MEM_ROWS, 128), step)
    o, s, xo = pl.pallas_call(
        _scan_prompt_kernel,
        grid=(batch // nb, n_t),
        in_specs=[spec] * 6 + [qspec, mspec, mspec],
        out_specs=[spec, sspec, qspec],
        out_shape=[jax.ShapeDtypeStruct((batch, seq, B_WIDTH), F32),
                   jax.ShapeDtypeStruct((batch, PAIRS, PAIR_W, PAIR_W), F32),
                   jax.ShapeDtypeStruct(xq.shape, F32)],
        scratch_shapes=[pltpu.VMEM((nb, PAIRS, PAIR_W, PAIR_W), F32)],
        compiler_params=_params(2),
        name="scan_prompt",
    )(*[x.reshape(batch, seq, B_WIDTH) for x in (r, lw, k, v, kk, b)], xq, xk, xv)
    return o.reshape(batch * seq, B_WIDTH), s, xo


def _scan_sample_kernel(s_ref, r_ref, w_ref, k_ref, v_ref, kk_ref, b_ref, o_ref, sout_ref,
                        t_ref, ot_ref):
    h = pl.program_id(0)

    @pl.when(h == 0)
    def _():
        for i, ref in enumerate((r_ref, w_ref, k_ref, v_ref, kk_ref, b_ref)):
            t_ref[i] = ref[...].T

    base = pl.multiple_of(h * HEAD, HEAD)
    keys = pl.ds(base, HEAD)
    r, w, k, kk, b = [t_ref[i, keys, :] for i in (0, 1, 2, 4, 5)]

    def body(v8, carry):
        rows = pl.ds(pl.multiple_of(base + v8 * 8, 8), 8)
        v_rows = t_ref[3, rows, :]
        outs = []
        for j in range(8):
            vi = v8 * 8 + j
            s = s_ref[0, vi]
            sa = jnp.sum(s * kk, axis=0, keepdims=True)
            s = s * w - sa * b + v_rows[j:j + 1, :] * k
            sout_ref[0, vi] = s
            outs.append(jnp.sum(s * r, axis=0, keepdims=True))
        ot_ref[rows, :] = jnp.concatenate(outs, axis=0)
        return carry

    lax.fori_loop(0, HEAD // 8, body, 0)

    @pl.when(h == pl.num_programs(0) - 1)
    def _():
        o_ref[...] = ot_ref[...].T


def _scan_sample(state_t, r, w, k, v, kk, b):
    rows = r.shape[0]
    sspec = pl.BlockSpec((1, HEAD, HEAD, rows), lambda h: (h, 0, 0, 0))
    spec = _const_spec((rows, B_WIDTH))
    return pl.pallas_call(
        _scan_sample_kernel,
        grid=(HEADS,),
        in_specs=[sspec] + [spec] * 6,
        out_specs=[pl.BlockSpec((rows, B_WIDTH), lambda h: (0, 0)), sspec],
        out_shape=[jax.ShapeDtypeStruct((rows, B_WIDTH), F32),
                   jax.ShapeDtypeStruct(state_t.shape, F32)],
        scratch_shapes=[pltpu.VMEM((6, B_WIDTH, rows), F32), pltpu.VMEM((B_WIDTH, rows), F32)],
        compiler_params=_params(),
        name="scan_sample",
    )(state_t, r, w, k, v, kk, b)


def _softmax_rows(s):
    e = jnp.exp(s - jnp.max(s, axis=-1, keepdims=True))
    return e / jnp.sum(e, axis=-1, keepdims=True)


def _post_kernel(*refs, attend):
    it = iter(refs)
    (x_ref, ya_ref, o_ref, g_ref, bonus_ref, gng_ref, gnb_ref, ones_ref, woa_ref, wob_ref,
     lnx_ref, wq_ref) = [next(it) for _ in range(12)]
    if attend:
        mk_ref, mv_ref = next(it), next(it)
    x2_ref, out_ref = next(it), next(it)

    ones_bd = ones_ref[...]
    o = o_ref[...]
    mu = _seg_sum(o, ones_bd) * (1.0 / HEAD)
    d = o - mu
    var = _seg_sum(d * d, ones_bd) * (1.0 / HEAD)
    on = d * lax.rsqrt(var + GN_EPS) * gng_ref[...] + gnb_ref[...]
    yb = (on + bonus_ref[...]) * g_ref[...]
    x2 = x_ref[...] + jnp.dot(ya_ref[...], woa_ref[...], preferred_element_type=F32) \
        + _dot(yb, wob_ref[...])
    x2_ref[...] = x2
    q = _dot(_rms(x2, lnx_ref[...]), wq_ref[...])
    if not attend:
        out_ref[...] = q
        return
    qb = q.astype(BF16)
    for h in range(XA_HEADS):
        sl = slice(h * XA_DIM, (h + 1) * XA_DIM)
        s = lax.dot_general(qb[:, sl], mk_ref[0, :, sl], (((1,), (1,)), ((), ())),
                            preferred_element_type=F32) * (XA_DIM ** -0.5)
        p = _softmax_rows(s)
        out_ref[:, sl] = _dot(p, mv_ref[0, :, sl]).astype(BF16)


def _post_mix(x, ya, o, g, bonus, p, *, tm, mk=None, mv=None):
    rows = x.shape[0]
    attend = mk is not None
    args = [x, ya, o, g, bonus, p["gn_g"], p["gn_b"], p["ones_bd"], p["w_out_a"], p["w_out_b"],
            p["ln_xattn"], p["xa_q"]]
    specs = [_rows_spec(tm, D_MODEL)] + [_rows_spec(tm, B_WIDTH)] * 4 + [
        _const_spec((1, B_WIDTH)), _const_spec((1, B_WIDTH)), _const_spec((B_WIDTH, B_WIDTH)),
        _const_spec((A_WIDTH, D_MODEL)), _const_spec((B_WIDTH, D_MODEL)),
        _const_spec((1, D_MODEL)), _const_spec((D_MODEL, D_MODEL))]
    if attend:
        tiles_per_seq = SEQ // tm
        mspec = pl.BlockSpec((1, N_MEM, D_MODEL), lambda i: (i // tiles_per_seq, 0, 0))
        args += [mk, mv]
        specs += [mspec, mspec]
    return pl.pallas_call(
        functools.partial(_post_kernel, attend=attend),
        grid=(rows // tm,),
        in_specs=specs,
        out_specs=[_rows_spec(tm, D_MODEL)] * 2,
        out_shape=[jax.ShapeDtypeStruct((rows, D_MODEL), F32),
                   jax.ShapeDtypeStruct((rows, D_MODEL), BF16 if attend else F32)],
        compiler_params=_params(),
        name="post_mix",
    )(*args)


MEM_ROWS = XA_HEADS * (XA_DIM // 128)


def _lane_allreduce(x, op):
    shift = MEM_ROWS
    while shift < 128:
        x = op(x, pltpu.roll(x, shift, axis=1))
        shift *= 2
    return x


def _xa_attend(q_ref, k_ref, v_ref, o_ref):
    f0 = jnp.zeros((), F32)
    n_blk = N_MEM * MEM_ROWS // 128
    sub = lax.broadcasted_iota(jnp.int32, (MEM_ROWS, 128), 0)
    lane = lax.broadcasted_iota(jnp.int32, (MEM_ROWS, 128), 1)
    diag = sub == (lane % MEM_ROWS)
    li = lax.broadcasted_iota(jnp.int32, (128, 128), 0)
    lj = lax.broadcasted_iota(jnp.int32, (128, 128), 1)
    comb = jnp.where((li // MEM_ROWS == lj // MEM_ROWS) & (li % XA_HEADS == lj % XA_HEADS),
                     1.0, 0.0).astype(BF16)
    samples = list(range(q_ref.shape[0]))
    scs = [_mm_nt(q_ref[j], k_ref[j]) for j in samples]

    def partial(sc):
        return jnp.concatenate(
            [jnp.sum(jnp.where(diag, sc[:, t * 128:(t + 1) * 128], f0), axis=0, keepdims=True)
             for t in range(n_blk)], axis=0)

    def scores(part):
        hi = part.astype(BF16)
        lo = (part - hi.astype(F32)).astype(BF16)
        return (jnp.dot(hi, comb, preferred_element_type=F32)
                + jnp.dot(lo, comb, preferred_element_type=F32)) * (XA_DIM ** -0.5)

    def softmax(s):
        mx = _lane_allreduce(jnp.broadcast_to(jnp.max(s, axis=0, keepdims=True), (MEM_ROWS, 128)),
                             jnp.maximum)
        e = jnp.exp(s - mx[0:1, :])
        den = _lane_allreduce(jnp.broadcast_to(jnp.sum(e, axis=0, keepdims=True), (MEM_ROWS, 128)),
                              jnp.add)
        p = e / den[0:1, :]
        return jnp.concatenate(
            [jnp.where(diag, jnp.broadcast_to(p[t:t + 1, :], (MEM_ROWS, 128)), f0)
             for t in range(n_blk)], axis=1)

    yield
    parts = _each(partial, scs)
    yield
    ss = _each(scores, parts)
    yield
    p_rows = _each(softmax, ss)
    yield
    for j, p in zip(samples, p_rows):
        o_ref[j] = _mm(p, v_ref[j])


def _pad_lora(x):
    wd = x[..., :DECAY_LORA]
    ad = x[..., DECAY_LORA:DECAY_LORA + AAA_LORA]
    gd = x[..., DECAY_LORA + AAA_LORA:]
    z = lambda n: jnp.zeros(x.shape[:-1] + (n,), x.dtype)
    return jnp.concatenate([wd, z(LORA_AD - DECAY_LORA), ad, z(LORA_GD - LORA_AD - AAA_LORA),
                            gd, z(LORA_W - LORA_GD - GATE_LORA)], axis=-1)


def _unpad_shift(zm, zl):
    return jnp.concatenate([zm, zl[..., LORA_WD:LORA_WD + DECAY_LORA],
                            zl[..., LORA_AD:LORA_AD + AAA_LORA],
                            zl[..., LORA_GD:LORA_GD + GATE_LORA]], axis=-1)


def _pad_rows(w, n):
    return jnp.pad(w, ((0, n - w.shape[0]), (0, 0)))


def _mem_rows(x):
    b = x.shape[0]
    return x.reshape(b, N_MEM, XA_HEADS, XA_DIM // 128, 128).transpose(0, 1, 3, 2, 4).reshape(
        b, N_MEM * MEM_ROWS, 128)


def _head_rows(x):
    b = x.shape[0]
    return x.reshape(b, XA_HEADS, XA_DIM // 128, 128).transpose(0, 2, 1, 3).reshape(b, MEM_ROWS, 128)


def _from_head_rows(x):
    b = x.shape[0]
    return x.reshape(b, XA_DIM // 128, XA_HEADS, 128).transpose(0, 2, 1, 3).reshape(b, D_MODEL)


def kernel(x_prompt, x_sample, state_rwkv, state_shift, cache_mem_k, cache_mem_v, mem_prompt, ln_ffn1, ffn1_gate, ffn1_up, ffn1_down, ln_mix, w_in, w_out, sgu_w, sgu_b, sgu_ln_g, sgu_ln_b, rwkv_mu, rwkv_w0, rwkv_w2, rwkv_a0, rwkv_a2, rwkv_g2, rwkv_k_k, rwkv_k_a, rwkv_r_k, rwkv_gn_g, rwkv_gn_b, ln_xattn, mem_norm, xa_q, xa_k, xa_v, xa_o, ln_ffn2, ffn2_gate, ffn2_up, ffn2_down, final_norm):
    assert ln_ffn1.shape[0] == 1, "single layer"
    bp, seq, _ = x_prompt.shape
    bs = x_sample.shape[0]
    row = lambda a: a.reshape(1, -1).astype(F32)
    bf = lambda a: a.astype(BF16)
    l = 0
    head_id = jnp.arange(B_WIDTH) // HEAD
    tril = jnp.tril(jnp.ones((CHUNK, CHUNK), dtype=bool))
    wmask = jnp.where(tril[None], sgu_w[l], 0)
    p = {
        "ln_mix": row(ln_mix[l]),
        "w_main": bf(w_in[l][:, :MAIN_W]),
        "w_lora": bf(_pad_lora(w_in[l][:, MAIN_W:])),
        "sgu_wcat": bf(wmask.transpose(1, 0, 2).reshape(CHUNK, A_GROUPS * CHUNK)),
        "sgu_bias": jnp.repeat(sgu_b[l].T, A_GROUP_DIM, axis=1),
        "sgu_w00": row(jnp.repeat(sgu_w[l][:, 0, 0], A_GROUP_DIM)),
        "sgu_b0": row(jnp.repeat(sgu_b[l][:, 0], A_GROUP_DIM)),
        "sgu_ln_g": row(sgu_ln_g[l]), "sgu_ln_b": row(sgu_ln_b[l]),
        "mu_main": row(rwkv_mu[l][:RKV_W]),
        "mu_lora": row(_pad_lora(rwkv_mu[l][RKV_W:])),
        "w0": row(rwkv_w0[l]), "w2": bf(_pad_rows(rwkv_w2[l], LORA_AD - LORA_WD)),
        "a0": row(rwkv_a0[l]), "a2": bf(_pad_rows(rwkv_a2[l], LORA_GD - LORA_AD)),
        "g2": bf(_pad_rows(rwkv_g2[l], LORA_W - LORA_GD)),
        "k_k": row(rwkv_k_k[l]), "k_a": row(rwkv_k_a[l]), "r_k": row(rwkv_r_k[l]),
        "ones_bd": (head_id[:, None] == head_id[None, :]).astype(BF16),
        "gn_g": row(rwkv_gn_g[l]), "gn_b": row(rwkv_gn_b[l]),
        "w_out_a": bf(w_out[l][:A_WIDTH]), "w_out_b": bf(w_out[l][A_WIDTH:]),
        "ln_xattn": row(ln_xattn[l]), "xa_q": bf(xa_q[l]),
    }
    ffn1 = (row(ln_ffn1[l]), bf(ffn1_gate[l]), bf(ffn1_up[l]), bf(ffn1_down[l]))
    ffn2 = (row(ln_ffn2[l]), bf(ffn2_gate[l]), bf(ffn2_up[l]), bf(ffn2_down[l]))
    xa_o_b = bf(xa_o[l])
    fnorm = row(final_norm)

    xs = x_sample.reshape(bs, D_MODEL)
    sh = state_shift[l].reshape(bs, B_PROJ)
    x1s = _ffn(xs, *ffn1, tm=bs)
    (ya_s, r_s, w_s, k_s, v_s, kk_s, b_s, g_s, bonus_s, va_s, zm_s, zl_s) = _mix_in(
        x1s, p, tm=bs, sample=True, shift_main=sh[:, :RKV_W], shift_lora=_pad_lora(sh[:, RKV_W:]))
    o_s, state_t = _scan_sample(jnp.transpose(state_rwkv[l], (1, 2, 3, 0)),
                                r_s, w_s, k_s, v_s, kk_s, b_s)
    state_s = jnp.transpose(state_t, (3, 0, 1, 2))
    x2s, q_s = _post_mix(x1s, ya_s, o_s, g_s, bonus_s, p, tm=bs)

    tm = 512
    xp = x_prompt.reshape(bp * seq, D_MODEL)
    mk, mv, mkb, mvb = _memkv(mem_prompt.reshape(bp * N_MEM, D_MODEL), row(mem_norm[l]),
                              bf(xa_k[l]), bf(xa_v[l]), tm=tm)
    x1 = _ffn(xp, *ffn1, tm=tm)
    ya, r, w, k, v, kk, b, g, bonus, zlast = _mix_in(x1, p, tm=tm, sample=False)
    o, s_bd, attn_rows = _scan_prompt(r, w, k, v, kk, b, _head_rows(q_s),
                                      _mem_rows(cache_mem_k[l]), _mem_rows(cache_mem_v[l]),
                                      batch=bp, seq=seq)
    state_p = jnp.stack([s_bd[:, :, :HEAD, :HEAD], s_bd[:, :, HEAD:, HEAD:]],
                        axis=2).reshape(bp, HEADS, HEAD, HEAD)
    x2, attn = _post_mix(x1, ya, o, g, bonus, p, tm=tm,
                         mk=mkb.reshape(bp, N_MEM, D_MODEL), mv=mvb.reshape(bp, N_MEM, D_MODEL))
    y_prompt = _ffn(x2, *ffn2, tm=tm, attn=attn, wo=xa_o_b, final_norm=fnorm)
    tiles_per_seq = seq // tm
    zl_rows = zlast.reshape(bp, tiles_per_seq, 8, RKV_W + LORA_W)[:, -1, 0]
    shift_p = _unpad_shift(zl_rows[:, :RKV_W], zl_rows[:, RKV_W:])

    y_sample = _ffn(x2s, *ffn2, tm=bs, attn=_from_head_rows(attn_rows), wo=xa_o_b,
                    final_norm=fnorm)

    return (y_prompt.reshape(bp, seq, D_MODEL),
            y_sample.reshape(bs, 1, D_MODEL),
            state_p[None],
            shift_p.reshape(1, bp, 1, B_PROJ),
            mk.reshape(1, bp, N_MEM, XA_HEADS, XA_DIM),
            mv.reshape(1, bp, N_MEM, XA_HEADS, XA_DIM),
            state_s[None],
            _unpad_shift(zm_s, zl_s).reshape(1, bs, 1, B_PROJ),
            va_s.reshape(1, bs, 1, A_WIDTH))
```

```python
import functools

import jax
import jax.numpy as jnp
from jax import lax
from jax.experimental import pallas as pl
from jax.experimental.pallas import tpu as pltpu

F32 = jnp.float32
BF16 = jnp.bfloat16

D_MODEL = 1024
SEQ = 2048
A_WIDTH = 512
A_GROUPS = 8
A_GROUP_DIM = 64
CHUNK = 128
B_WIDTH = 512
HEAD = 64
HEADS = 8
PAIRS = HEADS // 2
PAIR_W = 2 * HEAD
DECAY_LORA = 64
AAA_LORA = 64
GATE_LORA = 160
B_PROJ = 3 * B_WIDTH + DECAY_LORA + AAA_LORA + GATE_LORA
MAIN_W = 2 * A_WIDTH + 3 * B_WIDTH
RKV_W = 3 * B_WIDTH
LORA_W = 512
LORA_WD, LORA_AD, LORA_GD = 0, 128, 256
D_FF = 2816
N_MEM = 256
XA_HEADS = 4
XA_DIM = 256
NORM_EPS = 1e-6
LN_EPS = 1e-5
GN_EPS = 64e-5

VMEM_LIMIT = 56 * 1024 * 1024


def _params(n_axes=1):
    return pltpu.CompilerParams(dimension_semantics=("arbitrary",) * n_axes,
                                vmem_limit_bytes=VMEM_LIMIT)


def _const_spec(shape):
    nd = len(shape)
    return pl.BlockSpec(shape, lambda *_: (0,) * nd, pipeline_mode=pl.Buffered(1))


def _rows_spec(tm, width):
    return pl.BlockSpec((tm, width), lambda i: (i, 0))


def _rms(x, g):
    return x * lax.rsqrt(jnp.mean(x * x, axis=-1, keepdims=True) + NORM_EPS) * g


def _dot(a, b):
    return jnp.dot(a.astype(BF16), b, preferred_element_type=F32)


def _seg_sum(x, ones_bd):
    return jnp.dot(x.astype(BF16), ones_bd, preferred_element_type=F32)


FFN_ROWS = 1024
FFN_BLOCK = 768


def _ffn_kernel(*refs, pre, final):
    it = iter(refs)
    x_ref = next(it)
    if pre:
        attn_ref, wo_ref = next(it), next(it)
    ln_ref, wg_ref, wu_ref, wd_ref = next(it), next(it), next(it), next(it)
    if final:
        fn_ref = next(it)
    o_ref = next(it)
    x = x_ref[...]
    if pre:
        x = x + _dot(attn_ref[...], wo_ref[...])
    xb = _rms(x, ln_ref[...]).astype(BF16)
    y = None
    for c0 in range(0, D_FF, FFN_BLOCK):
        cols = slice(c0, min(c0 + FFN_BLOCK, D_FF))
        g = jnp.dot(xb, wg_ref[:, cols], preferred_element_type=F32)
        u = jnp.dot(xb, wu_ref[:, cols], preferred_element_type=F32)
        h = (g * jax.nn.sigmoid(g) * u).astype(BF16)
        part = jnp.dot(h, wd_ref[cols, :], preferred_element_type=F32)
        y = part if y is None else y + part
    x = x + 0.5 * y
    if final:
        x = _rms(x, fn_ref[...])
    o_ref[...] = x


def _ffn(x, ln, wg, wu, wd, *, tm, attn=None, wo=None, final_norm=None):
    rows = x.shape[0]
    pre = attn is not None
    final = final_norm is not None
    args, specs = [x], [_rows_spec(tm, D_MODEL)]
    if pre:
        args += [attn, wo]
        specs += [_rows_spec(tm, D_MODEL), _const_spec((D_MODEL, D_MODEL))]
    args += [ln, wg, wu, wd]
    specs += [_const_spec((1, D_MODEL)), _const_spec((D_MODEL, D_FF)),
              _const_spec((D_MODEL, D_FF)), _const_spec((D_FF, D_MODEL))]
    if final:
        args.append(final_norm)
        specs.append(_const_spec((1, D_MODEL)))
    return pl.pallas_call(
        functools.partial(_ffn_kernel, pre=pre, final=final),
        grid=(rows // tm,),
        in_specs=specs,
        out_specs=_rows_spec(tm, D_MODEL),
        out_shape=jax.ShapeDtypeStruct((rows, D_MODEL), F32),
        compiler_params=_params(),
        name="ffn",
    )(*args)


def _memkv_kernel(m_ref, g_ref, wk_ref, wv_ref, k_ref, v_ref, kb_ref, vb_ref):
    mb = _rms(m_ref[...], g_ref[...]).astype(BF16)
    tm = m_ref.shape[0]
    k = jnp.dot(mb, wk_ref[...], preferred_element_type=F32)
    v = jnp.dot(mb, wv_ref[...], preferred_element_type=F32)
    kb_ref[...] = k.astype(BF16)
    vb_ref[...] = v.astype(BF16)
    for c in range(MEM_ROWS):
        src = (c % XA_HEADS) * (XA_DIM // 128) + c // XA_HEADS
        k_ref[pl.ds(c, tm, stride=MEM_ROWS), :] = k[:, src * 128:(src + 1) * 128]
        v_ref[pl.ds(c, tm, stride=MEM_ROWS), :] = v[:, src * 128:(src + 1) * 128]


def _memkv(mem, g, wk, wv, *, tm):
    rows = mem.shape[0]
    out = jax.ShapeDtypeStruct((rows * MEM_ROWS, 128), F32)
    outb = jax.ShapeDtypeStruct((rows, D_MODEL), BF16)
    return pl.pallas_call(
        _memkv_kernel,
        grid=(rows // tm,),
        in_specs=[_rows_spec(tm, D_MODEL), _const_spec((1, D_MODEL)),
                  _const_spec((D_MODEL, D_MODEL)), _const_spec((D_MODEL, D_MODEL))],
        out_specs=[_rows_spec(tm * MEM_ROWS, 128)] * 2 + [_rows_spec(tm, D_MODEL)] * 2,
        out_shape=[out, out, outb, outb],
        compiler_params=_params(),
        name="memkv",
    )(mem, g, wk, wv)


def _mix_kernel(*refs, sample, tiles_per_seq):
    it = iter(refs)
    x_ref, ln_ref, wmain_ref, wlora_ref = next(it), next(it), next(it), next(it)
    if sample:
        w00_ref, b0_ref, spm_ref, spl_ref = next(it), next(it), next(it), next(it)
    else:
        wcat_ref, bias_ref = next(it), next(it)
    (lng_ref, lnb_ref, mum_ref, mul_ref, w0_ref, w2_ref, a0_ref, a2_ref, g2_ref,
     kk_ref, ka_ref, rk_ref, ones_ref) = [next(it) for _ in range(13)]
    (ya_ref, r_ref, w_ref, k_ref, v_ref, kn_ref, b_ref, g_ref, bonus_ref) = [
        next(it) for _ in range(9)]
    if sample:
        va_ref, zm_ref, zl_ref = next(it), next(it), next(it)
    else:
        zlast_ref, cm_ref, cl_ref = next(it), next(it), next(it)

    xb = _rms(x_ref[...], ln_ref[...]).astype(BF16)
    _mix_rest(_mm_nt(xb, wmain_ref[...]), _mm_nt(xb, wlora_ref[...]), dict(locals()))


def _mix_rest(zmain, zl, names):
    sample, tiles_per_seq = names["sample"], names["tiles_per_seq"]
    (lng_ref, lnb_ref, mum_ref, mul_ref, w0_ref, w2_ref, a0_ref, a2_ref, g2_ref, kk_ref, ka_ref,
     rk_ref, ones_ref, ya_ref, r_ref, w_ref, k_ref, v_ref, kn_ref, b_ref, g_ref, bonus_ref) = [
        names[n] for n in (
            "lng_ref", "lnb_ref", "mum_ref", "mul_ref", "w0_ref", "w2_ref", "a0_ref", "a2_ref",
            "g2_ref", "kk_ref", "ka_ref", "rk_ref", "ones_ref", "ya_ref", "r_ref", "w_ref",
            "k_ref", "v_ref", "kn_ref", "b_ref", "g_ref", "bonus_ref")]
    if sample:
        w00_ref, b0_ref, spm_ref, spl_ref, va_ref, zm_ref, zl_ref = [
            names[n] for n in ("w00_ref", "b0_ref", "spm_ref", "spl_ref", "va_ref", "zm_ref",
                               "zl_ref")]
    else:
        wcat_ref, bias_ref, zlast_ref, cm_ref, cl_ref = [
            names[n] for n in ("wcat_ref", "bias_ref", "zlast_ref", "cm_ref", "cl_ref")]
    tm = zl.shape[0]

    u = jax.nn.gelu(zmain[:, :A_WIDTH])
    vx = jax.nn.gelu(zmain[:, A_WIDTH:2 * A_WIDTH])
    mu = jnp.mean(vx, axis=-1, keepdims=True)
    var = jnp.mean(jnp.square(vx - mu), axis=-1, keepdims=True)
    va = (vx - mu) * lax.rsqrt(var + LN_EPS) * lng_ref[...] + lnb_ref[...]
    if sample:
        mixed = va * w00_ref[...] + b0_ref[...]
        ya_ref[...] = (u * mixed).astype(BF16)
        va_ref[...] = va
    else:
        vab = va.astype(BF16)
        first = lax.broadcasted_iota(jnp.int32, (CHUNK, 2 * A_GROUP_DIM), 1) < A_GROUP_DIM
        for c in range(tm // CHUNK):
            rows = slice(c * CHUNK, (c + 1) * CHUNK)
            for gp in range(A_GROUPS // 2):
                lanes = slice(gp * 2 * A_GROUP_DIM, (gp + 1) * 2 * A_GROUP_DIM)
                vc = vab[rows, lanes]
                zero = jnp.zeros_like(vc)
                rhs = jnp.concatenate([jnp.where(first, vc, zero), jnp.where(first, zero, vc)],
                                      axis=0)
                mixed = jnp.dot(wcat_ref[:, gp * 2 * CHUNK:(gp + 1) * 2 * CHUNK], rhs,
                                preferred_element_type=F32) + bias_ref[:, lanes]
                ya_ref[rows, lanes] = (u[rows, lanes] * mixed).astype(BF16)

    zbm = zmain[:, 2 * A_WIDTH:]
    if sample:
        zpm, zpl = spm_ref[...], spl_ref[...]
        zm_ref[...] = zbm
        zl_ref[...] = zl
    else:
        tile = pl.program_id(0)

        @pl.when(tile % tiles_per_seq == 0)
        def _():
            cm_ref[...] = jnp.zeros_like(cm_ref)
            cl_ref[...] = jnp.zeros_like(cl_ref)

        first_m = lax.broadcasted_iota(jnp.int32, zbm.shape, 0) == 0
        first_l = lax.broadcasted_iota(jnp.int32, zl.shape, 0) == 0
        zpm = jnp.where(first_m, cm_ref[0:1, :], pltpu.roll(zbm, 1, axis=0))
        zpl = jnp.where(first_l, cl_ref[0:1, :], pltpu.roll(zl, 1, axis=0))
        cm_ref[0:1, :] = zbm[tm - 1:tm, :]
        cl_ref[0:1, :] = zl[tm - 1:tm, :]
        zlast_ref[:, :RKV_W] = jnp.broadcast_to(zbm[tm - 1:tm, :], (8, RKV_W))
        zlast_ref[:, RKV_W:] = jnp.broadcast_to(zl[tm - 1:tm, :], (8, LORA_W))
    zsm = zbm + (zpm - zbm) * mum_ref[...]
    zsl = zl + (zpl - zl) * mul_ref[...]
    r = zsm[:, :B_WIDTH]
    k = zsm[:, B_WIDTH:2 * B_WIDTH]
    v = zsm[:, 2 * B_WIDTH:]
    wd = zsl[:, LORA_WD:LORA_AD]
    ad = zsl[:, LORA_AD:LORA_GD]
    gd = zsl[:, LORA_GD:]
    y = w0_ref[...] + _dot(jnp.tanh(wd), w2_ref[...])
    w_log = jnp.minimum(y, 0.0) - jnp.log1p(jnp.exp(-jnp.abs(y))) - 0.5
    log_decay = -jnp.exp(w_log)
    a = jax.nn.sigmoid(a0_ref[...] + _dot(ad, a2_ref[...]))
    gate = _dot(jax.nn.sigmoid(gd), g2_ref[...])
    ones_bd = ones_ref[...]
    kk = k * kk_ref[...]
    kk = kk * lax.rsqrt(jnp.maximum(_seg_sum(kk * kk, ones_bd), 1e-24))
    k2 = k * (1.0 + (a - 1.0) * ka_ref[...])
    r_ref[...] = r
    w_ref[...] = jnp.exp(log_decay) if sample else log_decay
    k_ref[...] = k2
    v_ref[...] = v
    kn_ref[...] = kk
    b_ref[...] = kk * a
    g_ref[...] = gate
    bonus_ref[...] = _seg_sum(r * k2 * rk_ref[...], ones_bd) * v


def _mix_in(x, p, *, tm, sample, shift_main=None, shift_lora=None):
    rows = x.shape[0]
    n_tiles = rows // tm
    args = [x, p["ln_mix"], p["w_main"], p["w_lora"]]
    specs = [_rows_spec(tm, D_MODEL), _const_spec((1, D_MODEL)),
             _const_spec((MAIN_W, D_MODEL)), _const_spec((LORA_W, D_MODEL))]
    if sample:
        args += [p["sgu_w00"], p["sgu_b0"], shift_main, shift_lora]
        specs += [_const_spec((1, A_WIDTH)), _const_spec((1, A_WIDTH)),
                  _rows_spec(tm, RKV_W), _rows_spec(tm, LORA_W)]
    else:
        args += [p["sgu_wcat"], p["sgu_bias"]]
        specs += [_const_spec((CHUNK, A_GROUPS * CHUNK)), _const_spec((CHUNK, A_WIDTH))]
    args += [p["sgu_ln_g"], p["sgu_ln_b"], p["mu_main"], p["mu_lora"], p["w0"], p["w2"],
             p["a0"], p["a2"], p["g2"], p["k_k"], p["k_a"], p["r_k"], p["ones_bd"]]
    specs += [_const_spec((1, A_WIDTH)), _const_spec((1, A_WIDTH)), _const_spec((1, RKV_W)),
              _const_spec((1, LORA_W)), _const_spec((1, B_WIDTH)),
              _const_spec((LORA_AD - LORA_WD, B_WIDTH)), _const_spec((1, B_WIDTH)),
              _const_spec((LORA_GD - LORA_AD, B_WIDTH)), _const_spec((LORA_W - LORA_GD, B_WIDTH)),
              _const_spec((1, B_WIDTH)), _const_spec((1, B_WIDTH)), _const_spec((1, B_WIDTH)),
              _const_spec((B_WIDTH, B_WIDTH))]
    wide = jax.ShapeDtypeStruct((rows, B_WIDTH), F32)
    out_shape = [jax.ShapeDtypeStruct((rows, A_WIDTH), BF16)] + [wide] * 8
    out_specs = [_rows_spec(tm, B_WIDTH)] * 9
    scratch = []
    if sample:
        out_shape += [wide, jax.ShapeDtypeStruct((rows, RKV_W), F32),
                      jax.ShapeDtypeStruct((rows, LORA_W), F32)]
        out_specs += [_rows_spec(tm, A_WIDTH), _rows_spec(tm, RKV_W), _rows_spec(tm, LORA_W)]
    else:
        out_shape += [jax.ShapeDtypeStruct((n_tiles * 8, RKV_W + LORA_W), F32)]
        out_specs += [pl.BlockSpec((8, RKV_W + LORA_W), lambda i: (i, 0))]
        scratch = [pltpu.VMEM((8, RKV_W), F32), pltpu.VMEM((8, LORA_W), F32)]
    return pl.pallas_call(
        functools.partial(_mix_kernel, sample=sample, tiles_per_seq=max(SEQ // tm, 1)),
        grid=(n_tiles,),
        in_specs=specs,
        out_specs=out_specs,
        out_shape=out_shape,
        scratch_shapes=scratch,
        compiler_params=_params(),
        name="mix_in",
    )(*args)


def _each(f, *lists):
    return [f(*xs) for xs in zip(*lists)]


SCAN_C = 64


def _mm(a, b):
    return jnp.dot(a.astype(BF16), b.astype(BF16), preferred_element_type=F32)


def _mm_nt(a, b):
    return lax.dot_general(a.astype(BF16), b.astype(BF16), (((1,), (1,)), ((), ())),
                           preferred_element_type=F32)


def _mm_tn(a, b):
    return lax.dot_general(a.astype(BF16), b.astype(BF16), (((0,), (0,)), ((), ())),
                           preferred_element_type=F32)


def _cumsum_rows(x):
    n = x.shape[0]
    row = lax.broadcasted_iota(jnp.int32, x.shape, 0)
    s = 1
    while s < n:
        x = x + jnp.where(row >= s, pltpu.roll(x, s, axis=0), 0.0)
        s *= 2
    return x


INV_BASE = 8


def _unit_lower_inverse(ns, row, col):
    f0 = jnp.zeros((), F32)
    same = lambda s: (row // s) == (col // s)
    eye = jnp.where(row == col, 1.0, f0)
    ps = _each(lambda n: jnp.where(same(INV_BASE), n, f0), ns)
    ts = _each(lambda p: eye + p, ps)
    s = 2
    while s < INV_BASE:
        ps = _each(lambda p: _mm(p, p), ps)
        yield
        ts = _each(lambda t, p: t + _mm(t, p), ts, ps)
        yield
        s *= 2
    s = INV_BASE
    while s < SCAN_C:
        level = same(2 * s) & jnp.logical_not(same(s))
        ws = _each(lambda n, t: _mm(jnp.where(level, n, f0), t), ns, ts)
        yield
        ts = _each(lambda t, w: t + _mm(t, w), ts, ws)
        yield
        s *= 2
    return ts


def _chunk_pairs(s0s, rs, lws, ks, vs, kks, bs):
    c = SCAN_C
    f0 = jnp.zeros((), F32)
    row = lax.broadcasted_iota(jnp.int32, (2 * c, PAIR_W), 0)
    col = lax.broadcasted_iota(jnp.int32, (2 * c, PAIR_W), 1)
    top, lft = row < c, col < HEAD
    same_head = top == lft
    strict = (row % c) > (col % HEAD)
    row_c = lax.broadcasted_iota(jnp.int32, (c, PAIR_W), 0)
    col_c = lax.broadcasted_iota(jnp.int32, (c, PAIR_W), 1)
    lft_c = col_c < HEAD
    strict_c = row_c > (col_c % HEAD)
    incl_c = row_c >= (col_c % HEAD)

    def prep(r, lw, k, v, kk, b):
        cum = _cumsum_rows(lw)
        end = cum[c - 1:c, :]
        a_t = -kk * jnp.exp(cum - lw)
        r_t = r * jnp.exp(cum)
        einv = jnp.exp(-cum)
        eend = jnp.exp(end - cum)
        return dict(
            x0=jnp.concatenate([a_t, r_t], axis=0), x1=jnp.concatenate([r_t, a_t], axis=0),
            bk=jnp.concatenate([b * einv, k * einv], axis=0),
            kb=jnp.concatenate([k * einv, b * einv], axis=0),
            bk_e=jnp.concatenate([b * eend, k * eend], axis=0),
            w_end=jnp.exp(end), v=v,
            v_l=jnp.where(lft_c, v, f0), v_r=jnp.where(lft_c, f0, v))

    fs = _each(prep, rs, lws, ks, vs, kks, bs)
    yield
    g0s = _each(lambda f: _mm_nt(jnp.where(lft, f["x0"], f0), f["bk"]), fs)
    g1s = _each(lambda f: _mm_nt(jnp.where(lft, f0, f["x1"]), f["kb"]), fs)
    pqs = _each(lambda f, s0: _mm_nt(f["x0"], s0), fs, s0s)
    yield

    def rhs(f, g0, g1, pq):
        ak = jnp.where(strict_c, jnp.where(lft_c, g1[c:], g0[:c]), f0)
        x = pq[:c] + _mm(ak, jnp.concatenate([f["v_r"], f["v_l"]], axis=0))
        return jnp.concatenate([jnp.where(lft_c, x, f0), jnp.where(lft_c, f0, x)], axis=0)

    ys = _each(rhs, fs, g0s, g1s, pqs)
    yield
    ns = _each(lambda g0, g1: jnp.where(strict & same_head, jnp.where(top, g0, g1), f0),
               g0s, g1s)
    ts = yield from _unit_lower_inverse(ns, row, col)
    ys = _each(_mm, ts, ys)
    yield

    def out(f, g0, g1, pq, y):
        lhs = jnp.concatenate([jnp.where(incl_c, g0[c:], f0), jnp.where(incl_c, g1[:c], f0)],
                              axis=1)
        return pq[c:] + _mm(lhs, jnp.concatenate([y[:c], f["v_l"], f["v_r"], y[c:]], axis=0))

    def state(f, s0, y):
        upd = _mm_tn(jnp.concatenate([y[:c] + y[c:], f["v"]], axis=0), f["bk_e"])
        return s0 * f["w_end"] + jnp.where(same_head, upd, f0)

    outs = _each(out, fs, g0s, g1s, pqs, ys)
    yield
    return outs, _each(state, fs, s0s, ys)


SCAN_BATCHES = 4
XA_EVERY = 3


def _run_with(main, side, *, every):
    n = 0
    while True:
        if n % every == 0:
            next(side, None)
        n += 1
        try:
            next(main)
        except StopIteration as stop:
            for _ in side:
                pass
            return stop.value


def _scan_prompt_kernel(r_ref, w_ref, k_ref, v_ref, kk_ref, b_ref, xq_ref, xk_ref, xv_ref,
                        o_ref, sout_ref, xo_ref, s_ref):
    t_blk = pl.program_id(1)

    @pl.when(t_blk == 0)
    def _():
        s_ref[...] = jnp.zeros_like(s_ref)

    chains = [(j, p) for j in range(SCAN_BATCHES) for p in range(PAIRS)]
    lanes = lambda p: slice(p * PAIR_W, (p + 1) * PAIR_W)
    take = lambda ref: [ref[j, :, lanes(p)] for j, p in chains]
    os_, ss = _run_with(
        _chunk_pairs([s_ref[j, p] for j, p in chains], take(r_ref), take(w_ref), take(k_ref),
                     take(v_ref), take(kk_ref), take(b_ref)),
        _xa_attend(xq_ref, xk_ref, xv_ref, xo_ref), every=XA_EVERY)
    for (j, p), o, s_new in zip(chains, os_, ss):
        o_ref[j, :, lanes(p)] = o
        s_ref[j, p] = s_new

    @pl.when(t_blk == pl.num_programs(1) - 1)
    def _():
        sout_ref[...] = s_ref[...]


def _scan_prompt(r, lw, k, v, kk, b, xq, xk, xv, *, batch, seq):
    n_t = seq // SCAN_C
    nb = SCAN_BATCHES
    steps = (batch // nb) * n_t
    n_s = xq.shape[0]
    assert n_s % steps == 0, (n_s, steps)
    per = n_s // steps
    spec = pl.BlockSpec((nb, SCAN_C, B_WIDTH), lambda bi, ti: (bi, ti, 0))
    sspec = pl.BlockSpec((nb, PAIRS, PAIR_W, PAIR_W), lambda bi, ti: (bi, 0, 0, 0))
    step = lambda bi, ti: (bi * n_t + ti, 0, 0)
    qspec = pl.BlockSpec((per, MEM_ROWS, 128), step)
    mspec = pl.BlockSpec((per, N_MEM * MEM_ROWS, 128), step)
    o, s, xo = pl.pallas_call(
        _scan_prompt_kernel,
        grid=(batch // nb, n_t),
        in_specs=[spec] * 6 + [qspec, mspec, mspec],
        out_specs=[spec, sspec, qspec],
        out_shape=[jax.ShapeDtypeStruct((batch, seq, B_WIDTH), F32),
                   jax.ShapeDtypeStruct((batch, PAIRS, PAIR_W, PAIR_W), F32),
                   jax.ShapeDtypeStruct(xq.shape, F32)],
        scratch_shapes=[pltpu.VMEM((nb, PAIRS, PAIR_W, PAIR_W), F32)],
        compiler_params=_params(2),
        name="scan_prompt",
    )(*[x.reshape(batch, seq, B_WIDTH) for x in (r, lw, k, v, kk, b)], xq, xk, xv)
    return o.reshape(batch * seq, B_WIDTH), s, xo


def _scan_sample_kernel(s_ref, r_ref, w_ref, k_ref, v_ref, kk_ref, b_ref, o_ref, sout_ref,
                        t_ref, ot_ref):
    h = pl.program_id(0)

    @pl.when(h == 0)
    def _():
        for i, ref in enumerate((r_ref, w_ref, k_ref, v_ref, kk_ref, b_ref)):
            t_ref[i] = ref[...].T

    base = pl.multiple_of(h * HEAD, HEAD)
    keys = pl.ds(base, HEAD)
    r, w, k, kk, b = [t_ref[i, keys, :] for i in (0, 1, 2, 4, 5)]

    def body(v8, carry):
        rows = pl.ds(pl.multiple_of(base + v8 * 8, 8), 8)
        v_rows = t_ref[3, rows, :]
        outs = []
        for j in range(8):
            vi = v8 * 8 + j
            s = s_ref[0, vi]
            sa = jnp.sum(s * kk, axis=0, keepdims=True)
            s = s * w - sa * b + v_rows[j:j + 1, :] * k
            sout_ref[0, vi] = s
            outs.append(jnp.sum(s * r, axis=0, keepdims=True))
        ot_ref[rows, :] = jnp.concatenate(outs, axis=0)
        return carry

    lax.fori_loop(0, HEAD // 8, body, 0)

    @pl.when(h == pl.num_programs(0) - 1)
    def _():
        o_ref[...] = ot_ref[...].T


def _scan_sample(state_t, r, w, k, v, kk, b):
    rows = r.shape[0]
    sspec = pl.BlockSpec((1, HEAD, HEAD, rows), lambda h: (h, 0, 0, 0))
    spec = _const_spec((rows, B_WIDTH))
    return pl.pallas_call(
        _scan_sample_kernel,
        grid=(HEADS,),
        in_specs=[sspec] + [spec] * 6,
        out_specs=[pl.BlockSpec((rows, B_WIDTH), lambda h: (0, 0)), sspec],
        out_shape=[jax.ShapeDtypeStruct((rows, B_WIDTH), F32),
                   jax.ShapeDtypeStruct(state_t.shape, F32)],
        scratch_shapes=[pltpu.VMEM((6, B_WIDTH, rows), F32), pltpu.VMEM((B_WIDTH, rows), F32)],
        compiler_params=_params(),
        name="scan_sample",
    )(state_t, r, w, k, v, kk, b)


def _softmax_rows(s):
    e = jnp.exp(s - jnp.max(s, axis=-1, keepdims=True))
    return e / jnp.sum(e, axis=-1, keepdims=True)


def _post_kernel(*refs, attend):
    it = iter(refs)
    (x_ref, ya_ref, o_ref, g_ref, bonus_ref, gng_ref, gnb_ref, ones_ref, woa_ref, wob_ref,
     lnx_ref, wq_ref) = [next(it) for _ in range(12)]
    if attend:
        mk_ref, mv_ref = next(it), next(it)
    x2_ref, out_ref = next(it), next(it)

    ones_bd = ones_ref[...]
    o = o_ref[...]
    mu = _seg_sum(o, ones_bd) * (1.0 / HEAD)
    d = o - mu
    var = _seg_sum(d * d, ones_bd) * (1.0 / HEAD)
    on = d * lax.rsqrt(var + GN_EPS) * gng_ref[...] + gnb_ref[...]
    yb = (on + bonus_ref[...]) * g_ref[...]
    x2 = x_ref[...] + jnp.dot(ya_ref[...], woa_ref[...], preferred_element_type=F32) \
        + _dot(yb, wob_ref[...])
    x2_ref[...] = x2
    q = _dot(_rms(x2, lnx_ref[...]), wq_ref[...])
    if not attend:
        out_ref[...] = q
        return
    qb = q.astype(BF16)
    for h in range(XA_HEADS):
        sl = slice(h * XA_DIM, (h + 1) * XA_DIM)
        s = lax.dot_general(qb[:, sl], mk_ref[0, :, sl], (((1,), (1,)), ((), ())),
                            preferred_element_type=F32) * (XA_DIM ** -0.5)
        p = _softmax_rows(s)
        out_ref[:, sl] = _dot(p, mv_ref[0, :, sl]).astype(BF16)


def _post_mix(x, ya, o, g, bonus, p, *, tm, mk=None, mv=None):
    rows = x.shape[0]
    attend = mk is not None
    args = [x, ya, o, g, bonus, p["gn_g"], p["gn_b"], p["ones_bd"], p["w_out_a"], p["w_out_b"],
            p["ln_xattn"], p["xa_q"]]
    specs = [_rows_spec(tm, D_MODEL)] + [_rows_spec(tm, B_WIDTH)] * 4 + [
        _const_spec((1, B_WIDTH)), _const_spec((1, B_WIDTH)), _const_spec((B_WIDTH, B_WIDTH)),
        _const_spec((A_WIDTH, D_MODEL)), _const_spec((B_WIDTH, D_MODEL)),
        _const_spec((1, D_MODEL)), _const_spec((D_MODEL, D_MODEL))]
    if attend:
        tiles_per_seq = SEQ // tm
        mspec = pl.BlockSpec((1, N_MEM, D_MODEL), lambda i: (i // tiles_per_seq, 0, 0))
        args += [mk, mv]
        specs += [mspec, mspec]
    return pl.pallas_call(
        functools.partial(_post_kernel, attend=attend),
        grid=(rows // tm,),
        in_specs=specs,
        out_specs=[_rows_spec(tm, D_MODEL)] * 2,
        out_shape=[jax.ShapeDtypeStruct((rows, D_MODEL), F32),
                   jax.ShapeDtypeStruct((rows, D_MODEL), BF16 if attend else F32)],
        compiler_params=_params(),
        name="post_mix",
    )(*args)


MEM_ROWS = XA_HEADS * (XA_DIM // 128)


def _lane_allreduce(x, op):
    shift = MEM_ROWS
    while shift < 128:
        x = op(x, pltpu.roll(x, shift, axis=1))
        shift *= 2
    return x


def _xa_attend(q_ref, k_ref, v_ref, o_ref):
    f0 = jnp.zeros((), F32)
    n_blk = N_MEM * MEM_ROWS // 128
    sub = lax.broadcasted_iota(jnp.int32, (MEM_ROWS, 128), 0)
    lane = lax.broadcasted_iota(jnp.int32, (MEM_ROWS, 128), 1)
    diag = sub == (lane % MEM_ROWS)
    li = lax.broadcasted_iota(jnp.int32, (128, 128), 0)
    lj = lax.broadcasted_iota(jnp.int32, (128, 128), 1)
    comb = jnp.where((li // MEM_ROWS == lj // MEM_ROWS) & (li % XA_HEADS == lj % XA_HEADS),
                     1.0, 0.0).astype(BF16)
    samples = list(range(q_ref.shape[0]))
    scs = [_mm_nt(q_ref[j], k_ref[j]) for j in samples]

    def partial(sc):
        return jnp.concatenate(
            [jnp.sum(jnp.where(diag, sc[:, t * 128:(t + 1) * 128], f0), axis=0, keepdims=True)
             for t in range(n_blk)], axis=0)

    def scores(part):
        hi = part.astype(BF16)
        lo = (part - hi.astype(F32)).astype(BF16)
        return (jnp.dot(hi, comb, preferred_element_type=F32)
                + jnp.dot(lo, comb, preferred_element_type=F32)) * (XA_DIM ** -0.5)

    def softmax(s):
        mx = _lane_allreduce(jnp.broadcast_to(jnp.max(s, axis=0, keepdims=True), (MEM_ROWS, 128)),
                             jnp.maximum)
        e = jnp.exp(s - mx[0:1, :])
        den = _lane_allreduce(jnp.broadcast_to(jnp.sum(e, axis=0, keepdims=True), (MEM_ROWS, 128)),
                              jnp.add)
        p = e / den[0:1, :]
        return jnp.concatenate(
            [jnp.where(diag, jnp.broadcast_to(p[t:t + 1, :], (MEM_ROWS, 128)), f0)
             for t in range(n_blk)], axis=1)

    yield
    parts = _each(partial, scs)
    yield
    ss = _each(scores, parts)
    yield
    p_rows = _each(softmax, ss)
    yield
    for j, p in zip(samples, p_rows):
        o_ref[j] = _mm(p, v_ref[j])


def _pad_lora(x, axis=-1):
    x = jnp.moveaxis(x, axis, -1)
    wd = x[..., :DECAY_LORA]
    ad = x[..., DECAY_LORA:DECAY_LORA + AAA_LORA]
    gd = x[..., DECAY_LORA + AAA_LORA:]
    z = lambda n: jnp.zeros(x.shape[:-1] + (n,), x.dtype)
    out = jnp.concatenate([wd, z(LORA_AD - DECAY_LORA), ad, z(LORA_GD - LORA_AD - AAA_LORA),
                           gd, z(LORA_W - LORA_GD - GATE_LORA)], axis=-1)
    return jnp.moveaxis(out, -1, axis)


def _unpad_shift(zm, zl):
    return jnp.concatenate([zm, zl[..., LORA_WD:LORA_WD + DECAY_LORA],
                            zl[..., LORA_AD:LORA_AD + AAA_LORA],
                            zl[..., LORA_GD:LORA_GD + GATE_LORA]], axis=-1)


def _pad_rows(w, n):
    return jnp.pad(w, ((0, n - w.shape[0]), (0, 0)))


def _mem_rows(x):
    b = x.shape[0]
    return x.reshape(b, N_MEM, XA_HEADS, XA_DIM // 128, 128).transpose(0, 1, 3, 2, 4).reshape(
        b, N_MEM * MEM_ROWS, 128)


def _from_mem_rows(x, b):
    return x.reshape(b, N_MEM, XA_DIM // 128, XA_HEADS, 128).transpose(0, 1, 3, 2, 4).reshape(
        b, N_MEM, XA_HEADS, XA_DIM)


def _head_rows(x):
    b = x.shape[0]
    return x.reshape(b, XA_HEADS, XA_DIM // 128, 128).transpose(0, 2, 1, 3).reshape(b, MEM_ROWS, 128)


def _from_head_rows(x):
    b = x.shape[0]
    return x.reshape(b, XA_DIM // 128, XA_HEADS, 128).transpose(0, 2, 1, 3).reshape(b, D_MODEL)


def kernel(x_prompt, x_sample, state_rwkv, state_shift, cache_mem_k, cache_mem_v, mem_prompt, ln_ffn1, ffn1_gate, ffn1_up, ffn1_down, ln_mix, w_in, w_out, sgu_w, sgu_b, sgu_ln_g, sgu_ln_b, rwkv_mu, rwkv_w0, rwkv_w2, rwkv_a0, rwkv_a2, rwkv_g2, rwkv_k_k, rwkv_k_a, rwkv_r_k, rwkv_gn_g, rwkv_gn_b, ln_xattn, mem_norm, xa_q, xa_k, xa_v, xa_o, ln_ffn2, ffn2_gate, ffn2_up, ffn2_down, final_norm):
    assert ln_ffn1.shape[0] == 1, "single layer"
    bp, seq, _ = x_prompt.shape
    bs = x_sample.shape[0]
    row = lambda a: a.reshape(1, -1).astype(F32)
    bf = lambda a: a.astype(BF16)
    l = 0
    head_id = jnp.arange(B_WIDTH) // HEAD
    tril = jnp.tril(jnp.ones((CHUNK, CHUNK), dtype=bool))
    wmask = jnp.where(tril[None], sgu_w[l], 0)
    p = {
        "ln_mix": row(ln_mix[l]),
        "w_main": bf(w_in[l].T[:MAIN_W]),
        "w_lora": bf(_pad_lora(w_in[l].T[MAIN_W:], axis=0)),
        "sgu_wcat": bf(wmask.transpose(1, 0, 2).reshape(CHUNK, A_GROUPS * CHUNK)),
        "sgu_bias": jnp.repeat(sgu_b[l].T, A_GROUP_DIM, axis=1),
        "sgu_w00": row(jnp.repeat(sgu_w[l][:, 0, 0], A_GROUP_DIM)),
        "sgu_b0": row(jnp.repeat(sgu_b[l][:, 0], A_GROUP_DIM)),
        "sgu_ln_g": row(sgu_ln_g[l]), "sgu_ln_b": row(sgu_ln_b[l]),
        "mu_main": row(rwkv_mu[l][:RKV_W]),
        "mu_lora": row(_pad_lora(rwkv_mu[l][RKV_W:])),
        "w0": row(rwkv_w0[l]), "w2": bf(_pad_rows(rwkv_w2[l], LORA_AD - LORA_WD)),
        "a0": row(rwkv_a0[l]), "a2": bf(_pad_rows(rwkv_a2[l], LORA_GD - LORA_AD)),
        "g2": bf(_pad_rows(rwkv_g2[l], LORA_W - LORA_GD)),
        "k_k": row(rwkv_k_k[l]), "k_a": row(rwkv_k_a[l]), "r_k": row(rwkv_r_k[l]),
        "ones_bd": (head_id[:, None] == head_id[None, :]).astype(BF16),
        "gn_g": row(rwkv_gn_g[l]), "gn_b": row(rwkv_gn_b[l]),
        "w_out_a": bf(w_out[l][:A_WIDTH]), "w_out_b": bf(w_out[l][A_WIDTH:]),
        "ln_xattn": row(ln_xattn[l]), "xa_q": bf(xa_q[l]),
    }
    ffn1 = (row(ln_ffn1[l]), bf(ffn1_gate[l]), bf(ffn1_up[l]), bf(ffn1_down[l]))
    ffn2 = (row(ln_ffn2[l]), bf(ffn2_gate[l]), bf(ffn2_up[l]), bf(ffn2_down[l]))
    xa_o_b = bf(xa_o[l])
    fnorm = row(final_norm)

    xs = x_sample.reshape(bs, D_MODEL)
    sh = state_shift[l].reshape(bs, B_PROJ)
    x1s = _ffn(xs, *ffn1, tm=bs)
    (ya_s, r_s, w_s, k_s, v_s, kk_s, b_s, g_s, bonus_s, va_s, zm_s, zl_s) = _mix_in(
        x1s, p, tm=bs, sample=True, shift_main=sh[:, :RKV_W], shift_lora=_pad_lora(sh[:, RKV_W:]))
    o_s, state_t = _scan_sample(jnp.transpose(state_rwkv[l], (1, 2, 3, 0)),
                                r_s, w_s, k_s, v_s, kk_s, b_s)
    state_s = jnp.transpose(state_t, (3, 0, 1, 2))
    x2s, q_s = _post_mix(x1s, ya_s, o_s, g_s, bonus_s, p, tm=bs)

    tm = 512
    xp = x_prompt.reshape(bp * seq, D_MODEL)
    mk, mv, mkb, mvb = _memkv(mem_prompt.reshape(bp * N_MEM, D_MODEL), row(mem_norm[l]),
                              bf(xa_k[l]), bf(xa_v[l]), tm=tm)
    x1 = _ffn(xp, *ffn1, tm=FFN_ROWS)
    ya, r, w, k, v, kk, b, g, bonus, zlast = _mix_in(x1, p, tm=tm, sample=False)
    o, s_bd, attn_rows = _scan_prompt(r, w, k, v, kk, b, _head_rows(q_s),
                                      _mem_rows(cache_mem_k[l]), _mem_rows(cache_mem_v[l]),
                                      batch=bp, seq=seq)
    state_p = jnp.stack([s_bd[:, :, :HEAD, :HEAD], s_bd[:, :, HEAD:, HEAD:]],
                        axis=2).reshape(bp, HEADS, HEAD, HEAD)
    x2, attn = _post_mix(x1, ya, o, g, bonus, p, tm=tm,
                         mk=mkb.reshape(bp, N_MEM, D_MODEL), mv=mvb.reshape(bp, N_MEM, D_MODEL))
    y_prompt = _ffn(x2, *ffn2, tm=FFN_ROWS, attn=attn, wo=xa_o_b, final_norm=fnorm)
    tiles_per_seq = seq // tm
    zl_rows = zlast.reshape(bp, tiles_per_seq, 8, RKV_W + LORA_W)[:, -1, 0]
    shift_p = _unpad_shift(zl_rows[:, :RKV_W], zl_rows[:, RKV_W:])

    y_sample = _ffn(x2s, *ffn2, tm=bs, attn=_from_head_rows(attn_rows), wo=xa_o_b,
                    final_norm=fnorm)

    return (y_prompt.reshape(bp, seq, D_MODEL),
            y_sample.reshape(bs, 1, D_MODEL),
            state_p[None],
            shift_p.reshape(1, bp, 1, B_PROJ),
            _from_mem_rows(mk, bp)[None],
            _from_mem_rows(mv, bp)[None],
            state_s[None],
            _unpad_shift(zm_s, zl_s).reshape(1, bs, 1, B_PROJ),
            va_s.reshape(1, bs, 1, A_WIDTH))
```

```python
import functools

import jax
import jax.numpy as jnp
from jax import lax
from jax.experimental import pallas as pl
from jax.experimental.pallas import tpu as pltpu

F32 = jnp.float32
BF16 = jnp.bfloat16

D_MODEL = 1024
SEQ = 2048
A_WIDTH = 512
A_GROUPS = 8
A_GROUP_DIM = 64
CHUNK = 128
B_WIDTH = 512
HEAD = 64
HEADS = 8
PAIRS = HEADS // 2
PAIR_W = 2 * HEAD
DECAY_LORA = 64
AAA_LORA = 64
GATE_LORA = 160
B_PROJ = 3 * B_WIDTH + DECAY_LORA + AAA_LORA + GATE_LORA
MAIN_W = 2 * A_WIDTH + 3 * B_WIDTH
RKV_W = 3 * B_WIDTH
LORA_W = 512
LORA_WD, LORA_AD, LORA_GD = 0, 128, 256
D_FF = 2816
N_MEM = 256
XA_HEADS = 4
XA_DIM = 256
NORM_EPS = 1e-6
LN_EPS = 1e-5
GN_EPS = 64e-5

VMEM_LIMIT = 56 * 1024 * 1024


def _params(n_axes=1):
    return pltpu.CompilerParams(dimension_semantics=("arbitrary",) * n_axes,
                                vmem_limit_bytes=VMEM_LIMIT)


def _const_spec(shape):
    nd = len(shape)
    return pl.BlockSpec(shape, lambda *_: (0,) * nd, pipeline_mode=pl.Buffered(1))


def _rows_spec(tm, width):
    return pl.BlockSpec((tm, width), lambda i: (i, 0))


def _rms(x, g):
    return x * lax.rsqrt(jnp.mean(x * x, axis=-1, keepdims=True) + NORM_EPS) * g


def _dot(a, b):
    return jnp.dot(a.astype(BF16), b, preferred_element_type=F32)


def _seg_sum(x, ones_bd):
    return jnp.dot(x.astype(BF16), ones_bd, preferred_element_type=F32)


FFN_ROWS = 1024
FFN_BLOCK = 768


def _ffn_kernel(*refs, pre, final, n_cast):
    it = iter(refs)
    x_ref = next(it)
    if pre:
        attn_ref, wo_ref = next(it), next(it)
    ln_ref, wg_ref, wu_ref, wd_ref = next(it), next(it), next(it), next(it)
    if final:
        fn_ref = next(it)
    cast_in = [next(it) for _ in range(n_cast)]
    o_ref = next(it)
    for src_ref in cast_in:
        next(it)[...] = src_ref[...].astype(BF16)
    x = x_ref[...]
    if pre:
        x = x + _dot(attn_ref[...], wo_ref[...])
    xb = _rms(x, ln_ref[...]).astype(BF16)
    y = None
    for c0 in range(0, D_FF, FFN_BLOCK):
        cols = slice(c0, min(c0 + FFN_BLOCK, D_FF))
        g = jnp.dot(xb, wg_ref[:, cols], preferred_element_type=F32)
        u = jnp.dot(xb, wu_ref[:, cols], preferred_element_type=F32)
        h = (g * jax.nn.sigmoid(g) * u).astype(BF16)
        part = jnp.dot(h, wd_ref[cols, :], preferred_element_type=F32)
        y = part if y is None else y + part
    x = x + 0.5 * y
    if final:
        x = _rms(x, fn_ref[...])
    o_ref[...] = x


def _ffn(x, ln, wg, wu, wd, *, tm, attn=None, wo=None, final_norm=None, cast=()):
    rows = x.shape[0]
    steps = rows // tm
    pre = attn is not None
    final = final_norm is not None
    args, specs = [x], [_rows_spec(tm, D_MODEL)]
    if pre:
        args += [attn, wo]
        specs += [_rows_spec(tm, D_MODEL), _const_spec((D_MODEL, D_MODEL))]
    args += [ln, wg, wu, wd]
    specs += [_const_spec((1, D_MODEL)), _const_spec((D_MODEL, D_FF)),
              _const_spec((D_MODEL, D_FF)), _const_spec((D_FF, D_MODEL))]
    if final:
        args.append(final_norm)
        specs.append(_const_spec((1, D_MODEL)))
    slabs = []
    for a in cast:
        assert a.shape[0] % (16 * steps) == 0, (a.shape, steps)
        slabs.append(_rows_spec(a.shape[0] // steps, a.shape[1]))
    out = pl.pallas_call(
        functools.partial(_ffn_kernel, pre=pre, final=final, n_cast=len(cast)),
        grid=(steps,),
        in_specs=specs + slabs,
        out_specs=[_rows_spec(tm, D_MODEL)] + slabs,
        out_shape=[jax.ShapeDtypeStruct((rows, D_MODEL), F32)]
        + [jax.ShapeDtypeStruct(a.shape, BF16) for a in cast],
        compiler_params=_params(),
        name="ffn",
    )(*args, *cast)
    return (out[0], list(out[1:])) if cast else out[0]


def _memkv_kernel(m_ref, g_ref, wk_ref, wv_ref, k_ref, v_ref, kb_ref, vb_ref):
    mb = _rms(m_ref[...], g_ref[...]).astype(BF16)
    tm = m_ref.shape[0]
    k = jnp.dot(mb, wk_ref[...], preferred_element_type=F32)
    v = jnp.dot(mb, wv_ref[...], preferred_element_type=F32)
    kb_ref[...] = k.astype(BF16)
    vb_ref[...] = v.astype(BF16)
    for c in range(MEM_ROWS):
        src = (c % XA_HEADS) * (XA_DIM // 128) + c // XA_HEADS
        k_ref[pl.ds(c, tm, stride=MEM_ROWS), :] = k[:, src * 128:(src + 1) * 128]
        v_ref[pl.ds(c, tm, stride=MEM_ROWS), :] = v[:, src * 128:(src + 1) * 128]


def _memkv(mem, g, wk, wv, *, tm):
    rows = mem.shape[0]
    out = jax.ShapeDtypeStruct((rows * MEM_ROWS, 128), F32)
    outb = jax.ShapeDtypeStruct((rows, D_MODEL), BF16)
    return pl.pallas_call(
        _memkv_kernel,
        grid=(rows // tm,),
        in_specs=[_rows_spec(tm, D_MODEL), _const_spec((1, D_MODEL)),
                  _const_spec((D_MODEL, D_MODEL)), _const_spec((D_MODEL, D_MODEL))],
        out_specs=[_rows_spec(tm * MEM_ROWS, 128)] * 2 + [_rows_spec(tm, D_MODEL)] * 2,
        out_shape=[out, out, outb, outb],
        compiler_params=_params(),
        name="memkv",
    )(mem, g, wk, wv)


def _mix_kernel(*refs, sample, tiles_per_seq):
    it = iter(refs)
    x_ref, ln_ref, wmain_ref, wlora_ref = next(it), next(it), next(it), next(it)
    if sample:
        w00_ref, b0_ref, spm_ref, spl_ref = next(it), next(it), next(it), next(it)
    else:
        wcat_ref, bias_ref = next(it), next(it)
    (lng_ref, lnb_ref, mum_ref, mul_ref, w0_ref, w2_ref, a0_ref, a2_ref, g2_ref,
     kk_ref, ka_ref, rk_ref, ones_ref) = [next(it) for _ in range(13)]
    (ya_ref, r_ref, w_ref, k_ref, v_ref, kn_ref, b_ref, g_ref, bonus_ref) = [
        next(it) for _ in range(9)]
    if sample:
        va_ref, zm_ref, zl_ref = next(it), next(it), next(it)
    else:
        zlast_ref, cm_ref, cl_ref = next(it), next(it), next(it)

    xb = _rms(x_ref[...], ln_ref[...]).astype(BF16)
    _mix_rest(_mm_nt(xb, wmain_ref[...]), _mm_nt(xb, wlora_ref[...]), dict(locals()))


def _mix_rest(zmain, zl, names):
    sample, tiles_per_seq = names["sample"], names["tiles_per_seq"]
    (lng_ref, lnb_ref, mum_ref, mul_ref, w0_ref, w2_ref, a0_ref, a2_ref, g2_ref, kk_ref, ka_ref,
     rk_ref, ones_ref, ya_ref, r_ref, w_ref, k_ref, v_ref, kn_ref, b_ref, g_ref, bonus_ref) = [
        names[n] for n in (
            "lng_ref", "lnb_ref", "mum_ref", "mul_ref", "w0_ref", "w2_ref", "a0_ref", "a2_ref",
            "g2_ref", "kk_ref", "ka_ref", "rk_ref", "ones_ref", "ya_ref", "r_ref", "w_ref",
            "k_ref", "v_ref", "kn_ref", "b_ref", "g_ref", "bonus_ref")]
    if sample:
        w00_ref, b0_ref, spm_ref, spl_ref, va_ref, zm_ref, zl_ref = [
            names[n] for n in ("w00_ref", "b0_ref", "spm_ref", "spl_ref", "va_ref", "zm_ref",
                               "zl_ref")]
    else:
        wcat_ref, bias_ref, zlast_ref, cm_ref, cl_ref = [
            names[n] for n in ("wcat_ref", "bias_ref", "zlast_ref", "cm_ref", "cl_ref")]
    tm = zl.shape[0]

    u = jax.nn.gelu(zmain[:, :A_WIDTH])
    vx = jax.nn.gelu(zmain[:, A_WIDTH:2 * A_WIDTH])
    mu = jnp.mean(vx, axis=-1, keepdims=True)
    var = jnp.mean(jnp.square(vx - mu), axis=-1, keepdims=True)
    va = (vx - mu) * lax.rsqrt(var + LN_EPS) * lng_ref[...] + lnb_ref[...]
    if sample:
        mixed = va * w00_ref[...] + b0_ref[...]
        ya_ref[...] = (u * mixed).astype(BF16)
        va_ref[...] = va
    else:
        vab = va.astype(BF16)
        first = lax.broadcasted_iota(jnp.int32, (CHUNK, 2 * A_GROUP_DIM), 1) < A_GROUP_DIM
        for c in range(tm // CHUNK):
            rows = slice(c * CHUNK, (c + 1) * CHUNK)
            for gp in range(A_GROUPS // 2):
                lanes = slice(gp * 2 * A_GROUP_DIM, (gp + 1) * 2 * A_GROUP_DIM)
                vc = vab[rows, lanes]
                zero = jnp.zeros_like(vc)
                rhs = jnp.concatenate([jnp.where(first, vc, zero), jnp.where(first, zero, vc)],
                                      axis=0)
                mixed = jnp.dot(wcat_ref[:, gp * 2 * CHUNK:(gp + 1) * 2 * CHUNK], rhs,
                                preferred_element_type=F32) + bias_ref[:, lanes]
                ya_ref[rows, lanes] = (u[rows, lanes] * mixed).astype(BF16)

    zbm = zmain[:, 2 * A_WIDTH:]
    if sample:
        zpm, zpl = spm_ref[...], spl_ref[...]
        zm_ref[...] = zbm
        zl_ref[...] = zl
    else:
        tile = pl.program_id(0)

        @pl.when(tile % tiles_per_seq == 0)
        def _():
            cm_ref[...] = jnp.zeros_like(cm_ref)
            cl_ref[...] = jnp.zeros_like(cl_ref)

        first_m = lax.broadcasted_iota(jnp.int32, zbm.shape, 0) == 0
        first_l = lax.broadcasted_iota(jnp.int32, zl.shape, 0) == 0
        zpm = jnp.where(first_m, cm_ref[0:1, :], pltpu.roll(zbm, 1, axis=0))
        zpl = jnp.where(first_l, cl_ref[0:1, :], pltpu.roll(zl, 1, axis=0))
        cm_ref[0:1, :] = zbm[tm - 1:tm, :]
        cl_ref[0:1, :] = zl[tm - 1:tm, :]
        zlast_ref[:, :RKV_W] = jnp.broadcast_to(zbm[tm - 1:tm, :], (8, RKV_W))
        zlast_ref[:, RKV_W:] = jnp.broadcast_to(zl[tm - 1:tm, :], (8, LORA_W))
    zsm = zbm + (zpm - zbm) * mum_ref[...]
    zsl = zl + (zpl - zl) * mul_ref[...]
    r = zsm[:, :B_WIDTH]
    k = zsm[:, B_WIDTH:2 * B_WIDTH]
    v = zsm[:, 2 * B_WIDTH:]
    wd = zsl[:, LORA_WD:LORA_AD]
    ad = zsl[:, LORA_AD:LORA_GD]
    gd = zsl[:, LORA_GD:]
    y = w0_ref[...] + _dot(jnp.tanh(wd), w2_ref[...])
    w_log = jnp.minimum(y, 0.0) - jnp.log1p(jnp.exp(-jnp.abs(y))) - 0.5
    log_decay = -jnp.exp(w_log)
    a = jax.nn.sigmoid(a0_ref[...] + _dot(ad, a2_ref[...]))
    gate = _dot(jax.nn.sigmoid(gd), g2_ref[...])
    ones_bd = ones_ref[...]
    kk = k * kk_ref[...]
    kk = kk * lax.rsqrt(jnp.maximum(_seg_sum(kk * kk, ones_bd), 1e-24))
    k2 = k * (1.0 + (a - 1.0) * ka_ref[...])
    r_ref[...] = r
    w_ref[...] = jnp.exp(log_decay) if sample else log_decay
    k_ref[...] = k2
    v_ref[...] = v
    kn_ref[...] = kk
    b_ref[...] = kk * a
    g_ref[...] = gate
    bonus_ref[...] = _seg_sum(r * k2 * rk_ref[...], ones_bd) * v


def _mix_in(x, p, *, tm, sample, shift_main=None, shift_lora=None):
    rows = x.shape[0]
    n_tiles = rows // tm
    args = [x, p["ln_mix"], p["w_main"], p["w_lora"]]
    specs = [_rows_spec(tm, D_MODEL), _const_spec((1, D_MODEL)),
             _const_spec((MAIN_W, D_MODEL)), _const_spec((LORA_W, D_MODEL))]
    if sample:
        args += [p["sgu_w00"], p["sgu_b0"], shift_main, shift_lora]
        specs += [_const_spec((1, A_WIDTH)), _const_spec((1, A_WIDTH)),
                  _rows_spec(tm, RKV_W), _rows_spec(tm, LORA_W)]
    else:
        args += [p["sgu_wcat"], p["sgu_bias"]]
        specs += [_const_spec((CHUNK, A_GROUPS * CHUNK)), _const_spec((CHUNK, A_WIDTH))]
    args += [p["sgu_ln_g"], p["sgu_ln_b"], p["mu_main"], p["mu_lora"], p["w0"], p["w2"],
             p["a0"], p["a2"], p["g2"], p["k_k"], p["k_a"], p["r_k"], p["ones_bd"]]
    specs += [_const_spec((1, A_WIDTH)), _const_spec((1, A_WIDTH)), _const_spec((1, RKV_W)),
              _const_spec((1, LORA_W)), _const_spec((1, B_WIDTH)),
              _const_spec((LORA_AD - LORA_WD, B_WIDTH)), _const_spec((1, B_WIDTH)),
              _const_spec((LORA_GD - LORA_AD, B_WIDTH)), _const_spec((LORA_W - LORA_GD, B_WIDTH)),
              _const_spec((1, B_WIDTH)), _const_spec((1, B_WIDTH)), _const_spec((1, B_WIDTH)),
              _const_spec((B_WIDTH, B_WIDTH))]
    wide = jax.ShapeDtypeStruct((rows, B_WIDTH), F32)
    out_shape = [jax.ShapeDtypeStruct((rows, A_WIDTH), BF16)] + [wide] * 8
    out_specs = [_rows_spec(tm, B_WIDTH)] * 9
    scratch = []
    if sample:
        out_shape += [wide, jax.ShapeDtypeStruct((rows, RKV_W), F32),
                      jax.ShapeDtypeStruct((rows, LORA_W), F32)]
        out_specs += [_rows_spec(tm, A_WIDTH), _rows_spec(tm, RKV_W), _rows_spec(tm, LORA_W)]
    else:
        out_shape += [jax.ShapeDtypeStruct((n_tiles * 8, RKV_W + LORA_W), F32)]
        out_specs += [pl.BlockSpec((8, RKV_W + LORA_W), lambda i: (i, 0))]
        scratch = [pltpu.VMEM((8, RKV_W), F32), pltpu.VMEM((8, LORA_W), F32)]
    return pl.pallas_call(
        functools.partial(_mix_kernel, sample=sample, tiles_per_seq=max(SEQ // tm, 1)),
        grid=(n_tiles,),
        in_specs=specs,
        out_specs=out_specs,
        out_shape=out_shape,
        scratch_shapes=scratch,
        compiler_params=_params(),
        name="mix_in",
    )(*args)


def _each(f, *lists):
    return [f(*xs) for xs in zip(*lists)]


SCAN_C = 64


def _mm(a, b):
    return jnp.dot(a.astype(BF16), b.astype(BF16), preferred_element_type=F32)


def _mm_nt(a, b):
    return lax.dot_general(a.astype(BF16), b.astype(BF16), (((1,), (1,)), ((), ())),
                           preferred_element_type=F32)


def _mm_tn(a, b):
    return lax.dot_general(a.astype(BF16), b.astype(BF16), (((0,), (0,)), ((), ())),
                           preferred_element_type=F32)


def _cumsum_rows(x):
    n = x.shape[0]
    row = lax.broadcasted_iota(jnp.int32, x.shape, 0)
    s = 1
    while s < n:
        x = x + jnp.where(row >= s, pltpu.roll(x, s, axis=0), 0.0)
        s *= 2
    return x


INV_BASE = 8


def _unit_lower_inverse(ns, row, col):
    f0 = jnp.zeros((), F32)
    same = lambda s: (row // s) == (col // s)
    eye = jnp.where(row == col, 1.0, f0)
    ps = _each(lambda n: jnp.where(same(INV_BASE), n, f0), ns)
    ts = _each(lambda p: eye + p, ps)
    s = 2
    while s < INV_BASE:
        ps = _each(lambda p: _mm(p, p), ps)
        yield
        ts = _each(lambda t, p: t + _mm(t, p), ts, ps)
        yield
        s *= 2
    s = INV_BASE
    while s < SCAN_C:
        level = same(2 * s) & jnp.logical_not(same(s))
        ws = _each(lambda n, t: _mm(jnp.where(level, n, f0), t), ns, ts)
        yield
        ts = _each(lambda t, w: t + _mm(t, w), ts, ws)
        yield
        s *= 2
    return ts


def _chunk_pairs(s0s, rs, lws, ks, vs, kks, bs):
    c = SCAN_C
    f0 = jnp.zeros((), F32)
    row = lax.broadcasted_iota(jnp.int32, (2 * c, PAIR_W), 0)
    col = lax.broadcasted_iota(jnp.int32, (2 * c, PAIR_W), 1)
    top, lft = row < c, col < HEAD
    same_head = top == lft
    strict = (row % c) > (col % HEAD)
    row_c = lax.broadcasted_iota(jnp.int32, (c, PAIR_W), 0)
    col_c = lax.broadcasted_iota(jnp.int32, (c, PAIR_W), 1)
    lft_c = col_c < HEAD
    strict_c = row_c > (col_c % HEAD)
    incl_c = row_c >= (col_c % HEAD)

    def prep(r, lw, k, v, kk, b):
        cum = _cumsum_rows(lw)
        end = cum[c - 1:c, :]
        a_t = -kk * jnp.exp(cum - lw)
        r_t = r * jnp.exp(cum)
        einv = jnp.exp(-cum)
        eend = jnp.exp(end - cum)
        return dict(
            x0=jnp.concatenate([a_t, r_t], axis=0), x1=jnp.concatenate([r_t, a_t], axis=0),
            bk=jnp.concatenate([b * einv, k * einv], axis=0),
            kb=jnp.concatenate([k * einv, b * einv], axis=0),
            bk_e=jnp.concatenate([b * eend, k * eend], axis=0),
            w_end=jnp.exp(end), v=v,
            v_l=jnp.where(lft_c, v, f0), v_r=jnp.where(lft_c, f0, v))

    fs = _each(prep, rs, lws, ks, vs, kks, bs)
    yield
    g0s = _each(lambda f: _mm_nt(jnp.where(lft, f["x0"], f0), f["bk"]), fs)
    g1s = _each(lambda f: _mm_nt(jnp.where(lft, f0, f["x1"]), f["kb"]), fs)
    pqs = _each(lambda f, s0: _mm_nt(f["x0"], s0), fs, s0s)
    yield

    def rhs(f, g0, g1, pq):
        ak = jnp.where(strict_c, jnp.where(lft_c, g1[c:], g0[:c]), f0)
        x = pq[:c] + _mm(ak, jnp.concatenate([f["v_r"], f["v_l"]], axis=0))
        return jnp.concatenate([jnp.where(lft_c, x, f0), jnp.where(lft_c, f0, x)], axis=0)

    ys = _each(rhs, fs, g0s, g1s, pqs)
    yield
    ns = _each(lambda g0, g1: jnp.where(strict & same_head, jnp.where(top, g0, g1), f0),
               g0s, g1s)
    ts = yield from _unit_lower_inverse(ns, row, col)
    ys = _each(_mm, ts, ys)
    yield

    def out(f, g0, g1, pq, y):
        lhs = jnp.concatenate([jnp.where(incl_c, g0[c:], f0), jnp.where(incl_c, g1[:c], f0)],
                              axis=1)
        return pq[c:] + _mm(lhs, jnp.concatenate([y[:c], f["v_l"], f["v_r"], y[c:]], axis=0))

    def state(f, s0, y):
        upd = _mm_tn(jnp.concatenate([y[:c] + y[c:], f["v"]], axis=0), f["bk_e"])
        return s0 * f["w_end"] + jnp.where(same_head, upd, f0)

    outs = _each(out, fs, g0s, g1s, pqs, ys)
    yield
    return outs, _each(state, fs, s0s, ys)


SCAN_BATCHES = 4
XA_EVERY = 3


def _run_with(main, side, *, every):
    n = 0
    while True:
        if n % every == 0:
            next(side, None)
        n += 1
        try:
            next(main)
        except StopIteration as stop:
            for _ in side:
                pass
            return stop.value


def _scan_prompt_kernel(r_ref, w_ref, k_ref, v_ref, kk_ref, b_ref, xq_ref, xk_ref, xv_ref,
                        o_ref, sout_ref, xo_ref, s_ref):
    t_blk = pl.program_id(1)

    @pl.when(t_blk == 0)
    def _():
        s_ref[...] = jnp.zeros_like(s_ref)

    chains = [(j, p) for j in range(SCAN_BATCHES) for p in range(PAIRS)]
    lanes = lambda p: slice(p * PAIR_W, (p + 1) * PAIR_W)
    take = lambda ref: [ref[j, :, lanes(p)] for j, p in chains]
    os_, ss = _run_with(
        _chunk_pairs([s_ref[j, p] for j, p in chains], take(r_ref), take(w_ref), take(k_ref),
                     take(v_ref), take(kk_ref), take(b_ref)),
        _xa_attend(xq_ref, xk_ref, xv_ref, xo_ref), every=XA_EVERY)
    for (j, p), o, s_new in zip(chains, os_, ss):
        o_ref[j, :, lanes(p)] = o
        s_ref[j, p] = s_new

    @pl.when(t_blk == pl.num_programs(1) - 1)
    def _():
        sout_ref[...] = s_ref[...]


def _scan_prompt(r, lw, k, v, kk, b, xq, xk, xv, *, batch, seq):
    n_t = seq // SCAN_C
    nb = SCAN_BATCHES
    steps = (batch // nb) * n_t
    n_s = xq.shape[0]
    assert n_s % steps == 0, (n_s, steps)
    per = n_s // steps
    spec = pl.BlockSpec((nb, SCAN_C, B_WIDTH), lambda bi, ti: (bi, ti, 0))
    sspec = pl.BlockSpec((nb, PAIRS, PAIR_W, PAIR_W), lambda bi, ti: (bi, 0, 0, 0))
    step = lambda bi, ti: (bi * n_t + ti, 0, 0)
    qspec = pl.BlockSpec((per, MEM_ROWS, 128), step)
    mspec = pl.BlockSpec((per, N_MEM * MEM_ROWS, 128), step)
    o, s, xo = pl.pallas_call(
        _scan_prompt_kernel,
        grid=(batch // nb, n_t),
        in_specs=[spec] * 6 + [qspec, mspec, mspec],
        out_specs=[spec, sspec, qspec],
        out_shape=[jax.ShapeDtypeStruct((batch, seq, B_WIDTH), F32),
                   jax.ShapeDtypeStruct((batch, PAIRS, PAIR_W, PAIR_W), F32),
                   jax.ShapeDtypeStruct(xq.shape, F32)],
        scratch_shapes=[pltpu.VMEM((nb, PAIRS, PAIR_W, PAIR_W), F32)],
        compiler_params=_params(2),
        name="scan_prompt",
    )(*[x.reshape(batch, seq, B_WIDTH) for x in (r, lw, k, v, kk, b)], xq, xk, xv)
    return o.reshape(batch * seq, B_WIDTH), s, xo


def _scan_sample_kernel(s_ref, r_ref, w_ref, k_ref, v_ref, kk_ref, b_ref, o_ref, sout_ref,
                        t_ref, ot_ref):
    h = pl.program_id(0)

    @pl.when(h == 0)
    def _():
        for i, ref in enumerate((r_ref, w_ref, k_ref, v_ref, kk_ref, b_ref)):
            t_ref[i] = ref[...].T

    base = pl.multiple_of(h * HEAD, HEAD)
    keys = pl.ds(base, HEAD)
    r, w, k, kk, b = [t_ref[i, keys, :] for i in (0, 1, 2, 4, 5)]

    def body(v8, carry):
        rows = pl.ds(pl.multiple_of(base + v8 * 8, 8), 8)
        v_rows = t_ref[3, rows, :]
        outs = []
        for j in range(8):
            vi = v8 * 8 + j
            s = s_ref[0, vi]
            sa = jnp.sum(s * kk, axis=0, keepdims=True)
            s = s * w - sa * b + v_rows[j:j + 1, :] * k
            sout_ref[0, vi] = s
            outs.append(jnp.sum(s * r, axis=0, keepdims=True))
        ot_ref[rows, :] = jnp.concatenate(outs, axis=0)
        return carry

    lax.fori_loop(0, HEAD // 8, body, 0)

    @pl.when(h == pl.num_programs(0) - 1)
    def _():
        o_ref[...] = ot_ref[...].T


def _scan_sample(state_t, r, w, k, v, kk, b):
    rows = r.shape[0]
    sspec = pl.BlockSpec((1, HEAD, HEAD, rows), lambda h: (h, 0, 0, 0))
    spec = _const_spec((rows, B_WIDTH))
    return pl.pallas_call(
        _scan_sample_kernel,
        grid=(HEADS,),
        in_specs=[sspec] + [spec] * 6,
        out_specs=[pl.BlockSpec((rows, B_WIDTH), lambda h: (0, 0)), sspec],
        out_shape=[jax.ShapeDtypeStruct((rows, B_WIDTH), F32),
                   jax.ShapeDtypeStruct(state_t.shape, F32)],
        scratch_shapes=[pltpu.VMEM((6, B_WIDTH, rows), F32), pltpu.VMEM((B_WIDTH, rows), F32)],
        compiler_params=_params(),
        name="scan_sample",
    )(state_t, r, w, k, v, kk, b)


def _softmax_rows(s):
    e = jnp.exp(s - jnp.max(s, axis=-1, keepdims=True))
    return e / jnp.sum(e, axis=-1, keepdims=True)


def _post_kernel(*refs, attend):
    it = iter(refs)
    (x_ref, ya_ref, o_ref, g_ref, bonus_ref, gng_ref, gnb_ref, ones_ref, wo_ref, lnx_ref,
     wq_ref) = [next(it) for _ in range(11)]
    if attend:
        mk_ref, mv_ref = next(it), next(it)
    x2_ref, out_ref = next(it), next(it)

    ones_bd = ones_ref[...]
    o = o_ref[...]
    mu = _seg_sum(o, ones_bd) * (1.0 / HEAD)
    d = o - mu
    var = _seg_sum(d * d, ones_bd) * (1.0 / HEAD)
    on = d * lax.rsqrt(var + GN_EPS) * gng_ref[...] + gnb_ref[...]
    yb = (on + bonus_ref[...]) * g_ref[...]
    x2 = x_ref[...] + jnp.dot(ya_ref[...], wo_ref[:A_WIDTH, :], preferred_element_type=F32) \
        + _dot(yb, wo_ref[A_WIDTH:, :])
    x2_ref[...] = x2
    q = _dot(_rms(x2, lnx_ref[...]), wq_ref[...])
    if not attend:
        out_ref[...] = q
        return
    qb = q.astype(BF16)
    for h in range(XA_HEADS):
        sl = slice(h * XA_DIM, (h + 1) * XA_DIM)
        s = lax.dot_general(qb[:, sl], mk_ref[0, :, sl], (((1,), (1,)), ((), ())),
                            preferred_element_type=F32) * (XA_DIM ** -0.5)
        p = _softmax_rows(s)
        out_ref[:, sl] = _dot(p, mv_ref[0, :, sl]).astype(BF16)


def _post_mix(x, ya, o, g, bonus, p, *, tm, mk=None, mv=None):
    rows = x.shape[0]
    attend = mk is not None
    args = [x, ya, o, g, bonus, p["gn_g"], p["gn_b"], p["ones_bd"], p["w_out"], p["ln_xattn"],
            p["xa_q"]]
    specs = [_rows_spec(tm, D_MODEL)] + [_rows_spec(tm, B_WIDTH)] * 4 + [
        _const_spec((1, B_WIDTH)), _const_spec((1, B_WIDTH)), _const_spec((B_WIDTH, B_WIDTH)),
        _const_spec((A_WIDTH + B_WIDTH, D_MODEL)),
        _const_spec((1, D_MODEL)), _const_spec((D_MODEL, D_MODEL))]
    if attend:
        tiles_per_seq = SEQ // tm
        mspec = pl.BlockSpec((1, N_MEM, D_MODEL), lambda i: (i // tiles_per_seq, 0, 0))
        args += [mk, mv]
        specs += [mspec, mspec]
    return pl.pallas_call(
        functools.partial(_post_kernel, attend=attend),
        grid=(rows // tm,),
        in_specs=specs,
        out_specs=[_rows_spec(tm, D_MODEL)] * 2,
        out_shape=[jax.ShapeDtypeStruct((rows, D_MODEL), F32),
                   jax.ShapeDtypeStruct((rows, D_MODEL), BF16 if attend else F32)],
        compiler_params=_params(),
        name="post_mix",
    )(*args)


MEM_ROWS = XA_HEADS * (XA_DIM // 128)


def _lane_allreduce(x, op):
    shift = MEM_ROWS
    while shift < 128:
        x = op(x, pltpu.roll(x, shift, axis=1))
        shift *= 2
    return x


def _xa_attend(q_ref, k_ref, v_ref, o_ref):
    f0 = jnp.zeros((), F32)
    n_blk = N_MEM * MEM_ROWS // 128
    sub = lax.broadcasted_iota(jnp.int32, (MEM_ROWS, 128), 0)
    lane = lax.broadcasted_iota(jnp.int32, (MEM_ROWS, 128), 1)
    diag = sub == (lane % MEM_ROWS)
    li = lax.broadcasted_iota(jnp.int32, (128, 128), 0)
    lj = lax.broadcasted_iota(jnp.int32, (128, 128), 1)
    comb = jnp.where((li // MEM_ROWS == lj // MEM_ROWS) & (li % XA_HEADS == lj % XA_HEADS),
                     1.0, 0.0).astype(BF16)
    samples = list(range(q_ref.shape[0]))
    scs = [_mm_nt(q_ref[j], k_ref[j]) for j in samples]

    def partial(sc):
        return jnp.concatenate(
            [jnp.sum(jnp.where(diag, sc[:, t * 128:(t + 1) * 128], f0), axis=0, keepdims=True)
             for t in range(n_blk)], axis=0)

    def scores(part):
        hi = part.astype(BF16)
        lo = (part - hi.astype(F32)).astype(BF16)
        return (jnp.dot(hi, comb, preferred_element_type=F32)
                + jnp.dot(lo, comb, preferred_element_type=F32)) * (XA_DIM ** -0.5)

    def softmax(s):
        mx = _lane_allreduce(jnp.broadcast_to(jnp.max(s, axis=0, keepdims=True), (MEM_ROWS, 128)),
                             jnp.maximum)
        e = jnp.exp(s - mx[0:1, :])
        den = _lane_allreduce(jnp.broadcast_to(jnp.sum(e, axis=0, keepdims=True), (MEM_ROWS, 128)),
                              jnp.add)
        p = e / den[0:1, :]
        return jnp.concatenate(
            [jnp.where(diag, jnp.broadcast_to(p[t:t + 1, :], (MEM_ROWS, 128)), f0)
             for t in range(n_blk)], axis=1)

    yield
    parts = _each(partial, scs)
    yield
    ss = _each(scores, parts)
    yield
    p_rows = _each(softmax, ss)
    yield
    for j, p in zip(samples, p_rows):
        o_ref[j] = _mm(p, v_ref[j])


def _pad_lora(x, axis=-1):
    x = jnp.moveaxis(x, axis, -1)
    wd = x[..., :DECAY_LORA]
    ad = x[..., DECAY_LORA:DECAY_LORA + AAA_LORA]
    gd = x[..., DECAY_LORA + AAA_LORA:]
    z = lambda n: jnp.zeros(x.shape[:-1] + (n,), x.dtype)
    out = jnp.concatenate([wd, z(LORA_AD - DECAY_LORA), ad, z(LORA_GD - LORA_AD - AAA_LORA),
                           gd, z(LORA_W - LORA_GD - GATE_LORA)], axis=-1)
    return jnp.moveaxis(out, -1, axis)


def _unpad_shift(zm, zl):
    return jnp.concatenate([zm, zl[..., LORA_WD:LORA_WD + DECAY_LORA],
                            zl[..., LORA_AD:LORA_AD + AAA_LORA],
                            zl[..., LORA_GD:LORA_GD + GATE_LORA]], axis=-1)


def _pad_rows(w, n):
    return jnp.pad(w, ((0, n - w.shape[0]), (0, 0)))


def _mem_rows(x):
    b = x.shape[0]
    return x.reshape(b, N_MEM, XA_HEADS, XA_DIM // 128, 128).transpose(0, 1, 3, 2, 4).reshape(
        b, N_MEM * MEM_ROWS, 128)


def _from_mem_rows(x, b):
    return x.reshape(b, N_MEM, XA_DIM // 128, XA_HEADS, 128).transpose(0, 1, 3, 2, 4).reshape(
        b, N_MEM, XA_HEADS, XA_DIM)


def _head_rows(x):
    b = x.shape[0]
    return x.reshape(b, XA_HEADS, XA_DIM // 128, 128).transpose(0, 2, 1, 3).reshape(b, MEM_ROWS, 128)


def _from_head_rows(x):
    b = x.shape[0]
    return x.reshape(b, XA_DIM // 128, XA_HEADS, 128).transpose(0, 2, 1, 3).reshape(b, D_MODEL)


def kernel(x_prompt, x_sample, state_rwkv, state_shift, cache_mem_k, cache_mem_v, mem_prompt, ln_ffn1, ffn1_gate, ffn1_up, ffn1_down, ln_mix, w_in, w_out, sgu_w, sgu_b, sgu_ln_g, sgu_ln_b, rwkv_mu, rwkv_w0, rwkv_w2, rwkv_a0, rwkv_a2, rwkv_g2, rwkv_k_k, rwkv_k_a, rwkv_r_k, rwkv_gn_g, rwkv_gn_b, ln_xattn, mem_norm, xa_q, xa_k, xa_v, xa_o, ln_ffn2, ffn2_gate, ffn2_up, ffn2_down, final_norm):
    assert ln_ffn1.shape[0] == 1, "single layer"
    bp, seq, _ = x_prompt.shape
    bs = x_sample.shape[0]
    row = lambda a: a.reshape(1, -1).astype(F32)
    bf = lambda a: a.astype(BF16)
    l = 0
    head_id = jnp.arange(B_WIDTH) // HEAD
    tril = jnp.tril(jnp.ones((CHUNK, CHUNK), dtype=bool))
    wmask = jnp.where(tril[None], sgu_w[l], 0)
    p = {
        "ln_mix": row(ln_mix[l]),
        "w_main": bf(w_in[l].T[:MAIN_W]),
        "w_lora": bf(_pad_lora(w_in[l].T[MAIN_W:], axis=0)),
        "sgu_wcat": bf(wmask.transpose(1, 0, 2).reshape(CHUNK, A_GROUPS * CHUNK)),
        "sgu_bias": jnp.repeat(sgu_b[l].T, A_GROUP_DIM, axis=1),
        "sgu_w00": row(jnp.repeat(sgu_w[l][:, 0, 0], A_GROUP_DIM)),
        "sgu_b0": row(jnp.repeat(sgu_b[l][:, 0], A_GROUP_DIM)),
        "sgu_ln_g": row(sgu_ln_g[l]), "sgu_ln_b": row(sgu_ln_b[l]),
        "mu_main": row(rwkv_mu[l][:RKV_W]),
        "mu_lora": row(_pad_lora(rwkv_mu[l][RKV_W:])),
        "w0": row(rwkv_w0[l]), "w2": bf(_pad_rows(rwkv_w2[l], LORA_AD - LORA_WD)),
        "a0": row(rwkv_a0[l]), "a2": bf(_pad_rows(rwkv_a2[l], LORA_GD - LORA_AD)),
        "g2": bf(_pad_rows(rwkv_g2[l], LORA_W - LORA_GD)),
        "k_k": row(rwkv_k_k[l]), "k_a": row(rwkv_k_a[l]), "r_k": row(rwkv_r_k[l]),
        "ones_bd": (head_id[:, None] == head_id[None, :]).astype(BF16),
        "gn_g": row(rwkv_gn_g[l]), "gn_b": row(rwkv_gn_b[l]),
        "ln_xattn": row(ln_xattn[l]),
    }
    ffn1 = (row(ln_ffn1[l]), bf(ffn1_gate[l]), bf(ffn1_up[l]), bf(ffn1_down[l]))
    fnorm = row(final_norm)

    tm = 512
    xp = x_prompt.reshape(bp * seq, D_MODEL)
    x1, (wg2, wu2, wd2, p["xa_q"], xa_k_b, xa_v_b, xa_o_b, p["w_out"]) = _ffn(
        xp, *ffn1, tm=FFN_ROWS,
        cast=(ffn2_gate[l], ffn2_up[l], ffn2_down[l], xa_q[l], xa_k[l], xa_v[l], xa_o[l], w_out[l]))
    ffn2 = (row(ln_ffn2[l]), wg2, wu2, wd2)

    xs = x_sample.reshape(bs, D_MODEL)
    sh = state_shift[l].reshape(bs, B_PROJ)
    x1s = _ffn(xs, *ffn1, tm=bs)
    (ya_s, r_s, w_s, k_s, v_s, kk_s, b_s, g_s, bonus_s, va_s, zm_s, zl_s) = _mix_in(
        x1s, p, tm=bs, sample=True, shift_main=sh[:, :RKV_W], shift_lora=_pad_lora(sh[:, RKV_W:]))
    o_s, state_t = _scan_sample(jnp.transpose(state_rwkv[l], (1, 2, 3, 0)),
                                r_s, w_s, k_s, v_s, kk_s, b_s)
    state_s = jnp.transpose(state_t, (3, 0, 1, 2))
    x2s, q_s = _post_mix(x1s, ya_s, o_s, g_s, bonus_s, p, tm=bs)

    mk, mv, mkb, mvb = _memkv(mem_prompt.reshape(bp * N_MEM, D_MODEL), row(mem_norm[l]),
                              xa_k_b, xa_v_b, tm=tm)
    ya, r, w, k, v, kk, b, g, bonus, zlast = _mix_in(x1, p, tm=tm, sample=False)
    o, s_bd, attn_rows = _scan_prompt(r, w, k, v, kk, b, _head_rows(q_s),
                                      _mem_rows(cache_mem_k[l]), _mem_rows(cache_mem_v[l]),
                                      batch=bp, seq=seq)
    state_p = jnp.stack([s_bd[:, :, :HEAD, :HEAD], s_bd[:, :, HEAD:, HEAD:]],
                        axis=2).reshape(bp, HEADS, HEAD, HEAD)
    x2, attn = _post_mix(x1, ya, o, g, bonus, p, tm=tm,
                         mk=mkb.reshape(bp, N_MEM, D_MODEL), mv=mvb.reshape(bp, N_MEM, D_MODEL))
    y_prompt = _ffn(x2, *ffn2, tm=FFN_ROWS, attn=attn, wo=xa_o_b, final_norm=fnorm)
    tiles_per_seq = seq // tm
    zl_rows = zlast.reshape(bp, tiles_per_seq, 8, RKV_W + LORA_W)[:, -1, 0]
    shift_p = _unpad_shift(zl_rows[:, :RKV_W], zl_rows[:, RKV_W:])

    y_sample = _ffn(x2s, *ffn2, tm=bs, attn=_from_head_rows(attn_rows), wo=xa_o_b,
                    final_norm=fnorm)

    return (y_prompt.reshape(bp, seq, D_MODEL),
            y_sample.reshape(bs, 1, D_MODEL),
            state_p[None],
            shift_p.reshape(1, bp, 1, B_PROJ),
            _from_mem_rows(mk, bp)[None],
            _from_mem_rows(mv, bp)[None],
            state_s[None],
            _unpad_shift(zm_s, zl_s).reshape(1, bs, 1, B_PROJ),
            va_s.reshape(1, bs, 1, A_WIDTH))
```

```python
import functools

import jax
import jax.numpy as jnp
from jax import lax
from jax.experimental import pallas as pl
from jax.experimental.pallas import tpu as pltpu

F32 = jnp.float32
BF16 = jnp.bfloat16

D_MODEL = 1024
SEQ = 2048
A_WIDTH = 512
A_GROUPS = 8
A_GROUP_DIM = 64
CHUNK = 128
B_WIDTH = 512
HEAD = 64
HEADS = 8
PAIRS = HEADS // 2
PAIR_W = 2 * HEAD
DECAY_LORA = 64
AAA_LORA = 64
GATE_LORA = 160
B_PROJ = 3 * B_WIDTH + DECAY_LORA + AAA_LORA + GATE_LORA
MAIN_W = 2 * A_WIDTH + 3 * B_WIDTH
RKV_W = 3 * B_WIDTH
LORA_W = 512
LORA_WD, LORA_AD, LORA_GD = 0, 128, 256
D_FF = 2816
N_MEM = 256
XA_HEADS = 4
XA_DIM = 256
NORM_EPS = 1e-6
LN_EPS = 1e-5
GN_EPS = 64e-5

VMEM_LIMIT = 56 * 1024 * 1024


def _params(n_axes=1):
    return pltpu.CompilerParams(dimension_semantics=("arbitrary",) * n_axes,
                                vmem_limit_bytes=VMEM_LIMIT)


def _const_spec(shape):
    nd = len(shape)
    return pl.BlockSpec(shape, lambda *_: (0,) * nd, pipeline_mode=pl.Buffered(1))


def _rows_spec(tm, width):
    return pl.BlockSpec((tm, width), lambda i: (i, 0))


def _rms(x, g):
    return x * lax.rsqrt(jnp.mean(x * x, axis=-1, keepdims=True) + NORM_EPS) * g


def _dot(a, b):
    return jnp.dot(a.astype(BF16), b, preferred_element_type=F32)


def _seg_sum(x, ones_bd):
    return jnp.dot(x.astype(BF16), ones_bd, preferred_element_type=F32)


FFN_ROWS = 1024
FFN_BLOCK = 768


def _ffn_kernel(*refs, pre, final, n_cast):
    it = iter(refs)
    x_ref = next(it)
    if pre:
        attn_ref, wo_ref = next(it), next(it)
    ln_ref, wg_ref, wu_ref, wd_ref = next(it), next(it), next(it), next(it)
    if final:
        fn_ref = next(it)
    cast_in = [next(it) for _ in range(n_cast)]
    o_ref = next(it)
    for src_ref in cast_in:
        next(it)[...] = src_ref[...].astype(BF16)
    x = x_ref[...]
    if pre:
        x = x + _dot(attn_ref[...], wo_ref[...])
    xb = _rms(x, ln_ref[...]).astype(BF16)
    y = None
    for c0 in range(0, D_FF, FFN_BLOCK):
        cols = slice(c0, min(c0 + FFN_BLOCK, D_FF))
        g = jnp.dot(xb, wg_ref[:, cols], preferred_element_type=F32)
        u = jnp.dot(xb, wu_ref[:, cols], preferred_element_type=F32)
        h = (g * jax.nn.sigmoid(g) * u).astype(BF16)
        part = jnp.dot(h, wd_ref[cols, :], preferred_element_type=F32)
        y = part if y is None else y + part
    x = x + 0.5 * y
    if final:
        x = _rms(x, fn_ref[...])
    o_ref[...] = x


def _ffn(x, ln, wg, wu, wd, *, tm, attn=None, wo=None, final_norm=None, cast=()):
    rows = x.shape[0]
    steps = rows // tm
    pre = attn is not None
    final = final_norm is not None
    args, specs = [x], [_rows_spec(tm, D_MODEL)]
    if pre:
        args += [attn, wo]
        specs += [_rows_spec(tm, D_MODEL), _const_spec((D_MODEL, D_MODEL))]
    args += [ln, wg, wu, wd]
    specs += [_const_spec((1, D_MODEL)), _const_spec((D_MODEL, D_FF)),
              _const_spec((D_MODEL, D_FF)), _const_spec((D_FF, D_MODEL))]
    if final:
        args.append(final_norm)
        specs.append(_const_spec((1, D_MODEL)))
    slabs, cast_shapes = _cast_slabs(cast, steps)
    out = pl.pallas_call(
        functools.partial(_ffn_kernel, pre=pre, final=final, n_cast=len(cast)),
        grid=(steps,),
        in_specs=specs + slabs,
        out_specs=[_rows_spec(tm, D_MODEL)] + slabs,
        out_shape=[jax.ShapeDtypeStruct((rows, D_MODEL), F32)] + cast_shapes,
        compiler_params=_params(),
        name="ffn",
    )(*args, *cast)
    return (out[0], list(out[1:])) if cast else out[0]


MEMKV_ROWS = 256


def _memkv_kernel(m_ref, g_ref, wk_ref, wv_ref, *rest):
    n_cast = (len(rest) - 4) // 2
    k_ref, v_ref, kb_ref, vb_ref = rest[n_cast:n_cast + 4]
    for src_ref, dst_ref in zip(rest[:n_cast], rest[n_cast + 4:]):
        dst_ref[...] = src_ref[...].astype(BF16)
    mb = _rms(m_ref[...], g_ref[...]).astype(BF16)
    tm = m_ref.shape[0]
    k = _dot(mb, wk_ref[...].astype(BF16))
    v = _dot(mb, wv_ref[...].astype(BF16))
    kb_ref[...] = k.astype(BF16)
    vb_ref[...] = v.astype(BF16)
    for c in range(MEM_ROWS):
        src = (c % XA_HEADS) * (XA_DIM // 128) + c // XA_HEADS
        k_ref[pl.ds(c, tm, stride=MEM_ROWS), :] = k[:, src * 128:(src + 1) * 128]
        v_ref[pl.ds(c, tm, stride=MEM_ROWS), :] = v[:, src * 128:(src + 1) * 128]


def _cast_slabs(cast, steps):
    for a in cast:
        assert a.shape[0] % (16 * steps) == 0, (a.shape, steps)
    return ([_rows_spec(a.shape[0] // steps, a.shape[1]) for a in cast],
            [jax.ShapeDtypeStruct(a.shape, BF16) for a in cast])


def _memkv(mem, g, wk, wv, *, tm, cast=()):
    rows = mem.shape[0]
    steps = rows // tm
    out = jax.ShapeDtypeStruct((rows * MEM_ROWS, 128), F32)
    outb = jax.ShapeDtypeStruct((rows, D_MODEL), BF16)
    slabs, cast_shapes = _cast_slabs(cast, steps)
    res = pl.pallas_call(
        _memkv_kernel,
        grid=(steps,),
        in_specs=[_rows_spec(tm, D_MODEL), _const_spec((1, D_MODEL)),
                  _const_spec((D_MODEL, D_MODEL)), _const_spec((D_MODEL, D_MODEL))] + slabs,
        out_specs=[_rows_spec(tm * MEM_ROWS, 128)] * 2 + [_rows_spec(tm, D_MODEL)] * 2 + slabs,
        out_shape=[out, out, outb, outb] + cast_shapes,
        compiler_params=_params(),
        name="memkv",
    )(mem, g, wk, wv, *cast)
    return res[0], res[1], res[2], res[3], list(res[4:])


def _mix_kernel(*refs, sample, tiles_per_seq):
    it = iter(refs)
    x_ref, ln_ref, wmain_ref, wlora_ref = next(it), next(it), next(it), next(it)
    if sample:
        w00_ref, b0_ref, spm_ref, spl_ref = next(it), next(it), next(it), next(it)
    else:
        wcat_ref, bias_ref = next(it), next(it)
    (lng_ref, lnb_ref, mum_ref, mul_ref, w0_ref, w2_ref, a0_ref, a2_ref, g2_ref,
     kk_ref, ka_ref, rk_ref, ones_ref) = [next(it) for _ in range(13)]
    (ya_ref, r_ref, w_ref, k_ref, v_ref, kn_ref, b_ref, g_ref, bonus_ref) = [
        next(it) for _ in range(9)]
    if sample:
        va_ref, zm_ref, zl_ref = next(it), next(it), next(it)
    else:
        zlast_ref, cm_ref, cl_ref = next(it), next(it), next(it)

    xb = _rms(x_ref[...], ln_ref[...]).astype(BF16)
    _mix_rest(_mm_nt(xb, wmain_ref[...]), _mm_nt(xb, wlora_ref[...]), dict(locals()))


def _mix_rest(zmain, zl, names):
    sample, tiles_per_seq = names["sample"], names["tiles_per_seq"]
    (lng_ref, lnb_ref, mum_ref, mul_ref, w0_ref, w2_ref, a0_ref, a2_ref, g2_ref, kk_ref, ka_ref,
     rk_ref, ones_ref, ya_ref, r_ref, w_ref, k_ref, v_ref, kn_ref, b_ref, g_ref, bonus_ref) = [
        names[n] for n in (
            "lng_ref", "lnb_ref", "mum_ref", "mul_ref", "w0_ref", "w2_ref", "a0_ref", "a2_ref",
            "g2_ref", "kk_ref", "ka_ref", "rk_ref", "ones_ref", "ya_ref", "r_ref", "w_ref",
            "k_ref", "v_ref", "kn_ref", "b_ref", "g_ref", "bonus_ref")]
    if sample:
        w00_ref, b0_ref, spm_ref, spl_ref, va_ref, zm_ref, zl_ref = [
            names[n] for n in ("w00_ref", "b0_ref", "spm_ref", "spl_ref", "va_ref", "zm_ref",
                               "zl_ref")]
    else:
        wcat_ref, bias_ref, zlast_ref, cm_ref, cl_ref = [
            names[n] for n in ("wcat_ref", "bias_ref", "zlast_ref", "cm_ref", "cl_ref")]
    tm = zl.shape[0]

    u = jax.nn.gelu(zmain[:, :A_WIDTH])
    vx = jax.nn.gelu(zmain[:, A_WIDTH:2 * A_WIDTH])
    mu = jnp.mean(vx, axis=-1, keepdims=True)
    var = jnp.mean(jnp.square(vx - mu), axis=-1, keepdims=True)
    va = (vx - mu) * lax.rsqrt(var + LN_EPS) * lng_ref[...] + lnb_ref[...]
    if sample:
        mixed = va * w00_ref[...] + b0_ref[...]
        ya_ref[...] = (u * mixed).astype(BF16)
        va_ref[...] = va
    else:
        vab = va.astype(BF16)
        first = lax.broadcasted_iota(jnp.int32, (CHUNK, 2 * A_GROUP_DIM), 1) < A_GROUP_DIM
        for c in range(tm // CHUNK):
            rows = slice(c * CHUNK, (c + 1) * CHUNK)
            for gp in range(A_GROUPS // 2):
                lanes = slice(gp * 2 * A_GROUP_DIM, (gp + 1) * 2 * A_GROUP_DIM)
                vc = vab[rows, lanes]
                zero = jnp.zeros_like(vc)
                rhs = jnp.concatenate([jnp.where(first, vc, zero), jnp.where(first, zero, vc)],
                                      axis=0)
                mixed = jnp.dot(wcat_ref[:, gp * 2 * CHUNK:(gp + 1) * 2 * CHUNK], rhs,
                                preferred_element_type=F32) + bias_ref[:, lanes]
                ya_ref[rows, lanes] = (u[rows, lanes] * mixed).astype(BF16)

    zbm = zmain[:, 2 * A_WIDTH:]
    if sample:
        zpm, zpl = spm_ref[...], spl_ref[...]
        zm_ref[...] = zbm
        zl_ref[...] = zl
    else:
        tile = pl.program_id(0)

        @pl.when(tile % tiles_per_seq == 0)
        def _():
            cm_ref[...] = jnp.zeros_like(cm_ref)
            cl_ref[...] = jnp.zeros_like(cl_ref)

        first_m = lax.broadcasted_iota(jnp.int32, zbm.shape, 0) == 0
        first_l = lax.broadcasted_iota(jnp.int32, zl.shape, 0) == 0
        zpm = jnp.where(first_m, cm_ref[0:1, :], pltpu.roll(zbm, 1, axis=0))
        zpl = jnp.where(first_l, cl_ref[0:1, :], pltpu.roll(zl, 1, axis=0))
        cm_ref[0:1, :] = zbm[tm - 1:tm, :]
        cl_ref[0:1, :] = zl[tm - 1:tm, :]
        zlast_ref[:, :RKV_W] = jnp.broadcast_to(zbm[tm - 1:tm, :], (8, RKV_W))
        zlast_ref[:, RKV_W:] = jnp.broadcast_to(zl[tm - 1:tm, :], (8, LORA_W))
    zsm = zbm + (zpm - zbm) * mum_ref[...]
    zsl = zl + (zpl - zl) * mul_ref[...]
    r = zsm[:, :B_WIDTH]
    k = zsm[:, B_WIDTH:2 * B_WIDTH]
    v = zsm[:, 2 * B_WIDTH:]
    wd = zsl[:, LORA_WD:LORA_AD]
    ad = zsl[:, LORA_AD:LORA_GD]
    gd = zsl[:, LORA_GD:]
    y = w0_ref[...] + _dot(jnp.tanh(wd), w2_ref[...])
    w_log = jnp.minimum(y, 0.0) - jnp.log1p(jnp.exp(-jnp.abs(y))) - 0.5
    log_decay = -jnp.exp(w_log)
    a = jax.nn.sigmoid(a0_ref[...] + _dot(ad, a2_ref[...]))
    gate = _dot(jax.nn.sigmoid(gd), g2_ref[...])
    ones_bd = ones_ref[...]
    kk = k * kk_ref[...]
    kk = kk * lax.rsqrt(jnp.maximum(_seg_sum(kk * kk, ones_bd), 1e-24))
    k2 = k * (1.0 + (a - 1.0) * ka_ref[...])
    r_ref[...] = r
    w_ref[...] = jnp.exp(log_decay) if sample else log_decay
    k_ref[...] = k2
    v_ref[...] = v
    kn_ref[...] = kk
    b_ref[...] = kk * a
    g_ref[...] = gate
    bonus_ref[...] = _seg_sum(r * k2 * rk_ref[...], ones_bd) * v


def _mix_in(x, p, *, tm, sample, shift_main=None, shift_lora=None):
    rows = x.shape[0]
    n_tiles = rows // tm
    args = [x, p["ln_mix"], p["w_main"], p["w_lora"]]
    specs = [_rows_spec(tm, D_MODEL), _const_spec((1, D_MODEL)),
             _const_spec((MAIN_W, D_MODEL)), _const_spec((LORA_W, D_MODEL))]
    if sample:
        args += [p["sgu_w00"], p["sgu_b0"], shift_main, shift_lora]
        specs += [_const_spec((1, A_WIDTH)), _const_spec((1, A_WIDTH)),
                  _rows_spec(tm, RKV_W), _rows_spec(tm, LORA_W)]
    else:
        args += [p["sgu_wcat"], p["sgu_bias"]]
        specs += [_const_spec((CHUNK, A_GROUPS * CHUNK)), _const_spec((CHUNK, A_WIDTH))]
    args += [p["sgu_ln_g"], p["sgu_ln_b"], p["mu_main"], p["mu_lora"], p["w0"], p["w2"],
             p["a0"], p["a2"], p["g2"], p["k_k"], p["k_a"], p["r_k"], p["ones_bd"]]
    specs += [_const_spec((1, A_WIDTH)), _const_spec((1, A_WIDTH)), _const_spec((1, RKV_W)),
              _const_spec((1, LORA_W)), _const_spec((1, B_WIDTH)),
              _const_spec((LORA_AD - LORA_WD, B_WIDTH)), _const_spec((1, B_WIDTH)),
              _const_spec((LORA_GD - LORA_AD, B_WIDTH)), _const_spec((LORA_W - LORA_GD, B_WIDTH)),
              _const_spec((1, B_WIDTH)), _const_spec((1, B_WIDTH)), _const_spec((1, B_WIDTH)),
              _const_spec((B_WIDTH, B_WIDTH))]
    wide = jax.ShapeDtypeStruct((rows, B_WIDTH), F32)
    out_shape = [jax.ShapeDtypeStruct((rows, A_WIDTH), BF16)] + [wide] * 8
    out_specs = [_rows_spec(tm, B_WIDTH)] * 9
    scratch = []
    if sample:
        out_shape += [wide, jax.ShapeDtypeStruct((rows, RKV_W), F32),
                      jax.ShapeDtypeStruct((rows, LORA_W), F32)]
        out_specs += [_rows_spec(tm, A_WIDTH), _rows_spec(tm, RKV_W), _rows_spec(tm, LORA_W)]
    else:
        out_shape += [jax.ShapeDtypeStruct((n_tiles * 8, RKV_W + LORA_W), F32)]
        out_specs += [pl.BlockSpec((8, RKV_W + LORA_W), lambda i: (i, 0))]
        scratch = [pltpu.VMEM((8, RKV_W), F32), pltpu.VMEM((8, LORA_W), F32)]
    return pl.pallas_call(
        functools.partial(_mix_kernel, sample=sample, tiles_per_seq=max(SEQ // tm, 1)),
        grid=(n_tiles,),
        in_specs=specs,
        out_specs=out_specs,
        out_shape=out_shape,
        scratch_shapes=scratch,
        compiler_params=_params(),
        name="mix_in",
    )(*args)


def _each(f, *lists):
    return [f(*xs) for xs in zip(*lists)]


SCAN_C = 64


def _mm(a, b):
    return jnp.dot(a.astype(BF16), b.astype(BF16), preferred_element_type=F32)


def _mm_nt(a, b):
    return lax.dot_general(a.astype(BF16), b.astype(BF16), (((1,), (1,)), ((), ())),
                           preferred_element_type=F32)


def _mm_tn(a, b):
    return lax.dot_general(a.astype(BF16), b.astype(BF16), (((0,), (0,)), ((), ())),
                           preferred_element_type=F32)


def _cumsum_rows(x):
    n = x.shape[0]
    row = lax.broadcasted_iota(jnp.int32, x.shape, 0)
    s = 1
    while s < n:
        x = x + jnp.where(row >= s, pltpu.roll(x, s, axis=0), 0.0)
        s *= 2
    return x


INV_BASE = 8


def _unit_lower_inverse(ns, row, col):
    f0 = jnp.zeros((), F32)
    same = lambda s: (row // s) == (col // s)
    eye = jnp.where(row == col, 1.0, f0)
    ps = _each(lambda n: jnp.where(same(INV_BASE), n, f0), ns)
    ts = _each(lambda p: eye + p, ps)
    s = 2
    while s < INV_BASE:
        ps = _each(lambda p: _mm(p, p), ps)
        yield
        ts = _each(lambda t, p: t + _mm(t, p), ts, ps)
        yield
        s *= 2
    s = INV_BASE
    while s < SCAN_C:
        level = same(2 * s) & jnp.logical_not(same(s))
        ws = _each(lambda n, t: _mm(jnp.where(level, n, f0), t), ns, ts)
        yield
        ts = _each(lambda t, w: t + _mm(t, w), ts, ws)
        yield
        s *= 2
    return ts


def _chunk_pairs(s0s, rs, lws, ks, vs, kks, bs):
    c = SCAN_C
    f0 = jnp.zeros((), F32)
    row = lax.broadcasted_iota(jnp.int32, (2 * c, PAIR_W), 0)
    col = lax.broadcasted_iota(jnp.int32, (2 * c, PAIR_W), 1)
    top, lft = row < c, col < HEAD
    same_head = top == lft
    strict = (row % c) > (col % HEAD)
    row_c = lax.broadcasted_iota(jnp.int32, (c, PAIR_W), 0)
    col_c = lax.broadcasted_iota(jnp.int32, (c, PAIR_W), 1)
    lft_c = col_c < HEAD
    strict_c = row_c > (col_c % HEAD)
    incl_c = row_c >= (col_c % HEAD)

    def prep(r, lw, k, v, kk, b):
        cum = _cumsum_rows(lw)
        end = cum[c - 1:c, :]
        a_t = -kk * jnp.exp(cum - lw)
        r_t = r * jnp.exp(cum)
        einv = jnp.exp(-cum)
        eend = jnp.exp(end - cum)
        return dict(
            x0=jnp.concatenate([a_t, r_t], axis=0), x1=jnp.concatenate([r_t, a_t], axis=0),
            bk=jnp.concatenate([b * einv, k * einv], axis=0),
            kb=jnp.concatenate([k * einv, b * einv], axis=0),
            bk_e=jnp.concatenate([b * eend, k * eend], axis=0),
            w_end=jnp.exp(end), v=v,
            v_l=jnp.where(lft_c, v, f0), v_r=jnp.where(lft_c, f0, v))

    fs = _each(prep, rs, lws, ks, vs, kks, bs)
    yield
    g0s = _each(lambda f: _mm_nt(jnp.where(lft, f["x0"], f0), f["bk"]), fs)
    g1s = _each(lambda f: _mm_nt(jnp.where(lft, f0, f["x1"]), f["kb"]), fs)
    pqs = _each(lambda f, s0: _mm_nt(f["x0"], s0), fs, s0s)
    yield

    def rhs(f, g0, g1, pq):
        ak = jnp.where(strict_c, jnp.where(lft_c, g1[c:], g0[:c]), f0)
        x = pq[:c] + _mm(ak, jnp.concatenate([f["v_r"], f["v_l"]], axis=0))
        return jnp.concatenate([jnp.where(lft_c, x, f0), jnp.where(lft_c, f0, x)], axis=0)

    ys = _each(rhs, fs, g0s, g1s, pqs)
    yield
    ns = _each(lambda g0, g1: jnp.where(strict & same_head, jnp.where(top, g0, g1), f0),
               g0s, g1s)
    ts = yield from _unit_lower_inverse(ns, row, col)
    ys = _each(_mm, ts, ys)
    yield

    def out(f, g0, g1, pq, y):
        lhs = jnp.concatenate([jnp.where(incl_c, g0[c:], f0), jnp.where(incl_c, g1[:c], f0)],
                              axis=1)
        return pq[c:] + _mm(lhs, jnp.concatenate([y[:c], f["v_l"], f["v_r"], y[c:]], axis=0))

    def state(f, s0, y):
        upd = _mm_tn(jnp.concatenate([y[:c] + y[c:], f["v"]], axis=0), f["bk_e"])
        return s0 * f["w_end"] + jnp.where(same_head, upd, f0)

    outs = _each(out, fs, g0s, g1s, pqs, ys)
    yield
    return outs, _each(state, fs, s0s, ys)


SCAN_BATCHES = 4
XA_EVERY = 3


def _run_with(main, side, *, every):
    n = 0
    while True:
        if n % every == 0:
            next(side, None)
        n += 1
        try:
            next(main)
        except StopIteration as stop:
            for _ in side:
                pass
            return stop.value


def _scan_prompt_kernel(r_ref, w_ref, k_ref, v_ref, kk_ref, b_ref, xq_ref, xk_ref, xv_ref,
                        o_ref, sout_ref, xo_ref, s_ref):
    t_blk = pl.program_id(1)

    @pl.when(t_blk == 0)
    def _():
        s_ref[...] = jnp.zeros_like(s_ref)

    chains = [(j, p) for j in range(SCAN_BATCHES) for p in range(PAIRS)]
    lanes = lambda p: slice(p * PAIR_W, (p + 1) * PAIR_W)
    take = lambda ref: [ref[j, :, lanes(p)] for j, p in chains]
    os_, ss = _run_with(
        _chunk_pairs([s_ref[j, p] for j, p in chains], take(r_ref), take(w_ref), take(k_ref),
                     take(v_ref), take(kk_ref), take(b_ref)),
        _xa_attend(xq_ref, xk_ref, xv_ref, xo_ref), every=XA_EVERY)
    for (j, p), o, s_new in zip(chains, os_, ss):
        o_ref[j, :, lanes(p)] = o
        s_ref[j, p] = s_new

    @pl.when(t_blk == pl.num_programs(1) - 1)
    def _():
        sout_ref[...] = s_ref[...]


def _scan_prompt(r, lw, k, v, kk, b, xq, xk, xv, *, batch, seq):
    n_t = seq // SCAN_C
    nb = SCAN_BATCHES
    steps = (batch // nb) * n_t
    n_s = xq.shape[0]
    assert n_s % steps == 0, (n_s, steps)
    per = n_s // steps
    spec = pl.BlockSpec((nb, SCAN_C, B_WIDTH), lambda bi, ti: (bi, ti, 0))
    sspec = pl.BlockSpec((nb, PAIRS, PAIR_W, PAIR_W), lambda bi, ti: (bi, 0, 0, 0))
    step = lambda bi, ti: (bi * n_t + ti, 0, 0)
    qspec = pl.BlockSpec((per, MEM_ROWS, 128), step)
    mspec = pl.BlockSpec((per, N_MEM * MEM_ROWS, 128), step)
    o, s, xo = pl.pallas_call(
        _scan_prompt_kernel,
        grid=(batch // nb, n_t),
        in_specs=[spec] * 6 + [qspec, mspec, mspec],
        out_specs=[spec, sspec, qspec],
        out_shape=[jax.ShapeDtypeStruct((batch, seq, B_WIDTH), F32),
                   jax.ShapeDtypeStruct((batch, PAIRS, PAIR_W, PAIR_W), F32),
                   jax.ShapeDtypeStruct(xq.shape, F32)],
        scratch_shapes=[pltpu.VMEM((nb, PAIRS, PAIR_W, PAIR_W), F32)],
        compiler_params=_params(2),
        name="scan_prompt",
    )(*[x.reshape(batch, seq, B_WIDTH) for x in (r, lw, k, v, kk, b)], xq, xk, xv)
    return o.reshape(batch * seq, B_WIDTH), s, xo


def _scan_sample_kernel(s_ref, r_ref, w_ref, k_ref, v_ref, kk_ref, b_ref, o_ref, sout_ref,
                        t_ref, ot_ref):
    h = pl.program_id(0)

    @pl.when(h == 0)
    def _():
        for i, ref in enumerate((r_ref, w_ref, k_ref, v_ref, kk_ref, b_ref)):
            t_ref[i] = ref[...].T

    base = pl.multiple_of(h * HEAD, HEAD)
    keys = pl.ds(base, HEAD)
    r, w, k, kk, b = [t_ref[i, keys, :] for i in (0, 1, 2, 4, 5)]

    def body(v8, carry):
        rows = pl.ds(pl.multiple_of(base + v8 * 8, 8), 8)
        v_rows = t_ref[3, rows, :]
        outs = []
        for j in range(8):
            vi = v8 * 8 + j
            s = s_ref[0, vi]
            sa = jnp.sum(s * kk, axis=0, keepdims=True)
            s = s * w - sa * b + v_rows[j:j + 1, :] * k
            sout_ref[0, vi] = s
            outs.append(jnp.sum(s * r, axis=0, keepdims=True))
        ot_ref[rows, :] = jnp.concatenate(outs, axis=0)
        return carry

    lax.fori_loop(0, HEAD // 8, body, 0)

    @pl.when(h == pl.num_programs(0) - 1)
    def _():
        o_ref[...] = ot_ref[...].T


def _scan_sample(state_t, r, w, k, v, kk, b):
    rows = r.shape[0]
    sspec = pl.BlockSpec((1, HEAD, HEAD, rows), lambda h: (h, 0, 0, 0))
    spec = _const_spec((rows, B_WIDTH))
    return pl.pallas_call(
        _scan_sample_kernel,
        grid=(HEADS,),
        in_specs=[sspec] + [spec] * 6,
        out_specs=[pl.BlockSpec((rows, B_WIDTH), lambda h: (0, 0)), sspec],
        out_shape=[jax.ShapeDtypeStruct((rows, B_WIDTH), F32),
                   jax.ShapeDtypeStruct(state_t.shape, F32)],
        scratch_shapes=[pltpu.VMEM((6, B_WIDTH, rows), F32), pltpu.VMEM((B_WIDTH, rows), F32)],
        compiler_params=_params(),
        name="scan_sample",
    )(state_t, r, w, k, v, kk, b)


def _softmax_rows(s):
    e = jnp.exp(s - jnp.max(s, axis=-1, keepdims=True))
    return e / jnp.sum(e, axis=-1, keepdims=True)


POST_ROWS = 1024


def _post_kernel(*refs, attend):
    it = iter(refs)
    (x_ref, ya_ref, o_ref, g_ref, bonus_ref, gng_ref, gnb_ref, ones_ref, wo_ref, lnx_ref,
     wq_ref) = [next(it) for _ in range(11)]
    if attend:
        mk_ref, mv_ref = next(it), next(it)
    x2_ref, out_ref = next(it), next(it)

    ones_bd = ones_ref[...]
    o = o_ref[...]
    mu = _seg_sum(o, ones_bd) * (1.0 / HEAD)
    d = o - mu
    var = _seg_sum(d * d, ones_bd) * (1.0 / HEAD)
    on = d * lax.rsqrt(var + GN_EPS) * gng_ref[...] + gnb_ref[...]
    yb = (on + bonus_ref[...]) * g_ref[...]
    x2 = x_ref[...] + jnp.dot(ya_ref[...], wo_ref[:A_WIDTH, :], preferred_element_type=F32) \
        + _dot(yb, wo_ref[A_WIDTH:, :])
    x2_ref[...] = x2
    q = _dot(_rms(x2, lnx_ref[...]), wq_ref[...])
    if not attend:
        out_ref[...] = q
        return
    qb = q.astype(BF16)
    heads = [slice(h * XA_DIM, (h + 1) * XA_DIM) for h in range(XA_HEADS)]
    ss = [lax.dot_general(qb[:, sl], mk_ref[0, :, sl], (((1,), (1,)), ((), ())),
                          preferred_element_type=F32) * (XA_DIM ** -0.5) for sl in heads]
    ps = [_softmax_rows(s) for s in ss]
    for sl, p in zip(heads, ps):
        out_ref[:, sl] = _dot(p, mv_ref[0, :, sl]).astype(BF16)


def _post_mix(x, ya, o, g, bonus, p, *, tm, mk=None, mv=None):
    rows = x.shape[0]
    attend = mk is not None
    args = [x, ya, o, g, bonus, p["gn_g"], p["gn_b"], p["ones_bd"], p["w_out"], p["ln_xattn"],
            p["xa_q"]]
    specs = [_rows_spec(tm, D_MODEL)] + [_rows_spec(tm, B_WIDTH)] * 4 + [
        _const_spec((1, B_WIDTH)), _const_spec((1, B_WIDTH)), _const_spec((B_WIDTH, B_WIDTH)),
        _const_spec((A_WIDTH + B_WIDTH, D_MODEL)),
        _const_spec((1, D_MODEL)), _const_spec((D_MODEL, D_MODEL))]
    if attend:
        tiles_per_seq = SEQ // tm
        mspec = pl.BlockSpec((1, N_MEM, D_MODEL), lambda i: (i // tiles_per_seq, 0, 0))
        args += [mk, mv]
        specs += [mspec, mspec]
    return pl.pallas_call(
        functools.partial(_post_kernel, attend=attend),
        grid=(rows // tm,),
        in_specs=specs,
        out_specs=[_rows_spec(tm, D_MODEL)] * 2,
        out_shape=[jax.ShapeDtypeStruct((rows, D_MODEL), F32),
                   jax.ShapeDtypeStruct((rows, D_MODEL), BF16 if attend else F32)],
        compiler_params=_params(),
        name="post_mix",
    )(*args)


MEM_ROWS = XA_HEADS * (XA_DIM // 128)


def _lane_allreduce(x, op):
    shift = MEM_ROWS
    while shift < 128:
        x = op(x, pltpu.roll(x, shift, axis=1))
        shift *= 2
    return x


def _xa_attend(q_ref, k_ref, v_ref, o_ref):
    f0 = jnp.zeros((), F32)
    n_blk = N_MEM * MEM_ROWS // 128
    sub = lax.broadcasted_iota(jnp.int32, (MEM_ROWS, 128), 0)
    lane = lax.broadcasted_iota(jnp.int32, (MEM_ROWS, 128), 1)
    diag = sub == (lane % MEM_ROWS)
    li = lax.broadcasted_iota(jnp.int32, (128, 128), 0)
    lj = lax.broadcasted_iota(jnp.int32, (128, 128), 1)
    comb = jnp.where((li // MEM_ROWS == lj // MEM_ROWS) & (li % XA_HEADS == lj % XA_HEADS),
                     1.0, 0.0).astype(BF16)
    samples = list(range(q_ref.shape[0]))
    scs = [_mm_nt(q_ref[j], k_ref[j]) for j in samples]

    def partial(sc):
        return jnp.concatenate(
            [jnp.sum(jnp.where(diag, sc[:, t * 128:(t + 1) * 128], f0), axis=0, keepdims=True)
             for t in range(n_blk)], axis=0)

    def scores(part):
        hi = part.astype(BF16)
        lo = (part - hi.astype(F32)).astype(BF16)
        return (jnp.dot(hi, comb, preferred_element_type=F32)
                + jnp.dot(lo, comb, preferred_element_type=F32)) * (XA_DIM ** -0.5)

    def softmax(s):
        mx = _lane_allreduce(jnp.broadcast_to(jnp.max(s, axis=0, keepdims=True), (MEM_ROWS, 128)),
                             jnp.maximum)
        e = jnp.exp(s - mx[0:1, :])
        den = _lane_allreduce(jnp.broadcast_to(jnp.sum(e, axis=0, keepdims=True), (MEM_ROWS, 128)),
                              jnp.add)
        p = e / den[0:1, :]
        return jnp.concatenate(
            [jnp.where(diag, jnp.broadcast_to(p[t:t + 1, :], (MEM_ROWS, 128)), f0)
             for t in range(n_blk)], axis=1)

    yield
    parts = _each(partial, scs)
    yield
    ss = _each(scores, parts)
    yield
    p_rows = _each(softmax, ss)
    yield
    for j, p in zip(samples, p_rows):
        o_ref[j] = _mm(p, v_ref[j])


def _pad_lora(x, axis=-1):
    x = jnp.moveaxis(x, axis, -1)
    wd = x[..., :DECAY_LORA]
    ad = x[..., DECAY_LORA:DECAY_LORA + AAA_LORA]
    gd = x[..., DECAY_LORA + AAA_LORA:]
    z = lambda n: jnp.zeros(x.shape[:-1] + (n,), x.dtype)
    out = jnp.concatenate([wd, z(LORA_AD - DECAY_LORA), ad, z(LORA_GD - LORA_AD - AAA_LORA),
                           gd, z(LORA_W - LORA_GD - GATE_LORA)], axis=-1)
    return jnp.moveaxis(out, -1, axis)


def _unpad_shift(zm, zl):
    return jnp.concatenate([zm, zl[..., LORA_WD:LORA_WD + DECAY_LORA],
                            zl[..., LORA_AD:LORA_AD + AAA_LORA],
                            zl[..., LORA_GD:LORA_GD + GATE_LORA]], axis=-1)


def _pad_rows(w, n):
    return jnp.pad(w, ((0, n - w.shape[0]), (0, 0)))


def _mem_rows(x):
    b = x.shape[0]
    return x.reshape(b, N_MEM, XA_HEADS, XA_DIM // 128, 128).transpose(0, 1, 3, 2, 4).reshape(
        b, N_MEM * MEM_ROWS, 128)


def _from_mem_rows(x, b):
    return x.reshape(b, N_MEM, XA_DIM // 128, XA_HEADS, 128).transpose(0, 1, 3, 2, 4).reshape(
        b, N_MEM, XA_HEADS, XA_DIM)


def _head_rows(x):
    b = x.shape[0]
    return x.reshape(b, XA_HEADS, XA_DIM // 128, 128).transpose(0, 2, 1, 3).reshape(b, MEM_ROWS, 128)


def _from_head_rows(x):
    b = x.shape[0]
    return x.reshape(b, XA_DIM // 128, XA_HEADS, 128).transpose(0, 2, 1, 3).reshape(b, D_MODEL)


def kernel(x_prompt, x_sample, state_rwkv, state_shift, cache_mem_k, cache_mem_v, mem_prompt, ln_ffn1, ffn1_gate, ffn1_up, ffn1_down, ln_mix, w_in, w_out, sgu_w, sgu_b, sgu_ln_g, sgu_ln_b, rwkv_mu, rwkv_w0, rwkv_w2, rwkv_a0, rwkv_a2, rwkv_g2, rwkv_k_k, rwkv_k_a, rwkv_r_k, rwkv_gn_g, rwkv_gn_b, ln_xattn, mem_norm, xa_q, xa_k, xa_v, xa_o, ln_ffn2, ffn2_gate, ffn2_up, ffn2_down, final_norm):
    assert ln_ffn1.shape[0] == 1, "single layer"
    bp, seq, _ = x_prompt.shape
    bs = x_sample.shape[0]
    row = lambda a: a.reshape(1, -1).astype(F32)
    bf = lambda a: a.astype(BF16)
    l = 0
    head_id = jnp.arange(B_WIDTH) // HEAD
    tril = jnp.tril(jnp.ones((CHUNK, CHUNK), dtype=bool))
    wmask = jnp.where(tril[None], sgu_w[l], 0)
    p = {
        "ln_mix": row(ln_mix[l]),
        "w_lora": bf(_pad_lora(w_in[l].T[MAIN_W:], axis=0)),
        "sgu_wcat": bf(wmask.transpose(1, 0, 2).reshape(CHUNK, A_GROUPS * CHUNK)),
        "sgu_bias": jnp.repeat(sgu_b[l].T, A_GROUP_DIM, axis=1),
        "sgu_w00": row(jnp.repeat(sgu_w[l][:, 0, 0], A_GROUP_DIM)),
        "sgu_b0": row(jnp.repeat(sgu_b[l][:, 0], A_GROUP_DIM)),
        "sgu_ln_g": row(sgu_ln_g[l]), "sgu_ln_b": row(sgu_ln_b[l]),
        "mu_main": row(rwkv_mu[l][:RKV_W]),
        "mu_lora": row(_pad_lora(rwkv_mu[l][RKV_W:])),
        "w0": row(rwkv_w0[l]), "w2": bf(_pad_rows(rwkv_w2[l], LORA_AD - LORA_WD)),
        "a0": row(rwkv_a0[l]), "a2": bf(_pad_rows(rwkv_a2[l], LORA_GD - LORA_AD)),
        "g2": bf(_pad_rows(rwkv_g2[l], LORA_W - LORA_GD)),
        "k_k": row(rwkv_k_k[l]), "k_a": row(rwkv_k_a[l]), "r_k": row(rwkv_r_k[l]),
        "ones_bd": (head_id[:, None] == head_id[None, :]).astype(BF16),
        "gn_g": row(rwkv_gn_g[l]), "gn_b": row(rwkv_gn_b[l]),
        "ln_xattn": row(ln_xattn[l]),
    }
    fnorm = row(final_norm)

    tm = 512
    mk, mv, mkb, mvb, (wg1, wu1, wd1, p["w_main"]) = _memkv(
        mem_prompt.reshape(bp * N_MEM, D_MODEL), row(mem_norm[l]), xa_k[l], xa_v[l], tm=MEMKV_ROWS,
        cast=(ffn1_gate[l], ffn1_up[l], ffn1_down[l], w_in[l].T[:MAIN_W]))
    ffn1 = (row(ln_ffn1[l]), wg1, wu1, wd1)

    xp = x_prompt.reshape(bp * seq, D_MODEL)
    x1, (wg2, wu2, wd2, p["xa_q"], xa_o_b, p["w_out"]) = _ffn(
        xp, *ffn1, tm=FFN_ROWS,
        cast=(ffn2_gate[l], ffn2_up[l], ffn2_down[l], xa_q[l], xa_o[l], w_out[l]))
    ffn2 = (row(ln_ffn2[l]), wg2, wu2, wd2)

    xs = x_sample.reshape(bs, D_MODEL)
    sh = state_shift[l].reshape(bs, B_PROJ)
    x1s = _ffn(xs, *ffn1, tm=bs)
    (ya_s, r_s, w_s, k_s, v_s, kk_s, b_s, g_s, bonus_s, va_s, zm_s, zl_s) = _mix_in(
        x1s, p, tm=bs, sample=True, shift_main=sh[:, :RKV_W], shift_lora=_pad_lora(sh[:, RKV_W:]))
    o_s, state_t = _scan_sample(jnp.transpose(state_rwkv[l], (1, 2, 3, 0)),
                                r_s, w_s, k_s, v_s, kk_s, b_s)
    state_s = jnp.transpose(state_t, (3, 0, 1, 2))
    x2s, q_s = _post_mix(x1s, ya_s, o_s, g_s, bonus_s, p, tm=bs)

    ya, r, w, k, v, kk, b, g, bonus, zlast = _mix_in(x1, p, tm=tm, sample=False)
    o, s_bd, attn_rows = _scan_prompt(r, w, k, v, kk, b, _head_rows(q_s),
                                      _mem_rows(cache_mem_k[l]), _mem_rows(cache_mem_v[l]),
                                      batch=bp, seq=seq)
    state_p = jnp.stack([s_bd[:, :, :HEAD, :HEAD], s_bd[:, :, HEAD:, HEAD:]],
                        axis=2).reshape(bp, HEADS, HEAD, HEAD)
    x2, attn = _post_mix(x1, ya, o, g, bonus, p, tm=POST_ROWS,
                         mk=mkb.reshape(bp, N_MEM, D_MODEL), mv=mvb.reshape(bp, N_MEM, D_MODEL))
    y_prompt = _ffn(x2, *ffn2, tm=FFN_ROWS, attn=attn, wo=xa_o_b, final_norm=fnorm)
    tiles_per_seq = seq // tm
    zl_rows = zlast.reshape(bp, tiles_per_seq, 8, RKV_W + LORA_W)[:, -1, 0]
    shift_p = _unpad_shift(zl_rows[:, :RKV_W], zl_rows[:, RKV_W:])

    y_sample = _ffn(x2s, *ffn2, tm=bs, attn=_from_head_rows(attn_rows), wo=xa_o_b,
                    final_norm=fnorm)

    return (y_prompt.reshape(bp, seq, D_MODEL),
            y_sample.reshape(bs, 1, D_MODEL),
            state_p[None],
            shift_p.reshape(1, bp, 1, B_PROJ),
            _from_mem_rows(mk, bp)[None],
            _from_mem_rows(mv, bp)[None],
            state_s[None],
            _unpad_shift(zm_s, zl_s).reshape(1, bs, 1, B_PROJ),
            va_s.reshape(1, bs, 1, A_WIDTH))
```

```python
import functools

import jax
import jax.numpy as jnp
from jax import lax
from jax.experimental import pallas as pl
from jax.experimental.pallas import tpu as pltpu

F32 = jnp.float32
BF16 = jnp.bfloat16

D_MODEL = 1024
SEQ = 2048
A_WIDTH = 512
A_GROUPS = 8
A_GROUP_DIM = 64
CHUNK = 128
B_WIDTH = 512
HEAD = 64
HEADS = 8
PAIRS = HEADS // 2
PAIR_W = 2 * HEAD
DECAY_LORA = 64
AAA_LORA = 64
GATE_LORA = 160
B_PROJ = 3 * B_WIDTH + DECAY_LORA + AAA_LORA + GATE_LORA
MAIN_W = 2 * A_WIDTH + 3 * B_WIDTH
RKV_W = 3 * B_WIDTH
LORA_W = 512
LORA_WD, LORA_AD, LORA_GD = 0, 128, 256
D_FF = 2816
N_MEM = 256
XA_HEADS = 4
XA_DIM = 256
NORM_EPS = 1e-6
LN_EPS = 1e-5
GN_EPS = 64e-5

VMEM_LIMIT = 56 * 1024 * 1024


def _params(n_axes=1):
    return pltpu.CompilerParams(dimension_semantics=("arbitrary",) * n_axes,
                                vmem_limit_bytes=VMEM_LIMIT)


def _const_spec(shape):
    nd = len(shape)
    return pl.BlockSpec(shape, lambda *_: (0,) * nd, pipeline_mode=pl.Buffered(1))


def _rows_spec(tm, width):
    return pl.BlockSpec((tm, width), lambda i: (i, 0))


def _rms(x, g):
    return x * lax.rsqrt(jnp.mean(x * x, axis=-1, keepdims=True) + NORM_EPS) * g


def _dot(a, b):
    return jnp.dot(a.astype(BF16), b, preferred_element_type=F32)


def _seg_sum(x, ones_bd):
    return jnp.dot(x.astype(BF16), ones_bd, preferred_element_type=F32)


FFN_ROWS = 1024
FFN_BLOCK = 768


def _ffn_kernel(*refs, pre, final, n_cast, n_main):
    it = iter(refs)
    x_ref, xs_ref = next(it), next(it)
    if pre:
        attn_ref, attns_ref, wo_ref = next(it), next(it), next(it)
    ln_ref, wg_ref, wu_ref, wd_ref = next(it), next(it), next(it), next(it)
    if final:
        fn_ref = next(it)
    cast_in = [next(it) for _ in range(n_cast)]
    o_ref, os_ref = next(it), next(it)
    cast_out = [next(it) for _ in range(n_cast)]

    def ffn(x, attn):
        if pre:
            x = x + _dot(attn, wo_ref[...])
        xb = _rms(x, ln_ref[...]).astype(BF16)
        y = None
        for c0 in range(0, D_FF, FFN_BLOCK):
            cols = slice(c0, min(c0 + FFN_BLOCK, D_FF))
            g = jnp.dot(xb, wg_ref[:, cols], preferred_element_type=F32)
            u = jnp.dot(xb, wu_ref[:, cols], preferred_element_type=F32)
            h = (g * jax.nn.sigmoid(g) * u).astype(BF16)
            part = jnp.dot(h, wd_ref[cols, :], preferred_element_type=F32)
            y = part if y is None else y + part
        x = x + 0.5 * y
        return _rms(x, fn_ref[...]) if final else x

    step = pl.program_id(0)

    @pl.when(step < n_main)
    def _():
        for src_ref, dst_ref in zip(cast_in, cast_out):
            dst_ref[...] = src_ref[...].astype(BF16)
        o_ref[...] = ffn(x_ref[...], attn_ref[...] if pre else None)

    @pl.when(step == n_main)
    def _():
        os_ref[...] = ffn(xs_ref[...], attns_ref[...] if pre else None)


def _ffn(x, xs, ln, wg, wu, wd, *, tm, attn=None, attns=None, wo=None, final_norm=None, cast=()):
    rows, rows_s = x.shape[0], xs.shape[0]
    n_main = rows // tm
    pre = attn is not None
    final = final_norm is not None
    main = lambda i: (jnp.minimum(i, n_main - 1), 0)
    main_spec = pl.BlockSpec((tm, D_MODEL), main)
    small_spec = _const_spec((rows_s, D_MODEL))
    args, specs = [x, xs], [main_spec, small_spec]
    if pre:
        args += [attn, attns, wo]
        specs += [main_spec, small_spec, _const_spec((D_MODEL, D_MODEL))]
    args += [ln, wg, wu, wd]
    specs += [_const_spec((1, D_MODEL)), _const_spec((D_MODEL, D_FF)),
              _const_spec((D_MODEL, D_FF)), _const_spec((D_FF, D_MODEL))]
    if final:
        args.append(final_norm)
        specs.append(_const_spec((1, D_MODEL)))
    slabs, cast_shapes = _cast_slabs(cast, n_main, index_map=main)
    out = pl.pallas_call(
        functools.partial(_ffn_kernel, pre=pre, final=final, n_cast=len(cast), n_main=n_main),
        grid=(n_main + 1,),
        in_specs=specs + slabs,
        out_specs=[main_spec, pl.BlockSpec((rows_s, D_MODEL), lambda i: (0, 0))] + slabs,
        out_shape=[jax.ShapeDtypeStruct((rows, D_MODEL), F32),
                   jax.ShapeDtypeStruct((rows_s, D_MODEL), F32)] + cast_shapes,
        compiler_params=_params(),
        name="ffn",
    )(*args, *_cast_arrays(cast))
    return out[0], out[1], list(out[2:])


MEMKV_ROWS = 256


def _memkv_kernel(m_ref, g_ref, wk_ref, wv_ref, *rest):
    n_cast = (len(rest) - 4) // 2
    k_ref, v_ref, kb_ref, vb_ref = rest[n_cast:n_cast + 4]
    for src_ref, dst_ref in zip(rest[:n_cast], rest[n_cast + 4:]):
        dst_ref[...] = src_ref[...].astype(BF16)
    mb = _rms(m_ref[...], g_ref[...]).astype(BF16)
    tm = m_ref.shape[0]
    k = _dot(mb, wk_ref[...].astype(BF16))
    v = _dot(mb, wv_ref[...].astype(BF16))
    kb_ref[...] = k.astype(BF16)
    vb_ref[...] = v.astype(BF16)
    for c in range(MEM_ROWS):
        src = (c % XA_HEADS) * (XA_DIM // 128) + c // XA_HEADS
        k_ref[pl.ds(c, tm, stride=MEM_ROWS), :] = k[:, src * 128:(src + 1) * 128]
        v_ref[pl.ds(c, tm, stride=MEM_ROWS), :] = v[:, src * 128:(src + 1) * 128]


def _cast_slabs(cast, steps, index_map=lambda i: (i, 0)):
    specs, shapes = [], []
    for a in cast:
        a, n = a if isinstance(a, tuple) else (a, a.shape[0])
        assert n % (16 * steps) == 0, (n, steps)
        specs.append(pl.BlockSpec((n // steps, a.shape[1]), index_map))
        shapes.append(jax.ShapeDtypeStruct((n, a.shape[1]), BF16))
    return specs, shapes


def _cast_arrays(cast):
    return [a[0] if isinstance(a, tuple) else a for a in cast]


def _memkv(mem, g, wk, wv, *, tm, cast=()):
    rows = mem.shape[0]
    steps = rows // tm
    out = jax.ShapeDtypeStruct((rows * MEM_ROWS, 128), F32)
    outb = jax.ShapeDtypeStruct((rows, D_MODEL), BF16)
    slabs, cast_shapes = _cast_slabs(cast, steps)
    res = pl.pallas_call(
        _memkv_kernel,
        grid=(steps,),
        in_specs=[_rows_spec(tm, D_MODEL), _const_spec((1, D_MODEL)),
                  _const_spec((D_MODEL, D_MODEL)), _const_spec((D_MODEL, D_MODEL))] + slabs,
        out_specs=[_rows_spec(tm * MEM_ROWS, 128)] * 2 + [_rows_spec(tm, D_MODEL)] * 2 + slabs,
        out_shape=[out, out, outb, outb] + cast_shapes,
        compiler_params=_params(),
        name="memkv",
    )(mem, g, wk, wv, *_cast_arrays(cast))
    return res[0], res[1], res[2], res[3], list(res[4:])


def _mix_kernel(*refs, sample, tiles_per_seq):
    it = iter(refs)
    x_ref, ln_ref, wmain_ref, wlora_ref = next(it), next(it), next(it), next(it)
    if sample:
        w00_ref, b0_ref, spm_ref, spl_ref = next(it), next(it), next(it), next(it)
    else:
        wcat_ref, bias_ref = next(it), next(it)
    (lng_ref, lnb_ref, mum_ref, mul_ref, w0_ref, w2_ref, a0_ref, a2_ref, g2_ref,
     kk_ref, ka_ref, rk_ref, ones_ref) = [next(it) for _ in range(13)]
    (ya_ref, r_ref, w_ref, k_ref, v_ref, kn_ref, b_ref, g_ref, bonus_ref) = [
        next(it) for _ in range(9)]
    if sample:
        va_ref, zm_ref, zl_ref = next(it), next(it), next(it)
    else:
        zlast_ref, cm_ref, cl_ref = next(it), next(it), next(it)

    xb = _rms(x_ref[...], ln_ref[...]).astype(BF16)
    _mix_rest(_mm_nt(xb, wmain_ref[...]), _mm_nt(xb, wlora_ref[...]), dict(locals()))


def _mix_rest(zmain, zl, names):
    sample, tiles_per_seq = names["sample"], names["tiles_per_seq"]
    (lng_ref, lnb_ref, mum_ref, mul_ref, w0_ref, w2_ref, a0_ref, a2_ref, g2_ref, kk_ref, ka_ref,
     rk_ref, ones_ref, ya_ref, r_ref, w_ref, k_ref, v_ref, kn_ref, b_ref, g_ref, bonus_ref) = [
        names[n] for n in (
            "lng_ref", "lnb_ref", "mum_ref", "mul_ref", "w0_ref", "w2_ref", "a0_ref", "a2_ref",
            "g2_ref", "kk_ref", "ka_ref", "rk_ref", "ones_ref", "ya_ref", "r_ref", "w_ref",
            "k_ref", "v_ref", "kn_ref", "b_ref", "g_ref", "bonus_ref")]
    if sample:
        w00_ref, b0_ref, spm_ref, spl_ref, va_ref, zm_ref, zl_ref = [
            names[n] for n in ("w00_ref", "b0_ref", "spm_ref", "spl_ref", "va_ref", "zm_ref",
                               "zl_ref")]
    else:
        wcat_ref, bias_ref, zlast_ref, cm_ref, cl_ref = [
            names[n] for n in ("wcat_ref", "bias_ref", "zlast_ref", "cm_ref", "cl_ref")]
    tm = zl.shape[0]

    u = jax.nn.gelu(zmain[:, :A_WIDTH])
    vx = jax.nn.gelu(zmain[:, A_WIDTH:2 * A_WIDTH])
    mu = jnp.mean(vx, axis=-1, keepdims=True)
    var = jnp.mean(jnp.square(vx - mu), axis=-1, keepdims=True)
    va = (vx - mu) * lax.rsqrt(var + LN_EPS) * lng_ref[...] + lnb_ref[...]
    if sample:
        mixed = va * w00_ref[...] + b0_ref[...]
        ya_ref[...] = (u * mixed).astype(BF16)
        va_ref[...] = va
    else:
        vab = va.astype(BF16)
        first = lax.broadcasted_iota(jnp.int32, (CHUNK, 2 * A_GROUP_DIM), 1) < A_GROUP_DIM
        for c in range(tm // CHUNK):
            rows = slice(c * CHUNK, (c + 1) * CHUNK)
            for gp in range(A_GROUPS // 2):
                lanes = slice(gp * 2 * A_GROUP_DIM, (gp + 1) * 2 * A_GROUP_DIM)
                vc = vab[rows, lanes]
                zero = jnp.zeros_like(vc)
                rhs = jnp.concatenate([jnp.where(first, vc, zero), jnp.where(first, zero, vc)],
                                      axis=0)
                mixed = jnp.dot(wcat_ref[:, gp * 2 * CHUNK:(gp + 1) * 2 * CHUNK], rhs,
                                preferred_element_type=F32) + bias_ref[:, lanes]
                ya_ref[rows, lanes] = (u[rows, lanes] * mixed).astype(BF16)

    zbm = zmain[:, 2 * A_WIDTH:]
    if sample:
        zpm, zpl = spm_ref[...], spl_ref[...]
        zm_ref[...] = zbm
        zl_ref[...] = zl
    else:
        tile = pl.program_id(0)

        @pl.when(tile % tiles_per_seq == 0)
        def _():
            cm_ref[...] = jnp.zeros_like(cm_ref)
            cl_ref[...] = jnp.zeros_like(cl_ref)

        first_m = lax.broadcasted_iota(jnp.int32, zbm.shape, 0) == 0
        first_l = lax.broadcasted_iota(jnp.int32, zl.shape, 0) == 0
        zpm = jnp.where(first_m, cm_ref[0:1, :], pltpu.roll(zbm, 1, axis=0))
        zpl = jnp.where(first_l, cl_ref[0:1, :], pltpu.roll(zl, 1, axis=0))
        cm_ref[0:1, :] = zbm[tm - 1:tm, :]
        cl_ref[0:1, :] = zl[tm - 1:tm, :]
        zlast_ref[:, :RKV_W] = jnp.broadcast_to(zbm[tm - 1:tm, :], (8, RKV_W))
        zlast_ref[:, RKV_W:] = jnp.broadcast_to(zl[tm - 1:tm, :], (8, LORA_W))
    zsm = zbm + (zpm - zbm) * mum_ref[...]
    zsl = zl + (zpl - zl) * mul_ref[...]
    r = zsm[:, :B_WIDTH]
    k = zsm[:, B_WIDTH:2 * B_WIDTH]
    v = zsm[:, 2 * B_WIDTH:]
    wd = zsl[:, LORA_WD:LORA_AD]
    ad = zsl[:, LORA_AD:LORA_GD]
    gd = zsl[:, LORA_GD:]
    y = w0_ref[...] + _dot(jnp.tanh(wd), w2_ref[...])
    w_log = jnp.minimum(y, 0.0) - jnp.log1p(jnp.exp(-jnp.abs(y))) - 0.5
    log_decay = -jnp.exp(w_log)
    a = jax.nn.sigmoid(a0_ref[...] + _dot(ad, a2_ref[...]))
    gate = _dot(jax.nn.sigmoid(gd), g2_ref[...])
    ones_bd = ones_ref[...]
    kk = k * kk_ref[...]
    kk = kk * lax.rsqrt(jnp.maximum(_seg_sum(kk * kk, ones_bd), 1e-24))
    k2 = k * (1.0 + (a - 1.0) * ka_ref[...])
    r_ref[...] = r
    w_ref[...] = jnp.exp(log_decay) if sample else log_decay
    k_ref[...] = k2
    v_ref[...] = v
    kn_ref[...] = kk
    b_ref[...] = kk * a
    g_ref[...] = gate
    bonus_ref[...] = _seg_sum(r * k2 * rk_ref[...], ones_bd) * v


def _mix_in(x, p, *, tm, sample, shift_main=None, shift_lora=None):
    rows = x.shape[0]
    n_tiles = rows // tm
    args = [x, p["ln_mix"], p["w_main"], p["w_lora"]]
    specs = [_rows_spec(tm, D_MODEL), _const_spec((1, D_MODEL)),
             _const_spec((MAIN_W, D_MODEL)), _const_spec((LORA_W, D_MODEL))]
    if sample:
        args += [p["sgu_w00"], p["sgu_b0"], shift_main, shift_lora]
        specs += [_const_spec((1, A_WIDTH)), _const_spec((1, A_WIDTH)),
                  _rows_spec(tm, RKV_W), _rows_spec(tm, LORA_W)]
    else:
        args += [p["sgu_wcat"], p["sgu_bias"]]
        specs += [_const_spec((CHUNK, A_GROUPS * CHUNK)), _const_spec((CHUNK, A_WIDTH))]
    args += [p["sgu_ln_g"], p["sgu_ln_b"], p["mu_main"], p["mu_lora"], p["w0"], p["w2"],
             p["a0"], p["a2"], p["g2"], p["k_k"], p["k_a"], p["r_k"], p["ones_bd"]]
    specs += [_const_spec((1, A_WIDTH)), _const_spec((1, A_WIDTH)), _const_spec((1, RKV_W)),
              _const_spec((1, LORA_W)), _const_spec((1, B_WIDTH)),
              _const_spec((LORA_AD - LORA_WD, B_WIDTH)), _const_spec((1, B_WIDTH)),
              _const_spec((LORA_GD - LORA_AD, B_WIDTH)), _const_spec((LORA_W - LORA_GD, B_WIDTH)),
              _const_spec((1, B_WIDTH)), _const_spec((1, B_WIDTH)), _const_spec((1, B_WIDTH)),
              _const_spec((B_WIDTH, B_WIDTH))]
    wide = jax.ShapeDtypeStruct((rows, B_WIDTH), F32)
    out_shape = [jax.ShapeDtypeStruct((rows, A_WIDTH), BF16)] + [wide] * 8
    out_specs = [_rows_spec(tm, B_WIDTH)] * 9
    scratch = []
    if sample:
        out_shape += [wide, jax.ShapeDtypeStruct((rows, RKV_W), F32),
                      jax.ShapeDtypeStruct((rows, LORA_W), F32)]
        out_specs += [_rows_spec(tm, A_WIDTH), _rows_spec(tm, RKV_W), _rows_spec(tm, LORA_W)]
    else:
        out_shape += [jax.ShapeDtypeStruct((n_tiles * 8, RKV_W + LORA_W), F32)]
        out_specs += [pl.BlockSpec((8, RKV_W + LORA_W), lambda i: (i, 0))]
        scratch = [pltpu.VMEM((8, RKV_W), F32), pltpu.VMEM((8, LORA_W), F32)]
    return pl.pallas_call(
        functools.partial(_mix_kernel, sample=sample, tiles_per_seq=max(SEQ // tm, 1)),
        grid=(n_tiles,),
        in_specs=specs,
        out_specs=out_specs,
        out_shape=out_shape,
        scratch_shapes=scratch,
        compiler_params=_params(),
        name="mix_in",
    )(*args)


def _each(f, *lists):
    return [f(*xs) for xs in zip(*lists)]


SCAN_C = 64


def _mm(a, b):
    return jnp.dot(a.astype(BF16), b.astype(BF16), preferred_element_type=F32)


def _mm_nt(a, b):
    return lax.dot_general(a.astype(BF16), b.astype(BF16), (((1,), (1,)), ((), ())),
                           preferred_element_type=F32)


def _mm_tn(a, b):
    return lax.dot_general(a.astype(BF16), b.astype(BF16), (((0,), (0,)), ((), ())),
                           preferred_element_type=F32)


def _cumsum_rows(x):
    n = x.shape[0]
    row = lax.broadcasted_iota(jnp.int32, x.shape, 0)
    s = 1
    while s < n:
        x = x + jnp.where(row >= s, pltpu.roll(x, s, axis=0), 0.0)
        s *= 2
    return x


INV_BASE = 8


def _unit_lower_inverse(ns, row, col):
    f0 = jnp.zeros((), F32)
    same = lambda s: (row // s) == (col // s)
    eye = jnp.where(row == col, 1.0, f0)
    ps = _each(lambda n: jnp.where(same(INV_BASE), n, f0), ns)
    ts = _each(lambda p: eye + p, ps)
    s = 2
    while s < INV_BASE:
        ps = _each(lambda p: _mm(p, p), ps)
        yield
        ts = _each(lambda t, p: t + _mm(t, p), ts, ps)
        yield
        s *= 2
    s = INV_BASE
    while s < SCAN_C:
        level = same(2 * s) & jnp.logical_not(same(s))
        ws = _each(lambda n, t: _mm(jnp.where(level, n, f0), t), ns, ts)
        yield
        ts = _each(lambda t, w: t + _mm(t, w), ts, ws)
        yield
        s *= 2
    return ts


def _chunk_pairs(s0s, rs, lws, ks, vs, kks, bs):
    c = SCAN_C
    f0 = jnp.zeros((), F32)
    row = lax.broadcasted_iota(jnp.int32, (2 * c, PAIR_W), 0)
    col = lax.broadcasted_iota(jnp.int32, (2 * c, PAIR_W), 1)
    top, lft = row < c, col < HEAD
    same_head = top == lft
    strict = (row % c) > (col % HEAD)
    row_c = lax.broadcasted_iota(jnp.int32, (c, PAIR_W), 0)
    col_c = lax.broadcasted_iota(jnp.int32, (c, PAIR_W), 1)
    lft_c = col_c < HEAD
    strict_c = row_c > (col_c % HEAD)
    incl_c = row_c >= (col_c % HEAD)

    def prep(r, lw, k, v, kk, b):
        cum = _cumsum_rows(lw)
        end = cum[c - 1:c, :]
        a_t = -kk * jnp.exp(cum - lw)
        r_t = r * jnp.exp(cum)
        einv = jnp.exp(-cum)
        eend = jnp.exp(end - cum)
        return dict(
            x0=jnp.concatenate([a_t, r_t], axis=0), x1=jnp.concatenate([r_t, a_t], axis=0),
            bk=jnp.concatenate([b * einv, k * einv], axis=0),
            kb=jnp.concatenate([k * einv, b * einv], axis=0),
            bk_e=jnp.concatenate([b * eend, k * eend], axis=0),
            w_end=jnp.exp(end), v=v,
            v_l=jnp.where(lft_c, v, f0), v_r=jnp.where(lft_c, f0, v))

    fs = _each(prep, rs, lws, ks, vs, kks, bs)
    yield
    g0s = _each(lambda f: _mm_nt(jnp.where(lft, f["x0"], f0), f["bk"]), fs)
    g1s = _each(lambda f: _mm_nt(jnp.where(lft, f0, f["x1"]), f["kb"]), fs)
    pqs = _each(lambda f, s0: _mm_nt(f["x0"], s0), fs, s0s)
    yield

    def rhs(f, g0, g1, pq):
        ak = jnp.where(strict_c, jnp.where(lft_c, g1[c:], g0[:c]), f0)
        x = pq[:c] + _mm(ak, jnp.concatenate([f["v_r"], f["v_l"]], axis=0))
        return jnp.concatenate([jnp.where(lft_c, x, f0), jnp.where(lft_c, f0, x)], axis=0)

    ys = _each(rhs, fs, g0s, g1s, pqs)
    yield
    ns = _each(lambda g0, g1: jnp.where(strict & same_head, jnp.where(top, g0, g1), f0),
               g0s, g1s)
    ts = yield from _unit_lower_inverse(ns, row, col)
    ys = _each(_mm, ts, ys)
    yield

    def out(f, g0, g1, pq, y):
        lhs = jnp.concatenate([jnp.where(incl_c, g0[c:], f0), jnp.where(incl_c, g1[:c], f0)],
                              axis=1)
        return pq[c:] + _mm(lhs, jnp.concatenate([y[:c], f["v_l"], f["v_r"], y[c:]], axis=0))

    def state(f, s0, y):
        upd = _mm_tn(jnp.concatenate([y[:c] + y[c:], f["v"]], axis=0), f["bk_e"])
        return s0 * f["w_end"] + jnp.where(same_head, upd, f0)

    outs = _each(out, fs, g0s, g1s, pqs, ys)
    yield
    return outs, _each(state, fs, s0s, ys)


SCAN_BATCHES = 4
XA_EVERY = 3


def _run_with(main, side, *, every):
    n = 0
    while True:
        if n % every == 0:
            next(side, None)
        n += 1
        try:
            next(main)
        except StopIteration as stop:
            for _ in side:
                pass
            return stop.value


def _scan_prompt_kernel(r_ref, w_ref, k_ref, v_ref, kk_ref, b_ref, xq_ref, xk_ref, xv_ref,
                        o_ref, sout_ref, xo_ref, s_ref):
    t_blk = pl.program_id(1)

    @pl.when(t_blk == 0)
    def _():
        s_ref[...] = jnp.zeros_like(s_ref)

    chains = [(j, p) for j in range(SCAN_BATCHES) for p in range(PAIRS)]
    lanes = lambda p: slice(p * PAIR_W, (p + 1) * PAIR_W)
    take = lambda ref: [ref[j, :, lanes(p)] for j, p in chains]
    os_, ss = _run_with(
        _chunk_pairs([s_ref[j, p] for j, p in chains], take(r_ref), take(w_ref), take(k_ref),
                     take(v_ref), take(kk_ref), take(b_ref)),
        _xa_attend(xq_ref, xk_ref, xv_ref, xo_ref), every=XA_EVERY)
    for (j, p), o, s_new in zip(chains, os_, ss):
        o_ref[j, :, lanes(p)] = o
        s_ref[j, p] = s_new

    @pl.when(t_blk == pl.num_programs(1) - 1)
    def _():
        sout_ref[...] = s_ref[...]


def _scan_prompt(r, lw, k, v, kk, b, xq, xk, xv, *, batch, seq):
    n_t = seq // SCAN_C
    nb = SCAN_BATCHES
    steps = (batch // nb) * n_t
    n_s = xq.shape[0]
    assert n_s % steps == 0, (n_s, steps)
    per = n_s // steps
    spec = pl.BlockSpec((nb, SCAN_C, B_WIDTH), lambda bi, ti: (bi, ti, 0))
    sspec = pl.BlockSpec((nb, PAIRS, PAIR_W, PAIR_W), lambda bi, ti: (bi, 0, 0, 0))
    step = lambda bi, ti: (bi * n_t + ti, 0, 0)
    qspec = pl.BlockSpec((per, MEM_ROWS, 128), step)
    mspec = pl.BlockSpec((per, N_MEM * MEM_ROWS, 128), step)
    o, s, xo = pl.pallas_call(
        _scan_prompt_kernel,
        grid=(batch // nb, n_t),
        in_specs=[spec] * 6 + [qspec, mspec, mspec],
        out_specs=[spec, sspec, qspec],
        out_shape=[jax.ShapeDtypeStruct((batch, seq, B_WIDTH), F32),
                   jax.ShapeDtypeStruct((batch, PAIRS, PAIR_W, PAIR_W), F32),
                   jax.ShapeDtypeStruct(xq.shape, F32)],
        scratch_shapes=[pltpu.VMEM((nb, PAIRS, PAIR_W, PAIR_W), F32)],
        compiler_params=_params(2),
        name="scan_prompt",
    )(*[x.reshape(batch, seq, B_WIDTH) for x in (r, lw, k, v, kk, b)], xq, xk, xv)
    return o.reshape(batch * seq, B_WIDTH), s, xo


def _scan_sample_kernel(s_ref, r_ref, w_ref, k_ref, v_ref, kk_ref, b_ref, o_ref, sout_ref,
                        t_ref, ot_ref):
    h = pl.program_id(0)

    @pl.when(h == 0)
    def _():
        for i, ref in enumerate((r_ref, w_ref, k_ref, v_ref, kk_ref, b_ref)):
            t_ref[i] = ref[...].T

    base = pl.multiple_of(h * HEAD, HEAD)
    keys = pl.ds(base, HEAD)
    r, w, k, kk, b = [t_ref[i, keys, :] for i in (0, 1, 2, 4, 5)]

    def body(v8, carry):
        rows = pl.ds(pl.multiple_of(base + v8 * 8, 8), 8)
        v_rows = t_ref[3, rows, :]
        outs = []
        for j in range(8):
            vi = v8 * 8 + j
            s = s_ref[0, vi]
            sa = jnp.sum(s * kk, axis=0, keepdims=True)
            s = s * w - sa * b + v_rows[j:j + 1, :] * k
            sout_ref[0, vi] = s
            outs.append(jnp.sum(s * r, axis=0, keepdims=True))
        ot_ref[rows, :] = jnp.concatenate(outs, axis=0)
        return carry

    lax.fori_loop(0, HEAD // 8, body, 0)

    @pl.when(h == pl.num_programs(0) - 1)
    def _():
        o_ref[...] = ot_ref[...].T


def _scan_sample(state_t, r, w, k, v, kk, b):
    rows = r.shape[0]
    sspec = pl.BlockSpec((1, HEAD, HEAD, rows), lambda h: (h, 0, 0, 0))
    spec = _const_spec((rows, B_WIDTH))
    return pl.pallas_call(
        _scan_sample_kernel,
        grid=(HEADS,),
        in_specs=[sspec] + [spec] * 6,
        out_specs=[pl.BlockSpec((rows, B_WIDTH), lambda h: (0, 0)), sspec],
        out_shape=[jax.ShapeDtypeStruct((rows, B_WIDTH), F32),
                   jax.ShapeDtypeStruct(state_t.shape, F32)],
        scratch_shapes=[pltpu.VMEM((6, B_WIDTH, rows), F32), pltpu.VMEM((B_WIDTH, rows), F32)],
        compiler_params=_params(),
        name="scan_sample",
    )(state_t, r, w, k, v, kk, b)


def _softmax_rows(s):
    e = jnp.exp(s - jnp.max(s, axis=-1, keepdims=True))
    return e / jnp.sum(e, axis=-1, keepdims=True)


POST_ROWS = 1024


def _post_kernel(*refs, attend):
    it = iter(refs)
    (x_ref, ya_ref, o_ref, g_ref, bonus_ref, gng_ref, gnb_ref, ones_ref, wo_ref, lnx_ref,
     wq_ref) = [next(it) for _ in range(11)]
    if attend:
        mk_ref, mv_ref = next(it), next(it)
    x2_ref, out_ref = next(it), next(it)

    ones_bd = ones_ref[...]
    o = o_ref[...]
    mu = _seg_sum(o, ones_bd) * (1.0 / HEAD)
    d = o - mu
    var = _seg_sum(d * d, ones_bd) * (1.0 / HEAD)
    on = d * lax.rsqrt(var + GN_EPS) * gng_ref[...] + gnb_ref[...]
    yb = (on + bonus_ref[...]) * g_ref[...]
    x2 = x_ref[...] + jnp.dot(ya_ref[...], wo_ref[:A_WIDTH, :], preferred_element_type=F32) \
        + _dot(yb, wo_ref[A_WIDTH:, :])
    x2_ref[...] = x2
    q = _dot(_rms(x2, lnx_ref[...]), wq_ref[...])
    if not attend:
        out_ref[...] = q
        return
    qb = q.astype(BF16)
    heads = [slice(h * XA_DIM, (h + 1) * XA_DIM) for h in range(XA_HEADS)]
    ss = [lax.dot_general(qb[:, sl], mk_ref[0, :, sl], (((1,), (1,)), ((), ())),
                          preferred_element_type=F32) * (XA_DIM ** -0.5) for sl in heads]
    ps = [_softmax_rows(s) for s in ss]
    for sl, p in zip(heads, ps):
        out_ref[:, sl] = _dot(p, mv_ref[0, :, sl]).astype(BF16)


def _post_mix(x, ya, o, g, bonus, p, *, tm, mk=None, mv=None):
    rows = x.shape[0]
    attend = mk is not None
    args = [x, ya, o, g, bonus, p["gn_g"], p["gn_b"], p["ones_bd"], p["w_out"], p["ln_xattn"],
            p["xa_q"]]
    specs = [_rows_spec(tm, D_MODEL)] + [_rows_spec(tm, B_WIDTH)] * 4 + [
        _const_spec((1, B_WIDTH)), _const_spec((1, B_WIDTH)), _const_spec((B_WIDTH, B_WIDTH)),
        _const_spec((A_WIDTH + B_WIDTH, D_MODEL)),
        _const_spec((1, D_MODEL)), _const_spec((D_MODEL, D_MODEL))]
    if attend:
        tiles_per_seq = SEQ // tm
        mspec = pl.BlockSpec((1, N_MEM, D_MODEL), lambda i: (i // tiles_per_seq, 0, 0))
        args += [mk, mv]
        specs += [mspec, mspec]
    return pl.pallas_call(
        functools.partial(_post_kernel, attend=attend),
        grid=(rows // tm,),
        in_specs=specs,
        out_specs=[_rows_spec(tm, D_MODEL)] * 2,
        out_shape=[jax.ShapeDtypeStruct((rows, D_MODEL), F32),
                   jax.ShapeDtypeStruct((rows, D_MODEL), BF16 if attend else F32)],
        compiler_params=_params(),
        name="post_mix",
    )(*args)


MEM_ROWS = XA_HEADS * (XA_DIM // 128)


def _lane_allreduce(x, op):
    shift = MEM_ROWS
    while shift < 128:
        x = op(x, pltpu.roll(x, shift, axis=1))
        shift *= 2
    return x


def _xa_attend(q_ref, k_ref, v_ref, o_ref):
    f0 = jnp.zeros((), F32)
    n_blk = N_MEM * MEM_ROWS // 128
    sub = lax.broadcasted_iota(jnp.int32, (MEM_ROWS, 128), 0)
    lane = lax.broadcasted_iota(jnp.int32, (MEM_ROWS, 128), 1)
    diag = sub == (lane % MEM_ROWS)
    li = lax.broadcasted_iota(jnp.int32, (128, 128), 0)
    lj = lax.broadcasted_iota(jnp.int32, (128, 128), 1)
    comb = jnp.where((li // MEM_ROWS == lj // MEM_ROWS) & (li % XA_HEADS == lj % XA_HEADS),
                     1.0, 0.0).astype(BF16)
    samples = list(range(q_ref.shape[0]))
    scs = [_mm_nt(q_ref[j], k_ref[j]) for j in samples]

    def partial(sc):
        return jnp.concatenate(
            [jnp.sum(jnp.where(diag, sc[:, t * 128:(t + 1) * 128], f0), axis=0, keepdims=True)
             for t in range(n_blk)], axis=0)

    def scores(part):
        hi = part.astype(BF16)
        lo = (part - hi.astype(F32)).astype(BF16)
        return (jnp.dot(hi, comb, preferred_element_type=F32)
                + jnp.dot(lo, comb, preferred_element_type=F32)) * (XA_DIM ** -0.5)

    def softmax(s):
        mx = _lane_allreduce(jnp.broadcast_to(jnp.max(s, axis=0, keepdims=True), (MEM_ROWS, 128)),
                             jnp.maximum)
        e = jnp.exp(s - mx[0:1, :])
        den = _lane_allreduce(jnp.broadcast_to(jnp.sum(e, axis=0, keepdims=True), (MEM_ROWS, 128)),
                              jnp.add)
        p = e / den[0:1, :]
        return jnp.concatenate(
            [jnp.where(diag, jnp.broadcast_to(p[t:t + 1, :], (MEM_ROWS, 128)), f0)
             for t in range(n_blk)], axis=1)

    yield
    parts = _each(partial, scs)
    yield
    ss = _each(scores, parts)
    yield
    p_rows = _each(softmax, ss)
    yield
    for j, p in zip(samples, p_rows):
        o_ref[j] = _mm(p, v_ref[j])


def _pad_lora(x, axis=-1):
    x = jnp.moveaxis(x, axis, -1)
    wd = x[..., :DECAY_LORA]
    ad = x[..., DECAY_LORA:DECAY_LORA + AAA_LORA]
    gd = x[..., DECAY_LORA + AAA_LORA:]
    z = lambda n: jnp.zeros(x.shape[:-1] + (n,), x.dtype)
    out = jnp.concatenate([wd, z(LORA_AD - DECAY_LORA), ad, z(LORA_GD - LORA_AD - AAA_LORA),
                           gd, z(LORA_W - LORA_GD - GATE_LORA)], axis=-1)
    return jnp.moveaxis(out, -1, axis)


def _unpad_shift(zm, zl):
    return jnp.concatenate([zm, zl[..., LORA_WD:LORA_WD + DECAY_LORA],
                            zl[..., LORA_AD:LORA_AD + AAA_LORA],
                            zl[..., LORA_GD:LORA_GD + GATE_LORA]], axis=-1)


def _pad_rows(w, n):
    return jnp.pad(w, ((0, n - w.shape[0]), (0, 0)))


def _mem_rows(x):
    b = x.shape[0]
    return x.reshape(b, N_MEM, XA_HEADS, XA_DIM // 128, 128).transpose(0, 1, 3, 2, 4).reshape(
        b, N_MEM * MEM_ROWS, 128)


def _from_mem_rows(x, b):
    return x.reshape(b, N_MEM, XA_DIM // 128, XA_HEADS, 128).transpose(0, 1, 3, 2, 4).reshape(
        b, N_MEM, XA_HEADS, XA_DIM)


def _head_rows(x):
    b = x.shape[0]
    return x.reshape(b, XA_HEADS, XA_DIM // 128, 128).transpose(0, 2, 1, 3).reshape(b, MEM_ROWS, 128)


def _from_head_rows(x):
    b = x.shape[0]
    return x.reshape(b, XA_DIM // 128, XA_HEADS, 128).transpose(0, 2, 1, 3).reshape(b, D_MODEL)


def kernel(x_prompt, x_sample, state_rwkv, state_shift, cache_mem_k, cache_mem_v, mem_prompt, ln_ffn1, ffn1_gate, ffn1_up, ffn1_down, ln_mix, w_in, w_out, sgu_w, sgu_b, sgu_ln_g, sgu_ln_b, rwkv_mu, rwkv_w0, rwkv_w2, rwkv_a0, rwkv_a2, rwkv_g2, rwkv_k_k, rwkv_k_a, rwkv_r_k, rwkv_gn_g, rwkv_gn_b, ln_xattn, mem_norm, xa_q, xa_k, xa_v, xa_o, ln_ffn2, ffn2_gate, ffn2_up, ffn2_down, final_norm):
    assert ln_ffn1.shape[0] == 1, "single layer"
    bp, seq, _ = x_prompt.shape
    bs = x_sample.shape[0]
    row = lambda a: a.reshape(1, -1).astype(F32)
    bf = lambda a: a.astype(BF16)
    l = 0
    head_id = jnp.arange(B_WIDTH) // HEAD
    tril = jnp.tril(jnp.ones((CHUNK, CHUNK), dtype=bool))
    wmask = jnp.where(tril[None], sgu_w[l], 0)
    p = {
        "ln_mix": row(ln_mix[l]),
        "w_lora": bf(_pad_lora(w_in[l].T[MAIN_W:], axis=0)),
        "sgu_wcat": bf(wmask.transpose(1, 0, 2).reshape(CHUNK, A_GROUPS * CHUNK)),
        "sgu_bias": jnp.repeat(sgu_b[l].T, A_GROUP_DIM, axis=1),
        "sgu_w00": row(jnp.repeat(sgu_w[l][:, 0, 0], A_GROUP_DIM)),
        "sgu_b0": row(jnp.repeat(sgu_b[l][:, 0], A_GROUP_DIM)),
        "sgu_ln_g": row(sgu_ln_g[l]), "sgu_ln_b": row(sgu_ln_b[l]),
        "mu_main": row(rwkv_mu[l][:RKV_W]),
        "mu_lora": row(_pad_lora(rwkv_mu[l][RKV_W:])),
        "w0": row(rwkv_w0[l]), "w2": bf(_pad_rows(rwkv_w2[l], LORA_AD - LORA_WD)),
        "a0": row(rwkv_a0[l]), "a2": bf(_pad_rows(rwkv_a2[l], LORA_GD - LORA_AD)),
        "g2": bf(_pad_rows(rwkv_g2[l], LORA_W - LORA_GD)),
        "k_k": row(rwkv_k_k[l]), "k_a": row(rwkv_k_a[l]), "r_k": row(rwkv_r_k[l]),
        "ones_bd": (head_id[:, None] == head_id[None, :]).astype(BF16),
        "gn_g": row(rwkv_gn_g[l]), "gn_b": row(rwkv_gn_b[l]),
        "ln_xattn": row(ln_xattn[l]),
    }
    fnorm = row(final_norm)

    tm = 512
    mk, mv, mkb, mvb, (wg1, wu1, wd1) = _memkv(
        mem_prompt.reshape(bp * N_MEM, D_MODEL), row(mem_norm[l]), xa_k[l], xa_v[l], tm=MEMKV_ROWS,
        cast=(ffn1_gate[l], ffn1_up[l], ffn1_down[l]))
    ffn1 = (row(ln_ffn1[l]), wg1, wu1, wd1)

    xp = x_prompt.reshape(bp * seq, D_MODEL)
    xs = x_sample.reshape(bs, D_MODEL)
    x1, x1s, (wg2, wu2, wd2, p["xa_q"], xa_o_b, p["w_out"], p["w_main"]) = _ffn(
        xp, xs, *ffn1, tm=FFN_ROWS,
        cast=(ffn2_gate[l], ffn2_up[l], ffn2_down[l], xa_q[l], xa_o[l], w_out[l],
              (w_in[l].T, MAIN_W)))
    ffn2 = (row(ln_ffn2[l]), wg2, wu2, wd2)

    sh = state_shift[l].reshape(bs, B_PROJ)
    (ya_s, r_s, w_s, k_s, v_s, kk_s, b_s, g_s, bonus_s, va_s, zm_s, zl_s) = _mix_in(
        x1s, p, tm=bs, sample=True, shift_main=sh[:, :RKV_W], shift_lora=_pad_lora(sh[:, RKV_W:]))
    o_s, state_t = _scan_sample(jnp.transpose(state_rwkv[l], (1, 2, 3, 0)),
                                r_s, w_s, k_s, v_s, kk_s, b_s)
    state_s = jnp.transpose(state_t, (3, 0, 1, 2))
    x2s, q_s = _post_mix(x1s, ya_s, o_s, g_s, bonus_s, p, tm=bs)

    ya, r, w, k, v, kk, b, g, bonus, zlast = _mix_in(x1, p, tm=tm, sample=False)
    o, s_bd, attn_rows = _scan_prompt(r, w, k, v, kk, b, _head_rows(q_s),
                                      _mem_rows(cache_mem_k[l]), _mem_rows(cache_mem_v[l]),
                                      batch=bp, seq=seq)
    state_p = jnp.stack([s_bd[:, :, :HEAD, :HEAD], s_bd[:, :, HEAD:, HEAD:]],
                        axis=2).reshape(bp, HEADS, HEAD, HEAD)
    x2, attn = _post_mix(x1, ya, o, g, bonus, p, tm=POST_ROWS,
                         mk=mkb.reshape(bp, N_MEM, D_MODEL), mv=mvb.reshape(bp, N_MEM, D_MODEL))
    y_prompt, y_sample, _ = _ffn(x2, x2s, *ffn2, tm=FFN_ROWS, attn=attn,
                                 attns=_from_head_rows(attn_rows), wo=xa_o_b, final_norm=fnorm)
    tiles_per_seq = seq // tm
    zl_rows = zlast.reshape(bp, tiles_per_seq, 8, RKV_W + LORA_W)[:, -1, 0]
    shift_p = _unpad_shift(zl_rows[:, :RKV_W], zl_rows[:, RKV_W:])

    return (y_prompt.reshape(bp, seq, D_MODEL),
            y_sample.reshape(bs, 1, D_MODEL),
            state_p[None],
            shift_p.reshape(1, bp, 1, B_PROJ),
            _from_mem_rows(mk, bp)[None],
            _from_mem_rows(mv, bp)[None],
            state_s[None],
            _unpad_shift(zm_s, zl_s).reshape(1, bs, 1, B_PROJ),
            va_s.reshape(1, bs, 1, A_WIDTH))
```

```python
import functools

import jax
import jax.numpy as jnp
from jax import lax
from jax.experimental import pallas as pl
from jax.experimental.pallas import tpu as pltpu

F32 = jnp.float32
BF16 = jnp.bfloat16

D_MODEL = 1024
SEQ = 2048
A_WIDTH = 512
A_GROUPS = 8
A_GROUP_DIM = 64
CHUNK = 128
B_WIDTH = 512
HEAD = 64
HEADS = 8
PAIRS = HEADS // 2
PAIR_W = 2 * HEAD
DECAY_LORA = 64
AAA_LORA = 64
GATE_LORA = 160
B_PROJ = 3 * B_WIDTH + DECAY_LORA + AAA_LORA + GATE_LORA
MAIN_W = 2 * A_WIDTH + 3 * B_WIDTH
RKV_W = 3 * B_WIDTH
LORA_W = 512
LORA_WD, LORA_AD, LORA_GD = 0, 128, 256
D_FF = 2816
N_MEM = 256
XA_HEADS = 4
XA_DIM = 256
NORM_EPS = 1e-6
LN_EPS = 1e-5
GN_EPS = 64e-5

VMEM_LIMIT = 56 * 1024 * 1024
MIX_ROWS = 512


def _params(n_axes=1):
    return pltpu.CompilerParams(dimension_semantics=("arbitrary",) * n_axes,
                                vmem_limit_bytes=VMEM_LIMIT)


def _const_spec(shape):
    nd = len(shape)
    return pl.BlockSpec(shape, lambda *_: (0,) * nd, pipeline_mode=pl.Buffered(1))


def _rows_spec(tm, width):
    return pl.BlockSpec((tm, width), lambda i: (i, 0))


def _rms(x, g):
    return x * lax.rsqrt(jnp.mean(x * x, axis=-1, keepdims=True) + NORM_EPS) * g


def _dot(a, b):
    return jnp.dot(a.astype(BF16), b, preferred_element_type=F32)


def _seg_sum(x, ones_bd):
    return jnp.dot(x.astype(BF16), ones_bd, preferred_element_type=F32)


FFN_ROWS = 1024
FFN_BLOCK = 768


def _ffn_kernel(*refs, pre, final, n_cast, n_main):
    it = iter(refs)
    x_ref, xs_ref = next(it), next(it)
    if pre:
        attn_ref, attns_ref, wo_ref = next(it), next(it), next(it)
    ln_ref, wg_ref, wu_ref, wd_ref = next(it), next(it), next(it), next(it)
    if final:
        fn_ref = next(it)
    cast_in = [next(it) for _ in range(n_cast)]
    o_ref, os_ref = next(it), next(it)
    cast_out = [next(it) for _ in range(n_cast)]

    def ffn(x, attn):
        if pre:
            x = x + _dot(attn, wo_ref[...])
        xb = _rms(x, ln_ref[...]).astype(BF16)
        y = None
        for c0 in range(0, D_FF, FFN_BLOCK):
            cols = slice(c0, min(c0 + FFN_BLOCK, D_FF))
            g = jnp.dot(xb, wg_ref[:, cols], preferred_element_type=F32)
            u = jnp.dot(xb, wu_ref[:, cols], preferred_element_type=F32)
            h = (g * jax.nn.sigmoid(g) * u).astype(BF16)
            part = jnp.dot(h, wd_ref[cols, :], preferred_element_type=F32)
            y = part if y is None else y + part
        x = x + 0.5 * y
        return _rms(x, fn_ref[...]) if final else x

    step = pl.program_id(0)

    @pl.when(step < n_main)
    def _():
        for src_ref, dst_ref in zip(cast_in, cast_out):
            dst_ref[...] = src_ref[...].astype(BF16)
        o_ref[...] = ffn(x_ref[...], attn_ref[...] if pre else None)

    @pl.when(step == n_main)
    def _():
        os_ref[...] = ffn(xs_ref[...], attns_ref[...] if pre else None)


def _ffn(x, xs, ln, wg, wu, wd, *, tm, attn=None, attns=None, wo=None, final_norm=None, cast=()):
    rows, rows_s = x.shape[0], xs.shape[0]
    n_main = rows // tm
    pre = attn is not None
    final = final_norm is not None
    main = lambda i: (jnp.minimum(i, n_main - 1), 0)
    main_spec = pl.BlockSpec((tm, D_MODEL), main)
    small_spec = _const_spec((rows_s, D_MODEL))
    args, specs = [x, xs], [main_spec, small_spec]
    if pre:
        args += [attn, attns, wo]
        specs += [main_spec, small_spec, _const_spec((D_MODEL, D_MODEL))]
    args += [ln, wg, wu, wd]
    specs += [_const_spec((1, D_MODEL)), _const_spec((D_MODEL, D_FF)),
              _const_spec((D_MODEL, D_FF)), _const_spec((D_FF, D_MODEL))]
    if final:
        args.append(final_norm)
        specs.append(_const_spec((1, D_MODEL)))
    slabs, cast_shapes = _cast_slabs(cast, n_main, index_map=main)
    out = pl.pallas_call(
        functools.partial(_ffn_kernel, pre=pre, final=final, n_cast=len(cast), n_main=n_main),
        grid=(n_main + 1,),
        in_specs=specs + slabs,
        out_specs=[main_spec, pl.BlockSpec((rows_s, D_MODEL), lambda i: (0, 0))] + slabs,
        out_shape=[jax.ShapeDtypeStruct((rows, D_MODEL), F32),
                   jax.ShapeDtypeStruct((rows_s, D_MODEL), F32)] + cast_shapes,
        compiler_params=_params(),
        name="ffn",
    )(*args, *_cast_arrays(cast))
    return out[0], out[1], list(out[2:])


MEMKV_ROWS = 256


def _memkv_kernel(m_ref, g_ref, wk_ref, wv_ref, *rest):
    n_cast = (len(rest) - 4) // 2
    k_ref, v_ref, kb_ref, vb_ref = rest[n_cast:n_cast + 4]
    for src_ref, dst_ref in zip(rest[:n_cast], rest[n_cast + 4:]):
        dst_ref[...] = src_ref[...].astype(BF16)
    mb = _rms(m_ref[...], g_ref[...]).astype(BF16)
    tm = m_ref.shape[0]
    k = _dot(mb, wk_ref[...].astype(BF16))
    v = _dot(mb, wv_ref[...].astype(BF16))
    kb_ref[...] = k.astype(BF16)
    vb_ref[...] = v.astype(BF16)
    for c in range(MEM_ROWS):
        src = (c % XA_HEADS) * (XA_DIM // 128) + c // XA_HEADS
        k_ref[pl.ds(c, tm, stride=MEM_ROWS), :] = k[:, src * 128:(src + 1) * 128]
        v_ref[pl.ds(c, tm, stride=MEM_ROWS), :] = v[:, src * 128:(src + 1) * 128]


def _cast_slabs(cast, steps, index_map=lambda i: (i, 0)):
    specs, shapes = [], []
    for a in cast:
        a, n = a if isinstance(a, tuple) else (a, a.shape[0])
        assert n % (16 * steps) == 0, (n, steps)
        specs.append(pl.BlockSpec((n // steps, a.shape[1]), index_map))
        shapes.append(jax.ShapeDtypeStruct((n, a.shape[1]), BF16))
    return specs, shapes


def _cast_arrays(cast):
    return [a[0] if isinstance(a, tuple) else a for a in cast]


def _memkv(mem, g, wk, wv, *, tm, cast=()):
    rows = mem.shape[0]
    steps = rows // tm
    out = jax.ShapeDtypeStruct((rows * MEM_ROWS, 128), F32)
    outb = jax.ShapeDtypeStruct((rows, D_MODEL), BF16)
    slabs, cast_shapes = _cast_slabs(cast, steps)
    res = pl.pallas_call(
        _memkv_kernel,
        grid=(steps,),
        in_specs=[_rows_spec(tm, D_MODEL), _const_spec((1, D_MODEL)),
                  _const_spec((D_MODEL, D_MODEL)), _const_spec((D_MODEL, D_MODEL))] + slabs,
        out_specs=[_rows_spec(tm * MEM_ROWS, 128)] * 2 + [_rows_spec(tm, D_MODEL)] * 2 + slabs,
        out_shape=[out, out, outb, outb] + cast_shapes,
        compiler_params=_params(),
        name="memkv",
    )(mem, g, wk, wv, *_cast_arrays(cast))
    return res[0], res[1], res[2], res[3], list(res[4:])


def _mix_kernel(*refs, sample, tiles_per_seq):
    it = iter(refs)
    x_ref, ln_ref, wmain_ref, wlora_ref = next(it), next(it), next(it), next(it)
    if sample:
        w00_ref, b0_ref, spm_ref, spl_ref = next(it), next(it), next(it), next(it)
    else:
        wcat_ref, bias_ref = next(it), next(it)
    (lng_ref, lnb_ref, mum_ref, mul_ref, w0_ref, w2_ref, a0_ref, a2_ref, g2_ref,
     kk_ref, ka_ref, rk_ref, ones_ref) = [next(it) for _ in range(13)]
    (ya_ref, r_ref, w_ref, k_ref, v_ref, kn_ref, b_ref, g_ref, bonus_ref) = [
        next(it) for _ in range(9)]
    if sample:
        va_ref, zm_ref, zl_ref = next(it), next(it), next(it)
    else:
        zlast_ref, cm_ref, cl_ref = next(it), next(it), next(it)

    tm = x_ref.shape[0]
    xb = _rms(x_ref[...], ln_ref[...]).astype(BF16)
    zmain = _mm_nt(xb, wmain_ref[...])
    zl = _mm_nt(xb, wlora_ref[...])

    u = jax.nn.gelu(zmain[:, :A_WIDTH])
    vx = jax.nn.gelu(zmain[:, A_WIDTH:2 * A_WIDTH])
    mu = jnp.mean(vx, axis=-1, keepdims=True)
    var = jnp.mean(jnp.square(vx - mu), axis=-1, keepdims=True)
    va = (vx - mu) * lax.rsqrt(var + LN_EPS) * lng_ref[...] + lnb_ref[...]
    if sample:
        mixed = va * w00_ref[...] + b0_ref[...]
        ya_ref[...] = (u * mixed).astype(BF16)
        va_ref[...] = va
    else:
        vab = va.astype(BF16)
        first = lax.broadcasted_iota(jnp.int32, (CHUNK, 2 * A_GROUP_DIM), 1) < A_GROUP_DIM
        for c in range(tm // CHUNK):
            rows = slice(c * CHUNK, (c + 1) * CHUNK)
            for gp in range(A_GROUPS // 2):
                lanes = slice(gp * 2 * A_GROUP_DIM, (gp + 1) * 2 * A_GROUP_DIM)
                vc = vab[rows, lanes]
                zero = jnp.zeros_like(vc)
                rhs = jnp.concatenate([jnp.where(first, vc, zero), jnp.where(first, zero, vc)],
                                      axis=0)
                mixed = jnp.dot(wcat_ref[:, gp * 2 * CHUNK:(gp + 1) * 2 * CHUNK], rhs,
                                preferred_element_type=F32) + bias_ref[:, lanes]
                ya_ref[rows, lanes] = (u[rows, lanes] * mixed).astype(BF16)

    zbm = zmain[:, 2 * A_WIDTH:]
    if sample:
        zpm, zpl = spm_ref[...], spl_ref[...]
        zm_ref[...] = zbm
        zl_ref[...] = zl
    else:
        tile = pl.program_id(0)

        @pl.when(tile % tiles_per_seq == 0)
        def _():
            cm_ref[...] = jnp.zeros_like(cm_ref)
            cl_ref[...] = jnp.zeros_like(cl_ref)

        first_m = lax.broadcasted_iota(jnp.int32, zbm.shape, 0) == 0
        first_l = lax.broadcasted_iota(jnp.int32, zl.shape, 0) == 0
        zpm = jnp.where(first_m, cm_ref[0:1, :], pltpu.roll(zbm, 1, axis=0))
        zpl = jnp.where(first_l, cl_ref[0:1, :], pltpu.roll(zl, 1, axis=0))
        cm_ref[0:1, :] = zbm[tm - 1:tm, :]
        cl_ref[0:1, :] = zl[tm - 1:tm, :]
        zlast_ref[:, :RKV_W] = jnp.broadcast_to(zbm[tm - 1:tm, :], (8, RKV_W))
        zlast_ref[:, RKV_W:] = jnp.broadcast_to(zl[tm - 1:tm, :], (8, LORA_W))
    zsm = zbm + (zpm - zbm) * mum_ref[...]
    zsl = zl + (zpl - zl) * mul_ref[...]
    r = zsm[:, :B_WIDTH]
    k = zsm[:, B_WIDTH:2 * B_WIDTH]
    v = zsm[:, 2 * B_WIDTH:]
    wd = zsl[:, LORA_WD:LORA_AD]
    ad = zsl[:, LORA_AD:LORA_GD]
    gd = zsl[:, LORA_GD:]
    y = w0_ref[...] + _dot(jnp.tanh(wd), w2_ref[...])
    w_log = jnp.minimum(y, 0.0) - jnp.log1p(jnp.exp(-jnp.abs(y))) - 0.5
    log_decay = -jnp.exp(w_log)
    a = jax.nn.sigmoid(a0_ref[...] + _dot(ad, a2_ref[...]))
    gate = _dot(jax.nn.sigmoid(gd), g2_ref[...])
    ones_bd = ones_ref[...]
    kk = k * kk_ref[...]
    kk = kk * lax.rsqrt(jnp.maximum(_seg_sum(kk * kk, ones_bd), 1e-24))
    k2 = k * (1.0 + (a - 1.0) * ka_ref[...])
    r_ref[...] = r
    w_ref[...] = jnp.exp(log_decay) if sample else log_decay
    k_ref[...] = k2
    v_ref[...] = v
    kn_ref[...] = kk
    b_ref[...] = kk * a
    g_ref[...] = gate
    bonus_ref[...] = _seg_sum(r * k2 * rk_ref[...], ones_bd) * v


def _mix_in(x, p, *, tm, sample, shift_main=None, shift_lora=None):
    rows = x.shape[0]
    n_tiles = rows // tm
    args = [x, p["ln_mix"], p["w_main"], p["w_lora"]]
    specs = [_rows_spec(tm, D_MODEL), _const_spec((1, D_MODEL)),
             _const_spec((MAIN_W, D_MODEL)), _const_spec((LORA_W, D_MODEL))]
    if sample:
        args += [p["sgu_w00"], p["sgu_b0"], shift_main, shift_lora]
        specs += [_const_spec((1, A_WIDTH)), _const_spec((1, A_WIDTH)),
                  _rows_spec(tm, RKV_W), _rows_spec(tm, LORA_W)]
    else:
        args += [p["sgu_wcat"], p["sgu_bias"]]
        specs += [_const_spec((CHUNK, A_GROUPS * CHUNK)), _const_spec((CHUNK, A_WIDTH))]
    args += [p["sgu_ln_g"], p["sgu_ln_b"], p["mu_main"], p["mu_lora"], p["w0"], p["w2"],
             p["a0"], p["a2"], p["g2"], p["k_k"], p["k_a"], p["r_k"], p["ones_bd"]]
    specs += [_const_spec((1, A_WIDTH)), _const_spec((1, A_WIDTH)), _const_spec((1, RKV_W)),
              _const_spec((1, LORA_W)), _const_spec((1, B_WIDTH)),
              _const_spec((LORA_AD - LORA_WD, B_WIDTH)), _const_spec((1, B_WIDTH)),
              _const_spec((LORA_GD - LORA_AD, B_WIDTH)), _const_spec((LORA_W - LORA_GD, B_WIDTH)),
              _const_spec((1, B_WIDTH)), _const_spec((1, B_WIDTH)), _const_spec((1, B_WIDTH)),
              _const_spec((B_WIDTH, B_WIDTH))]
    wide = jax.ShapeDtypeStruct((rows, B_WIDTH), F32)
    out_shape = [jax.ShapeDtypeStruct((rows, A_WIDTH), BF16)] + [wide] * 8
    out_specs = [_rows_spec(tm, B_WIDTH)] * 9
    scratch = []
    if sample:
        out_shape += [wide, jax.ShapeDtypeStruct((rows, RKV_W), F32),
                      jax.ShapeDtypeStruct((rows, LORA_W), F32)]
        out_specs += [_rows_spec(tm, A_WIDTH), _rows_spec(tm, RKV_W), _rows_spec(tm, LORA_W)]
    else:
        out_shape += [jax.ShapeDtypeStruct((n_tiles * 8, RKV_W + LORA_W), F32)]
        out_specs += [pl.BlockSpec((8, RKV_W + LORA_W), lambda i: (i, 0))]
        scratch = [pltpu.VMEM((8, RKV_W), F32), pltpu.VMEM((8, LORA_W), F32)]
    return pl.pallas_call(
        functools.partial(_mix_kernel, sample=sample, tiles_per_seq=max(SEQ // tm, 1)),
        grid=(n_tiles,),
        in_specs=specs,
        out_specs=out_specs,
        out_shape=out_shape,
        scratch_shapes=scratch,
        compiler_params=_params(),
        name="mix_in",
    )(*args)


def _each(f, *lists):
    return [f(*xs) for xs in zip(*lists)]


SCAN_C = 64


def _mm(a, b):
    return jnp.dot(a.astype(BF16), b.astype(BF16), preferred_element_type=F32)


def _mm_nt(a, b):
    return lax.dot_general(a.astype(BF16), b.astype(BF16), (((1,), (1,)), ((), ())),
                           preferred_element_type=F32)


def _mm_tn(a, b):
    return lax.dot_general(a.astype(BF16), b.astype(BF16), (((0,), (0,)), ((), ())),
                           preferred_element_type=F32)


def _cumsum_rows(x):
    n = x.shape[0]
    row = lax.broadcasted_iota(jnp.int32, x.shape, 0)
    s = 1
    while s < n:
        x = x + jnp.where(row >= s, pltpu.roll(x, s, axis=0), 0.0)
        s *= 2
    return x


INV_BASE = 8


def _unit_lower_inverse(ns, row, col):
    f0 = jnp.zeros((), F32)
    same = lambda s: (row // s) == (col // s)
    eye = jnp.where(row == col, 1.0, f0)
    ps = _each(lambda n: jnp.where(same(INV_BASE), n, f0), ns)
    ts = _each(lambda p: eye + p, ps)
    s = 2
    while s < INV_BASE:
        ps = _each(lambda p: _mm(p, p), ps)
        yield
        ts = _each(lambda t, p: t + _mm(t, p), ts, ps)
        yield
        s *= 2
    s = INV_BASE
    while s < SCAN_C:
        level = same(2 * s) & jnp.logical_not(same(s))
        ws = _each(lambda n, t: _mm(jnp.where(level, n, f0), t), ns, ts)
        yield
        ts = _each(lambda t, w: t + _mm(t, w), ts, ws)
        yield
        s *= 2
    return ts


def _chunk_pairs(s0s, rs, lws, ks, vs, kks, bs):
    c = SCAN_C
    f0 = jnp.zeros((), F32)
    row = lax.broadcasted_iota(jnp.int32, (2 * c, PAIR_W), 0)
    col = lax.broadcasted_iota(jnp.int32, (2 * c, PAIR_W), 1)
    top, lft = row < c, col < HEAD
    same_head = top == lft
    strict = (row % c) > (col % HEAD)
    row_c = lax.broadcasted_iota(jnp.int32, (c, PAIR_W), 0)
    col_c = lax.broadcasted_iota(jnp.int32, (c, PAIR_W), 1)
    lft_c = col_c < HEAD
    strict_c = row_c > (col_c % HEAD)
    incl_c = row_c >= (col_c % HEAD)

    def prep(r, lw, k, v, kk, b):
        cum = _cumsum_rows(lw)
        end = cum[c - 1:c, :]
        a_t = -kk * jnp.exp(cum - lw)
        r_t = r * jnp.exp(cum)
        einv = jnp.exp(-cum)
        eend = jnp.exp(end - cum)
        return dict(
            x0=jnp.concatenate([a_t, r_t], axis=0),
            bk=jnp.concatenate([b * einv, k * einv], axis=0),
            bk_e=jnp.concatenate([b * eend, k * eend], axis=0),
            w_end=jnp.exp(end), v=v,
            v_l=jnp.where(lft_c, v, f0), v_r=jnp.where(lft_c, f0, v))

    fs = _each(prep, rs, lws, ks, vs, kks, bs)
    yield
    def grams(f, s0):
        bk = f["bk"]
        g = _mm_nt(f["x0"], jnp.concatenate(
            [jnp.where(lft, bk, f0), jnp.where(lft, f0, bk), s0], axis=0))
        g1 = pltpu.roll(g[:, PAIR_W:2 * PAIR_W], HEAD, axis=1)
        return (g[:, :PAIR_W], jnp.concatenate([g1[c:], g1[:c]], axis=0),
                g[:, 2 * PAIR_W:])

    g0s, g1s, pqs = zip(*_each(grams, fs, s0s))
    yield

    def rhs(f, g0, g1, pq):
        ak = jnp.where(strict_c, jnp.where(lft_c, g1[c:], g0[:c]), f0)
        x = pq[:c] + _mm(ak, jnp.concatenate([f["v_r"], f["v_l"]], axis=0))
        return jnp.concatenate([jnp.where(lft_c, x, f0), jnp.where(lft_c, f0, x)], axis=0)

    ys = _each(rhs, fs, g0s, g1s, pqs)
    yield
    ns = _each(lambda g0, g1: jnp.where(strict & same_head, jnp.where(top, g0, g1), f0),
               g0s, g1s)
    ts = yield from _unit_lower_inverse(ns, row, col)
    ys = _each(_mm, ts, ys)
    yield

    def out(f, g0, g1, pq, y):
        lhs = jnp.concatenate([jnp.where(incl_c, g0[c:], f0), jnp.where(incl_c, g1[:c], f0)],
                              axis=1)
        return pq[c:] + _mm(lhs, jnp.concatenate([y[:c], f["v_l"], f["v_r"], y[c:]], axis=0))

    def state(f, s0, y):
        upd = _mm_tn(jnp.concatenate([y[:c] + y[c:], f["v"]], axis=0), f["bk_e"])
        return s0 * f["w_end"] + jnp.where(same_head, upd, f0)

    outs = _each(out, fs, g0s, g1s, pqs, ys)
    yield
    return outs, _each(state, fs, s0s, ys)


SCAN_BATCHES = 4
XA_EVERY = 3


def _run_with(main, side, *, every):
    n = 0
    while True:
        if n % every == 0:
            next(side, None)
        n += 1
        try:
            next(main)
        except StopIteration as stop:
            for _ in side:
                pass
            return stop.value


def _scan_prompt_kernel(r_ref, w_ref, k_ref, v_ref, kk_ref, b_ref, xq_ref, xk_ref, xv_ref,
                        o_ref, sout_ref, xo_ref, s_ref):
    t_blk = pl.program_id(1)

    @pl.when(t_blk == 0)
    def _():
        s_ref[...] = jnp.zeros_like(s_ref)

    chains = [(j, p) for j in range(SCAN_BATCHES) for p in range(PAIRS)]
    lanes = lambda p: slice(p * PAIR_W, (p + 1) * PAIR_W)
    take = lambda ref: [ref[j, :, lanes(p)] for j, p in chains]
    os_, ss = _run_with(
        _chunk_pairs([s_ref[j, p] for j, p in chains], take(r_ref), take(w_ref), take(k_ref),
                     take(v_ref), take(kk_ref), take(b_ref)),
        _xa_attend(xq_ref, xk_ref, xv_ref, xo_ref), every=XA_EVERY)
    for (j, p), o, s_new in zip(chains, os_, ss):
        o_ref[j, :, lanes(p)] = o
        s_ref[j, p] = s_new

    @pl.when(t_blk == pl.num_programs(1) - 1)
    def _():
        sout_ref[...] = s_ref[...]


def _scan_prompt(r, lw, k, v, kk, b, xq, xk, xv, *, batch, seq):
    n_t = seq // SCAN_C
    nb = SCAN_BATCHES
    steps = (batch // nb) * n_t
    n_s = xq.shape[0]
    assert n_s % steps == 0, (n_s, steps)
    per = n_s // steps
    spec = pl.BlockSpec((nb, SCAN_C, B_WIDTH), lambda bi, ti: (bi, ti, 0))
    sspec = pl.BlockSpec((nb, PAIRS, PAIR_W, PAIR_W), lambda bi, ti: (bi, 0, 0, 0))
    step = lambda bi, ti: (bi * n_t + ti, 0, 0)
    qspec = pl.BlockSpec((per, MEM_ROWS, 128), step)
    mspec = pl.BlockSpec((per, N_MEM * MEM_ROWS, 128), step)
    o, s, xo = pl.pallas_call(
        _scan_prompt_kernel,
        grid=(batch // nb, n_t),
        in_specs=[spec] * 6 + [qspec, mspec, mspec],
        out_specs=[spec, sspec, qspec],
        out_shape=[jax.ShapeDtypeStruct((batch, seq, B_WIDTH), F32),
                   jax.ShapeDtypeStruct((batch, PAIRS, PAIR_W, PAIR_W), F32),
                   jax.ShapeDtypeStruct(xq.shape, F32)],
        scratch_shapes=[pltpu.VMEM((nb, PAIRS, PAIR_W, PAIR_W), F32)],
        compiler_params=_params(2),
        name="scan_prompt",
    )(*[x.reshape(batch, seq, B_WIDTH) for x in (r, lw, k, v, kk, b)], xq, xk, xv)
    return o.reshape(batch * seq, B_WIDTH), s, xo


def _scan_sample_kernel(s_ref, r_ref, w_ref, k_ref, v_ref, kk_ref, b_ref, o_ref, sout_ref,
                        t_ref, ot_ref):
    h = pl.program_id(0)

    @pl.when(h == 0)
    def _():
        for i, ref in enumerate((r_ref, w_ref, k_ref, v_ref, kk_ref, b_ref)):
            t_ref[i] = ref[...].T

    base = pl.multiple_of(h * HEAD, HEAD)
    keys = pl.ds(base, HEAD)
    r, w, k, kk, b = [t_ref[i, keys, :] for i in (0, 1, 2, 4, 5)]

    def body(v8, carry):
        rows = pl.ds(pl.multiple_of(base + v8 * 8, 8), 8)
        v_rows = t_ref[3, rows, :]
        outs = []
        for j in range(8):
            vi = v8 * 8 + j
            s = s_ref[0, vi]
            sa = jnp.sum(s * kk, axis=0, keepdims=True)
            s = s * w - sa * b + v_rows[j:j + 1, :] * k
            sout_ref[0, vi] = s
            outs.append(jnp.sum(s * r, axis=0, keepdims=True))
        ot_ref[rows, :] = jnp.concatenate(outs, axis=0)
        return carry

    lax.fori_loop(0, HEAD // 8, body, 0)

    @pl.when(h == pl.num_programs(0) - 1)
    def _():
        o_ref[...] = ot_ref[...].T


def _scan_sample(state_t, r, w, k, v, kk, b):
    rows = r.shape[0]
    sspec = pl.BlockSpec((1, HEAD, HEAD, rows), lambda h: (h, 0, 0, 0))
    spec = _const_spec((rows, B_WIDTH))
    return pl.pallas_call(
        _scan_sample_kernel,
        grid=(HEADS,),
        in_specs=[sspec] + [spec] * 6,
        out_specs=[pl.BlockSpec((rows, B_WIDTH), lambda h: (0, 0)), sspec],
        out_shape=[jax.ShapeDtypeStruct((rows, B_WIDTH), F32),
                   jax.ShapeDtypeStruct(state_t.shape, F32)],
        scratch_shapes=[pltpu.VMEM((6, B_WIDTH, rows), F32), pltpu.VMEM((B_WIDTH, rows), F32)],
        compiler_params=_params(),
        name="scan_sample",
    )(state_t, r, w, k, v, kk, b)


def _softmax_rows(s):
    e = jnp.exp(s - jnp.max(s, axis=-1, keepdims=True))
    return e * (1.0 / jnp.sum(e, axis=-1, keepdims=True))


POST_ROWS = 1024


def _post_kernel(*refs, attend):
    it = iter(refs)
    (x_ref, ya_ref, o_ref, g_ref, bonus_ref, gng_ref, gnb_ref, ones_ref, wo_ref, lnx_ref,
     wq_ref) = [next(it) for _ in range(11)]
    if attend:
        mk_ref, mv_ref = next(it), next(it)
    x2_ref, out_ref = next(it), next(it)

    ones_bd = ones_ref[...]
    o = o_ref[...]
    mu = _seg_sum(o, ones_bd) * (1.0 / HEAD)
    d = o - mu
    var = _seg_sum(d * d, ones_bd) * (1.0 / HEAD)
    on = d * lax.rsqrt(var + GN_EPS) * gng_ref[...] + gnb_ref[...]
    yb = (on + bonus_ref[...]) * g_ref[...]
    x2 = x_ref[...] + jnp.dot(ya_ref[...], wo_ref[:A_WIDTH, :], preferred_element_type=F32) \
        + _dot(yb, wo_ref[A_WIDTH:, :])
    x2_ref[...] = x2
    q = _dot(_rms(x2, lnx_ref[...]), wq_ref[...])
    if not attend:
        out_ref[...] = q
        return
    qb = q.astype(BF16)
    heads = [slice(h * XA_DIM, (h + 1) * XA_DIM) for h in range(XA_HEADS)]
    ss = [lax.dot_general(qb[:, sl], mk_ref[0, :, sl], (((1,), (1,)), ((), ())),
                          preferred_element_type=F32) * (XA_DIM ** -0.5) for sl in heads]
    ps = [_softmax_rows(s) for s in ss]
    for sl, p in zip(heads, ps):
        out_ref[:, sl] = _dot(p, mv_ref[0, :, sl]).astype(BF16)


def _post_mix(x, ya, o, g, bonus, p, *, tm, mk=None, mv=None):
    rows = x.shape[0]
    attend = mk is not None
    args = [x, ya, o, g, bonus, p["gn_g"], p["gn_b"], p["ones_bd"], p["w_out"], p["ln_xattn"],
            p["xa_q"]]
    specs = [_rows_spec(tm, D_MODEL)] + [_rows_spec(tm, B_WIDTH)] * 4 + [
        _const_spec((1, B_WIDTH)), _const_spec((1, B_WIDTH)), _const_spec((B_WIDTH, B_WIDTH)),
        _const_spec((A_WIDTH + B_WIDTH, D_MODEL)),
        _const_spec((1, D_MODEL)), _const_spec((D_MODEL, D_MODEL))]
    if attend:
        tiles_per_seq = SEQ // tm
        mspec = pl.BlockSpec((1, N_MEM, D_MODEL), lambda i: (i // tiles_per_seq, 0, 0))
        args += [mk, mv]
        specs += [mspec, mspec]
    return pl.pallas_call(
        functools.partial(_post_kernel, attend=attend),
        grid=(rows // tm,),
        in_specs=specs,
        out_specs=[_rows_spec(tm, D_MODEL)] * 2,
        out_shape=[jax.ShapeDtypeStruct((rows, D_MODEL), F32),
                   jax.ShapeDtypeStruct((rows, D_MODEL), BF16 if attend else F32)],
        compiler_params=_params(),
        name="post_mix",
    )(*args)


MEM_ROWS = XA_HEADS * (XA_DIM // 128)


def _lane_allreduce(x, op):
    shift = MEM_ROWS
    while shift < 128:
        x = op(x, pltpu.roll(x, shift, axis=1))
        shift *= 2
    return x


def _xa_attend(q_ref, k_ref, v_ref, o_ref):
    f0 = jnp.zeros((), F32)
    n_blk = N_MEM * MEM_ROWS // 128
    sub = lax.broadcasted_iota(jnp.int32, (MEM_ROWS, 128), 0)
    lane = lax.broadcasted_iota(jnp.int32, (MEM_ROWS, 128), 1)
    diag = sub == (lane % MEM_ROWS)
    li = lax.broadcasted_iota(jnp.int32, (128, 128), 0)
    lj = lax.broadcasted_iota(jnp.int32, (128, 128), 1)
    comb = jnp.where((li // MEM_ROWS == lj // MEM_ROWS) & (li % XA_HEADS == lj % XA_HEADS),
                     1.0, 0.0).astype(BF16)
    samples = list(range(q_ref.shape[0]))
    scs = [_mm_nt(q_ref[j], k_ref[j]) for j in samples]

    def partial(sc):
        return jnp.concatenate(
            [jnp.sum(jnp.where(diag, sc[:, t * 128:(t + 1) * 128], f0), axis=0, keepdims=True)
             for t in range(n_blk)], axis=0)

    def scores(part):
        hi = part.astype(BF16)
        lo = (part - hi.astype(F32)).astype(BF16)
        return (jnp.dot(hi, comb, preferred_element_type=F32)
                + jnp.dot(lo, comb, preferred_element_type=F32)) * (XA_DIM ** -0.5)

    def softmax(s):
        mx = _lane_allreduce(jnp.broadcast_to(jnp.max(s, axis=0, keepdims=True), (MEM_ROWS, 128)),
                             jnp.maximum)
        e = jnp.exp(s - mx[0:1, :])
        den = _lane_allreduce(jnp.broadcast_to(jnp.sum(e, axis=0, keepdims=True), (MEM_ROWS, 128)),
                              jnp.add)
        p = e / den[0:1, :]
        return jnp.concatenate(
            [jnp.where(diag, jnp.broadcast_to(p[t:t + 1, :], (MEM_ROWS, 128)), f0)
             for t in range(n_blk)], axis=1)

    yield
    parts = _each(partial, scs)
    yield
    ss = _each(scores, parts)
    yield
    p_rows = _each(softmax, ss)
    yield
    for j, p in zip(samples, p_rows):
        o_ref[j] = _mm(p, v_ref[j])


def _pad_lora(x, axis=-1):
    x = jnp.moveaxis(x, axis, -1)
    wd = x[..., :DECAY_LORA]
    ad = x[..., DECAY_LORA:DECAY_LORA + AAA_LORA]
    gd = x[..., DECAY_LORA + AAA_LORA:]
    z = lambda n: jnp.zeros(x.shape[:-1] + (n,), x.dtype)
    out = jnp.concatenate([wd, z(LORA_AD - DECAY_LORA), ad, z(LORA_GD - LORA_AD - AAA_LORA),
                           gd, z(LORA_W - LORA_GD - GATE_LORA)], axis=-1)
    return jnp.moveaxis(out, -1, axis)


def _unpad_shift(zm, zl):
    return jnp.concatenate([zm, zl[..., LORA_WD:LORA_WD + DECAY_LORA],
                            zl[..., LORA_AD:LORA_AD + AAA_LORA],
                            zl[..., LORA_GD:LORA_GD + GATE_LORA]], axis=-1)


def _pad_rows(w, n):
    return jnp.pad(w, ((0, n - w.shape[0]), (0, 0)))


def _mem_rows(x):
    b = x.shape[0]
    return x.reshape(b, N_MEM, XA_HEADS, XA_DIM // 128, 128).transpose(0, 1, 3, 2, 4).reshape(
        b, N_MEM * MEM_ROWS, 128)


def _from_mem_rows(x, b):
    return x.reshape(b, N_MEM, XA_DIM // 128, XA_HEADS, 128).transpose(0, 1, 3, 2, 4).reshape(
        b, N_MEM, XA_HEADS, XA_DIM)


def _head_rows(x):
    b = x.shape[0]
    return x.reshape(b, XA_HEADS, XA_DIM // 128, 128).transpose(0, 2, 1, 3).reshape(b, MEM_ROWS, 128)


def _from_head_rows(x):
    b = x.shape[0]
    return x.reshape(b, XA_DIM // 128, XA_HEADS, 128).transpose(0, 2, 1, 3).reshape(b, D_MODEL)


def kernel(x_prompt, x_sample, state_rwkv, state_shift, cache_mem_k, cache_mem_v, mem_prompt, ln_ffn1, ffn1_gate, ffn1_up, ffn1_down, ln_mix, w_in, w_out, sgu_w, sgu_b, sgu_ln_g, sgu_ln_b, rwkv_mu, rwkv_w0, rwkv_w2, rwkv_a0, rwkv_a2, rwkv_g2, rwkv_k_k, rwkv_k_a, rwkv_r_k, rwkv_gn_g, rwkv_gn_b, ln_xattn, mem_norm, xa_q, xa_k, xa_v, xa_o, ln_ffn2, ffn2_gate, ffn2_up, ffn2_down, final_norm):
    assert ln_ffn1.shape[0] == 1, "single layer"
    bp, seq, _ = x_prompt.shape
    bs = x_sample.shape[0]
    row = lambda a: a.reshape(1, -1).astype(F32)
    bf = lambda a: a.astype(BF16)
    l = 0
    head_id = jnp.arange(B_WIDTH) // HEAD
    tril = jnp.tril(jnp.ones((CHUNK, CHUNK), dtype=bool))
    wmask = jnp.where(tril[None], sgu_w[l], 0)
    p = {
        "ln_mix": row(ln_mix[l]),
        "w_lora": bf(_pad_lora(w_in[l].T[MAIN_W:], axis=0)),
        "sgu_wcat": bf(wmask.transpose(1, 0, 2).reshape(CHUNK, A_GROUPS * CHUNK)),
        "sgu_bias": jnp.repeat(sgu_b[l].T, A_GROUP_DIM, axis=1),
        "sgu_w00": row(jnp.repeat(sgu_w[l][:, 0, 0], A_GROUP_DIM)),
        "sgu_b0": row(jnp.repeat(sgu_b[l][:, 0], A_GROUP_DIM)),
        "sgu_ln_g": row(sgu_ln_g[l]), "sgu_ln_b": row(sgu_ln_b[l]),
        "mu_main": row(rwkv_mu[l][:RKV_W]),
        "mu_lora": row(_pad_lora(rwkv_mu[l][RKV_W:])),
        "w0": row(rwkv_w0[l]), "w2": bf(_pad_rows(rwkv_w2[l], LORA_AD - LORA_WD)),
        "a0": row(rwkv_a0[l]), "a2": bf(_pad_rows(rwkv_a2[l], LORA_GD - LORA_AD)),
        "g2": bf(_pad_rows(rwkv_g2[l], LORA_W - LORA_GD)),
        "k_k": row(rwkv_k_k[l]), "k_a": row(rwkv_k_a[l]), "r_k": row(rwkv_r_k[l]),
        "ones_bd": (head_id[:, None] == head_id[None, :]).astype(BF16),
        "gn_g": row(rwkv_gn_g[l]), "gn_b": row(rwkv_gn_b[l]),
        "ln_xattn": row(ln_xattn[l]),
    }
    fnorm = row(final_norm)

    mk, mv, mkb, mvb, (wg1, wu1, wd1) = _memkv(
        mem_prompt.reshape(bp * N_MEM, D_MODEL), row(mem_norm[l]), xa_k[l], xa_v[l], tm=MEMKV_ROWS,
        cast=(ffn1_gate[l], ffn1_up[l], ffn1_down[l]))
    ffn1 = (row(ln_ffn1[l]), wg1, wu1, wd1)

    xp = x_prompt.reshape(bp * seq, D_MODEL)
    xs = x_sample.reshape(bs, D_MODEL)
    x1, x1s, (wg2, wu2, wd2, p["xa_q"], xa_o_b, p["w_out"], p["w_main"]) = _ffn(
        xp, xs, *ffn1, tm=FFN_ROWS,
        cast=(ffn2_gate[l], ffn2_up[l], ffn2_down[l], xa_q[l], xa_o[l], w_out[l],
              (w_in[l].T, MAIN_W)))
    ffn2 = (row(ln_ffn2[l]), wg2, wu2, wd2)

    sh = state_shift[l].reshape(bs, B_PROJ)
    (ya_s, r_s, w_s, k_s, v_s, kk_s, b_s, g_s, bonus_s, va_s, zm_s, zl_s) = _mix_in(
        x1s, p, tm=bs, sample=True, shift_main=sh[:, :RKV_W], shift_lora=_pad_lora(sh[:, RKV_W:]))
    o_s, state_t = _scan_sample(jnp.transpose(state_rwkv[l], (1, 2, 3, 0)),
                                r_s, w_s, k_s, v_s, kk_s, b_s)
    state_s = jnp.transpose(state_t, (3, 0, 1, 2))
    x2s, q_s = _post_mix(x1s, ya_s, o_s, g_s, bonus_s, p, tm=bs)

    ya, r, w, k, v, kk, b, g, bonus, zlast = _mix_in(x1, p, tm=MIX_ROWS, sample=False)
    o, s_bd, attn_rows = _scan_prompt(r, w, k, v, kk, b, _head_rows(q_s),
                                      _mem_rows(cache_mem_k[l]), _mem_rows(cache_mem_v[l]),
                                      batch=bp, seq=seq)
    state_p = jnp.stack([s_bd[:, :, :HEAD, :HEAD], s_bd[:, :, HEAD:, HEAD:]],
                        axis=2).reshape(bp, HEADS, HEAD, HEAD)
    x2, attn = _post_mix(x1, ya, o, g, bonus, p, tm=POST_ROWS,
                         mk=mkb.reshape(bp, N_MEM, D_MODEL), mv=mvb.reshape(bp, N_MEM, D_MODEL))
    y_prompt, y_sample, _ = _ffn(x2, x2s, *ffn2, tm=FFN_ROWS, attn=attn,
                                 attns=_from_head_rows(attn_rows), wo=xa_o_b, final_norm=fnorm)
    tiles_per_seq = seq // MIX_ROWS
    zl_rows = zlast.reshape(bp, tiles_per_seq, 8, RKV_W + LORA_W)[:, -1, 0]
    shift_p = _unpad_shift(zl_rows[:, :RKV_W], zl_rows[:, RKV_W:])

    return (y_prompt.reshape(bp, seq, D_MODEL),
            y_sample.reshape(bs, 1, D_MODEL),
            state_p[None],
            shift_p.reshape(1, bp, 1, B_PROJ),
            _from_mem_rows(mk, bp)[None],
            _from_mem_rows(mv, bp)[None],
            state_s[None],
            _unpad_shift(zm_s, zl_s).reshape(1, bs, 1, B_PROJ),
            va_s.reshape(1, bs, 1, A_WIDTH))
```

```python
import functools

import jax
import jax.numpy as jnp
from jax import lax
from jax.experimental import pallas as pl
from jax.experimental.pallas import tpu as pltpu

F32 = jnp.float32
BF16 = jnp.bfloat16

D_MODEL = 1024
SEQ = 2048
A_WIDTH = 512
A_GROUPS = 8
A_GROUP_DIM = 64
CHUNK = 128
B_WIDTH = 512
HEAD = 64
HEADS = 8
PAIRS = HEADS // 2
PAIR_W = 2 * HEAD
DECAY_LORA = 64
AAA_LORA = 64
GATE_LORA = 160
B_PROJ = 3 * B_WIDTH + DECAY_LORA + AAA_LORA + GATE_LORA
MAIN_W = 2 * A_WIDTH + 3 * B_WIDTH
RKV_W = 3 * B_WIDTH
LORA_W = 512
LORA_WD, LORA_AD, LORA_GD = 0, 128, 256
D_FF = 2816
N_MEM = 256
XA_HEADS = 4
XA_DIM = 256
NORM_EPS = 1e-6
LN_EPS = 1e-5
GN_EPS = 64e-5

VMEM_LIMIT = 56 * 1024 * 1024
MIX_ROWS = 512


def _params(n_axes=1):
    return pltpu.CompilerParams(dimension_semantics=("arbitrary",) * n_axes,
                                vmem_limit_bytes=VMEM_LIMIT)


def _const_spec(shape):
    nd = len(shape)
    return pl.BlockSpec(shape, lambda *_: (0,) * nd, pipeline_mode=pl.Buffered(1))


def _rows_spec(tm, width):
    return pl.BlockSpec((tm, width), lambda i: (i, 0))


def _rms(x, g):
    return x * lax.rsqrt(jnp.mean(x * x, axis=-1, keepdims=True) + NORM_EPS) * g


def _dot(a, b):
    return jnp.dot(a.astype(BF16), b, preferred_element_type=F32)


def _seg_sum(x, ones_bd):
    return jnp.dot(x.astype(BF16), ones_bd, preferred_element_type=F32)


FFN_ROWS = 1024
FFN_BLOCK = 768


def _ffn_kernel(*refs, pre, final, n_cast, n_main):
    it = iter(refs)
    x_ref, xs_ref = next(it), next(it)
    if pre:
        attn_ref, attns_ref, wo_ref = next(it), next(it), next(it)
    ln_ref, wg_ref, wu_ref, wd_ref = next(it), next(it), next(it), next(it)
    if final:
        fn_ref = next(it)
    cast_in = [next(it) for _ in range(n_cast)]
    o_ref, os_ref = next(it), next(it)
    cast_out = [next(it) for _ in range(n_cast)]

    def ffn(x, attn):
        if pre:
            x = x + _dot(attn, wo_ref[...])
        xb = _rms(x, ln_ref[...]).astype(BF16)
        y = None
        for c0 in range(0, D_FF, FFN_BLOCK):
            cols = slice(c0, min(c0 + FFN_BLOCK, D_FF))
            g = jnp.dot(xb, wg_ref[:, cols], preferred_element_type=F32)
            u = jnp.dot(xb, wu_ref[:, cols], preferred_element_type=F32)
            h = (g * jax.nn.sigmoid(g) * u).astype(BF16)
            part = jnp.dot(h, wd_ref[cols, :], preferred_element_type=F32)
            y = part if y is None else y + part
        x = x + 0.5 * y
        return _rms(x, fn_ref[...]) if final else x

    step = pl.program_id(0)

    @pl.when(step < n_main)
    def _():
        for src_ref, dst_ref in zip(cast_in, cast_out):
            dst_ref[...] = src_ref[...].astype(BF16)
        o_ref[...] = ffn(x_ref[...], attn_ref[...] if pre else None)

    @pl.when(step == n_main)
    def _():
        os_ref[...] = ffn(xs_ref[...], attns_ref[...] if pre else None)


def _ffn(x, xs, ln, wg, wu, wd, *, tm, attn=None, attns=None, wo=None, final_norm=None, cast=()):
    rows, rows_s = x.shape[0], xs.shape[0]
    n_main = rows // tm
    pre = attn is not None
    final = final_norm is not None
    main = lambda i: (jnp.minimum(i, n_main - 1), 0)
    main_spec = pl.BlockSpec((tm, D_MODEL), main)
    small_spec = _const_spec((rows_s, D_MODEL))
    args, specs = [x, xs], [main_spec, small_spec]
    if pre:
        args += [attn, attns, wo]
        specs += [main_spec, small_spec, _const_spec((D_MODEL, D_MODEL))]
    args += [ln, wg, wu, wd]
    specs += [_const_spec((1, D_MODEL)), _const_spec((D_MODEL, D_FF)),
              _const_spec((D_MODEL, D_FF)), _const_spec((D_FF, D_MODEL))]
    if final:
        args.append(final_norm)
        specs.append(_const_spec((1, D_MODEL)))
    slabs, cast_shapes = _cast_slabs(cast, n_main, index_map=main)
    out = pl.pallas_call(
        functools.partial(_ffn_kernel, pre=pre, final=final, n_cast=len(cast), n_main=n_main),
        grid=(n_main + 1,),
        in_specs=specs + slabs,
        out_specs=[main_spec, pl.BlockSpec((rows_s, D_MODEL), lambda i: (0, 0))] + slabs,
        out_shape=[jax.ShapeDtypeStruct((rows, D_MODEL), F32),
                   jax.ShapeDtypeStruct((rows_s, D_MODEL), F32)] + cast_shapes,
        compiler_params=_params(),
        name="ffn",
    )(*args, *_cast_arrays(cast))
    return out[0], out[1], list(out[2:])


MEMKV_ROWS = 256


def _memkv_kernel(m_ref, g_ref, wk_ref, wv_ref, *rest):
    n_cast = (len(rest) - 4) // 2
    k_ref, v_ref, kb_ref, vb_ref = rest[n_cast:n_cast + 4]
    for src_ref, dst_ref in zip(rest[:n_cast], rest[n_cast + 4:]):
        dst_ref[...] = src_ref[...].astype(BF16)
    mb = _rms(m_ref[...], g_ref[...]).astype(BF16)
    tm = m_ref.shape[0]
    k = _dot(mb, wk_ref[...].astype(BF16))
    v = _dot(mb, wv_ref[...].astype(BF16))
    kb_ref[...] = k.astype(BF16)
    vb_ref[...] = v.astype(BF16)
    for c in range(MEM_ROWS):
        src = (c % XA_HEADS) * (XA_DIM // 128) + c // XA_HEADS
        k_ref[pl.ds(c, tm, stride=MEM_ROWS), :] = k[:, src * 128:(src + 1) * 128]
        v_ref[pl.ds(c, tm, stride=MEM_ROWS), :] = v[:, src * 128:(src + 1) * 128]


def _cast_slabs(cast, steps, index_map=lambda i: (i, 0)):
    specs, shapes = [], []
    for a in cast:
        a, n = a if isinstance(a, tuple) else (a, a.shape[0])
        assert n % (16 * steps) == 0, (n, steps)
        specs.append(pl.BlockSpec((n // steps, a.shape[1]), index_map))
        shapes.append(jax.ShapeDtypeStruct((n, a.shape[1]), BF16))
    return specs, shapes


def _cast_arrays(cast):
    return [a[0] if isinstance(a, tuple) else a for a in cast]


def _memkv(mem, g, wk, wv, *, tm, cast=()):
    rows = mem.shape[0]
    steps = rows // tm
    out = jax.ShapeDtypeStruct((rows * MEM_ROWS, 128), F32)
    outb = jax.ShapeDtypeStruct((rows, D_MODEL), BF16)
    slabs, cast_shapes = _cast_slabs(cast, steps)
    res = pl.pallas_call(
        _memkv_kernel,
        grid=(steps,),
        in_specs=[_rows_spec(tm, D_MODEL), _const_spec((1, D_MODEL)),
                  _const_spec((D_MODEL, D_MODEL)), _const_spec((D_MODEL, D_MODEL))] + slabs,
        out_specs=[_rows_spec(tm * MEM_ROWS, 128)] * 2 + [_rows_spec(tm, D_MODEL)] * 2 + slabs,
        out_shape=[out, out, outb, outb] + cast_shapes,
        compiler_params=_params(),
        name="memkv",
    )(mem, g, wk, wv, *_cast_arrays(cast))
    return res[0], res[1], res[2], res[3], list(res[4:])


def _mix_kernel(*refs, sample, tiles_per_seq):
    it = iter(refs)
    x_ref, ln_ref, wmain_ref, wlora_ref = next(it), next(it), next(it), next(it)
    if sample:
        w00_ref, b0_ref, spm_ref, spl_ref = next(it), next(it), next(it), next(it)
    else:
        wcat_ref, bias_ref = next(it), next(it)
    (lng_ref, lnb_ref, mum_ref, mul_ref, w0_ref, w2_ref, a0_ref, a2_ref, g2_ref,
     kk_ref, ka_ref, rk_ref, ones_ref) = [next(it) for _ in range(13)]
    (ya_ref, r_ref, w_ref, k_ref, v_ref, kn_ref, b_ref, g_ref, bonus_ref) = [
        next(it) for _ in range(9)]
    if sample:
        va_ref, zm_ref, zl_ref = next(it), next(it), next(it)
    else:
        zlast_ref, cm_ref, cl_ref = next(it), next(it), next(it)

    tm = x_ref.shape[0]
    if not sample:
        @pl.when(pl.program_id(0) % tiles_per_seq == 0)
        def _():
            cm_ref[...] = jnp.zeros_like(cm_ref)
            cl_ref[...] = jnp.zeros_like(cl_ref)

    xb = _rms(x_ref[...], ln_ref[...]).astype(BF16)
    zmain = _mm_nt(xb, wmain_ref[...])
    zl = _mm_nt(xb, wlora_ref[...])

    u = jax.nn.gelu(zmain[:, :A_WIDTH])
    vx = jax.nn.gelu(zmain[:, A_WIDTH:2 * A_WIDTH])
    mu = jnp.mean(vx, axis=-1, keepdims=True)
    var = jnp.mean(jnp.square(vx - mu), axis=-1, keepdims=True)
    va = (vx - mu) * lax.rsqrt(var + LN_EPS) * lng_ref[...] + lnb_ref[...]
    if sample:
        mixed = va * w00_ref[...] + b0_ref[...]
        ya_ref[...] = (u * mixed).astype(BF16)
        va_ref[...] = va
    else:
        vab = va.astype(BF16)
        first = lax.broadcasted_iota(jnp.int32, (CHUNK, 2 * A_GROUP_DIM), 1) < A_GROUP_DIM
        for c in range(tm // CHUNK):
            rows = slice(c * CHUNK, (c + 1) * CHUNK)
            for gp in range(A_GROUPS // 2):
                lanes = slice(gp * 2 * A_GROUP_DIM, (gp + 1) * 2 * A_GROUP_DIM)
                vc = vab[rows, lanes]
                zero = jnp.zeros_like(vc)
                rhs = jnp.concatenate([jnp.where(first, vc, zero), jnp.where(first, zero, vc)],
                                      axis=0)
                mixed = jnp.dot(wcat_ref[:, gp * 2 * CHUNK:(gp + 1) * 2 * CHUNK], rhs,
                                preferred_element_type=F32) + bias_ref[:, lanes]
                ya_ref[rows, lanes] = (u[rows, lanes] * mixed).astype(BF16)

    zbm = zmain[:, 2 * A_WIDTH:]
    if sample:
        zpm, zpl = spm_ref[...], spl_ref[...]
        zm_ref[...] = zbm
        zl_ref[...] = zl
    else:
        first_m = lax.broadcasted_iota(jnp.int32, zbm.shape, 0) == 0
        first_l = lax.broadcasted_iota(jnp.int32, zl.shape, 0) == 0
        zpm = jnp.where(first_m, cm_ref[0:1, :], pltpu.roll(zbm, 1, axis=0))
        zpl = jnp.where(first_l, cl_ref[0:1, :], pltpu.roll(zl, 1, axis=0))
        cm_ref[0:1, :] = zbm[tm - 1:tm, :]
        cl_ref[0:1, :] = zl[tm - 1:tm, :]
        zlast_ref[:, :RKV_W] = jnp.broadcast_to(zbm[tm - 1:tm, :], (8, RKV_W))
        zlast_ref[:, RKV_W:] = jnp.broadcast_to(zl[tm - 1:tm, :], (8, LORA_W))
    zsm = zbm + (zpm - zbm) * mum_ref[...]
    zsl = zl + (zpl - zl) * mul_ref[...]
    r = zsm[:, :B_WIDTH]
    k = zsm[:, B_WIDTH:2 * B_WIDTH]
    v = zsm[:, 2 * B_WIDTH:]
    wd = zsl[:, LORA_WD:LORA_AD]
    ad = zsl[:, LORA_AD:LORA_GD]
    gd = zsl[:, LORA_GD:]
    y = w0_ref[...] + _dot(jnp.tanh(wd), w2_ref[...])
    w_log = jnp.minimum(y, 0.0) - jnp.log1p(jnp.exp(-jnp.abs(y))) - 0.5
    log_decay = -jnp.exp(w_log)
    a = jax.nn.sigmoid(a0_ref[...] + _dot(ad, a2_ref[...]))
    gate = _dot(jax.nn.sigmoid(gd), g2_ref[...])
    ones_bd = ones_ref[...]
    kk = k * kk_ref[...]
    kk = kk * lax.rsqrt(jnp.maximum(_seg_sum(kk * kk, ones_bd), 1e-24))
    k2 = k * (1.0 + (a - 1.0) * ka_ref[...])
    r_ref[...] = r
    w_ref[...] = jnp.exp(log_decay) if sample else log_decay
    k_ref[...] = k2
    v_ref[...] = v
    kn_ref[...] = kk
    b_ref[...] = kk * a
    g_ref[...] = gate
    bonus_ref[...] = _seg_sum(r * k2 * rk_ref[...], ones_bd) * v


def _mix_in(x, p, *, tm, sample, shift_main=None, shift_lora=None):
    rows = x.shape[0]
    n_tiles = rows // tm
    args = [x, p["ln_mix"], p["w_main"], p["w_lora"]]
    specs = [_rows_spec(tm, D_MODEL), _const_spec((1, D_MODEL)),
             _const_spec((MAIN_W, D_MODEL)), _const_spec((LORA_W, D_MODEL))]
    if sample:
        args += [p["sgu_w00"], p["sgu_b0"], shift_main, shift_lora]
        specs += [_const_spec((1, A_WIDTH)), _const_spec((1, A_WIDTH)),
                  _rows_spec(tm, RKV_W), _rows_spec(tm, LORA_W)]
    else:
        args += [p["sgu_wcat"], p["sgu_bias"]]
        specs += [_const_spec((CHUNK, A_GROUPS * CHUNK)), _const_spec((CHUNK, A_WIDTH))]
    args += [p["sgu_ln_g"], p["sgu_ln_b"], p["mu_main"], p["mu_lora"], p["w0"], p["w2"],
             p["a0"], p["a2"], p["g2"], p["k_k"], p["k_a"], p["r_k"], p["ones_bd"]]
    specs += [_const_spec((1, A_WIDTH)), _const_spec((1, A_WIDTH)), _const_spec((1, RKV_W)),
              _const_spec((1, LORA_W)), _const_spec((1, B_WIDTH)),
              _const_spec((LORA_AD - LORA_WD, B_WIDTH)), _const_spec((1, B_WIDTH)),
              _const_spec((LORA_GD - LORA_AD, B_WIDTH)), _const_spec((LORA_W - LORA_GD, B_WIDTH)),
              _const_spec((1, B_WIDTH)), _const_spec((1, B_WIDTH)), _const_spec((1, B_WIDTH)),
              _const_spec((B_WIDTH, B_WIDTH))]
    wide = jax.ShapeDtypeStruct((rows, B_WIDTH), F32)
    out_shape = [jax.ShapeDtypeStruct((rows, A_WIDTH), BF16)] + [wide] * 8
    out_specs = [_rows_spec(tm, B_WIDTH)] * 9
    scratch = []
    if sample:
        out_shape += [wide, jax.ShapeDtypeStruct((rows, RKV_W), F32),
                      jax.ShapeDtypeStruct((rows, LORA_W), F32)]
        out_specs += [_rows_spec(tm, A_WIDTH), _rows_spec(tm, RKV_W), _rows_spec(tm, LORA_W)]
    else:
        out_shape += [jax.ShapeDtypeStruct((n_tiles * 8, RKV_W + LORA_W), F32)]
        out_specs += [pl.BlockSpec((8, RKV_W + LORA_W), lambda i: (i, 0))]
        scratch = [pltpu.VMEM((8, RKV_W), F32), pltpu.VMEM((8, LORA_W), F32)]
    return pl.pallas_call(
        functools.partial(_mix_kernel, sample=sample, tiles_per_seq=max(SEQ // tm, 1)),
        grid=(n_tiles,),
        in_specs=specs,
        out_specs=out_specs,
        out_shape=out_shape,
        scratch_shapes=scratch,
        compiler_params=_params(),
        name="mix_in",
    )(*args)


def _each(f, *lists):
    return [f(*xs) for xs in zip(*lists)]


SCAN_C = 64


def _mm(a, b):
    return jnp.dot(a.astype(BF16), b.astype(BF16), preferred_element_type=F32)


def _mm_nt(a, b):
    return lax.dot_general(a.astype(BF16), b.astype(BF16), (((1,), (1,)), ((), ())),
                           preferred_element_type=F32)


def _mm_tn(a, b):
    return lax.dot_general(a.astype(BF16), b.astype(BF16), (((0,), (0,)), ((), ())),
                           preferred_element_type=F32)


def _cumsum_rows(x):
    n = x.shape[0]
    row = lax.broadcasted_iota(jnp.int32, x.shape, 0)
    s = 1
    while s < n:
        x = x + jnp.where(row >= s, pltpu.roll(x, s, axis=0), 0.0)
        s *= 2
    return x


INV_BASE = 8


def _unit_lower_inverse(ns, row, col):
    f0 = jnp.zeros((), F32)
    same = lambda s: (row // s) == (col // s)
    eye = jnp.where(row == col, 1.0, f0)
    ps = _each(lambda n: jnp.where(same(INV_BASE), n, f0), ns)
    ts = _each(lambda p: eye + p, ps)
    s = 2
    while s < INV_BASE:
        ps = _each(lambda p: _mm(p, p), ps)
        yield
        ts = _each(lambda t, p: t + _mm(t, p), ts, ps)
        yield
        s *= 2
    s = INV_BASE
    while s < SCAN_C:
        level = same(2 * s) & jnp.logical_not(same(s))
        ws = _each(lambda n, t: _mm(jnp.where(level, n, f0), t), ns, ts)
        yield
        ts = _each(lambda t, w: t + _mm(t, w), ts, ws)
        yield
        s *= 2
    return ts


def _chunk_pairs(s0s, rs, lws, ks, vs, kks, bs):
    c = SCAN_C
    f0 = jnp.zeros((), F32)
    row = lax.broadcasted_iota(jnp.int32, (2 * c, PAIR_W), 0)
    col = lax.broadcasted_iota(jnp.int32, (2 * c, PAIR_W), 1)
    top, lft = row < c, col < HEAD
    same_head = top == lft
    strict = (row % c) > (col % HEAD)
    row_c = lax.broadcasted_iota(jnp.int32, (c, PAIR_W), 0)
    col_c = lax.broadcasted_iota(jnp.int32, (c, PAIR_W), 1)
    lft_c = col_c < HEAD
    strict_c = row_c > (col_c % HEAD)
    incl_c = row_c >= (col_c % HEAD)

    def prep(r, lw, k, v, kk, b):
        cum = _cumsum_rows(lw)
        end = cum[c - 1:c, :]
        a_t = -kk * jnp.exp(cum - lw)
        r_t = r * jnp.exp(cum)
        einv = jnp.exp(-cum)
        eend = jnp.exp(end - cum)
        return dict(
            x0=jnp.concatenate([a_t, r_t], axis=0),
            bk=jnp.concatenate([b * einv, k * einv], axis=0),
            bk_e=jnp.concatenate([b * eend, k * eend], axis=0),
            w_end=jnp.exp(end), v=v,
            v_l=jnp.where(lft_c, v, f0), v_r=jnp.where(lft_c, f0, v))

    fs = _each(prep, rs, lws, ks, vs, kks, bs)
    yield
    def grams(f, s0):
        bk = f["bk"]
        g = _mm_nt(f["x0"], jnp.concatenate(
            [jnp.where(lft, bk, f0), jnp.where(lft, f0, bk), s0], axis=0))
        g1 = pltpu.roll(g[:, PAIR_W:2 * PAIR_W], HEAD, axis=1)
        return (g[:, :PAIR_W], jnp.concatenate([g1[c:], g1[:c]], axis=0),
                g[:, 2 * PAIR_W:])

    g0s, g1s, pqs = zip(*_each(grams, fs, s0s))
    yield

    def rhs(f, g0, g1, pq):
        ak = jnp.where(strict_c, jnp.where(lft_c, g1[c:], g0[:c]), f0)
        x = pq[:c] + _mm(ak, jnp.concatenate([f["v_r"], f["v_l"]], axis=0))
        return jnp.concatenate([jnp.where(lft_c, x, f0), jnp.where(lft_c, f0, x)], axis=0)

    ys = _each(rhs, fs, g0s, g1s, pqs)
    yield
    ns = _each(lambda g0, g1: jnp.where(strict & same_head, jnp.where(top, g0, g1), f0),
               g0s, g1s)
    ts = yield from _unit_lower_inverse(ns, row, col)
    ys = _each(_mm, ts, ys)
    yield

    def out(f, g0, g1, pq, y):
        lhs = jnp.concatenate([jnp.where(incl_c, g0[c:], f0), jnp.where(incl_c, g1[:c], f0)],
                              axis=1)
        return pq[c:] + _mm(lhs, jnp.concatenate([y[:c], f["v_l"], f["v_r"], y[c:]], axis=0))

    def state(f, s0, y):
        upd = _mm_tn(jnp.concatenate([y[:c] + y[c:], f["v"]], axis=0), f["bk_e"])
        return s0 * f["w_end"] + jnp.where(same_head, upd, f0)

    outs = _each(out, fs, g0s, g1s, pqs, ys)
    yield
    return outs, _each(state, fs, s0s, ys)


SCAN_BATCHES = 4
XA_EVERY = 3


def _run_with(main, side, *, every):
    n = 0
    while True:
        if n % every == 0:
            next(side, None)
        n += 1
        try:
            next(main)
        except StopIteration as stop:
            for _ in side:
                pass
            return stop.value


def _scan_prompt_kernel(r_ref, w_ref, k_ref, v_ref, kk_ref, b_ref, xq_ref, xk_ref, xv_ref,
                        o_ref, sout_ref, xo_ref, s_ref):
    t_blk = pl.program_id(1)

    @pl.when(t_blk == 0)
    def _():
        s_ref[...] = jnp.zeros_like(s_ref)

    chains = [(j, p) for j in range(SCAN_BATCHES) for p in range(PAIRS)]
    lanes = lambda p: slice(p * PAIR_W, (p + 1) * PAIR_W)
    take = lambda ref: [ref[j, :, lanes(p)] for j, p in chains]
    os_, ss = _run_with(
        _chunk_pairs([s_ref[j, p] for j, p in chains], take(r_ref), take(w_ref), take(k_ref),
                     take(v_ref), take(kk_ref), take(b_ref)),
        _xa_attend(xq_ref, xk_ref, xv_ref, xo_ref), every=XA_EVERY)
    for (j, p), o, s_new in zip(chains, os_, ss):
        o_ref[j, :, lanes(p)] = o
        s_ref[j, p] = s_new

    @pl.when(t_blk == pl.num_programs(1) - 1)
    def _():
        sout_ref[...] = s_ref[...]


def _scan_prompt(r, lw, k, v, kk, b, xq, xk, xv, *, batch, seq):
    n_t = seq // SCAN_C
    nb = SCAN_BATCHES
    steps = (batch // nb) * n_t
    n_s = xq.shape[0]
    assert n_s % steps == 0, (n_s, steps)
    per = n_s // steps
    spec = pl.BlockSpec((nb, SCAN_C, B_WIDTH), lambda bi, ti: (bi, ti, 0))
    sspec = pl.BlockSpec((nb, PAIRS, PAIR_W, PAIR_W), lambda bi, ti: (bi, 0, 0, 0))
    step = lambda bi, ti: (bi * n_t + ti, 0, 0)
    qspec = pl.BlockSpec((per, MEM_ROWS, 128), step)
    mspec = pl.BlockSpec((per, N_MEM * MEM_ROWS, 128), step)
    o, s, xo = pl.pallas_call(
        _scan_prompt_kernel,
        grid=(batch // nb, n_t),
        in_specs=[spec] * 6 + [qspec, mspec, mspec],
        out_specs=[spec, sspec, qspec],
        out_shape=[jax.ShapeDtypeStruct((batch, seq, B_WIDTH), F32),
                   jax.ShapeDtypeStruct((batch, PAIRS, PAIR_W, PAIR_W), F32),
                   jax.ShapeDtypeStruct(xq.shape, F32)],
        scratch_shapes=[pltpu.VMEM((nb, PAIRS, PAIR_W, PAIR_W), F32)],
        compiler_params=_params(2),
        name="scan_prompt",
    )(*[x.reshape(batch, seq, B_WIDTH) for x in (r, lw, k, v, kk, b)], xq, xk, xv)
    return o.reshape(batch * seq, B_WIDTH), s, xo


def _scan_sample_kernel(s_ref, r_ref, w_ref, k_ref, v_ref, kk_ref, b_ref, o_ref, sout_ref,
                        t_ref, ot_ref):
    h = pl.program_id(0)

    @pl.when(h == 0)
    def _():
        for i, ref in enumerate((r_ref, w_ref, k_ref, v_ref, kk_ref, b_ref)):
            t_ref[i] = ref[...].T

    base = pl.multiple_of(h * HEAD, HEAD)
    keys = pl.ds(base, HEAD)
    r, w, k, kk, b = [t_ref[i, keys, :] for i in (0, 1, 2, 4, 5)]

    def body(v8, carry):
        rows = pl.ds(pl.multiple_of(base + v8 * 8, 8), 8)
        v_rows = t_ref[3, rows, :]
        outs = []
        for j in range(8):
            vi = v8 * 8 + j
            s = s_ref[0, vi]
            sa = jnp.sum(s * kk, axis=0, keepdims=True)
            s = s * w - sa * b + v_rows[j:j + 1, :] * k
            sout_ref[0, vi] = s
            outs.append(jnp.sum(s * r, axis=0, keepdims=True))
        ot_ref[rows, :] = jnp.concatenate(outs, axis=0)
        return carry

    lax.fori_loop(0, HEAD // 8, body, 0)

    @pl.when(h == pl.num_programs(0) - 1)
    def _():
        o_ref[...] = ot_ref[...].T


def _scan_sample(state_t, r, w, k, v, kk, b):
    rows = r.shape[0]
    sspec = pl.BlockSpec((1, HEAD, HEAD, rows), lambda h: (h, 0, 0, 0))
    spec = _const_spec((rows, B_WIDTH))
    return pl.pallas_call(
        _scan_sample_kernel,
        grid=(HEADS,),
        in_specs=[sspec] + [spec] * 6,
        out_specs=[pl.BlockSpec((rows, B_WIDTH), lambda h: (0, 0)), sspec],
        out_shape=[jax.ShapeDtypeStruct((rows, B_WIDTH), F32),
                   jax.ShapeDtypeStruct(state_t.shape, F32)],
        scratch_shapes=[pltpu.VMEM((6, B_WIDTH, rows), F32), pltpu.VMEM((B_WIDTH, rows), F32)],
        compiler_params=_params(),
        name="scan_sample",
    )(state_t, r, w, k, v, kk, b)


def _softmax_rows(s):
    e = jnp.exp(s - jnp.max(s, axis=-1, keepdims=True))
    return e * (1.0 / jnp.sum(e, axis=-1, keepdims=True))


POST_ROWS = 1024


def _post_kernel(*refs, attend):
    it = iter(refs)
    (x_ref, ya_ref, o_ref, g_ref, bonus_ref, gng_ref, gnb_ref, ones_ref, wo_ref, lnx_ref,
     wq_ref) = [next(it) for _ in range(11)]
    if attend:
        mk_ref, mv_ref = next(it), next(it)
    x2_ref, out_ref = next(it), next(it)

    ones_bd = ones_ref[...]
    o = o_ref[...]
    mu = _seg_sum(o, ones_bd) * (1.0 / HEAD)
    d = o - mu
    var = _seg_sum(d * d, ones_bd) * (1.0 / HEAD)
    on = d * lax.rsqrt(var + GN_EPS) * gng_ref[...] + gnb_ref[...]
    yb = (on + bonus_ref[...]) * g_ref[...]
    y = jnp.concatenate([ya_ref[...], yb.astype(BF16)], axis=1)
    x2 = x_ref[...] + jnp.dot(y, wo_ref[...], preferred_element_type=F32)
    x2_ref[...] = x2
    q = _dot(_rms(x2, lnx_ref[...]), wq_ref[...])
    if not attend:
        out_ref[...] = q
        return
    qb = q.astype(BF16)
    heads = [slice(h * XA_DIM, (h + 1) * XA_DIM) for h in range(XA_HEADS)]
    ss = [lax.dot_general(qb[:, sl], mk_ref[0, :, sl], (((1,), (1,)), ((), ())),
                          preferred_element_type=F32) * (XA_DIM ** -0.5) for sl in heads]
    ps = [_softmax_rows(s) for s in ss]
    for sl, p in zip(heads, ps):
        out_ref[:, sl] = _dot(p, mv_ref[0, :, sl]).astype(BF16)


def _post_mix(x, ya, o, g, bonus, p, *, tm, mk=None, mv=None):
    rows = x.shape[0]
    attend = mk is not None
    args = [x, ya, o, g, bonus, p["gn_g"], p["gn_b"], p["ones_bd"], p["w_out"], p["ln_xattn"],
            p["xa_q"]]
    specs = [_rows_spec(tm, D_MODEL)] + [_rows_spec(tm, B_WIDTH)] * 4 + [
        _const_spec((1, B_WIDTH)), _const_spec((1, B_WIDTH)), _const_spec((B_WIDTH, B_WIDTH)),
        _const_spec((A_WIDTH + B_WIDTH, D_MODEL)),
        _const_spec((1, D_MODEL)), _const_spec((D_MODEL, D_MODEL))]
    if attend:
        tiles_per_seq = SEQ // tm
        mspec = pl.BlockSpec((1, N_MEM, D_MODEL), lambda i: (i // tiles_per_seq, 0, 0))
        args += [mk, mv]
        specs += [mspec, mspec]
    return pl.pallas_call(
        functools.partial(_post_kernel, attend=attend),
        grid=(rows // tm,),
        in_specs=specs,
        out_specs=[_rows_spec(tm, D_MODEL)] * 2,
        out_shape=[jax.ShapeDtypeStruct((rows, D_MODEL), F32),
                   jax.ShapeDtypeStruct((rows, D_MODEL), BF16 if attend else F32)],
        compiler_params=_params(),
        name="post_mix",
    )(*args)


MEM_ROWS = XA_HEADS * (XA_DIM // 128)


def _lane_allreduce(x, op):
    shift = MEM_ROWS
    while shift < 128:
        x = op(x, pltpu.roll(x, shift, axis=1))
        shift *= 2
    return x


def _xa_attend(q_ref, k_ref, v_ref, o_ref):
    f0 = jnp.zeros((), F32)
    n_blk = N_MEM * MEM_ROWS // 128
    sub = lax.broadcasted_iota(jnp.int32, (MEM_ROWS, 128), 0)
    lane = lax.broadcasted_iota(jnp.int32, (MEM_ROWS, 128), 1)
    diag = sub == (lane % MEM_ROWS)
    li = lax.broadcasted_iota(jnp.int32, (128, 128), 0)
    lj = lax.broadcasted_iota(jnp.int32, (128, 128), 1)
    comb = jnp.where((li // MEM_ROWS == lj // MEM_ROWS) & (li % XA_HEADS == lj % XA_HEADS),
                     1.0, 0.0).astype(BF16)
    samples = list(range(q_ref.shape[0]))
    scs = [_mm_nt(q_ref[j], k_ref[j]) for j in samples]

    def partial(sc):
        return jnp.concatenate(
            [jnp.sum(jnp.where(diag, sc[:, t * 128:(t + 1) * 128], f0), axis=0, keepdims=True)
             for t in range(n_blk)], axis=0)

    def scores(part):
        hi = part.astype(BF16)
        lo = (part - hi.astype(F32)).astype(BF16)
        return (jnp.dot(hi, comb, preferred_element_type=F32)
                + jnp.dot(lo, comb, preferred_element_type=F32)) * (XA_DIM ** -0.5)

    def softmax(s):
        mx = _lane_allreduce(jnp.broadcast_to(jnp.max(s, axis=0, keepdims=True), (MEM_ROWS, 128)),
                             jnp.maximum)
        e = jnp.exp(s - mx[0:1, :])
        den = _lane_allreduce(jnp.broadcast_to(jnp.sum(e, axis=0, keepdims=True), (MEM_ROWS, 128)),
                              jnp.add)
        p = e / den[0:1, :]
        return jnp.concatenate(
            [jnp.where(diag, jnp.broadcast_to(p[t:t + 1, :], (MEM_ROWS, 128)), f0)
             for t in range(n_blk)], axis=1)

    yield
    parts = _each(partial, scs)
    yield
    ss = _each(scores, parts)
    yield
    p_rows = _each(softmax, ss)
    yield
    for j, p in zip(samples, p_rows):
        o_ref[j] = _mm(p, v_ref[j])


def _pad_lora(x, axis=-1):
    x = jnp.moveaxis(x, axis, -1)
    wd = x[..., :DECAY_LORA]
    ad = x[..., DECAY_LORA:DECAY_LORA + AAA_LORA]
    gd = x[..., DECAY_LORA + AAA_LORA:]
    z = lambda n: jnp.zeros(x.shape[:-1] + (n,), x.dtype)
    out = jnp.concatenate([wd, z(LORA_AD - DECAY_LORA), ad, z(LORA_GD - LORA_AD - AAA_LORA),
                           gd, z(LORA_W - LORA_GD - GATE_LORA)], axis=-1)
    return jnp.moveaxis(out, -1, axis)


def _unpad_shift(zm, zl):
    return jnp.concatenate([zm, zl[..., LORA_WD:LORA_WD + DECAY_LORA],
                            zl[..., LORA_AD:LORA_AD + AAA_LORA],
                            zl[..., LORA_GD:LORA_GD + GATE_LORA]], axis=-1)


def _pad_rows(w, n):
    return jnp.pad(w, ((0, n - w.shape[0]), (0, 0)))


def _mem_rows(x):
    b = x.shape[0]
    return x.reshape(b, N_MEM, XA_HEADS, XA_DIM // 128, 128).transpose(0, 1, 3, 2, 4).reshape(
        b, N_MEM * MEM_ROWS, 128)


def _from_mem_rows(x, b):
    return x.reshape(b, N_MEM, XA_DIM // 128, XA_HEADS, 128).transpose(0, 1, 3, 2, 4).reshape(
        b, N_MEM, XA_HEADS, XA_DIM)


def _head_rows(x):
    b = x.shape[0]
    return x.reshape(b, XA_HEADS, XA_DIM // 128, 128).transpose(0, 2, 1, 3).reshape(b, MEM_ROWS, 128)


def _from_head_rows(x):
    b = x.shape[0]
    return x.reshape(b, XA_DIM // 128, XA_HEADS, 128).transpose(0, 2, 1, 3).reshape(b, D_MODEL)


def kernel(x_prompt, x_sample, state_rwkv, state_shift, cache_mem_k, cache_mem_v, mem_prompt, ln_ffn1, ffn1_gate, ffn1_up, ffn1_down, ln_mix, w_in, w_out, sgu_w, sgu_b, sgu_ln_g, sgu_ln_b, rwkv_mu, rwkv_w0, rwkv_w2, rwkv_a0, rwkv_a2, rwkv_g2, rwkv_k_k, rwkv_k_a, rwkv_r_k, rwkv_gn_g, rwkv_gn_b, ln_xattn, mem_norm, xa_q, xa_k, xa_v, xa_o, ln_ffn2, ffn2_gate, ffn2_up, ffn2_down, final_norm):
    assert ln_ffn1.shape[0] == 1, "single layer"
    bp, seq, _ = x_prompt.shape
    bs = x_sample.shape[0]
    row = lambda a: a.reshape(1, -1).astype(F32)
    bf = lambda a: a.astype(BF16)
    l = 0
    head_id = jnp.arange(B_WIDTH) // HEAD
    tril = jnp.tril(jnp.ones((CHUNK, CHUNK), dtype=bool))
    wmask = jnp.where(tril[None], sgu_w[l], 0)
    p = {
        "ln_mix": row(ln_mix[l]),
        "w_lora": bf(_pad_lora(w_in[l].T[MAIN_W:], axis=0)),
        "sgu_wcat": bf(wmask.transpose(1, 0, 2).reshape(CHUNK, A_GROUPS * CHUNK)),
        "sgu_bias": jnp.repeat(sgu_b[l].T, A_GROUP_DIM, axis=1),
        "sgu_w00": row(jnp.repeat(sgu_w[l][:, 0, 0], A_GROUP_DIM)),
        "sgu_b0": row(jnp.repeat(sgu_b[l][:, 0], A_GROUP_DIM)),
        "sgu_ln_g": row(sgu_ln_g[l]), "sgu_ln_b": row(sgu_ln_b[l]),
        "mu_main": row(rwkv_mu[l][:RKV_W]),
        "mu_lora": row(_pad_lora(rwkv_mu[l][RKV_W:])),
        "w0": row(rwkv_w0[l]), "w2": bf(_pad_rows(rwkv_w2[l], LORA_AD - LORA_WD)),
        "a0": row(rwkv_a0[l]), "a2": bf(_pad_rows(rwkv_a2[l], LORA_GD - LORA_AD)),
        "g2": bf(_pad_rows(rwkv_g2[l], LORA_W - LORA_GD)),
        "k_k": row(rwkv_k_k[l]), "k_a": row(rwkv_k_a[l]), "r_k": row(rwkv_r_k[l]),
        "ones_bd": (head_id[:, None] == head_id[None, :]).astype(BF16),
        "gn_g": row(rwkv_gn_g[l]), "gn_b": row(rwkv_gn_b[l]),
        "ln_xattn": row(ln_xattn[l]),
    }
    fnorm = row(final_norm)

    mk, mv, mkb, mvb, (wg1, wu1, wd1) = _memkv(
        mem_prompt.reshape(bp * N_MEM, D_MODEL), row(mem_norm[l]), xa_k[l], xa_v[l], tm=MEMKV_ROWS,
        cast=(ffn1_gate[l], ffn1_up[l], ffn1_down[l]))
    ffn1 = (row(ln_ffn1[l]), wg1, wu1, wd1)

    xp = x_prompt.reshape(bp * seq, D_MODEL)
    xs = x_sample.reshape(bs, D_MODEL)
    x1, x1s, (wg2, wu2, wd2, p["xa_q"], xa_o_b, p["w_out"], p["w_main"]) = _ffn(
        xp, xs, *ffn1, tm=FFN_ROWS,
        cast=(ffn2_gate[l], ffn2_up[l], ffn2_down[l], xa_q[l], xa_o[l], w_out[l],
              (w_in[l].T, MAIN_W)))
    ffn2 = (row(ln_ffn2[l]), wg2, wu2, wd2)

    sh = state_shift[l].reshape(bs, B_PROJ)
    (ya_s, r_s, w_s, k_s, v_s, kk_s, b_s, g_s, bonus_s, va_s, zm_s, zl_s) = _mix_in(
        x1s, p, tm=bs, sample=True, shift_main=sh[:, :RKV_W], shift_lora=_pad_lora(sh[:, RKV_W:]))
    o_s, state_t = _scan_sample(jnp.transpose(state_rwkv[l], (1, 2, 3, 0)),
                                r_s, w_s, k_s, v_s, kk_s, b_s)
    state_s = jnp.transpose(state_t, (3, 0, 1, 2))
    x2s, q_s = _post_mix(x1s, ya_s, o_s, g_s, bonus_s, p, tm=bs)

    ya, r, w, k, v, kk, b, g, bonus, zlast = _mix_in(x1, p, tm=MIX_ROWS, sample=False)
    o, s_bd, attn_rows = _scan_prompt(r, w, k, v, kk, b, _head_rows(q_s),
                                      _mem_rows(cache_mem_k[l]), _mem_rows(cache_mem_v[l]),
                                      batch=bp, seq=seq)
    state_p = jnp.stack([s_bd[:, :, :HEAD, :HEAD], s_bd[:, :, HEAD:, HEAD:]],
                        axis=2).reshape(bp, HEADS, HEAD, HEAD)
    x2, attn = _post_mix(x1, ya, o, g, bonus, p, tm=POST_ROWS,
                         mk=mkb.reshape(bp, N_MEM, D_MODEL), mv=mvb.reshape(bp, N_MEM, D_MODEL))
    y_prompt, y_sample, _ = _ffn(x2, x2s, *ffn2, tm=FFN_ROWS, attn=attn,
                                 attns=_from_head_rows(attn_rows), wo=xa_o_b, final_norm=fnorm)
    tiles_per_seq = seq // MIX_ROWS
    zl_rows = zlast.reshape(bp, tiles_per_seq, 8, RKV_W + LORA_W)[:, -1, 0]
    shift_p = _unpad_shift(zl_rows[:, :RKV_W], zl_rows[:, RKV_W:])

    return (y_prompt.reshape(bp, seq, D_MODEL),
            y_sample.reshape(bs, 1, D_MODEL),
            state_p[None],
            shift_p.reshape(1, bp, 1, B_PROJ),
            _from_mem_rows(mk, bp)[None],
            _from_mem_rows(mv, bp)[None],
            state_s[None],
            _unpad_shift(zm_s, zl_s).reshape(1, bs, 1, B_PROJ),
            va_s.reshape(1, bs, 1, A_WIDTH))
```

```python
import functools

import jax
import jax.numpy as jnp
from jax import lax
from jax.experimental import pallas as pl
from jax.experimental.pallas import tpu as pltpu

F32 = jnp.float32
BF16 = jnp.bfloat16

D_MODEL = 1024
SEQ = 2048
A_WIDTH = 512
A_GROUPS = 8
A_GROUP_DIM = 64
CHUNK = 128
B_WIDTH = 512
HEAD = 64
HEADS = 8
PAIRS = HEADS // 2
PAIR_W = 2 * HEAD
DECAY_LORA = 64
AAA_LORA = 64
GATE_LORA = 160
B_PROJ = 3 * B_WIDTH + DECAY_LORA + AAA_LORA + GATE_LORA
MAIN_W = 2 * A_WIDTH + 3 * B_WIDTH
RKV_W = 3 * B_WIDTH
LORA_W = 512
LORA_WD, LORA_AD, LORA_GD = 0, 128, 256
D_FF = 2816
N_MEM = 256
XA_HEADS = 4
XA_DIM = 256
NORM_EPS = 1e-6
LN_EPS = 1e-5
GN_EPS = 64e-5

V7X_VMEM_BYTES = 64 * 1024 * 1024
VMEM_LIMIT = V7X_VMEM_BYTES * 7 // 8
MIX_ROWS = 512


def _params(n_axes=1):
    return pltpu.CompilerParams(dimension_semantics=("arbitrary",) * n_axes,
                                vmem_limit_bytes=VMEM_LIMIT)


def _const_spec(shape):
    nd = len(shape)
    return pl.BlockSpec(shape, lambda *_: (0,) * nd, pipeline_mode=pl.Buffered(1))


def _rows_spec(tm, width):
    return pl.BlockSpec((tm, width), lambda i: (i, 0))


def _rms(x, g):
    return x * lax.rsqrt(jnp.mean(x * x, axis=-1, keepdims=True) + NORM_EPS) * g


def _dot(a, b):
    return jnp.dot(a.astype(BF16), b, preferred_element_type=F32)


def _seg_sum(x, ones_bd):
    return jnp.dot(x.astype(BF16), ones_bd, preferred_element_type=F32)


FFN_ROWS = 1024
FFN_BLOCK = 768


def _ffn_kernel(*refs, pre, final, n_cast, n_main):
    it = iter(refs)
    x_ref, xs_ref = next(it), next(it)
    if pre:
        attn_ref, attns_ref, wo_ref = next(it), next(it), next(it)
    ln_ref, wg_ref, wu_ref, wd_ref = next(it), next(it), next(it), next(it)
    if final:
        fn_ref = next(it)
    cast_in = [next(it) for _ in range(n_cast)]
    o_ref, os_ref = next(it), next(it)
    cast_out = [next(it) for _ in range(n_cast)]

    def ffn(x, attn):
        if pre:
            x = x + _dot(attn, wo_ref[...])
        xb = _rms(x, ln_ref[...]).astype(BF16)
        y = None
        for c0 in range(0, D_FF, FFN_BLOCK):
            cols = slice(c0, min(c0 + FFN_BLOCK, D_FF))
            g = jnp.dot(xb, wg_ref[:, cols], preferred_element_type=F32)
            u = jnp.dot(xb, wu_ref[:, cols], preferred_element_type=F32)
            h = (g * jax.nn.sigmoid(g) * u).astype(BF16)
            part = jnp.dot(h, wd_ref[cols, :], preferred_element_type=F32)
            y = part if y is None else y + part
        x = x + 0.5 * y
        return _rms(x, fn_ref[...]) if final else x

    step = pl.program_id(0)

    @pl.when(step < n_main)
    def _():
        for src_ref, dst_ref in zip(cast_in, cast_out):
            dst_ref[...] = src_ref[...].astype(BF16)
        o_ref[...] = ffn(x_ref[...], attn_ref[...] if pre else None)

    @pl.when(step == n_main)
    def _():
        os_ref[...] = ffn(xs_ref[...], attns_ref[...] if pre else None)


def _ffn(x, xs, ln, wg, wu, wd, *, tm, attn=None, attns=None, wo=None, final_norm=None, cast=()):
    rows, rows_s = x.shape[0], xs.shape[0]
    n_main = rows // tm
    pre = attn is not None
    final = final_norm is not None
    main = lambda i: (jnp.minimum(i, n_main - 1), 0)
    main_spec = pl.BlockSpec((tm, D_MODEL), main)
    small_spec = _const_spec((rows_s, D_MODEL))
    args, specs = [x, xs], [main_spec, small_spec]
    if pre:
        args += [attn, attns, wo]
        specs += [main_spec, small_spec, _const_spec((D_MODEL, D_MODEL))]
    args += [ln, wg, wu, wd]
    specs += [_const_spec((1, D_MODEL)), _const_spec((D_MODEL, D_FF)),
              _const_spec((D_MODEL, D_FF)), _const_spec((D_FF, D_MODEL))]
    if final:
        args.append(final_norm)
        specs.append(_const_spec((1, D_MODEL)))
    slabs, cast_shapes = _cast_slabs(cast, n_main, index_map=main)
    out = pl.pallas_call(
        functools.partial(_ffn_kernel, pre=pre, final=final, n_cast=len(cast), n_main=n_main),
        grid=(n_main + 1,),
        in_specs=specs + slabs,
        out_specs=[main_spec, pl.BlockSpec((rows_s, D_MODEL), lambda i: (0, 0))] + slabs,
        out_shape=[jax.ShapeDtypeStruct((rows, D_MODEL), F32),
                   jax.ShapeDtypeStruct((rows_s, D_MODEL), F32)] + cast_shapes,
        compiler_params=_params(),
        name="ffn",
    )(*args, *_cast_arrays(cast))
    return out[0], out[1], list(out[2:])


MEMKV_ROWS = 256


def _memkv_kernel(m_ref, g_ref, wk_ref, wv_ref, *rest):
    n_cast = (len(rest) - 4) // 2
    k_ref, v_ref, kb_ref, vb_ref = rest[n_cast:n_cast + 4]
    for src_ref, dst_ref in zip(rest[:n_cast], rest[n_cast + 4:]):
        dst_ref[...] = src_ref[...].astype(BF16)
    mb = _rms(m_ref[...], g_ref[...]).astype(BF16)
    tm = m_ref.shape[0]
    k = _dot(mb, wk_ref[...].astype(BF16))
    v = _dot(mb, wv_ref[...].astype(BF16))
    kb_ref[...] = k.astype(BF16)
    vb_ref[...] = v.astype(BF16)
    for c in range(MEM_ROWS):
        src = (c % XA_HEADS) * (XA_DIM // 128) + c // XA_HEADS
        k_ref[pl.ds(c, tm, stride=MEM_ROWS), :] = k[:, src * 128:(src + 1) * 128]
        v_ref[pl.ds(c, tm, stride=MEM_ROWS), :] = v[:, src * 128:(src + 1) * 128]


def _cast_slabs(cast, steps, index_map=lambda i: (i, 0)):
    specs, shapes = [], []
    for a in cast:
        a, n = a if isinstance(a, tuple) else (a, a.shape[0])
        assert n % (16 * steps) == 0, (n, steps)
        specs.append(pl.BlockSpec((n // steps, a.shape[1]), index_map))
        shapes.append(jax.ShapeDtypeStruct((n, a.shape[1]), BF16))
    return specs, shapes


def _cast_arrays(cast):
    return [a[0] if isinstance(a, tuple) else a for a in cast]


def _memkv(mem, g, wk, wv, *, tm, cast=()):
    rows = mem.shape[0]
    steps = rows // tm
    out = jax.ShapeDtypeStruct((rows * MEM_ROWS, 128), F32)
    outb = jax.ShapeDtypeStruct((rows, D_MODEL), BF16)
    slabs, cast_shapes = _cast_slabs(cast, steps)
    res = pl.pallas_call(
        _memkv_kernel,
        grid=(steps,),
        in_specs=[_rows_spec(tm, D_MODEL), _const_spec((1, D_MODEL)),
                  _const_spec((D_MODEL, D_MODEL)), _const_spec((D_MODEL, D_MODEL))] + slabs,
        out_specs=[_rows_spec(tm * MEM_ROWS, 128)] * 2 + [_rows_spec(tm, D_MODEL)] * 2 + slabs,
        out_shape=[out, out, outb, outb] + cast_shapes,
        compiler_params=_params(),
        name="memkv",
    )(mem, g, wk, wv, *_cast_arrays(cast))
    return res[0], res[1], res[2], res[3], list(res[4:])


def _mix_kernel(*refs, sample, tiles_per_seq):
    it = iter(refs)
    x_ref, ln_ref, wmain_ref, wlora_ref = next(it), next(it), next(it), next(it)
    if sample:
        w00_ref, b0_ref, spm_ref, spl_ref = next(it), next(it), next(it), next(it)
    else:
        wcat_ref, bias_ref = next(it), next(it)
    (lng_ref, lnb_ref, mum_ref, mul_ref, w0_ref, w2_ref, a0_ref, a2_ref, g2_ref,
     kk_ref, ka_ref, rk_ref, ones_ref) = [next(it) for _ in range(13)]
    (ya_ref, r_ref, w_ref, k_ref, v_ref, kn_ref, b_ref, g_ref, bonus_ref) = [
        next(it) for _ in range(9)]
    if sample:
        va_ref, zm_ref, zl_ref = next(it), next(it), next(it)
    else:
        zlast_ref, cm_ref, cl_ref = next(it), next(it), next(it)

    tm = x_ref.shape[0]
    if not sample:
        @pl.when(pl.program_id(0) % tiles_per_seq == 0)
        def _():
            cm_ref[...] = jnp.zeros_like(cm_ref)
            cl_ref[...] = jnp.zeros_like(cl_ref)

    xb = _rms(x_ref[...], ln_ref[...]).astype(BF16)
    zl = _mm_nt(xb, wlora_ref[...])
    zmain = _mm_nt(xb, wmain_ref[...])

    u = jax.nn.gelu(zmain[:, :A_WIDTH])
    vx = jax.nn.gelu(zmain[:, A_WIDTH:2 * A_WIDTH])
    mu = jnp.mean(vx, axis=-1, keepdims=True)
    var = jnp.mean(jnp.square(vx - mu), axis=-1, keepdims=True)
    va = (vx - mu) * lax.rsqrt(var + LN_EPS) * lng_ref[...] + lnb_ref[...]
    if sample:
        mixed = va * w00_ref[...] + b0_ref[...]
        ya_ref[...] = (u * mixed).astype(BF16)
        va_ref[...] = va
    else:
        vab = va.astype(BF16)
        first = lax.broadcasted_iota(jnp.int32, (CHUNK, 2 * A_GROUP_DIM), 1) < A_GROUP_DIM
        for c in range(tm // CHUNK):
            rows = slice(c * CHUNK, (c + 1) * CHUNK)
            for gp in range(A_GROUPS // 2):
                lanes = slice(gp * 2 * A_GROUP_DIM, (gp + 1) * 2 * A_GROUP_DIM)
                vc = vab[rows, lanes]
                zero = jnp.zeros_like(vc)
                rhs = jnp.concatenate([jnp.where(first, vc, zero), jnp.where(first, zero, vc)],
                                      axis=0)
                mixed = jnp.dot(wcat_ref[:, gp * 2 * CHUNK:(gp + 1) * 2 * CHUNK], rhs,
                                preferred_element_type=F32) + bias_ref[:, lanes]
                ya_ref[rows, lanes] = (u[rows, lanes] * mixed).astype(BF16)

    zbm = zmain[:, 2 * A_WIDTH:]
    if sample:
        zpm, zpl = spm_ref[...], spl_ref[...]
        zm_ref[...] = zbm
        zl_ref[...] = zl
    else:
        first_m = lax.broadcasted_iota(jnp.int32, zbm.shape, 0) == 0
        first_l = lax.broadcasted_iota(jnp.int32, zl.shape, 0) == 0
        zpm = jnp.where(first_m, cm_ref[0:1, :], pltpu.roll(zbm, 1, axis=0))
        zpl = jnp.where(first_l, cl_ref[0:1, :], pltpu.roll(zl, 1, axis=0))
        cm_ref[0:1, :] = zbm[tm - 1:tm, :]
        cl_ref[0:1, :] = zl[tm - 1:tm, :]
        zlast_ref[:, :RKV_W] = jnp.broadcast_to(zbm[tm - 1:tm, :], (8, RKV_W))
        zlast_ref[:, RKV_W:] = jnp.broadcast_to(zl[tm - 1:tm, :], (8, LORA_W))
    zsm = zbm + (zpm - zbm) * mum_ref[...]
    zsl = zl + (zpl - zl) * mul_ref[...]
    r = zsm[:, :B_WIDTH]
    k = zsm[:, B_WIDTH:2 * B_WIDTH]
    v = zsm[:, 2 * B_WIDTH:]
    wd = zsl[:, LORA_WD:LORA_AD]
    ad = zsl[:, LORA_AD:LORA_GD]
    gd = zsl[:, LORA_GD:]
    y = w0_ref[...] + _dot(jnp.tanh(wd), w2_ref[...])
    w_log = jnp.minimum(y, 0.0) - jnp.log1p(jnp.exp(-jnp.abs(y))) - 0.5
    log_decay = -jnp.exp(w_log)
    a = jax.nn.sigmoid(a0_ref[...] + _dot(ad, a2_ref[...]))
    gate = _dot(jax.nn.sigmoid(gd), g2_ref[...])
    ones_bd = ones_ref[...]
    kk = k * kk_ref[...]
    kk = kk * lax.rsqrt(jnp.maximum(_seg_sum(kk * kk, ones_bd), 1e-24))
    k2 = k * (1.0 + (a - 1.0) * ka_ref[...])
    r_ref[...] = r
    w_ref[...] = jnp.exp(log_decay) if sample else log_decay
    k_ref[...] = k2
    v_ref[...] = v
    kn_ref[...] = kk
    b_ref[...] = kk * a
    g_ref[...] = gate
    bonus_ref[...] = _seg_sum(r * k2 * rk_ref[...], ones_bd) * v


def _mix_in(x, p, *, tm, sample, shift_main=None, shift_lora=None):
    rows = x.shape[0]
    n_tiles = rows // tm
    args = [x, p["ln_mix"], p["w_main"], p["w_lora"]]
    specs = [_rows_spec(tm, D_MODEL), _const_spec((1, D_MODEL)),
             _const_spec((MAIN_W, D_MODEL)), _const_spec((LORA_W, D_MODEL))]
    if sample:
        args += [p["sgu_w00"], p["sgu_b0"], shift_main, shift_lora]
        specs += [_const_spec((1, A_WIDTH)), _const_spec((1, A_WIDTH)),
                  _rows_spec(tm, RKV_W), _rows_spec(tm, LORA_W)]
    else:
        args += [p["sgu_wcat"], p["sgu_bias"]]
        specs += [_const_spec((CHUNK, A_GROUPS * CHUNK)), _const_spec((CHUNK, A_WIDTH))]
    args += [p["sgu_ln_g"], p["sgu_ln_b"], p["mu_main"], p["mu_lora"], p["w0"], p["w2"],
             p["a0"], p["a2"], p["g2"], p["k_k"], p["k_a"], p["r_k"], p["ones_bd"]]
    specs += [_const_spec((1, A_WIDTH)), _const_spec((1, A_WIDTH)), _const_spec((1, RKV_W)),
              _const_spec((1, LORA_W)), _const_spec((1, B_WIDTH)),
              _const_spec((LORA_AD - LORA_WD, B_WIDTH)), _const_spec((1, B_WIDTH)),
              _const_spec((LORA_GD - LORA_AD, B_WIDTH)), _const_spec((LORA_W - LORA_GD, B_WIDTH)),
              _const_spec((1, B_WIDTH)), _const_spec((1, B_WIDTH)), _const_spec((1, B_WIDTH)),
              _const_spec((B_WIDTH, B_WIDTH))]
    wide = jax.ShapeDtypeStruct((rows, B_WIDTH), F32)
    out_shape = [jax.ShapeDtypeStruct((rows, A_WIDTH), BF16)] + [wide] * 8
    out_specs = [_rows_spec(tm, B_WIDTH)] * 9
    scratch = []
    if sample:
        out_shape += [wide, jax.ShapeDtypeStruct((rows, RKV_W), F32),
                      jax.ShapeDtypeStruct((rows, LORA_W), F32)]
        out_specs += [_rows_spec(tm, A_WIDTH), _rows_spec(tm, RKV_W), _rows_spec(tm, LORA_W)]
    else:
        out_shape += [jax.ShapeDtypeStruct((n_tiles * 8, RKV_W + LORA_W), F32)]
        out_specs += [pl.BlockSpec((8, RKV_W + LORA_W), lambda i: (i, 0))]
        scratch = [pltpu.VMEM((8, RKV_W), F32), pltpu.VMEM((8, LORA_W), F32)]
    return pl.pallas_call(
        functools.partial(_mix_kernel, sample=sample, tiles_per_seq=max(SEQ // tm, 1)),
        grid=(n_tiles,),
        in_specs=specs,
        out_specs=out_specs,
        out_shape=out_shape,
        scratch_shapes=scratch,
        compiler_params=_params(),
        name="mix_in",
    )(*args)


def _each(f, *lists):
    return [f(*xs) for xs in zip(*lists)]


SCAN_C = 64


def _mm(a, b):
    return jnp.dot(a.astype(BF16), b.astype(BF16), preferred_element_type=F32)


def _mm_nt(a, b):
    return lax.dot_general(a.astype(BF16), b.astype(BF16), (((1,), (1,)), ((), ())),
                           preferred_element_type=F32)


def _mm_tn(a, b):
    return lax.dot_general(a.astype(BF16), b.astype(BF16), (((0,), (0,)), ((), ())),
                           preferred_element_type=F32)


def _cumsum_rows(x):
    n = x.shape[0]
    row = lax.broadcasted_iota(jnp.int32, x.shape, 0)
    s = 1
    while s < n:
        x = x + jnp.where(row >= s, pltpu.roll(x, s, axis=0), 0.0)
        s *= 2
    return x


INV_BASE = 8


def _unit_lower_inverse(ns, row, col):
    f0 = jnp.zeros((), F32)
    same = lambda s: (row // s) == (col // s)
    eye = jnp.where(row == col, 1.0, f0)
    ps = _each(lambda n: jnp.where(same(INV_BASE), n, f0), ns)
    ts = _each(lambda p: eye + p, ps)
    s = 2
    while s < INV_BASE:
        ps = _each(lambda p: _mm(p, p), ps)
        yield
        ts = _each(lambda t, p: t + _mm(t, p), ts, ps)
        yield
        s *= 2
    s = INV_BASE
    while s < SCAN_C:
        level = same(2 * s) & jnp.logical_not(same(s))
        ws = _each(lambda n, t: _mm(jnp.where(level, n, f0), t), ns, ts)
        yield
        ts = _each(lambda t, w: t + _mm(t, w), ts, ws)
        yield
        s *= 2
    return ts


def _chunk_pairs(s0s, rs, lws, ks, vs, kks, bs):
    c = SCAN_C
    f0 = jnp.zeros((), F32)
    row = lax.broadcasted_iota(jnp.int32, (2 * c, PAIR_W), 0)
    col = lax.broadcasted_iota(jnp.int32, (2 * c, PAIR_W), 1)
    top, lft = row < c, col < HEAD
    same_head = top == lft
    strict = (row % c) > (col % HEAD)
    row_c = lax.broadcasted_iota(jnp.int32, (c, PAIR_W), 0)
    col_c = lax.broadcasted_iota(jnp.int32, (c, PAIR_W), 1)
    lft_c = col_c < HEAD
    strict_c = row_c > (col_c % HEAD)
    incl_c = row_c >= (col_c % HEAD)

    def prep(r, lw, k, v, kk, b):
        cum = _cumsum_rows(lw)
        end = cum[c - 1:c, :]
        a_t = -kk * jnp.exp(cum - lw)
        r_t = r * jnp.exp(cum)
        einv = jnp.exp(-cum)
        eend = jnp.exp(end - cum)
        return dict(
            x0=jnp.concatenate([a_t, r_t], axis=0),
            bk=jnp.concatenate([b * einv, k * einv], axis=0),
            bk_e=jnp.concatenate([b * eend, k * eend], axis=0),
            w_end=jnp.exp(end), v=v,
            v_l=jnp.where(lft_c, v, f0), v_r=jnp.where(lft_c, f0, v))

    fs = _each(prep, rs, lws, ks, vs, kks, bs)
    yield
    def grams(f, s0):
        bk = f["bk"]
        g = _mm_nt(f["x0"], jnp.concatenate(
            [jnp.where(lft, bk, f0), jnp.where(lft, f0, bk), s0], axis=0))
        g1 = pltpu.roll(g[:, PAIR_W:2 * PAIR_W], HEAD, axis=1)
        return (g[:, :PAIR_W], jnp.concatenate([g1[c:], g1[:c]], axis=0),
                g[:, 2 * PAIR_W:])

    g0s, g1s, pqs = zip(*_each(grams, fs, s0s))
    yield

    def rhs(f, g0, g1, pq):
        ak = jnp.where(strict_c, jnp.where(lft_c, g1[c:], g0[:c]), f0)
        x = pq[:c] + _mm(ak, jnp.concatenate([f["v_r"], f["v_l"]], axis=0))
        return jnp.concatenate([jnp.where(lft_c, x, f0), jnp.where(lft_c, f0, x)], axis=0)

    ys = _each(rhs, fs, g0s, g1s, pqs)
    yield
    ns = _each(lambda g0, g1: jnp.where(strict & same_head, jnp.where(top, g0, g1), f0),
               g0s, g1s)
    ts = yield from _unit_lower_inverse(ns, row, col)
    ys = _each(_mm, ts, ys)
    yield

    def out(f, g0, g1, pq, y):
        lhs = jnp.concatenate([jnp.where(incl_c, g0[c:], f0), jnp.where(incl_c, g1[:c], f0)],
                              axis=1)
        return pq[c:] + _mm(lhs, jnp.concatenate([y[:c], f["v_l"], f["v_r"], y[c:]], axis=0))

    def state(f, s0, y):
        upd = _mm_tn(jnp.concatenate([y[:c] + y[c:], f["v"]], axis=0), f["bk_e"])
        return s0 * f["w_end"] + jnp.where(same_head, upd, f0)

    outs = _each(out, fs, g0s, g1s, pqs, ys)
    yield
    return outs, _each(state, fs, s0s, ys)


SCAN_BATCHES = 4
XA_EVERY = 3


def _run_with(main, side, *, every):
    n = 0
    while True:
        if n % every == 0:
            next(side, None)
        n += 1
        try:
            next(main)
        except StopIteration as stop:
            for _ in side:
                pass
            return stop.value


def _scan_prompt_kernel(r_ref, w_ref, k_ref, v_ref, kk_ref, b_ref, xq_ref, xk_ref, xv_ref,
                        o_ref, sout_ref, xo_ref, s_ref):
    t_blk = pl.program_id(1)

    @pl.when(t_blk == 0)
    def _():
        s_ref[...] = jnp.zeros_like(s_ref)

    chains = [(j, p) for j in range(SCAN_BATCHES) for p in range(PAIRS)]
    lanes = lambda p: slice(p * PAIR_W, (p + 1) * PAIR_W)
    take = lambda ref: [ref[j, :, lanes(p)] for j, p in chains]
    os_, ss = _run_with(
        _chunk_pairs([s_ref[j, p] for j, p in chains], take(r_ref), take(w_ref), take(k_ref),
                     take(v_ref), take(kk_ref), take(b_ref)),
        _xa_attend(xq_ref, xk_ref, xv_ref, xo_ref), every=XA_EVERY)
    for (j, p), o, s_new in zip(chains, os_, ss):
        o_ref[j, :, lanes(p)] = o
        s_ref[j, p] = s_new

    @pl.when(t_blk == pl.num_programs(1) - 1)
    def _():
        sout_ref[...] = s_ref[...]


def _scan_prompt(r, lw, k, v, kk, b, xq, xk, xv, *, batch, seq):
    n_t = seq // SCAN_C
    nb = SCAN_BATCHES
    steps = (batch // nb) * n_t
    n_s = xq.shape[0]
    assert n_s % steps == 0, (n_s, steps)
    per = n_s // steps
    spec = pl.BlockSpec((nb, SCAN_C, B_WIDTH), lambda bi, ti: (bi, ti, 0))
    sspec = pl.BlockSpec((nb, PAIRS, PAIR_W, PAIR_W), lambda bi, ti: (bi, 0, 0, 0))
    step = lambda bi, ti: (bi * n_t + ti, 0, 0)
    qspec = pl.BlockSpec((per, MEM_ROWS, 128), step)
    mspec = pl.BlockSpec((per, N_MEM * MEM_ROWS, 128), step)
    o, s, xo = pl.pallas_call(
        _scan_prompt_kernel,
        grid=(batch // nb, n_t),
        in_specs=[spec] * 6 + [qspec, mspec, mspec],
        out_specs=[spec, sspec, qspec],
        out_shape=[jax.ShapeDtypeStruct((batch, seq, B_WIDTH), F32),
                   jax.ShapeDtypeStruct((batch, PAIRS, PAIR_W, PAIR_W), F32),
                   jax.ShapeDtypeStruct(xq.shape, F32)],
        scratch_shapes=[pltpu.VMEM((nb, PAIRS, PAIR_W, PAIR_W), F32)],
        compiler_params=_params(2),
        name="scan_prompt",
    )(*[x.reshape(batch, seq, B_WIDTH) for x in (r, lw, k, v, kk, b)], xq, xk, xv)
    return o.reshape(batch * seq, B_WIDTH), s, xo


def _scan_sample_kernel(s_ref, r_ref, w_ref, k_ref, v_ref, kk_ref, b_ref, o_ref, sout_ref,
                        t_ref, ot_ref):
    h = pl.program_id(0)

    @pl.when(h == 0)
    def _():
        for i, ref in enumerate((r_ref, w_ref, k_ref, v_ref, kk_ref, b_ref)):
            t_ref[i] = ref[...].T

    base = pl.multiple_of(h * HEAD, HEAD)
    keys = pl.ds(base, HEAD)
    r, w, k, kk, b = [t_ref[i, keys, :] for i in (0, 1, 2, 4, 5)]

    def body(v8, carry):
        rows = pl.ds(pl.multiple_of(base + v8 * 8, 8), 8)
        v_rows = t_ref[3, rows, :]
        outs = []
        for j in range(8):
            vi = v8 * 8 + j
            s = s_ref[0, vi]
            sa = jnp.sum(s * kk, axis=0, keepdims=True)
            s = s * w - sa * b + v_rows[j:j + 1, :] * k
            sout_ref[0, vi] = s
            outs.append(jnp.sum(s * r, axis=0, keepdims=True))
        ot_ref[rows, :] = jnp.concatenate(outs, axis=0)
        return carry

    lax.fori_loop(0, HEAD // 8, body, 0)

    @pl.when(h == pl.num_programs(0) - 1)
    def _():
        o_ref[...] = ot_ref[...].T


def _scan_sample(state_t, r, w, k, v, kk, b):
    rows = r.shape[0]
    sspec = pl.BlockSpec((1, HEAD, HEAD, rows), lambda h: (h, 0, 0, 0))
    spec = _const_spec((rows, B_WIDTH))
    return pl.pallas_call(
        _scan_sample_kernel,
        grid=(HEADS,),
        in_specs=[sspec] + [spec] * 6,
        out_specs=[pl.BlockSpec((rows, B_WIDTH), lambda h: (0, 0)), sspec],
        out_shape=[jax.ShapeDtypeStruct((rows, B_WIDTH), F32),
                   jax.ShapeDtypeStruct(state_t.shape, F32)],
        scratch_shapes=[pltpu.VMEM((6, B_WIDTH, rows), F32), pltpu.VMEM((B_WIDTH, rows), F32)],
        compiler_params=_params(),
        name="scan_sample",
    )(state_t, r, w, k, v, kk, b)


def _softmax_rows(s):
    e = jnp.exp(s - jnp.max(s, axis=-1, keepdims=True))
    return e * (1.0 / jnp.sum(e, axis=-1, keepdims=True))


POST_ROWS = 1024


def _post_kernel(*refs, attend):
    it = iter(refs)
    (x_ref, ya_ref, o_ref, g_ref, bonus_ref, gng_ref, gnb_ref, ones_ref, wo_ref, lnx_ref,
     wq_ref) = [next(it) for _ in range(11)]
    if attend:
        mk_ref, mv_ref = next(it), next(it)
    x2_ref, out_ref = next(it), next(it)

    ones_bd = ones_ref[...]
    o = o_ref[...]
    mu = _seg_sum(o, ones_bd) * (1.0 / HEAD)
    d = o - mu
    var = _seg_sum(d * d, ones_bd) * (1.0 / HEAD)
    on = d * lax.rsqrt(var + GN_EPS) * gng_ref[...] + gnb_ref[...]
    yb = (on + bonus_ref[...]) * g_ref[...]
    y = jnp.concatenate([ya_ref[...], yb.astype(BF16)], axis=1)
    x2 = x_ref[...] + jnp.dot(y, wo_ref[...], preferred_element_type=F32)
    x2_ref[...] = x2
    q = _dot(_rms(x2, lnx_ref[...]), wq_ref[...])
    if not attend:
        out_ref[...] = q
        return
    qb = q.astype(BF16)
    heads = [slice(h * XA_DIM, (h + 1) * XA_DIM) for h in range(XA_HEADS)]
    ss = [lax.dot_general(qb[:, sl], mk_ref[0, :, sl], (((1,), (1,)), ((), ())),
                          preferred_element_type=F32) * (XA_DIM ** -0.5) for sl in heads]
    ps = [_softmax_rows(s) for s in ss]
    for sl, p in zip(heads, ps):
        out_ref[:, sl] = _dot(p, mv_ref[0, :, sl]).astype(BF16)


def _post_mix(x, ya, o, g, bonus, p, *, tm, mk=None, mv=None):
    rows = x.shape[0]
    attend = mk is not None
    args = [x, ya, o, g, bonus, p["gn_g"], p["gn_b"], p["ones_bd"], p["w_out"], p["ln_xattn"],
            p["xa_q"]]
    specs = [_rows_spec(tm, D_MODEL)] + [_rows_spec(tm, B_WIDTH)] * 4 + [
        _const_spec((1, B_WIDTH)), _const_spec((1, B_WIDTH)), _const_spec((B_WIDTH, B_WIDTH)),
        _const_spec((A_WIDTH + B_WIDTH, D_MODEL)),
        _const_spec((1, D_MODEL)), _const_spec((D_MODEL, D_MODEL))]
    if attend:
        tiles_per_seq = SEQ // tm
        mspec = pl.BlockSpec((1, N_MEM, D_MODEL), lambda i: (i // tiles_per_seq, 0, 0))
        args += [mk, mv]
        specs += [mspec, mspec]
    return pl.pallas_call(
        functools.partial(_post_kernel, attend=attend),
        grid=(rows // tm,),
        in_specs=specs,
        out_specs=[_rows_spec(tm, D_MODEL)] * 2,
        out_shape=[jax.ShapeDtypeStruct((rows, D_MODEL), F32),
                   jax.ShapeDtypeStruct((rows, D_MODEL), BF16 if attend else F32)],
        compiler_params=_params(),
        name="post_mix",
    )(*args)


MEM_ROWS = XA_HEADS * (XA_DIM // 128)


def _lane_allreduce(x, op):
    shift = MEM_ROWS
    while shift < 128:
        x = op(x, pltpu.roll(x, shift, axis=1))
        shift *= 2
    return x


def _xa_attend(q_ref, k_ref, v_ref, o_ref):
    f0 = jnp.zeros((), F32)
    n_blk = N_MEM * MEM_ROWS // 128
    sub = lax.broadcasted_iota(jnp.int32, (MEM_ROWS, 128), 0)
    lane = lax.broadcasted_iota(jnp.int32, (MEM_ROWS, 128), 1)
    diag = sub == (lane % MEM_ROWS)
    li = lax.broadcasted_iota(jnp.int32, (128, 128), 0)
    lj = lax.broadcasted_iota(jnp.int32, (128, 128), 1)
    comb = jnp.where((li // MEM_ROWS == lj // MEM_ROWS) & (li % XA_HEADS == lj % XA_HEADS),
                     1.0, 0.0).astype(BF16)
    samples = list(range(q_ref.shape[0]))
    scs = [_mm_nt(q_ref[j], k_ref[j]) for j in samples]

    def partial(sc):
        return jnp.concatenate(
            [jnp.sum(jnp.where(diag, sc[:, t * 128:(t + 1) * 128], f0), axis=0, keepdims=True)
             for t in range(n_blk)], axis=0)

    def scores(part):
        hi = part.astype(BF16)
        lo = (part - hi.astype(F32)).astype(BF16)
        return (jnp.dot(hi, comb, preferred_element_type=F32)
                + jnp.dot(lo, comb, preferred_element_type=F32)) * (XA_DIM ** -0.5)

    def softmax(s):
        mx = _lane_allreduce(jnp.broadcast_to(jnp.max(s, axis=0, keepdims=True), (MEM_ROWS, 128)),
                             jnp.maximum)
        e = jnp.exp(s - mx[0:1, :])
        den = _lane_allreduce(jnp.broadcast_to(jnp.sum(e, axis=0, keepdims=True), (MEM_ROWS, 128)),
                              jnp.add)
        p = e / den[0:1, :]
        return jnp.concatenate(
            [jnp.where(diag, jnp.broadcast_to(p[t:t + 1, :], (MEM_ROWS, 128)), f0)
             for t in range(n_blk)], axis=1)

    yield
    parts = _each(partial, scs)
    yield
    ss = _each(scores, parts)
    yield
    p_rows = _each(softmax, ss)
    yield
    for j, p in zip(samples, p_rows):
        o_ref[j] = _mm(p, v_ref[j])


def _pad_lora(x, axis=-1):
    x = jnp.moveaxis(x, axis, -1)
    wd = x[..., :DECAY_LORA]
    ad = x[..., DECAY_LORA:DECAY_LORA + AAA_LORA]
    gd = x[..., DECAY_LORA + AAA_LORA:]
    z = lambda n: jnp.zeros(x.shape[:-1] + (n,), x.dtype)
    out = jnp.concatenate([wd, z(LORA_AD - DECAY_LORA), ad, z(LORA_GD - LORA_AD - AAA_LORA),
                           gd, z(LORA_W - LORA_GD - GATE_LORA)], axis=-1)
    return jnp.moveaxis(out, -1, axis)


def _unpad_shift(zm, zl):
    return jnp.concatenate([zm, zl[..., LORA_WD:LORA_WD + DECAY_LORA],
                            zl[..., LORA_AD:LORA_AD + AAA_LORA],
                            zl[..., LORA_GD:LORA_GD + GATE_LORA]], axis=-1)


def _pad_rows(w, n):
    return jnp.pad(w, ((0, n - w.shape[0]), (0, 0)))


def _mem_rows(x):
    b = x.shape[0]
    return x.reshape(b, N_MEM, XA_HEADS, XA_DIM // 128, 128).transpose(0, 1, 3, 2, 4).reshape(
        b, N_MEM * MEM_ROWS, 128)


def _from_mem_rows(x, b):
    return x.reshape(b, N_MEM, XA_DIM // 128, XA_HEADS, 128).transpose(0, 1, 3, 2, 4).reshape(
        b, N_MEM, XA_HEADS, XA_DIM)


def _head_rows(x):
    b = x.shape[0]
    return x.reshape(b, XA_HEADS, XA_DIM // 128, 128).transpose(0, 2, 1, 3).reshape(b, MEM_ROWS, 128)


def _from_head_rows(x):
    b = x.shape[0]
    return x.reshape(b, XA_DIM // 128, XA_HEADS, 128).transpose(0, 2, 1, 3).reshape(b, D_MODEL)


def kernel(x_prompt, x_sample, state_rwkv, state_shift, cache_mem_k, cache_mem_v, mem_prompt, ln_ffn1, ffn1_gate, ffn1_up, ffn1_down, ln_mix, w_in, w_out, sgu_w, sgu_b, sgu_ln_g, sgu_ln_b, rwkv_mu, rwkv_w0, rwkv_w2, rwkv_a0, rwkv_a2, rwkv_g2, rwkv_k_k, rwkv_k_a, rwkv_r_k, rwkv_gn_g, rwkv_gn_b, ln_xattn, mem_norm, xa_q, xa_k, xa_v, xa_o, ln_ffn2, ffn2_gate, ffn2_up, ffn2_down, final_norm):
    assert ln_ffn1.shape[0] == 1, "single layer"
    bp, seq, _ = x_prompt.shape
    bs = x_sample.shape[0]
    row = lambda a: a.reshape(1, -1).astype(F32)
    bf = lambda a: a.astype(BF16)
    l = 0
    head_id = jnp.arange(B_WIDTH) // HEAD
    tril = jnp.tril(jnp.ones((CHUNK, CHUNK), dtype=bool))
    wmask = jnp.where(tril[None], sgu_w[l], 0)
    p = {
        "ln_mix": row(ln_mix[l]),
        "w_lora": bf(_pad_lora(w_in[l].T[MAIN_W:], axis=0)),
        "sgu_wcat": bf(wmask.transpose(1, 0, 2).reshape(CHUNK, A_GROUPS * CHUNK)),
        "sgu_bias": jnp.repeat(sgu_b[l].T, A_GROUP_DIM, axis=1),
        "sgu_w00": row(jnp.repeat(sgu_w[l][:, 0, 0], A_GROUP_DIM)),
        "sgu_b0": row(jnp.repeat(sgu_b[l][:, 0], A_GROUP_DIM)),
        "sgu_ln_g": row(sgu_ln_g[l]), "sgu_ln_b": row(sgu_ln_b[l]),
        "mu_main": row(rwkv_mu[l][:RKV_W]),
        "mu_lora": row(_pad_lora(rwkv_mu[l][RKV_W:])),
        "w0": row(rwkv_w0[l]), "w2": bf(_pad_rows(rwkv_w2[l], LORA_AD - LORA_WD)),
        "a0": row(rwkv_a0[l]), "a2": bf(_pad_rows(rwkv_a2[l], LORA_GD - LORA_AD)),
        "g2": bf(_pad_rows(rwkv_g2[l], LORA_W - LORA_GD)),
        "k_k": row(rwkv_k_k[l]), "k_a": row(rwkv_k_a[l]), "r_k": row(rwkv_r_k[l]),
        "ones_bd": (head_id[:, None] == head_id[None, :]).astype(BF16),
        "gn_g": row(rwkv_gn_g[l]), "gn_b": row(rwkv_gn_b[l]),
        "ln_xattn": row(ln_xattn[l]),
    }
    fnorm = row(final_norm)

    mk, mv, mkb, mvb, (wg1, wu1, wd1) = _memkv(
        mem_prompt.reshape(bp * N_MEM, D_MODEL), row(mem_norm[l]), xa_k[l], xa_v[l], tm=MEMKV_ROWS,
        cast=(ffn1_gate[l], ffn1_up[l], ffn1_down[l]))
    ffn1 = (row(ln_ffn1[l]), wg1, wu1, wd1)

    xp = x_prompt.reshape(bp * seq, D_MODEL)
    xs = x_sample.reshape(bs, D_MODEL)
    x1, x1s, (wg2, wu2, wd2, p["xa_q"], xa_o_b, p["w_out"], p["w_main"]) = _ffn(
        xp, xs, *ffn1, tm=FFN_ROWS,
        cast=(ffn2_gate[l], ffn2_up[l], ffn2_down[l], xa_q[l], xa_o[l], w_out[l],
              (w_in[l].T, MAIN_W)))
    ffn2 = (row(ln_ffn2[l]), wg2, wu2, wd2)

    sh = state_shift[l].reshape(bs, B_PROJ)
    (ya_s, r_s, w_s, k_s, v_s, kk_s, b_s, g_s, bonus_s, va_s, zm_s, zl_s) = _mix_in(
        x1s, p, tm=bs, sample=True, shift_main=sh[:, :RKV_W], shift_lora=_pad_lora(sh[:, RKV_W:]))
    o_s, state_t = _scan_sample(jnp.transpose(state_rwkv[l], (1, 2, 3, 0)),
                                r_s, w_s, k_s, v_s, kk_s, b_s)
    state_s = jnp.transpose(state_t, (3, 0, 1, 2))
    x2s, q_s = _post_mix(x1s, ya_s, o_s, g_s, bonus_s, p, tm=bs)

    ya, r, w, k, v, kk, b, g, bonus, zlast = _mix_in(x1, p, tm=MIX_ROWS, sample=False)
    o, s_bd, attn_rows = _scan_prompt(r, w, k, v, kk, b, _head_rows(q_s),
                                      _mem_rows(cache_mem_k[l]), _mem_rows(cache_mem_v[l]),
                                      batch=bp, seq=seq)
    state_p = jnp.stack([s_bd[:, :, :HEAD, :HEAD], s_bd[:, :, HEAD:, HEAD:]],
                        axis=2).reshape(bp, HEADS, HEAD, HEAD)
    x2, attn = _post_mix(x1, ya, o, g, bonus, p, tm=POST_ROWS,
                         mk=mkb.reshape(bp, N_MEM, D_MODEL), mv=mvb.reshape(bp, N_MEM, D_MODEL))
    y_prompt, y_sample, _ = _ffn(x2, x2s, *ffn2, tm=FFN_ROWS, attn=attn,
                                 attns=_from_head_rows(attn_rows), wo=xa_o_b, final_norm=fnorm)
    tiles_per_seq = seq // MIX_ROWS
    zl_rows = zlast.reshape(bp, tiles_per_seq, 8, RKV_W + LORA_W)[:, -1, 0]
    shift_p = _unpad_shift(zl_rows[:, :RKV_W], zl_rows[:, RKV_W:])

    return (y_prompt.reshape(bp, seq, D_MODEL),
            y_sample.reshape(bs, 1, D_MODEL),
            state_p[None],
            shift_p.reshape(1, bp, 1, B_PROJ),
            _from_mem_rows(mk, bp)[None],
            _from_mem_rows(mv, bp)[None],
            state_s[None],
            _unpad_shift(zm_s, zl_s).reshape(1, bs, 1, B_PROJ),
            va_s.reshape(1, bs, 1, A_WIDTH))
```

```python
import functools

import jax
import jax.numpy as jnp
from jax import lax
from jax.experimental import pallas as pl
from jax.experimental.pallas import tpu as pltpu

F32 = jnp.float32
BF16 = jnp.bfloat16

D_MODEL = 1024
SEQ = 2048
A_WIDTH = 512
A_GROUPS = 8
A_GROUP_DIM = 64
CHUNK = 128
B_WIDTH = 512
HEAD = 64
HEADS = 8
PAIRS = HEADS // 2
PAIR_W = 2 * HEAD
DECAY_LORA = 64
AAA_LORA = 64
GATE_LORA = 160
B_PROJ = 3 * B_WIDTH + DECAY_LORA + AAA_LORA + GATE_LORA
MAIN_W = 2 * A_WIDTH + 3 * B_WIDTH
RKV_W = 3 * B_WIDTH
LORA_W = 512
LORA_WD, LORA_AD, LORA_GD = 0, 128, 256
D_FF = 2816
N_MEM = 256
XA_HEADS = 4
XA_DIM = 256
NORM_EPS = 1e-6
LN_EPS = 1e-5
GN_EPS = 64e-5

V7X_VMEM_BYTES = 64 * 1024 * 1024
VMEM_LIMIT = V7X_VMEM_BYTES * 7 // 8
MIX_ROWS = 512


def _params(n_axes=1):
    return pltpu.CompilerParams(dimension_semantics=("arbitrary",) * n_axes,
                                vmem_limit_bytes=VMEM_LIMIT)


def _const_spec(shape):
    nd = len(shape)
    return pl.BlockSpec(shape, lambda *_: (0,) * nd, pipeline_mode=pl.Buffered(1))


def _rows_spec(tm, width):
    return pl.BlockSpec((tm, width), lambda i: (i, 0))


def _rms(x, g):
    return x * lax.rsqrt(jnp.mean(x * x, axis=-1, keepdims=True) + NORM_EPS) * g


def _dot(a, b):
    return jnp.dot(a.astype(BF16), b, preferred_element_type=F32)


def _seg_sum(x, ones_bd):
    xb = x.astype(BF16)
    return jnp.concatenate(
        [jnp.dot(xb[:, t:t + PAIR_W], ones_bd, preferred_element_type=F32)
         for t in range(0, x.shape[1], PAIR_W)], axis=1)


FFN_ROWS = 1024
FFN_BLOCK = 768


def _ffn_kernel(*refs, pre, final, n_cast, n_main):
    it = iter(refs)
    x_ref, xs_ref = next(it), next(it)
    if pre:
        attn_ref, attns_ref, wo_ref = next(it), next(it), next(it)
    ln_ref, wg_ref, wu_ref, wd_ref = next(it), next(it), next(it), next(it)
    if final:
        fn_ref = next(it)
    cast_in = [next(it) for _ in range(n_cast)]
    o_ref, os_ref = next(it), next(it)
    cast_out = [next(it) for _ in range(n_cast)]

    def ffn(x, attn):
        if pre:
            x = x + _dot(attn, wo_ref[...])
        xb = _rms(x, ln_ref[...]).astype(BF16)
        y = None
        for c0 in range(0, D_FF, FFN_BLOCK):
            cols = slice(c0, min(c0 + FFN_BLOCK, D_FF))
            g = jnp.dot(xb, wg_ref[:, cols], preferred_element_type=F32)
            u = jnp.dot(xb, wu_ref[:, cols], preferred_element_type=F32)
            h = (g * jax.nn.sigmoid(g) * u).astype(BF16)
            part = jnp.dot(h, wd_ref[cols, :], preferred_element_type=F32)
            y = part if y is None else y + part
        x = x + 0.5 * y
        return _rms(x, fn_ref[...]) if final else x

    step = pl.program_id(0)

    @pl.when(step < n_main)
    def _():
        for src_ref, dst_ref in zip(cast_in, cast_out):
            dst_ref[...] = src_ref[...].astype(BF16)
        o_ref[...] = ffn(x_ref[...], attn_ref[...] if pre else None)

    @pl.when(step == n_main)
    def _():
        os_ref[...] = ffn(xs_ref[...], attns_ref[...] if pre else None)


def _ffn(x, xs, ln, wg, wu, wd, *, tm, attn=None, attns=None, wo=None, final_norm=None, cast=()):
    rows, rows_s = x.shape[0], xs.shape[0]
    n_main = rows // tm
    pre = attn is not None
    final = final_norm is not None
    main = lambda i: (jnp.minimum(i, n_main - 1), 0)
    main_spec = pl.BlockSpec((tm, D_MODEL), main)
    small_spec = _const_spec((rows_s, D_MODEL))
    args, specs = [x, xs], [main_spec, small_spec]
    if pre:
        args += [attn, attns, wo]
        specs += [main_spec, small_spec, _const_spec((D_MODEL, D_MODEL))]
    args += [ln, wg, wu, wd]
    specs += [_const_spec((1, D_MODEL)), _const_spec((D_MODEL, D_FF)),
              _const_spec((D_MODEL, D_FF)), _const_spec((D_FF, D_MODEL))]
    if final:
        args.append(final_norm)
        specs.append(_const_spec((1, D_MODEL)))
    slabs, cast_shapes = _cast_slabs(cast, n_main, index_map=main)
    out = pl.pallas_call(
        functools.partial(_ffn_kernel, pre=pre, final=final, n_cast=len(cast), n_main=n_main),
        grid=(n_main + 1,),
        in_specs=specs + slabs,
        out_specs=[main_spec, pl.BlockSpec((rows_s, D_MODEL), lambda i: (0, 0))] + slabs,
        out_shape=[jax.ShapeDtypeStruct((rows, D_MODEL), F32),
                   jax.ShapeDtypeStruct((rows_s, D_MODEL), F32)] + cast_shapes,
        compiler_params=_params(),
        name="ffn",
    )(*args, *_cast_arrays(cast))
    return out[0], out[1], list(out[2:])


MEMKV_ROWS = 256


def _memkv_kernel(m_ref, g_ref, wk_ref, wv_ref, *rest):
    n_cast = (len(rest) - 4) // 2
    k_ref, v_ref, kb_ref, vb_ref = rest[n_cast:n_cast + 4]
    for src_ref, dst_ref in zip(rest[:n_cast], rest[n_cast + 4:]):
        dst_ref[...] = src_ref[...].astype(BF16)
    mb = _rms(m_ref[...], g_ref[...]).astype(BF16)
    tm = m_ref.shape[0]
    k = _dot(mb, wk_ref[...].astype(BF16))
    v = _dot(mb, wv_ref[...].astype(BF16))
    kb_ref[...] = k.astype(BF16)
    vb_ref[...] = v.astype(BF16)
    for c in range(MEM_ROWS):
        src = (c % XA_HEADS) * (XA_DIM // 128) + c // XA_HEADS
        k_ref[pl.ds(c, tm, stride=MEM_ROWS), :] = k[:, src * 128:(src + 1) * 128]
        v_ref[pl.ds(c, tm, stride=MEM_ROWS), :] = v[:, src * 128:(src + 1) * 128]


def _cast_slabs(cast, steps, index_map=lambda i: (i, 0)):
    specs, shapes = [], []
    for a in cast:
        a, n = a if isinstance(a, tuple) else (a, a.shape[0])
        assert n % (16 * steps) == 0, (n, steps)
        specs.append(pl.BlockSpec((n // steps, a.shape[1]), index_map))
        shapes.append(jax.ShapeDtypeStruct((n, a.shape[1]), BF16))
    return specs, shapes


def _cast_arrays(cast):
    return [a[0] if isinstance(a, tuple) else a for a in cast]


def _memkv(mem, g, wk, wv, *, tm, cast=()):
    rows = mem.shape[0]
    steps = rows // tm
    out = jax.ShapeDtypeStruct((rows * MEM_ROWS, 128), F32)
    outb = jax.ShapeDtypeStruct((rows, D_MODEL), BF16)
    slabs, cast_shapes = _cast_slabs(cast, steps)
    res = pl.pallas_call(
        _memkv_kernel,
        grid=(steps,),
        in_specs=[_rows_spec(tm, D_MODEL), _const_spec((1, D_MODEL)),
                  _const_spec((D_MODEL, D_MODEL)), _const_spec((D_MODEL, D_MODEL))] + slabs,
        out_specs=[_rows_spec(tm * MEM_ROWS, 128)] * 2 + [_rows_spec(tm, D_MODEL)] * 2 + slabs,
        out_shape=[out, out, outb, outb] + cast_shapes,
        compiler_params=_params(),
        name="memkv",
    )(mem, g, wk, wv, *_cast_arrays(cast))
    return res[0], res[1], res[2], res[3], list(res[4:])


def _mix_kernel(*refs, sample, tiles_per_seq):
    it = iter(refs)
    x_ref, ln_ref, wmain_ref, wlora_ref = next(it), next(it), next(it), next(it)
    if sample:
        w00_ref, b0_ref, spm_ref, spl_ref = next(it), next(it), next(it), next(it)
    else:
        wcat_ref, bias_ref = next(it), next(it)
    (lng_ref, lnb_ref, mum_ref, mul_ref, w0_ref, w2_ref, a0_ref, a2_ref, g2_ref,
     kk_ref, ka_ref, rk_ref, ones_ref) = [next(it) for _ in range(13)]
    (ya_ref, r_ref, w_ref, k_ref, v_ref, kn_ref, b_ref, g_ref, bonus_ref) = [
        next(it) for _ in range(9)]
    if sample:
        va_ref, zm_ref, zl_ref = next(it), next(it), next(it)
    else:
        zlast_ref, cm_ref, cl_ref = next(it), next(it), next(it)

    tm = x_ref.shape[0]
    if not sample:
        @pl.when(pl.program_id(0) % tiles_per_seq == 0)
        def _():
            cm_ref[...] = jnp.zeros_like(cm_ref)
            cl_ref[...] = jnp.zeros_like(cl_ref)

    xb = _rms(x_ref[...], ln_ref[...]).astype(BF16)
    zl = _mm_nt(xb, wlora_ref[...])
    zmain = _mm_nt(xb, wmain_ref[...])

    u = jax.nn.gelu(zmain[:, :A_WIDTH])
    vx = jax.nn.gelu(zmain[:, A_WIDTH:2 * A_WIDTH])
    mu = jnp.mean(vx, axis=-1, keepdims=True)
    var = jnp.mean(jnp.square(vx - mu), axis=-1, keepdims=True)
    va = (vx - mu) * lax.rsqrt(var + LN_EPS) * lng_ref[...] + lnb_ref[...]
    if sample:
        mixed = va * w00_ref[...] + b0_ref[...]
        ya_ref[...] = (u * mixed).astype(BF16)
        va_ref[...] = va
    else:
        vab = va.astype(BF16)
        first = lax.broadcasted_iota(jnp.int32, (CHUNK, 2 * A_GROUP_DIM), 1) < A_GROUP_DIM
        for c in range(tm // CHUNK):
            rows = slice(c * CHUNK, (c + 1) * CHUNK)
            for gp in range(A_GROUPS // 2):
                lanes = slice(gp * 2 * A_GROUP_DIM, (gp + 1) * 2 * A_GROUP_DIM)
                vc = vab[rows, lanes]
                zero = jnp.zeros_like(vc)
                rhs = jnp.concatenate([jnp.where(first, vc, zero), jnp.where(first, zero, vc)],
                                      axis=0)
                mixed = jnp.dot(wcat_ref[:, gp * 2 * CHUNK:(gp + 1) * 2 * CHUNK], rhs,
                                preferred_element_type=F32) + bias_ref[:, lanes]
                ya_ref[rows, lanes] = (u[rows, lanes] * mixed).astype(BF16)

    zbm = zmain[:, 2 * A_WIDTH:]
    if sample:
        zpm, zpl = spm_ref[...], spl_ref[...]
        zm_ref[...] = zbm
        zl_ref[...] = zl
    else:
        first_m = lax.broadcasted_iota(jnp.int32, zbm.shape, 0) == 0
        first_l = lax.broadcasted_iota(jnp.int32, zl.shape, 0) == 0
        zpm = jnp.where(first_m, cm_ref[0:1, :], pltpu.roll(zbm, 1, axis=0))
        zpl = jnp.where(first_l, cl_ref[0:1, :], pltpu.roll(zl, 1, axis=0))
        cm_ref[0:1, :] = zbm[tm - 1:tm, :]
        cl_ref[0:1, :] = zl[tm - 1:tm, :]
        zlast_ref[:, :RKV_W] = jnp.broadcast_to(zbm[tm - 1:tm, :], (8, RKV_W))
        zlast_ref[:, RKV_W:] = jnp.broadcast_to(zl[tm - 1:tm, :], (8, LORA_W))
    zsm = zbm + (zpm - zbm) * mum_ref[...]
    zsl = zl + (zpl - zl) * mul_ref[...]
    r = zsm[:, :B_WIDTH]
    k = zsm[:, B_WIDTH:2 * B_WIDTH]
    v = zsm[:, 2 * B_WIDTH:]
    wd = zsl[:, LORA_WD:LORA_AD]
    ad = zsl[:, LORA_AD:LORA_GD]
    gd = zsl[:, LORA_GD:]
    y = w0_ref[...] + _dot(jnp.tanh(wd), w2_ref[...])
    w_log = jnp.minimum(y, 0.0) - jnp.log1p(jnp.exp(-jnp.abs(y))) - 0.5
    log_decay = -jnp.exp(w_log)
    a = jax.nn.sigmoid(a0_ref[...] + _dot(ad, a2_ref[...]))
    gate = _dot(jax.nn.sigmoid(gd), g2_ref[...])
    ones_bd = ones_ref[...]
    kk = k * kk_ref[...]
    kk = kk * lax.rsqrt(jnp.maximum(_seg_sum(kk * kk, ones_bd), 1e-24))
    k2 = k * (1.0 + (a - 1.0) * ka_ref[...])
    r_ref[...] = r
    w_ref[...] = jnp.exp(log_decay) if sample else log_decay
    k_ref[...] = k2
    v_ref[...] = v
    kn_ref[...] = kk
    b_ref[...] = kk * a
    g_ref[...] = gate
    bonus_ref[...] = _seg_sum(r * k2 * rk_ref[...], ones_bd) * v


def _mix_in(x, p, *, tm, sample, shift_main=None, shift_lora=None):
    rows = x.shape[0]
    n_tiles = rows // tm
    args = [x, p["ln_mix"], p["w_main"], p["w_lora"]]
    specs = [_rows_spec(tm, D_MODEL), _const_spec((1, D_MODEL)),
             _const_spec((MAIN_W, D_MODEL)), _const_spec((LORA_W, D_MODEL))]
    if sample:
        args += [p["sgu_w00"], p["sgu_b0"], shift_main, shift_lora]
        specs += [_const_spec((1, A_WIDTH)), _const_spec((1, A_WIDTH)),
                  _rows_spec(tm, RKV_W), _rows_spec(tm, LORA_W)]
    else:
        args += [p["sgu_wcat"], p["sgu_bias"]]
        specs += [_const_spec((CHUNK, A_GROUPS * CHUNK)), _const_spec((CHUNK, A_WIDTH))]
    args += [p["sgu_ln_g"], p["sgu_ln_b"], p["mu_main"], p["mu_lora"], p["w0"], p["w2"],
             p["a0"], p["a2"], p["g2"], p["k_k"], p["k_a"], p["r_k"], p["ones_bd"]]
    specs += [_const_spec((1, A_WIDTH)), _const_spec((1, A_WIDTH)), _const_spec((1, RKV_W)),
              _const_spec((1, LORA_W)), _const_spec((1, B_WIDTH)),
              _const_spec((LORA_AD - LORA_WD, B_WIDTH)), _const_spec((1, B_WIDTH)),
              _const_spec((LORA_GD - LORA_AD, B_WIDTH)), _const_spec((LORA_W - LORA_GD, B_WIDTH)),
              _const_spec((1, B_WIDTH)), _const_spec((1, B_WIDTH)), _const_spec((1, B_WIDTH)),
              _const_spec((PAIR_W, PAIR_W))]
    wide = jax.ShapeDtypeStruct((rows, B_WIDTH), F32)
    out_shape = [jax.ShapeDtypeStruct((rows, A_WIDTH), BF16)] + [wide] * 8
    out_specs = [_rows_spec(tm, B_WIDTH)] * 9
    scratch = []
    if sample:
        out_shape += [wide, jax.ShapeDtypeStruct((rows, RKV_W), F32),
                      jax.ShapeDtypeStruct((rows, LORA_W), F32)]
        out_specs += [_rows_spec(tm, A_WIDTH), _rows_spec(tm, RKV_W), _rows_spec(tm, LORA_W)]
    else:
        out_shape += [jax.ShapeDtypeStruct((n_tiles * 8, RKV_W + LORA_W), F32)]
        out_specs += [pl.BlockSpec((8, RKV_W + LORA_W), lambda i: (i, 0))]
        scratch = [pltpu.VMEM((8, RKV_W), F32), pltpu.VMEM((8, LORA_W), F32)]
    return pl.pallas_call(
        functools.partial(_mix_kernel, sample=sample, tiles_per_seq=max(SEQ // tm, 1)),
        grid=(n_tiles,),
        in_specs=specs,
        out_specs=out_specs,
        out_shape=out_shape,
        scratch_shapes=scratch,
        compiler_params=_params(),
        name="mix_in",
    )(*args)


def _each(f, *lists):
    return [f(*xs) for xs in zip(*lists)]


SCAN_C = 64


def _mm(a, b):
    return jnp.dot(a.astype(BF16), b.astype(BF16), preferred_element_type=F32)


def _mm_nt(a, b):
    return lax.dot_general(a.astype(BF16), b.astype(BF16), (((1,), (1,)), ((), ())),
                           preferred_element_type=F32)


def _mm_tn(a, b):
    return lax.dot_general(a.astype(BF16), b.astype(BF16), (((0,), (0,)), ((), ())),
                           preferred_element_type=F32)


def _cumsum_rows(x):
    n = x.shape[0]
    row = lax.broadcasted_iota(jnp.int32, x.shape, 0)
    s = 1
    while s < n:
        x = x + jnp.where(row >= s, pltpu.roll(x, s, axis=0), 0.0)
        s *= 2
    return x


INV_BASE = 8


def _unit_lower_inverse(ns, row, col):
    f0 = jnp.zeros((), F32)
    same = lambda s: (row // s) == (col // s)
    eye = jnp.where(row == col, 1.0, f0)
    ps = _each(lambda n: jnp.where(same(INV_BASE), n, f0), ns)
    ts = _each(lambda p: eye + p, ps)
    s = 2
    while s < INV_BASE:
        ps = _each(lambda p: _mm(p, p), ps)
        yield
        ts = _each(lambda t, p: t + _mm(t, p), ts, ps)
        yield
        s *= 2
    s = INV_BASE
    while s < SCAN_C:
        level = same(2 * s) & jnp.logical_not(same(s))
        ws = _each(lambda n, t: _mm(jnp.where(level, n, f0), t), ns, ts)
        yield
        ts = _each(lambda t, w: t + _mm(t, w), ts, ws)
        yield
        s *= 2
    return ts


def _chunk_pairs(s0s, rs, lws, ks, vs, kks, bs):
    c = SCAN_C
    f0 = jnp.zeros((), F32)
    row = lax.broadcasted_iota(jnp.int32, (2 * c, PAIR_W), 0)
    col = lax.broadcasted_iota(jnp.int32, (2 * c, PAIR_W), 1)
    top, lft = row < c, col < HEAD
    same_head = top == lft
    strict = (row % c) > (col % HEAD)
    row_c = lax.broadcasted_iota(jnp.int32, (c, PAIR_W), 0)
    col_c = lax.broadcasted_iota(jnp.int32, (c, PAIR_W), 1)
    lft_c = col_c < HEAD
    strict_c = row_c > (col_c % HEAD)
    incl_c = row_c >= (col_c % HEAD)

    def prep(r, lw, k, v, kk, b):
        cum = _cumsum_rows(lw)
        end = cum[c - 1:c, :]
        a_t = -kk * jnp.exp(cum - lw)
        r_t = r * jnp.exp(cum)
        einv = jnp.exp(-cum)
        eend = jnp.exp(end - cum)
        return dict(
            x0=jnp.concatenate([a_t, r_t], axis=0),
            bk=jnp.concatenate([b * einv, k * einv], axis=0),
            bk_e=jnp.concatenate([b * eend, k * eend], axis=0),
            w_end=jnp.exp(end), v=v,
            v_l=jnp.where(lft_c, v, f0), v_r=jnp.where(lft_c, f0, v))

    fs = _each(prep, rs, lws, ks, vs, kks, bs)
    yield
    def grams(f, s0):
        bk = f["bk"]
        g = _mm_nt(f["x0"], jnp.concatenate(
            [jnp.where(lft, bk, f0), jnp.where(lft, f0, bk), s0], axis=0))
        g1 = pltpu.roll(g[:, PAIR_W:2 * PAIR_W], HEAD, axis=1)
        return (g[:, :PAIR_W], jnp.concatenate([g1[c:], g1[:c]], axis=0),
                g[:, 2 * PAIR_W:])

    g0s, g1s, pqs = zip(*_each(grams, fs, s0s))
    yield

    def rhs(f, g0, g1, pq):
        ak = jnp.where(strict_c, jnp.where(lft_c, g1[c:], g0[:c]), f0)
        x = pq[:c] + _mm(ak, jnp.concatenate([f["v_r"], f["v_l"]], axis=0))
        return jnp.concatenate([jnp.where(lft_c, x, f0), jnp.where(lft_c, f0, x)], axis=0)

    ys = _each(rhs, fs, g0s, g1s, pqs)
    yield
    ns = _each(lambda g0, g1: jnp.where(strict & same_head, jnp.where(top, g0, g1), f0),
               g0s, g1s)
    ts = yield from _unit_lower_inverse(ns, row, col)
    ys = _each(_mm, ts, ys)
    yield

    def out(f, g0, g1, pq, y):
        lhs = jnp.concatenate([jnp.where(incl_c, g0[c:], f0), jnp.where(incl_c, g1[:c], f0)],
                              axis=1)
        return pq[c:] + _mm(lhs, jnp.concatenate([y[:c], f["v_l"], f["v_r"], y[c:]], axis=0))

    def state(f, s0, y):
        upd = _mm_tn(jnp.concatenate([y[:c] + y[c:], f["v"]], axis=0), f["bk_e"])
        return s0 * f["w_end"] + jnp.where(same_head, upd, f0)

    outs = _each(out, fs, g0s, g1s, pqs, ys)
    yield
    return outs, _each(state, fs, s0s, ys)


SCAN_BATCHES = 4
XA_EVERY = 3


def _run_with(main, side, *, every):
    n = 0
    while True:
        if n % every == 0:
            next(side, None)
        n += 1
        try:
            next(main)
        except StopIteration as stop:
            for _ in side:
                pass
            return stop.value


def _scan_prompt_kernel(r_ref, w_ref, k_ref, v_ref, kk_ref, b_ref, xq_ref, xk_ref, xv_ref,
                        o_ref, sout_ref, xo_ref, s_ref):
    t_blk = pl.program_id(1)

    @pl.when(t_blk == 0)
    def _():
        s_ref[...] = jnp.zeros_like(s_ref)

    chains = [(j, p) for j in range(SCAN_BATCHES) for p in range(PAIRS)]
    lanes = lambda p: slice(p * PAIR_W, (p + 1) * PAIR_W)
    take = lambda ref: [ref[j, :, lanes(p)] for j, p in chains]
    os_, ss = _run_with(
        _chunk_pairs([s_ref[j, p] for j, p in chains], take(r_ref), take(w_ref), take(k_ref),
                     take(v_ref), take(kk_ref), take(b_ref)),
        _xa_attend(xq_ref, xk_ref, xv_ref, xo_ref), every=XA_EVERY)
    for (j, p), o, s_new in zip(chains, os_, ss):
        o_ref[j, :, lanes(p)] = o
        s_ref[j, p] = s_new

    @pl.when(t_blk == pl.num_programs(1) - 1)
    def _():
        sout_ref[...] = s_ref[...]


def _scan_prompt(r, lw, k, v, kk, b, xq, xk, xv, *, batch, seq):
    n_t = seq // SCAN_C
    nb = SCAN_BATCHES
    steps = (batch // nb) * n_t
    n_s = xq.shape[0]
    assert n_s % steps == 0, (n_s, steps)
    per = n_s // steps
    spec = pl.BlockSpec((nb, SCAN_C, B_WIDTH), lambda bi, ti: (bi, ti, 0))
    sspec = pl.BlockSpec((nb, PAIRS, PAIR_W, PAIR_W), lambda bi, ti: (bi, 0, 0, 0))
    step = lambda bi, ti: (bi * n_t + ti, 0, 0)
    qspec = pl.BlockSpec((per, MEM_ROWS, 128), step)
    mspec = pl.BlockSpec((per, N_MEM * MEM_ROWS, 128), step)
    o, s, xo = pl.pallas_call(
        _scan_prompt_kernel,
        grid=(batch // nb, n_t),
        in_specs=[spec] * 6 + [qspec, mspec, mspec],
        out_specs=[spec, sspec, qspec],
        out_shape=[jax.ShapeDtypeStruct((batch, seq, B_WIDTH), F32),
                   jax.ShapeDtypeStruct((batch, PAIRS, PAIR_W, PAIR_W), F32),
                   jax.ShapeDtypeStruct(xq.shape, F32)],
        scratch_shapes=[pltpu.VMEM((nb, PAIRS, PAIR_W, PAIR_W), F32)],
        compiler_params=_params(2),
        name="scan_prompt",
    )(*[x.reshape(batch, seq, B_WIDTH) for x in (r, lw, k, v, kk, b)], xq, xk, xv)
    return o.reshape(batch * seq, B_WIDTH), s, xo


def _scan_sample_kernel(s_ref, r_ref, w_ref, k_ref, v_ref, kk_ref, b_ref, o_ref, sout_ref,
                        t_ref, ot_ref):
    h = pl.program_id(0)

    @pl.when(h == 0)
    def _():
        for i, ref in enumerate((r_ref, w_ref, k_ref, v_ref, kk_ref, b_ref)):
            t_ref[i] = ref[...].T

    base = pl.multiple_of(h * HEAD, HEAD)
    keys = pl.ds(base, HEAD)
    r, w, k, kk, b = [t_ref[i, keys, :] for i in (0, 1, 2, 4, 5)]

    def body(v8, carry):
        rows = pl.ds(pl.multiple_of(base + v8 * 8, 8), 8)
        v_rows = t_ref[3, rows, :]
        outs = []
        for j in range(8):
            vi = v8 * 8 + j
            s = s_ref[0, vi]
            sa = jnp.sum(s * kk, axis=0, keepdims=True)
            s = s * w - sa * b + v_rows[j:j + 1, :] * k
            sout_ref[0, vi] = s
            outs.append(jnp.sum(s * r, axis=0, keepdims=True))
        ot_ref[rows, :] = jnp.concatenate(outs, axis=0)
        return carry

    lax.fori_loop(0, HEAD // 8, body, 0)

    @pl.when(h == pl.num_programs(0) - 1)
    def _():
        o_ref[...] = ot_ref[...].T


def _scan_sample(state_t, r, w, k, v, kk, b):
    rows = r.shape[0]
    sspec = pl.BlockSpec((1, HEAD, HEAD, rows), lambda h: (h, 0, 0, 0))
    spec = _const_spec((rows, B_WIDTH))
    return pl.pallas_call(
        _scan_sample_kernel,
        grid=(HEADS,),
        in_specs=[sspec] + [spec] * 6,
        out_specs=[pl.BlockSpec((rows, B_WIDTH), lambda h: (0, 0)), sspec],
        out_shape=[jax.ShapeDtypeStruct((rows, B_WIDTH), F32),
                   jax.ShapeDtypeStruct(state_t.shape, F32)],
        scratch_shapes=[pltpu.VMEM((6, B_WIDTH, rows), F32), pltpu.VMEM((B_WIDTH, rows), F32)],
        compiler_params=_params(),
        name="scan_sample",
    )(state_t, r, w, k, v, kk, b)


def _softmax_rows(s):
    e = jnp.exp(s - jnp.max(s, axis=-1, keepdims=True))
    return e * (1.0 / jnp.sum(e, axis=-1, keepdims=True))


POST_ROWS = 1024


def _post_kernel(*refs, attend):
    it = iter(refs)
    (x_ref, ya_ref, o_ref, g_ref, bonus_ref, gng_ref, gnb_ref, ones_ref, wo_ref, lnx_ref,
     wq_ref) = [next(it) for _ in range(11)]
    if attend:
        mk_ref, mv_ref = next(it), next(it)
    x2_ref, out_ref = next(it), next(it)

    ones_bd = ones_ref[...]
    o = o_ref[...]
    mu = _seg_sum(o, ones_bd) * (1.0 / HEAD)
    d = o - mu
    var = _seg_sum(d * d, ones_bd) * (1.0 / HEAD)
    on = d * lax.rsqrt(var + GN_EPS) * gng_ref[...] + gnb_ref[...]
    yb = (on + bonus_ref[...]) * g_ref[...]
    y = jnp.concatenate([ya_ref[...], yb.astype(BF16)], axis=1)
    x2 = x_ref[...] + jnp.dot(y, wo_ref[...], preferred_element_type=F32)
    x2_ref[...] = x2
    q = _dot(_rms(x2, lnx_ref[...]), wq_ref[...])
    if not attend:
        out_ref[...] = q
        return
    qb = q.astype(BF16)
    heads = [slice(h * XA_DIM, (h + 1) * XA_DIM) for h in range(XA_HEADS)]
    ss = [lax.dot_general(qb[:, sl], mk_ref[0, :, sl], (((1,), (1,)), ((), ())),
                          preferred_element_type=F32) * (XA_DIM ** -0.5) for sl in heads]
    ps = [_softmax_rows(s) for s in ss]
    for sl, p in zip(heads, ps):
        out_ref[:, sl] = _dot(p, mv_ref[0, :, sl]).astype(BF16)


def _post_mix(x, ya, o, g, bonus, p, *, tm, mk=None, mv=None):
    rows = x.shape[0]
    attend = mk is not None
    args = [x, ya, o, g, bonus, p["gn_g"], p["gn_b"], p["ones_bd"], p["w_out"], p["ln_xattn"],
            p["xa_q"]]
    specs = [_rows_spec(tm, D_MODEL)] + [_rows_spec(tm, B_WIDTH)] * 4 + [
        _const_spec((1, B_WIDTH)), _const_spec((1, B_WIDTH)), _const_spec((PAIR_W, PAIR_W)),
        _const_spec((A_WIDTH + B_WIDTH, D_MODEL)),
        _const_spec((1, D_MODEL)), _const_spec((D_MODEL, D_MODEL))]
    if attend:
        tiles_per_seq = SEQ // tm
        mspec = pl.BlockSpec((1, N_MEM, D_MODEL), lambda i: (i // tiles_per_seq, 0, 0))
        args += [mk, mv]
        specs += [mspec, mspec]
    return pl.pallas_call(
        functools.partial(_post_kernel, attend=attend),
        grid=(rows // tm,),
        in_specs=specs,
        out_specs=[_rows_spec(tm, D_MODEL)] * 2,
        out_shape=[jax.ShapeDtypeStruct((rows, D_MODEL), F32),
                   jax.ShapeDtypeStruct((rows, D_MODEL), BF16 if attend else F32)],
        compiler_params=_params(),
        name="post_mix",
    )(*args)


MEM_ROWS = XA_HEADS * (XA_DIM // 128)


def _lane_allreduce(x, op):
    shift = MEM_ROWS
    while shift < 128:
        x = op(x, pltpu.roll(x, shift, axis=1))
        shift *= 2
    return x


def _xa_attend(q_ref, k_ref, v_ref, o_ref):
    f0 = jnp.zeros((), F32)
    n_blk = N_MEM * MEM_ROWS // 128
    sub = lax.broadcasted_iota(jnp.int32, (MEM_ROWS, 128), 0)
    lane = lax.broadcasted_iota(jnp.int32, (MEM_ROWS, 128), 1)
    diag = sub == (lane % MEM_ROWS)
    li = lax.broadcasted_iota(jnp.int32, (128, 128), 0)
    lj = lax.broadcasted_iota(jnp.int32, (128, 128), 1)
    comb = jnp.where((li // MEM_ROWS == lj // MEM_ROWS) & (li % XA_HEADS == lj % XA_HEADS),
                     1.0, 0.0).astype(BF16)
    samples = list(range(q_ref.shape[0]))
    scs = [_mm_nt(q_ref[j], k_ref[j]) for j in samples]

    def partial(sc):
        return jnp.concatenate(
            [jnp.sum(jnp.where(diag, sc[:, t * 128:(t + 1) * 128], f0), axis=0, keepdims=True)
             for t in range(n_blk)], axis=0)

    def scores(part):
        hi = part.astype(BF16)
        lo = (part - hi.astype(F32)).astype(BF16)
        return (jnp.dot(hi, comb, preferred_element_type=F32)
                + jnp.dot(lo, comb, preferred_element_type=F32)) * (XA_DIM ** -0.5)

    def softmax(s):
        mx = _lane_allreduce(jnp.broadcast_to(jnp.max(s, axis=0, keepdims=True), (MEM_ROWS, 128)),
                             jnp.maximum)
        e = jnp.exp(s - mx[0:1, :])
        den = _lane_allreduce(jnp.broadcast_to(jnp.sum(e, axis=0, keepdims=True), (MEM_ROWS, 128)),
                              jnp.add)
        p = e / den[0:1, :]
        return jnp.concatenate(
            [jnp.where(diag, jnp.broadcast_to(p[t:t + 1, :], (MEM_ROWS, 128)), f0)
             for t in range(n_blk)], axis=1)

    yield
    parts = _each(partial, scs)
    yield
    ss = _each(scores, parts)
    yield
    p_rows = _each(softmax, ss)
    yield
    for j, p in zip(samples, p_rows):
        o_ref[j] = _mm(p, v_ref[j])


def _pad_lora(x, axis=-1):
    x = jnp.moveaxis(x, axis, -1)
    wd = x[..., :DECAY_LORA]
    ad = x[..., DECAY_LORA:DECAY_LORA + AAA_LORA]
    gd = x[..., DECAY_LORA + AAA_LORA:]
    z = lambda n: jnp.zeros(x.shape[:-1] + (n,), x.dtype)
    out = jnp.concatenate([wd, z(LORA_AD - DECAY_LORA), ad, z(LORA_GD - LORA_AD - AAA_LORA),
                           gd, z(LORA_W - LORA_GD - GATE_LORA)], axis=-1)
    return jnp.moveaxis(out, -1, axis)


def _unpad_shift(zm, zl):
    return jnp.concatenate([zm, zl[..., LORA_WD:LORA_WD + DECAY_LORA],
                            zl[..., LORA_AD:LORA_AD + AAA_LORA],
                            zl[..., LORA_GD:LORA_GD + GATE_LORA]], axis=-1)


def _pad_rows(w, n):
    return jnp.pad(w, ((0, n - w.shape[0]), (0, 0)))


def _mem_rows(x):
    b = x.shape[0]
    return x.reshape(b, N_MEM, XA_HEADS, XA_DIM // 128, 128).transpose(0, 1, 3, 2, 4).reshape(
        b, N_MEM * MEM_ROWS, 128)


def _from_mem_rows(x, b):
    return x.reshape(b, N_MEM, XA_DIM // 128, XA_HEADS, 128).transpose(0, 1, 3, 2, 4).reshape(
        b, N_MEM, XA_HEADS, XA_DIM)


def _head_rows(x):
    b = x.shape[0]
    return x.reshape(b, XA_HEADS, XA_DIM // 128, 128).transpose(0, 2, 1, 3).reshape(b, MEM_ROWS, 128)


def _from_head_rows(x):
    b = x.shape[0]
    return x.reshape(b, XA_DIM // 128, XA_HEADS, 128).transpose(0, 2, 1, 3).reshape(b, D_MODEL)


def kernel(x_prompt, x_sample, state_rwkv, state_shift, cache_mem_k, cache_mem_v, mem_prompt, ln_ffn1, ffn1_gate, ffn1_up, ffn1_down, ln_mix, w_in, w_out, sgu_w, sgu_b, sgu_ln_g, sgu_ln_b, rwkv_mu, rwkv_w0, rwkv_w2, rwkv_a0, rwkv_a2, rwkv_g2, rwkv_k_k, rwkv_k_a, rwkv_r_k, rwkv_gn_g, rwkv_gn_b, ln_xattn, mem_norm, xa_q, xa_k, xa_v, xa_o, ln_ffn2, ffn2_gate, ffn2_up, ffn2_down, final_norm):
    assert ln_ffn1.shape[0] == 1, "single layer"
    bp, seq, _ = x_prompt.shape
    bs = x_sample.shape[0]
    row = lambda a: a.reshape(1, -1).astype(F32)
    bf = lambda a: a.astype(BF16)
    l = 0
    head_id = jnp.arange(PAIR_W) // HEAD
    tril = jnp.tril(jnp.ones((CHUNK, CHUNK), dtype=bool))
    wmask = jnp.where(tril[None], sgu_w[l], 0)
    p = {
        "ln_mix": row(ln_mix[l]),
        "w_lora": bf(_pad_lora(w_in[l].T[MAIN_W:], axis=0)),
        "sgu_wcat": bf(wmask.transpose(1, 0, 2).reshape(CHUNK, A_GROUPS * CHUNK)),
        "sgu_bias": jnp.repeat(sgu_b[l].T, A_GROUP_DIM, axis=1),
        "sgu_w00": row(jnp.repeat(sgu_w[l][:, 0, 0], A_GROUP_DIM)),
        "sgu_b0": row(jnp.repeat(sgu_b[l][:, 0], A_GROUP_DIM)),
        "sgu_ln_g": row(sgu_ln_g[l]), "sgu_ln_b": row(sgu_ln_b[l]),
        "mu_main": row(rwkv_mu[l][:RKV_W]),
        "mu_lora": row(_pad_lora(rwkv_mu[l][RKV_W:])),
        "w0": row(rwkv_w0[l]), "w2": bf(_pad_rows(rwkv_w2[l], LORA_AD - LORA_WD)),
        "a0": row(rwkv_a0[l]), "a2": bf(_pad_rows(rwkv_a2[l], LORA_GD - LORA_AD)),
        "g2": bf(_pad_rows(rwkv_g2[l], LORA_W - LORA_GD)),
        "k_k": row(rwkv_k_k[l]), "k_a": row(rwkv_k_a[l]), "r_k": row(rwkv_r_k[l]),
        "ones_bd": (head_id[:, None] == head_id[None, :]).astype(BF16),
        "gn_g": row(rwkv_gn_g[l]), "gn_b": row(rwkv_gn_b[l]),
        "ln_xattn": row(ln_xattn[l]),
    }
    fnorm = row(final_norm)

    mk, mv, mkb, mvb, (wg1, wu1, wd1) = _memkv(
        mem_prompt.reshape(bp * N_MEM, D_MODEL), row(mem_norm[l]), xa_k[l], xa_v[l], tm=MEMKV_ROWS,
        cast=(ffn1_gate[l], ffn1_up[l], ffn1_down[l]))
    ffn1 = (row(ln_ffn1[l]), wg1, wu1, wd1)

    xp = x_prompt.reshape(bp * seq, D_MODEL)
    xs = x_sample.reshape(bs, D_MODEL)
    x1, x1s, (wg2, wu2, wd2, p["xa_q"], xa_o_b, p["w_out"], p["w_main"]) = _ffn(
        xp, xs, *ffn1, tm=FFN_ROWS,
        cast=(ffn2_gate[l], ffn2_up[l], ffn2_down[l], xa_q[l], xa_o[l], w_out[l],
              (w_in[l].T, MAIN_W)))
    ffn2 = (row(ln_ffn2[l]), wg2, wu2, wd2)

    sh = state_shift[l].reshape(bs, B_PROJ)
    (ya_s, r_s, w_s, k_s, v_s, kk_s, b_s, g_s, bonus_s, va_s, zm_s, zl_s) = _mix_in(
        x1s, p, tm=bs, sample=True, shift_main=sh[:, :RKV_W], shift_lora=_pad_lora(sh[:, RKV_W:]))
    o_s, state_t = _scan_sample(jnp.transpose(state_rwkv[l], (1, 2, 3, 0)),
                                r_s, w_s, k_s, v_s, kk_s, b_s)
    state_s = jnp.transpose(state_t, (3, 0, 1, 2))
    x2s, q_s = _post_mix(x1s, ya_s, o_s, g_s, bonus_s, p, tm=bs)

    ya, r, w, k, v, kk, b, g, bonus, zlast = _mix_in(x1, p, tm=MIX_ROWS, sample=False)
    o, s_bd, attn_rows = _scan_prompt(r, w, k, v, kk, b, _head_rows(q_s),
                                      _mem_rows(cache_mem_k[l]), _mem_rows(cache_mem_v[l]),
                                      batch=bp, seq=seq)
    state_p = jnp.stack([s_bd[:, :, :HEAD, :HEAD], s_bd[:, :, HEAD:, HEAD:]],
                        axis=2).reshape(bp, HEADS, HEAD, HEAD)
    x2, attn = _post_mix(x1, ya, o, g, bonus, p, tm=POST_ROWS,
                         mk=mkb.reshape(bp, N_MEM, D_MODEL), mv=mvb.reshape(bp, N_MEM, D_MODEL))
    y_prompt, y_sample, _ = _ffn(x2, x2s, *ffn2, tm=FFN_ROWS, attn=attn,
                                 attns=_from_head_rows(attn_rows), wo=xa_o_b, final_norm=fnorm)
    tiles_per_seq = seq // MIX_ROWS
    zl_rows = zlast.reshape(bp, tiles_per_seq, 8, RKV_W + LORA_W)[:, -1, 0]
    shift_p = _unpad_shift(zl_rows[:, :RKV_W], zl_rows[:, RKV_W:])

    return (y_prompt.reshape(bp, seq, D_MODEL),
            y_sample.reshape(bs, 1, D_MODEL),
            state_p[None],
            shift_p.reshape(1, bp, 1, B_PROJ),
            _from_mem_rows(mk, bp)[None],
            _from_mem_rows(mv, bp)[None],
            state_s[None],
            _unpad_shift(zm_s, zl_s).reshape(1, bs, 1, B_PROJ),
            va_s.reshape(1, bs, 1, A_WIDTH))
```

```python
import functools

import jax
import jax.numpy as jnp
from jax import lax
from jax.experimental import pallas as pl
from jax.experimental.pallas import tpu as pltpu

F32 = jnp.float32
BF16 = jnp.bfloat16

D_MODEL = 1024
SEQ = 2048
A_WIDTH = 512
A_GROUPS = 8
A_GROUP_DIM = 64
CHUNK = 128
B_WIDTH = 512
HEAD = 64
HEADS = 8
PAIRS = HEADS // 2
PAIR_W = 2 * HEAD
DECAY_LORA = 64
AAA_LORA = 64
GATE_LORA = 160
B_PROJ = 3 * B_WIDTH + DECAY_LORA + AAA_LORA + GATE_LORA
MAIN_W = 2 * A_WIDTH + 3 * B_WIDTH
RKV_W = 3 * B_WIDTH
LORA_W = 512
LORA_WD, LORA_AD, LORA_GD = 0, 128, 256
D_FF = 2816
N_MEM = 256
XA_HEADS = 4
XA_DIM = 256
NORM_EPS = 1e-6
LN_EPS = 1e-5
GN_EPS = 64e-5

V7X_VMEM_BYTES = 64 * 1024 * 1024
VMEM_LIMIT = V7X_VMEM_BYTES * 7 // 8
MIX_ROWS = 512


def _params(n_axes=1):
    return pltpu.CompilerParams(dimension_semantics=("arbitrary",) * n_axes,
                                vmem_limit_bytes=VMEM_LIMIT)


def _const_spec(shape):
    nd = len(shape)
    return pl.BlockSpec(shape, lambda *_: (0,) * nd, pipeline_mode=pl.Buffered(1))


def _rows_spec(tm, width):
    return pl.BlockSpec((tm, width), lambda i: (i, 0))


def _rms(x, g):
    return x * lax.rsqrt(jnp.mean(x * x, axis=-1, keepdims=True) + NORM_EPS) * g


def _dot(a, b):
    return jnp.dot(a.astype(BF16), b, preferred_element_type=F32)


def _seg_sum(x, ones_bd):
    xb = x.astype(BF16)
    return jnp.concatenate(
        [jnp.dot(xb[:, t:t + PAIR_W], ones_bd, preferred_element_type=F32)
         for t in range(0, x.shape[1], PAIR_W)], axis=1)


FFN_ROWS = 1024
FFN_BLOCK = 768


def _ffn_kernel(*refs, pre, final, n_cast, n_main):
    it = iter(refs)
    x_ref, xs_ref = next(it), next(it)
    if pre:
        attn_ref, attns_ref, wo_ref = next(it), next(it), next(it)
    ln_ref, wg_ref, wu_ref, wd_ref = next(it), next(it), next(it), next(it)
    if final:
        fn_ref = next(it)
    cast_in = [next(it) for _ in range(n_cast)]
    o_ref, os_ref = next(it), next(it)
    cast_out = [next(it) for _ in range(n_cast)]

    def ffn(x, attn):
        if pre:
            x = x + _dot(attn, wo_ref[...])
        xb = _rms(x, ln_ref[...]).astype(BF16)
        y = None
        for c0 in range(0, D_FF, FFN_BLOCK):
            cols = slice(c0, min(c0 + FFN_BLOCK, D_FF))
            g = jnp.dot(xb, wg_ref[:, cols], preferred_element_type=F32)
            u = jnp.dot(xb, wu_ref[:, cols], preferred_element_type=F32)
            h = (g * jax.nn.sigmoid(g) * u).astype(BF16)
            part = jnp.dot(h, wd_ref[cols, :], preferred_element_type=F32)
            y = part if y is None else y + part
        x = x + 0.5 * y
        return _rms(x, fn_ref[...]) if final else x

    step = pl.program_id(0)

    @pl.when(step < n_main)
    def _():
        for src_ref, dst_ref in zip(cast_in, cast_out):
            dst_ref[...] = src_ref[...].astype(BF16)
        o_ref[...] = ffn(x_ref[...], attn_ref[...] if pre else None)

    @pl.when(step == n_main)
    def _():
        os_ref[...] = ffn(xs_ref[...], attns_ref[...] if pre else None)


def _ffn(x, xs, ln, wg, wu, wd, *, tm, attn=None, attns=None, wo=None, final_norm=None, cast=()):
    rows, rows_s = x.shape[0], xs.shape[0]
    n_main = rows // tm
    pre = attn is not None
    final = final_norm is not None
    main = lambda i: (jnp.minimum(i, n_main - 1), 0)
    main_spec = pl.BlockSpec((tm, D_MODEL), main)
    small_spec = _const_spec((rows_s, D_MODEL))
    args, specs = [x, xs], [main_spec, small_spec]
    if pre:
        args += [attn, attns, wo]
        specs += [main_spec, small_spec, _const_spec((D_MODEL, D_MODEL))]
    args += [ln, wg, wu, wd]
    specs += [_const_spec((1, D_MODEL)), _const_spec((D_MODEL, D_FF)),
              _const_spec((D_MODEL, D_FF)), _const_spec((D_FF, D_MODEL))]
    if final:
        args.append(final_norm)
        specs.append(_const_spec((1, D_MODEL)))
    slabs, cast_shapes = _cast_slabs(cast, n_main, index_map=main)
    out = pl.pallas_call(
        functools.partial(_ffn_kernel, pre=pre, final=final, n_cast=len(cast), n_main=n_main),
        grid=(n_main + 1,),
        in_specs=specs + slabs,
        out_specs=[main_spec, pl.BlockSpec((rows_s, D_MODEL), lambda i: (0, 0))] + slabs,
        out_shape=[jax.ShapeDtypeStruct((rows, D_MODEL), F32),
                   jax.ShapeDtypeStruct((rows_s, D_MODEL), F32)] + cast_shapes,
        compiler_params=_params(),
        name="ffn",
    )(*args, *_cast_arrays(cast))
    return out[0], out[1], list(out[2:])


MEMKV_ROWS = 512


def _memkv_kernel(m_ref, g_ref, wk_ref, wv_ref, *rest):
    n_cast = (len(rest) - 4) // 2
    k_ref, v_ref, kb_ref, vb_ref = rest[n_cast:n_cast + 4]
    for src_ref, dst_ref in zip(rest[:n_cast], rest[n_cast + 4:]):
        dst_ref[...] = src_ref[...].astype(BF16)
    mb = _rms(m_ref[...], g_ref[...]).astype(BF16)
    tm = m_ref.shape[0]
    k = _dot(mb, wk_ref[...].astype(BF16))
    v = _dot(mb, wv_ref[...].astype(BF16))
    kb_ref[...] = k.astype(BF16)
    vb_ref[...] = v.astype(BF16)
    for c in range(MEM_ROWS):
        src = (c % XA_HEADS) * (XA_DIM // 128) + c // XA_HEADS
        k_ref[pl.ds(c, tm, stride=MEM_ROWS), :] = k[:, src * 128:(src + 1) * 128]
        v_ref[pl.ds(c, tm, stride=MEM_ROWS), :] = v[:, src * 128:(src + 1) * 128]


def _cast_slabs(cast, steps, index_map=lambda i: (i, 0)):
    specs, shapes = [], []
    for a in cast:
        a, n = a if isinstance(a, tuple) else (a, a.shape[0])
        assert n % (16 * steps) == 0, (n, steps)
        specs.append(pl.BlockSpec((n // steps, a.shape[1]), index_map))
        shapes.append(jax.ShapeDtypeStruct((n, a.shape[1]), BF16))
    return specs, shapes


def _cast_arrays(cast):
    return [a[0] if isinstance(a, tuple) else a for a in cast]


def _memkv(mem, g, wk, wv, *, tm, cast=()):
    rows = mem.shape[0]
    steps = rows // tm
    out = jax.ShapeDtypeStruct((rows * MEM_ROWS, 128), F32)
    outb = jax.ShapeDtypeStruct((rows, D_MODEL), BF16)
    slabs, cast_shapes = _cast_slabs(cast, steps)
    res = pl.pallas_call(
        _memkv_kernel,
        grid=(steps,),
        in_specs=[_rows_spec(tm, D_MODEL), _const_spec((1, D_MODEL)),
                  _const_spec((D_MODEL, D_MODEL)), _const_spec((D_MODEL, D_MODEL))] + slabs,
        out_specs=[_rows_spec(tm * MEM_ROWS, 128)] * 2 + [_rows_spec(tm, D_MODEL)] * 2 + slabs,
        out_shape=[out, out, outb, outb] + cast_shapes,
        compiler_params=_params(),
        name="memkv",
    )(mem, g, wk, wv, *_cast_arrays(cast))
    return res[0], res[1], res[2], res[3], list(res[4:])


def _mix_kernel(*refs, sample, tiles_per_seq):
    it = iter(refs)
    x_ref, ln_ref, wmain_ref, wlora_ref = next(it), next(it), next(it), next(it)
    if sample:
        w00_ref, b0_ref, spm_ref, spl_ref = next(it), next(it), next(it), next(it)
    else:
        wcat_ref, bias_ref = next(it), next(it)
    (lng_ref, lnb_ref, mum_ref, mul_ref, w0_ref, w2_ref, a0_ref, a2_ref, g2_ref,
     kk_ref, ka_ref, rk_ref, ones_ref) = [next(it) for _ in range(13)]
    (ya_ref, r_ref, w_ref, k_ref, v_ref, kn_ref, b_ref, g_ref, bonus_ref) = [
        next(it) for _ in range(9)]
    if sample:
        va_ref, zm_ref, zl_ref = next(it), next(it), next(it)
    else:
        zlast_ref, cm_ref, cl_ref = next(it), next(it), next(it)

    tm = x_ref.shape[0]
    if not sample:
        @pl.when(pl.program_id(0) % tiles_per_seq == 0)
        def _():
            cm_ref[...] = jnp.zeros_like(cm_ref)
            cl_ref[...] = jnp.zeros_like(cl_ref)

    xb = _rms(x_ref[...], ln_ref[...]).astype(BF16)
    zl = _mm_nt(xb, wlora_ref[...])
    zmain = _mm_nt(xb, wmain_ref[...])

    u = jax.nn.gelu(zmain[:, :A_WIDTH])
    vx = jax.nn.gelu(zmain[:, A_WIDTH:2 * A_WIDTH])
    mu = jnp.mean(vx, axis=-1, keepdims=True)
    var = jnp.mean(jnp.square(vx - mu), axis=-1, keepdims=True)
    va = (vx - mu) * lax.rsqrt(var + LN_EPS) * lng_ref[...] + lnb_ref[...]
    if sample:
        mixed = va * w00_ref[...] + b0_ref[...]
        ya_ref[...] = (u * mixed).astype(BF16)
        va_ref[...] = va
    else:
        vab = va.astype(BF16)
        first = lax.broadcasted_iota(jnp.int32, (CHUNK, 2 * A_GROUP_DIM), 1) < A_GROUP_DIM
        for c in range(tm // CHUNK):
            rows = slice(c * CHUNK, (c + 1) * CHUNK)
            for gp in range(A_GROUPS // 2):
                lanes = slice(gp * 2 * A_GROUP_DIM, (gp + 1) * 2 * A_GROUP_DIM)
                vc = vab[rows, lanes]
                zero = jnp.zeros_like(vc)
                rhs = jnp.concatenate([jnp.where(first, vc, zero), jnp.where(first, zero, vc)],
                                      axis=0)
                mixed = jnp.dot(wcat_ref[:, gp * 2 * CHUNK:(gp + 1) * 2 * CHUNK], rhs,
                                preferred_element_type=F32) + bias_ref[:, lanes]
                ya_ref[rows, lanes] = (u[rows, lanes] * mixed).astype(BF16)

    zbm = zmain[:, 2 * A_WIDTH:]
    if sample:
        zpm, zpl = spm_ref[...], spl_ref[...]
        zm_ref[...] = zbm
        zl_ref[...] = zl
    else:
        first_m = lax.broadcasted_iota(jnp.int32, zbm.shape, 0) == 0
        first_l = lax.broadcasted_iota(jnp.int32, zl.shape, 0) == 0
        zpm = jnp.where(first_m, cm_ref[0:1, :], pltpu.roll(zbm, 1, axis=0))
        zpl = jnp.where(first_l, cl_ref[0:1, :], pltpu.roll(zl, 1, axis=0))
        cm_ref[0:1, :] = zbm[tm - 1:tm, :]
        cl_ref[0:1, :] = zl[tm - 1:tm, :]
        zlast_ref[:, :RKV_W] = jnp.broadcast_to(zbm[tm - 1:tm, :], (8, RKV_W))
        zlast_ref[:, RKV_W:] = jnp.broadcast_to(zl[tm - 1:tm, :], (8, LORA_W))
    zsm = zbm + (zpm - zbm) * mum_ref[...]
    zsl = zl + (zpl - zl) * mul_ref[...]
    r = zsm[:, :B_WIDTH]
    k = zsm[:, B_WIDTH:2 * B_WIDTH]
    v = zsm[:, 2 * B_WIDTH:]
    wd = zsl[:, LORA_WD:LORA_AD]
    ad = zsl[:, LORA_AD:LORA_GD]
    gd = zsl[:, LORA_GD:]
    y = w0_ref[...] + _dot(jnp.tanh(wd), w2_ref[...])
    w_log = jnp.minimum(y, 0.0) - jnp.log1p(jnp.exp(-jnp.abs(y))) - 0.5
    log_decay = -jnp.exp(w_log)
    a = jax.nn.sigmoid(a0_ref[...] + _dot(ad, a2_ref[...]))
    gate = _dot(jax.nn.sigmoid(gd), g2_ref[...])
    ones_bd = ones_ref[...]
    kk = k * kk_ref[...]
    kk = kk * lax.rsqrt(jnp.maximum(_seg_sum(kk * kk, ones_bd), 1e-24))
    k2 = k * (1.0 + (a - 1.0) * ka_ref[...])
    r_ref[...] = r
    w_ref[...] = jnp.exp(log_decay) if sample else log_decay
    k_ref[...] = k2
    v_ref[...] = v
    kn_ref[...] = kk
    b_ref[...] = kk * a
    g_ref[...] = gate
    bonus_ref[...] = _seg_sum(r * k2 * rk_ref[...], ones_bd) * v


def _mix_in(x, p, *, tm, sample, shift_main=None, shift_lora=None):
    rows = x.shape[0]
    n_tiles = rows // tm
    args = [x, p["ln_mix"], p["w_main"], p["w_lora"]]
    specs = [_rows_spec(tm, D_MODEL), _const_spec((1, D_MODEL)),
             _const_spec((MAIN_W, D_MODEL)), _const_spec((LORA_W, D_MODEL))]
    if sample:
        args += [p["sgu_w00"], p["sgu_b0"], shift_main, shift_lora]
        specs += [_const_spec((1, A_WIDTH)), _const_spec((1, A_WIDTH)),
                  _rows_spec(tm, RKV_W), _rows_spec(tm, LORA_W)]
    else:
        args += [p["sgu_wcat"], p["sgu_bias"]]
        specs += [_const_spec((CHUNK, A_GROUPS * CHUNK)), _const_spec((CHUNK, A_WIDTH))]
    args += [p["sgu_ln_g"], p["sgu_ln_b"], p["mu_main"], p["mu_lora"], p["w0"], p["w2"],
             p["a0"], p["a2"], p["g2"], p["k_k"], p["k_a"], p["r_k"], p["ones_bd"]]
    specs += [_const_spec((1, A_WIDTH)), _const_spec((1, A_WIDTH)), _const_spec((1, RKV_W)),
              _const_spec((1, LORA_W)), _const_spec((1, B_WIDTH)),
              _const_spec((LORA_AD - LORA_WD, B_WIDTH)), _const_spec((1, B_WIDTH)),
              _const_spec((LORA_GD - LORA_AD, B_WIDTH)), _const_spec((LORA_W - LORA_GD, B_WIDTH)),
              _const_spec((1, B_WIDTH)), _const_spec((1, B_WIDTH)), _const_spec((1, B_WIDTH)),
              _const_spec((PAIR_W, PAIR_W))]
    wide = jax.ShapeDtypeStruct((rows, B_WIDTH), F32)
    out_shape = [jax.ShapeDtypeStruct((rows, A_WIDTH), BF16)] + [wide] * 8
    out_specs = [_rows_spec(tm, B_WIDTH)] * 9
    scratch = []
    if sample:
        out_shape += [wide, jax.ShapeDtypeStruct((rows, RKV_W), F32),
                      jax.ShapeDtypeStruct((rows, LORA_W), F32)]
        out_specs += [_rows_spec(tm, A_WIDTH), _rows_spec(tm, RKV_W), _rows_spec(tm, LORA_W)]
    else:
        out_shape += [jax.ShapeDtypeStruct((n_tiles * 8, RKV_W + LORA_W), F32)]
        out_specs += [pl.BlockSpec((8, RKV_W + LORA_W), lambda i: (i, 0))]
        scratch = [pltpu.VMEM((8, RKV_W), F32), pltpu.VMEM((8, LORA_W), F32)]
    return pl.pallas_call(
        functools.partial(_mix_kernel, sample=sample, tiles_per_seq=max(SEQ // tm, 1)),
        grid=(n_tiles,),
        in_specs=specs,
        out_specs=out_specs,
        out_shape=out_shape,
        scratch_shapes=scratch,
        compiler_params=_params(),
        name="mix_in",
    )(*args)


def _each(f, *lists):
    return [f(*xs) for xs in zip(*lists)]


SCAN_C = 64


def _mm(a, b):
    return jnp.dot(a.astype(BF16), b.astype(BF16), preferred_element_type=F32)


def _mm_nt(a, b):
    return lax.dot_general(a.astype(BF16), b.astype(BF16), (((1,), (1,)), ((), ())),
                           preferred_element_type=F32)


def _mm_tn(a, b):
    return lax.dot_general(a.astype(BF16), b.astype(BF16), (((0,), (0,)), ((), ())),
                           preferred_element_type=F32)


def _cumsum_rows(x):
    n = x.shape[0]
    row = lax.broadcasted_iota(jnp.int32, x.shape, 0)
    s = 1
    while s < n:
        x = x + jnp.where(row >= s, pltpu.roll(x, s, axis=0), 0.0)
        s *= 2
    return x


INV_BASE = 8


def _unit_lower_inverse(ns, row, col):
    f0 = jnp.zeros((), F32)
    same = lambda s: (row // s) == (col // s)
    eye = jnp.where(row == col, 1.0, f0)
    ps = _each(lambda n: jnp.where(same(INV_BASE), n, f0), ns)
    ts = _each(lambda p: eye + p, ps)
    s = 2
    while s < INV_BASE:
        ps = _each(lambda p: _mm(p, p), ps)
        yield
        ts = _each(lambda t, p: t + _mm(t, p), ts, ps)
        yield
        s *= 2
    s = INV_BASE
    while s < SCAN_C:
        level = same(2 * s) & jnp.logical_not(same(s))
        ws = _each(lambda n, t: _mm(jnp.where(level, n, f0), t), ns, ts)
        yield
        ts = _each(lambda t, w: t + _mm(t, w), ts, ws)
        yield
        s *= 2
    return ts


def _chunk_pairs(s0s, rs, lws, ks, vs, kks, bs):
    c = SCAN_C
    f0 = jnp.zeros((), F32)
    row = lax.broadcasted_iota(jnp.int32, (2 * c, PAIR_W), 0)
    col = lax.broadcasted_iota(jnp.int32, (2 * c, PAIR_W), 1)
    top, lft = row < c, col < HEAD
    same_head = top == lft
    strict = (row % c) > (col % HEAD)
    row_c = lax.broadcasted_iota(jnp.int32, (c, PAIR_W), 0)
    col_c = lax.broadcasted_iota(jnp.int32, (c, PAIR_W), 1)
    lft_c = col_c < HEAD
    strict_c = row_c > (col_c % HEAD)
    incl_c = row_c >= (col_c % HEAD)

    def prep(r, lw, k, v, kk, b):
        cum = _cumsum_rows(lw)
        end = cum[c - 1:c, :]
        a_t = -kk * jnp.exp(cum - lw)
        r_t = r * jnp.exp(cum)
        einv = jnp.exp(-cum)
        eend = jnp.exp(end - cum)
        return dict(
            x0=jnp.concatenate([a_t, r_t], axis=0),
            bk=jnp.concatenate([b * einv, k * einv], axis=0),
            bk_e=jnp.concatenate([b * eend, k * eend], axis=0),
            w_end=jnp.exp(end), v=v,
            v_l=jnp.where(lft_c, v, f0), v_r=jnp.where(lft_c, f0, v))

    fs = _each(prep, rs, lws, ks, vs, kks, bs)
    yield
    def grams(f, s0):
        bk = f["bk"]
        g = _mm_nt(f["x0"], jnp.concatenate(
            [jnp.where(lft, bk, f0), jnp.where(lft, f0, bk), s0], axis=0))
        g1 = pltpu.roll(g[:, PAIR_W:2 * PAIR_W], HEAD, axis=1)
        return (g[:, :PAIR_W], jnp.concatenate([g1[c:], g1[:c]], axis=0),
                g[:, 2 * PAIR_W:])

    g0s, g1s, pqs = zip(*_each(grams, fs, s0s))
    yield

    def rhs(f, g0, g1, pq):
        ak = jnp.where(strict_c, jnp.where(lft_c, g1[c:], g0[:c]), f0)
        x = pq[:c] + _mm(ak, jnp.concatenate([f["v_r"], f["v_l"]], axis=0))
        return jnp.concatenate([jnp.where(lft_c, x, f0), jnp.where(lft_c, f0, x)], axis=0)

    ys = _each(rhs, fs, g0s, g1s, pqs)
    yield
    ns = _each(lambda g0, g1: jnp.where(strict & same_head, jnp.where(top, g0, g1), f0),
               g0s, g1s)
    ts = yield from _unit_lower_inverse(ns, row, col)
    ys = _each(_mm, ts, ys)
    yield

    def out(f, g0, g1, pq, y):
        lhs = jnp.concatenate([jnp.where(incl_c, g0[c:], f0), jnp.where(incl_c, g1[:c], f0)],
                              axis=1)
        return pq[c:] + _mm(lhs, jnp.concatenate([y[:c], f["v_l"], f["v_r"], y[c:]], axis=0))

    def state(f, s0, y):
        upd = _mm_tn(jnp.concatenate([y[:c] + y[c:], f["v"]], axis=0), f["bk_e"])
        return s0 * f["w_end"] + jnp.where(same_head, upd, f0)

    outs = _each(out, fs, g0s, g1s, pqs, ys)
    yield
    return outs, _each(state, fs, s0s, ys)


SCAN_BATCHES = 4
XA_EVERY = 3


def _run_with(main, side, *, every):
    n = 0
    while True:
        if n % every == 0:
            next(side, None)
        n += 1
        try:
            next(main)
        except StopIteration as stop:
            for _ in side:
                pass
            return stop.value


def _scan_prompt_kernel(r_ref, w_ref, k_ref, v_ref, kk_ref, b_ref, xq_ref, xk_ref, xv_ref,
                        o_ref, sout_ref, xo_ref, s_ref):
    t_blk = pl.program_id(1)

    @pl.when(t_blk == 0)
    def _():
        s_ref[...] = jnp.zeros_like(s_ref)

    chains = [(j, p) for j in range(SCAN_BATCHES) for p in range(PAIRS)]
    lanes = lambda p: slice(p * PAIR_W, (p + 1) * PAIR_W)
    take = lambda ref: [ref[j, :, lanes(p)] for j, p in chains]
    os_, ss = _run_with(
        _chunk_pairs([s_ref[j, p] for j, p in chains], take(r_ref), take(w_ref), take(k_ref),
                     take(v_ref), take(kk_ref), take(b_ref)),
        _xa_attend(xq_ref, xk_ref, xv_ref, xo_ref), every=XA_EVERY)
    for (j, p), o, s_new in zip(chains, os_, ss):
        o_ref[j, :, lanes(p)] = o
        s_ref[j, p] = s_new

    @pl.when(t_blk == pl.num_programs(1) - 1)
    def _():
        sout_ref[...] = s_ref[...]


def _scan_prompt(r, lw, k, v, kk, b, xq, xk, xv, *, batch, seq):
    n_t = seq // SCAN_C
    nb = SCAN_BATCHES
    steps = (batch // nb) * n_t
    n_s = xq.shape[0]
    assert n_s % steps == 0, (n_s, steps)
    per = n_s // steps
    spec = pl.BlockSpec((nb, SCAN_C, B_WIDTH), lambda bi, ti: (bi, ti, 0))
    sspec = pl.BlockSpec((nb, PAIRS, PAIR_W, PAIR_W), lambda bi, ti: (bi, 0, 0, 0))
    step = lambda bi, ti: (bi * n_t + ti, 0, 0)
    qspec = pl.BlockSpec((per, MEM_ROWS, 128), step)
    mspec = pl.BlockSpec((per, N_MEM * MEM_ROWS, 128), step)
    o, s, xo = pl.pallas_call(
        _scan_prompt_kernel,
        grid=(batch // nb, n_t),
        in_specs=[spec] * 6 + [qspec, mspec, mspec],
        out_specs=[spec, sspec, qspec],
        out_shape=[jax.ShapeDtypeStruct((batch, seq, B_WIDTH), F32),
                   jax.ShapeDtypeStruct((batch, PAIRS, PAIR_W, PAIR_W), F32),
                   jax.ShapeDtypeStruct(xq.shape, F32)],
        scratch_shapes=[pltpu.VMEM((nb, PAIRS, PAIR_W, PAIR_W), F32)],
        compiler_params=_params(2),
        name="scan_prompt",
    )(*[x.reshape(batch, seq, B_WIDTH) for x in (r, lw, k, v, kk, b)], xq, xk, xv)
    return o.reshape(batch * seq, B_WIDTH), s, xo


SAMPLE_HEADS = 2


def _scan_sample_kernel(s_ref, r_ref, w_ref, k_ref, v_ref, kk_ref, b_ref, o_ref, sout_ref,
                        t_ref, ot_ref):
    h = pl.program_id(0)

    @pl.when(h == 0)
    def _():
        for i, ref in enumerate((r_ref, w_ref, k_ref, v_ref, kk_ref, b_ref)):
            t_ref[i] = ref[...].T

    for hh in range(s_ref.shape[0]):
        base = pl.multiple_of((h * s_ref.shape[0] + hh) * HEAD, HEAD)
        keys = pl.ds(base, HEAD)
        r, w, k, kk, b = [t_ref[i, keys, :] for i in (0, 1, 2, 4, 5)]

        def body(v8, carry, hh=hh, base=base, r=r, w=w, k=k, kk=kk, b=b):
            rows = pl.ds(pl.multiple_of(base + v8 * 8, 8), 8)
            v_rows = t_ref[3, rows, :]
            outs = []
            for j in range(8):
                vi = v8 * 8 + j
                s = s_ref[hh, vi]
                sa = jnp.sum(s * kk, axis=0, keepdims=True)
                s = s * w - sa * b + v_rows[j:j + 1, :] * k
                sout_ref[hh, vi] = s
                outs.append(jnp.sum(s * r, axis=0, keepdims=True))
            ot_ref[rows, :] = jnp.concatenate(outs, axis=0)
            return carry

        lax.fori_loop(0, HEAD // 8, body, 0)

    @pl.when(h == pl.num_programs(0) - 1)
    def _():
        o_ref[...] = ot_ref[...].T


def _scan_sample(state_t, r, w, k, v, kk, b):
    rows = r.shape[0]
    sspec = pl.BlockSpec((SAMPLE_HEADS, HEAD, HEAD, rows), lambda h: (h, 0, 0, 0))
    spec = _const_spec((rows, B_WIDTH))
    return pl.pallas_call(
        _scan_sample_kernel,
        grid=(HEADS // SAMPLE_HEADS,),
        in_specs=[sspec] + [spec] * 6,
        out_specs=[pl.BlockSpec((rows, B_WIDTH), lambda h: (0, 0)), sspec],
        out_shape=[jax.ShapeDtypeStruct((rows, B_WIDTH), F32),
                   jax.ShapeDtypeStruct(state_t.shape, F32)],
        scratch_shapes=[pltpu.VMEM((6, B_WIDTH, rows), F32), pltpu.VMEM((B_WIDTH, rows), F32)],
        compiler_params=_params(),
        name="scan_sample",
    )(state_t, r, w, k, v, kk, b)


def _softmax_rows(s):
    e = jnp.exp(s - jnp.max(s, axis=-1, keepdims=True))
    return e * (1.0 / jnp.sum(e, axis=-1, keepdims=True))


POST_ROWS = 1024


def _post_kernel(*refs, attend):
    it = iter(refs)
    (x_ref, ya_ref, o_ref, g_ref, bonus_ref, gng_ref, gnb_ref, ones_ref, wo_ref, lnx_ref,
     wq_ref) = [next(it) for _ in range(11)]
    if attend:
        mk_ref, mv_ref = next(it), next(it)
    x2_ref, out_ref = next(it), next(it)

    ones_bd = ones_ref[...]
    o = o_ref[...]
    mu = _seg_sum(o, ones_bd) * (1.0 / HEAD)
    d = o - mu
    var = _seg_sum(d * d, ones_bd) * (1.0 / HEAD)
    on = d * lax.rsqrt(var + GN_EPS) * gng_ref[...] + gnb_ref[...]
    yb = (on + bonus_ref[...]) * g_ref[...]
    y = jnp.concatenate([ya_ref[...], yb.astype(BF16)], axis=1)
    x2 = x_ref[...] + jnp.dot(y, wo_ref[...], preferred_element_type=F32)
    x2_ref[...] = x2
    q = _dot(_rms(x2, lnx_ref[...]), wq_ref[...])
    if not attend:
        out_ref[...] = q
        return
    qb = q.astype(BF16)
    heads = [slice(h * XA_DIM, (h + 1) * XA_DIM) for h in range(XA_HEADS)]
    ss = [lax.dot_general(qb[:, sl], mk_ref[0, :, sl], (((1,), (1,)), ((), ())),
                          preferred_element_type=F32) * (XA_DIM ** -0.5) for sl in heads]
    ps = [_softmax_rows(s) for s in ss]
    for sl, p in zip(heads, ps):
        out_ref[:, sl] = _dot(p, mv_ref[0, :, sl]).astype(BF16)


def _post_mix(x, ya, o, g, bonus, p, *, tm, mk=None, mv=None):
    rows = x.shape[0]
    attend = mk is not None
    args = [x, ya, o, g, bonus, p["gn_g"], p["gn_b"], p["ones_bd"], p["w_out"], p["ln_xattn"],
            p["xa_q"]]
    specs = [_rows_spec(tm, D_MODEL)] + [_rows_spec(tm, B_WIDTH)] * 4 + [
        _const_spec((1, B_WIDTH)), _const_spec((1, B_WIDTH)), _const_spec((PAIR_W, PAIR_W)),
        _const_spec((A_WIDTH + B_WIDTH, D_MODEL)),
        _const_spec((1, D_MODEL)), _const_spec((D_MODEL, D_MODEL))]
    if attend:
        tiles_per_seq = SEQ // tm
        mspec = pl.BlockSpec((1, N_MEM, D_MODEL), lambda i: (i // tiles_per_seq, 0, 0))
        args += [mk, mv]
        specs += [mspec, mspec]
    return pl.pallas_call(
        functools.partial(_post_kernel, attend=attend),
        grid=(rows // tm,),
        in_specs=specs,
        out_specs=[_rows_spec(tm, D_MODEL)] * 2,
        out_shape=[jax.ShapeDtypeStruct((rows, D_MODEL), F32),
                   jax.ShapeDtypeStruct((rows, D_MODEL), BF16 if attend else F32)],
        compiler_params=_params(),
        name="post_mix",
    )(*args)


MEM_ROWS = XA_HEADS * (XA_DIM // 128)


def _lane_allreduce(x, op):
    shift = MEM_ROWS
    while shift < 128:
        x = op(x, pltpu.roll(x, shift, axis=1))
        shift *= 2
    return x


def _xa_attend(q_ref, k_ref, v_ref, o_ref):
    f0 = jnp.zeros((), F32)
    n_blk = N_MEM * MEM_ROWS // 128
    sub = lax.broadcasted_iota(jnp.int32, (MEM_ROWS, 128), 0)
    lane = lax.broadcasted_iota(jnp.int32, (MEM_ROWS, 128), 1)
    diag = sub == (lane % MEM_ROWS)
    li = lax.broadcasted_iota(jnp.int32, (128, 128), 0)
    lj = lax.broadcasted_iota(jnp.int32, (128, 128), 1)
    comb = jnp.where((li // MEM_ROWS == lj // MEM_ROWS) & (li % XA_HEADS == lj % XA_HEADS),
                     1.0, 0.0).astype(BF16)
    samples = list(range(q_ref.shape[0]))
    scs = [_mm_nt(q_ref[j], k_ref[j]) for j in samples]

    def partial(sc):
        return jnp.concatenate(
            [jnp.sum(jnp.where(diag, sc[:, t * 128:(t + 1) * 128], f0), axis=0, keepdims=True)
             for t in range(n_blk)], axis=0)

    def scores(part):
        hi = part.astype(BF16)
        lo = (part - hi.astype(F32)).astype(BF16)
        return (jnp.dot(hi, comb, preferred_element_type=F32)
                + jnp.dot(lo, comb, preferred_element_type=F32)) * (XA_DIM ** -0.5)

    def softmax(s):
        mx = _lane_allreduce(jnp.broadcast_to(jnp.max(s, axis=0, keepdims=True), (MEM_ROWS, 128)),
                             jnp.maximum)
        e = jnp.exp(s - mx[0:1, :])
        den = _lane_allreduce(jnp.broadcast_to(jnp.sum(e, axis=0, keepdims=True), (MEM_ROWS, 128)),
                              jnp.add)
        p = e / den[0:1, :]
        return jnp.concatenate(
            [jnp.where(diag, jnp.broadcast_to(p[t:t + 1, :], (MEM_ROWS, 128)), f0)
             for t in range(n_blk)], axis=1)

    yield
    parts = _each(partial, scs)
    yield
    ss = _each(scores, parts)
    yield
    p_rows = _each(softmax, ss)
    yield
    for j, p in zip(samples, p_rows):
        o_ref[j] = _mm(p, v_ref[j])


def _pad_lora(x, axis=-1):
    x = jnp.moveaxis(x, axis, -1)
    wd = x[..., :DECAY_LORA]
    ad = x[..., DECAY_LORA:DECAY_LORA + AAA_LORA]
    gd = x[..., DECAY_LORA + AAA_LORA:]
    z = lambda n: jnp.zeros(x.shape[:-1] + (n,), x.dtype)
    out = jnp.concatenate([wd, z(LORA_AD - DECAY_LORA), ad, z(LORA_GD - LORA_AD - AAA_LORA),
                           gd, z(LORA_W - LORA_GD - GATE_LORA)], axis=-1)
    return jnp.moveaxis(out, -1, axis)


def _unpad_shift(zm, zl):
    return jnp.concatenate([zm, zl[..., LORA_WD:LORA_WD + DECAY_LORA],
                            zl[..., LORA_AD:LORA_AD + AAA_LORA],
                            zl[..., LORA_GD:LORA_GD + GATE_LORA]], axis=-1)


def _pad_rows(w, n):
    return jnp.pad(w, ((0, n - w.shape[0]), (0, 0)))


def _mem_rows(x):
    b = x.shape[0]
    return x.reshape(b, N_MEM, XA_HEADS, XA_DIM // 128, 128).transpose(0, 1, 3, 2, 4).reshape(
        b, N_MEM * MEM_ROWS, 128)


def _from_mem_rows(x, b):
    return x.reshape(b, N_MEM, XA_DIM // 128, XA_HEADS, 128).transpose(0, 1, 3, 2, 4).reshape(
        b, N_MEM, XA_HEADS, XA_DIM)


def _head_rows(x):
    b = x.shape[0]
    return x.reshape(b, XA_HEADS, XA_DIM // 128, 128).transpose(0, 2, 1, 3).reshape(b, MEM_ROWS, 128)


def _from_head_rows(x):
    b = x.shape[0]
    return x.reshape(b, XA_DIM // 128, XA_HEADS, 128).transpose(0, 2, 1, 3).reshape(b, D_MODEL)


def kernel(x_prompt, x_sample, state_rwkv, state_shift, cache_mem_k, cache_mem_v, mem_prompt, ln_ffn1, ffn1_gate, ffn1_up, ffn1_down, ln_mix, w_in, w_out, sgu_w, sgu_b, sgu_ln_g, sgu_ln_b, rwkv_mu, rwkv_w0, rwkv_w2, rwkv_a0, rwkv_a2, rwkv_g2, rwkv_k_k, rwkv_k_a, rwkv_r_k, rwkv_gn_g, rwkv_gn_b, ln_xattn, mem_norm, xa_q, xa_k, xa_v, xa_o, ln_ffn2, ffn2_gate, ffn2_up, ffn2_down, final_norm):
    assert ln_ffn1.shape[0] == 1, "single layer"
    bp, seq, _ = x_prompt.shape
    bs = x_sample.shape[0]
    row = lambda a: a.reshape(1, -1).astype(F32)
    bf = lambda a: a.astype(BF16)
    l = 0
    head_id = jnp.arange(PAIR_W) // HEAD
    tril = jnp.tril(jnp.ones((CHUNK, CHUNK), dtype=bool))
    wmask = jnp.where(tril[None], sgu_w[l], 0)
    p = {
        "ln_mix": row(ln_mix[l]),
        "w_lora": bf(_pad_lora(w_in[l].T[MAIN_W:], axis=0)),
        "sgu_wcat": bf(wmask.transpose(1, 0, 2).reshape(CHUNK, A_GROUPS * CHUNK)),
        "sgu_bias": jnp.repeat(sgu_b[l].T, A_GROUP_DIM, axis=1),
        "sgu_w00": row(jnp.repeat(sgu_w[l][:, 0, 0], A_GROUP_DIM)),
        "sgu_b0": row(jnp.repeat(sgu_b[l][:, 0], A_GROUP_DIM)),
        "sgu_ln_g": row(sgu_ln_g[l]), "sgu_ln_b": row(sgu_ln_b[l]),
        "mu_main": row(rwkv_mu[l][:RKV_W]),
        "mu_lora": row(_pad_lora(rwkv_mu[l][RKV_W:])),
        "w0": row(rwkv_w0[l]), "w2": bf(_pad_rows(rwkv_w2[l], LORA_AD - LORA_WD)),
        "a0": row(rwkv_a0[l]), "a2": bf(_pad_rows(rwkv_a2[l], LORA_GD - LORA_AD)),
        "g2": bf(_pad_rows(rwkv_g2[l], LORA_W - LORA_GD)),
        "k_k": row(rwkv_k_k[l]), "k_a": row(rwkv_k_a[l]), "r_k": row(rwkv_r_k[l]),
        "ones_bd": (head_id[:, None] == head_id[None, :]).astype(BF16),
        "gn_g": row(rwkv_gn_g[l]), "gn_b": row(rwkv_gn_b[l]),
        "ln_xattn": row(ln_xattn[l]),
    }
    fnorm = row(final_norm)

    mk, mv, mkb, mvb, (wg1, wu1, wd1) = _memkv(
        mem_prompt.reshape(bp * N_MEM, D_MODEL), row(mem_norm[l]), xa_k[l], xa_v[l], tm=MEMKV_ROWS,
        cast=(ffn1_gate[l], ffn1_up[l], ffn1_down[l]))
    ffn1 = (row(ln_ffn1[l]), wg1, wu1, wd1)

    xp = x_prompt.reshape(bp * seq, D_MODEL)
    xs = x_sample.reshape(bs, D_MODEL)
    x1, x1s, (wg2, wu2, wd2, p["xa_q"], xa_o_b, p["w_out"], p["w_main"]) = _ffn(
        xp, xs, *ffn1, tm=FFN_ROWS,
        cast=(ffn2_gate[l], ffn2_up[l], ffn2_down[l], xa_q[l], xa_o[l], w_out[l],
              (w_in[l].T, MAIN_W)))
    ffn2 = (row(ln_ffn2[l]), wg2, wu2, wd2)

    sh = state_shift[l].reshape(bs, B_PROJ)
    (ya_s, r_s, w_s, k_s, v_s, kk_s, b_s, g_s, bonus_s, va_s, zm_s, zl_s) = _mix_in(
        x1s, p, tm=bs, sample=True, shift_main=sh[:, :RKV_W], shift_lora=_pad_lora(sh[:, RKV_W:]))
    o_s, state_t = _scan_sample(jnp.transpose(state_rwkv[l], (1, 2, 3, 0)),
                                r_s, w_s, k_s, v_s, kk_s, b_s)
    state_s = jnp.transpose(state_t, (3, 0, 1, 2))
    x2s, q_s = _post_mix(x1s, ya_s, o_s, g_s, bonus_s, p, tm=bs)

    ya, r, w, k, v, kk, b, g, bonus, zlast = _mix_in(x1, p, tm=MIX_ROWS, sample=False)
    o, s_bd, attn_rows = _scan_prompt(r, w, k, v, kk, b, _head_rows(q_s),
                                      _mem_rows(cache_mem_k[l]), _mem_rows(cache_mem_v[l]),
                                      batch=bp, seq=seq)
    state_p = jnp.stack([s_bd[:, :, :HEAD, :HEAD], s_bd[:, :, HEAD:, HEAD:]],
                        axis=2).reshape(bp, HEADS, HEAD, HEAD)
    x2, attn = _post_mix(x1, ya, o, g, bonus, p, tm=POST_ROWS,
                         mk=mkb.reshape(bp, N_MEM, D_MODEL), mv=mvb.reshape(bp, N_MEM, D_MODEL))
    y_prompt, y_sample, _ = _ffn(x2, x2s, *ffn2, tm=FFN_ROWS, attn=attn,
                                 attns=_from_head_rows(attn_rows), wo=xa_o_b, final_norm=fnorm)
    tiles_per_seq = seq // MIX_ROWS
    zl_rows = zlast.reshape(bp, tiles_per_seq, 8, RKV_W + LORA_W)[:, -1, 0]
    shift_p = _unpad_shift(zl_rows[:, :RKV_W], zl_rows[:, RKV_W:])

    return (y_prompt.reshape(bp, seq, D_MODEL),
            y_sample.reshape(bs, 1, D_MODEL),
            state_p[None],
            shift_p.reshape(1, bp, 1, B_PROJ),
            _from_mem_rows(mk, bp)[None],
            _from_mem_rows(mv, bp)[None],
            state_s[None],
            _unpad_shift(zm_s, zl_s).reshape(1, bs, 1, B_PROJ),
            va_s.reshape(1, bs, 1, A_WIDTH))
```

```python
import functools

import jax
import jax.numpy as jnp
from jax import lax
from jax.experimental import pallas as pl
from jax.experimental.pallas import tpu as pltpu

F32 = jnp.float32
BF16 = jnp.bfloat16

D_MODEL = 1024
SEQ = 2048
A_WIDTH = 512
A_GROUPS = 8
A_GROUP_DIM = 64
CHUNK = 128
B_WIDTH = 512
HEAD = 64
HEADS = 8
PAIRS = HEADS // 2
PAIR_W = 2 * HEAD
DECAY_LORA = 64
AAA_LORA = 64
GATE_LORA = 160
B_PROJ = 3 * B_WIDTH + DECAY_LORA + AAA_LORA + GATE_LORA
MAIN_W = 2 * A_WIDTH + 3 * B_WIDTH
RKV_W = 3 * B_WIDTH
LORA_W = 512
LORA_WD, LORA_AD, LORA_GD = 0, 128, 256
D_FF = 2816
N_MEM = 256
XA_HEADS = 4
XA_DIM = 256
NORM_EPS = 1e-6
LN_EPS = 1e-5
GN_EPS = 64e-5

V7X_VMEM_BYTES = 64 * 1024 * 1024
VMEM_LIMIT = V7X_VMEM_BYTES * 7 // 8
MIX_ROWS = 1024


def _params(n_axes=1):
    return pltpu.CompilerParams(dimension_semantics=("arbitrary",) * n_axes,
                                vmem_limit_bytes=VMEM_LIMIT)


def _const_spec(shape):
    nd = len(shape)
    return pl.BlockSpec(shape, lambda *_: (0,) * nd, pipeline_mode=pl.Buffered(1))


def _rows_spec(tm, width):
    return pl.BlockSpec((tm, width), lambda i: (i, 0))


def _rms(x, g):
    return x * lax.rsqrt(jnp.mean(x * x, axis=-1, keepdims=True) + NORM_EPS) * g


def _dot(a, b):
    return jnp.dot(a.astype(BF16), b, preferred_element_type=F32)


def _seg_sum(x, ones_bd):
    xb = x.astype(BF16)
    return jnp.concatenate(
        [jnp.dot(xb[:, t:t + PAIR_W], ones_bd, preferred_element_type=F32)
         for t in range(0, x.shape[1], PAIR_W)], axis=1)


FFN_ROWS = 1024
FFN_BLOCK = 768


def _ffn_kernel(*refs, pre, final, n_cast, n_main):
    it = iter(refs)
    x_ref, xs_ref = next(it), next(it)
    if pre:
        attn_ref, attns_ref, wo_ref = next(it), next(it), next(it)
    ln_ref, wg_ref, wu_ref, wd_ref = next(it), next(it), next(it), next(it)
    if final:
        fn_ref = next(it)
    cast_in = [next(it) for _ in range(n_cast)]
    o_ref, os_ref = next(it), next(it)
    cast_out = [next(it) for _ in range(n_cast)]

    def ffn(x, attn):
        if pre:
            x = x + _dot(attn, wo_ref[...])
        xb = _rms(x, ln_ref[...]).astype(BF16)
        y = None
        for c0 in range(0, D_FF, FFN_BLOCK):
            cols = slice(c0, min(c0 + FFN_BLOCK, D_FF))
            g = jnp.dot(xb, wg_ref[:, cols], preferred_element_type=F32)
            u = jnp.dot(xb, wu_ref[:, cols], preferred_element_type=F32)
            h = (g * jax.nn.sigmoid(g) * u).astype(BF16)
            part = jnp.dot(h, wd_ref[cols, :], preferred_element_type=F32)
            y = part if y is None else y + part
        x = x + 0.5 * y
        return _rms(x, fn_ref[...]) if final else x

    step = pl.program_id(0)

    @pl.when(step < n_main)
    def _():
        for src_ref, dst_ref in zip(cast_in, cast_out):
            dst_ref[...] = src_ref[...].astype(BF16)
        o_ref[...] = ffn(x_ref[...], attn_ref[...] if pre else None)

    @pl.when(step == n_main)
    def _():
        os_ref[...] = ffn(xs_ref[...], attns_ref[...] if pre else None)


def _ffn(x, xs, ln, wg, wu, wd, *, tm, attn=None, attns=None, wo=None, final_norm=None, cast=()):
    rows, rows_s = x.shape[0], xs.shape[0]
    n_main = rows // tm
    pre = attn is not None
    final = final_norm is not None
    main = lambda i: (jnp.minimum(i, n_main - 1), 0)
    main_spec = pl.BlockSpec((tm, D_MODEL), main)
    small_spec = _const_spec((rows_s, D_MODEL))
    args, specs = [x, xs], [main_spec, small_spec]
    if pre:
        args += [attn, attns, wo]
        specs += [main_spec, small_spec, _const_spec((D_MODEL, D_MODEL))]
    args += [ln, wg, wu, wd]
    specs += [_const_spec((1, D_MODEL)), _const_spec((D_MODEL, D_FF)),
              _const_spec((D_MODEL, D_FF)), _const_spec((D_FF, D_MODEL))]
    if final:
        args.append(final_norm)
        specs.append(_const_spec((1, D_MODEL)))
    slabs, cast_shapes = _cast_slabs(cast, n_main, index_map=main)
    out = pl.pallas_call(
        functools.partial(_ffn_kernel, pre=pre, final=final, n_cast=len(cast), n_main=n_main),
        grid=(n_main + 1,),
        in_specs=specs + slabs,
        out_specs=[main_spec, pl.BlockSpec((rows_s, D_MODEL), lambda i: (0, 0))] + slabs,
        out_shape=[jax.ShapeDtypeStruct((rows, D_MODEL), F32),
                   jax.ShapeDtypeStruct((rows_s, D_MODEL), F32)] + cast_shapes,
        compiler_params=_params(),
        name="ffn",
    )(*args, *_cast_arrays(cast))
    return out[0], out[1], list(out[2:])


MEMKV_ROWS = 512


def _memkv_kernel(m_ref, g_ref, wk_ref, wv_ref, *rest):
    n_cast = (len(rest) - 4) // 2
    k_ref, v_ref, kb_ref, vb_ref = rest[n_cast:n_cast + 4]
    for src_ref, dst_ref in zip(rest[:n_cast], rest[n_cast + 4:]):
        dst_ref[...] = src_ref[...].astype(BF16)
    mb = _rms(m_ref[...], g_ref[...]).astype(BF16)
    tm = m_ref.shape[0]
    k = _dot(mb, wk_ref[...].astype(BF16))
    v = _dot(mb, wv_ref[...].astype(BF16))
    kb_ref[...] = k.astype(BF16)
    vb_ref[...] = v.astype(BF16)
    for c in range(MEM_ROWS):
        src = (c % XA_HEADS) * (XA_DIM // 128) + c // XA_HEADS
        k_ref[pl.ds(c, tm, stride=MEM_ROWS), :] = k[:, src * 128:(src + 1) * 128]
        v_ref[pl.ds(c, tm, stride=MEM_ROWS), :] = v[:, src * 128:(src + 1) * 128]


def _cast_slabs(cast, steps, index_map=lambda i: (i, 0)):
    specs, shapes = [], []
    for a in cast:
        a, n = a if isinstance(a, tuple) else (a, a.shape[0])
        assert n % (16 * steps) == 0, (n, steps)
        specs.append(pl.BlockSpec((n // steps, a.shape[1]), index_map))
        shapes.append(jax.ShapeDtypeStruct((n, a.shape[1]), BF16))
    return specs, shapes


def _cast_arrays(cast):
    return [a[0] if isinstance(a, tuple) else a for a in cast]


def _memkv(mem, g, wk, wv, *, tm, cast=()):
    rows = mem.shape[0]
    steps = rows // tm
    out = jax.ShapeDtypeStruct((rows * MEM_ROWS, 128), F32)
    outb = jax.ShapeDtypeStruct((rows, D_MODEL), BF16)
    slabs, cast_shapes = _cast_slabs(cast, steps)
    res = pl.pallas_call(
        _memkv_kernel,
        grid=(steps,),
        in_specs=[_rows_spec(tm, D_MODEL), _const_spec((1, D_MODEL)),
                  _const_spec((D_MODEL, D_MODEL)), _const_spec((D_MODEL, D_MODEL))] + slabs,
        out_specs=[_rows_spec(tm * MEM_ROWS, 128)] * 2 + [_rows_spec(tm, D_MODEL)] * 2 + slabs,
        out_shape=[out, out, outb, outb] + cast_shapes,
        compiler_params=_params(),
        name="memkv",
    )(mem, g, wk, wv, *_cast_arrays(cast))
    return res[0], res[1], res[2], res[3], list(res[4:])


def _mix_kernel(*refs, sample, tiles_per_seq):
    it = iter(refs)
    x_ref, ln_ref, wmain_ref, wlora_ref = next(it), next(it), next(it), next(it)
    if sample:
        w00_ref, b0_ref, spm_ref, spl_ref = next(it), next(it), next(it), next(it)
    else:
        wcat_ref, bias_ref = next(it), next(it)
    (lng_ref, lnb_ref, mum_ref, mul_ref, w0_ref, w2_ref, a0_ref, a2_ref, g2_ref,
     kk_ref, ka_ref, rk_ref, ones_ref) = [next(it) for _ in range(13)]
    (ya_ref, r_ref, w_ref, k_ref, v_ref, kn_ref, b_ref, g_ref, bonus_ref) = [
        next(it) for _ in range(9)]
    if sample:
        va_ref, zm_ref, zl_ref = next(it), next(it), next(it)
    else:
        zlast_ref, cm_ref, cl_ref = next(it), next(it), next(it)

    tm = x_ref.shape[0]
    if not sample:
        @pl.when(pl.program_id(0) % tiles_per_seq == 0)
        def _():
            cm_ref[...] = jnp.zeros_like(cm_ref)
            cl_ref[...] = jnp.zeros_like(cl_ref)

    xb = _rms(x_ref[...], ln_ref[...]).astype(BF16)
    zl = _mm_nt(xb, wlora_ref[...])
    zmain = _mm_nt(xb, wmain_ref[...])

    u = jax.nn.gelu(zmain[:, :A_WIDTH])
    vx = jax.nn.gelu(zmain[:, A_WIDTH:2 * A_WIDTH])
    mu = jnp.mean(vx, axis=-1, keepdims=True)
    var = jnp.mean(jnp.square(vx - mu), axis=-1, keepdims=True)
    va = (vx - mu) * lax.rsqrt(var + LN_EPS) * lng_ref[...] + lnb_ref[...]
    if sample:
        mixed = va * w00_ref[...] + b0_ref[...]
        ya_ref[...] = (u * mixed).astype(BF16)
        va_ref[...] = va
    else:
        vab = va.astype(BF16)
        first = lax.broadcasted_iota(jnp.int32, (CHUNK, 2 * A_GROUP_DIM), 1) < A_GROUP_DIM
        for c in range(tm // CHUNK):
            rows = slice(c * CHUNK, (c + 1) * CHUNK)
            for gp in range(A_GROUPS // 2):
                lanes = slice(gp * 2 * A_GROUP_DIM, (gp + 1) * 2 * A_GROUP_DIM)
                vc = vab[rows, lanes]
                zero = jnp.zeros_like(vc)
                rhs = jnp.concatenate([jnp.where(first, vc, zero), jnp.where(first, zero, vc)],
                                      axis=0)
                mixed = jnp.dot(wcat_ref[:, gp * 2 * CHUNK:(gp + 1) * 2 * CHUNK], rhs,
                                preferred_element_type=F32) + bias_ref[:, lanes]
                ya_ref[rows, lanes] = (u[rows, lanes] * mixed).astype(BF16)

    zbm = zmain[:, 2 * A_WIDTH:]
    if sample:
        zpm, zpl = spm_ref[...], spl_ref[...]
        zm_ref[...] = zbm
        zl_ref[...] = zl
    else:
        first_m = lax.broadcasted_iota(jnp.int32, zbm.shape, 0) == 0
        first_l = lax.broadcasted_iota(jnp.int32, zl.shape, 0) == 0
        zpm = jnp.where(first_m, cm_ref[0:1, :], pltpu.roll(zbm, 1, axis=0))
        zpl = jnp.where(first_l, cl_ref[0:1, :], pltpu.roll(zl, 1, axis=0))
        cm_ref[0:1, :] = zbm[tm - 1:tm, :]
        cl_ref[0:1, :] = zl[tm - 1:tm, :]
        zlast_ref[:, :RKV_W] = jnp.broadcast_to(zbm[tm - 1:tm, :], (8, RKV_W))
        zlast_ref[:, RKV_W:] = jnp.broadcast_to(zl[tm - 1:tm, :], (8, LORA_W))
    zsm = zbm + (zpm - zbm) * mum_ref[...]
    zsl = zl + (zpl - zl) * mul_ref[...]
    r = zsm[:, :B_WIDTH]
    k = zsm[:, B_WIDTH:2 * B_WIDTH]
    v = zsm[:, 2 * B_WIDTH:]
    wd = zsl[:, LORA_WD:LORA_AD]
    ad = zsl[:, LORA_AD:LORA_GD]
    gd = zsl[:, LORA_GD:]
    y = w0_ref[...] + _dot(jnp.tanh(wd), w2_ref[...])
    w_log = jnp.minimum(y, 0.0) - jnp.log1p(jnp.exp(-jnp.abs(y))) - 0.5
    log_decay = -jnp.exp(w_log)
    a = jax.nn.sigmoid(a0_ref[...] + _dot(ad, a2_ref[...]))
    gate = _dot(jax.nn.sigmoid(gd), g2_ref[...])
    ones_bd = ones_ref[...]
    kk = k * kk_ref[...]
    kk = kk * lax.rsqrt(jnp.maximum(_seg_sum(kk * kk, ones_bd), 1e-24))
    k2 = k * (1.0 + (a - 1.0) * ka_ref[...])
    r_ref[...] = r.astype(r_ref.dtype)
    w_ref[...] = jnp.exp(log_decay) if sample else log_decay
    k_ref[...] = k2.astype(k_ref.dtype)
    v_ref[...] = v.astype(v_ref.dtype)
    kn_ref[...] = kk.astype(kn_ref.dtype)
    b_ref[...] = (kk * a).astype(b_ref.dtype)
    g_ref[...] = gate.astype(g_ref.dtype)
    bonus_ref[...] = (_seg_sum(r * k2 * rk_ref[...], ones_bd) * v).astype(bonus_ref.dtype)


def _mix_in(x, p, *, tm, sample, shift_main=None, shift_lora=None):
    rows = x.shape[0]
    n_tiles = rows // tm
    args = [x, p["ln_mix"], p["w_main"], p["w_lora"]]
    specs = [_rows_spec(tm, D_MODEL), _const_spec((1, D_MODEL)),
             _const_spec((MAIN_W, D_MODEL)), _const_spec((LORA_W, D_MODEL))]
    if sample:
        args += [p["sgu_w00"], p["sgu_b0"], shift_main, shift_lora]
        specs += [_const_spec((1, A_WIDTH)), _const_spec((1, A_WIDTH)),
                  _rows_spec(tm, RKV_W), _rows_spec(tm, LORA_W)]
    else:
        args += [p["sgu_wcat"], p["sgu_bias"]]
        specs += [_const_spec((CHUNK, A_GROUPS * CHUNK)), _const_spec((CHUNK, A_WIDTH))]
    args += [p["sgu_ln_g"], p["sgu_ln_b"], p["mu_main"], p["mu_lora"], p["w0"], p["w2"],
             p["a0"], p["a2"], p["g2"], p["k_k"], p["k_a"], p["r_k"], p["ones_bd"]]
    specs += [_const_spec((1, A_WIDTH)), _const_spec((1, A_WIDTH)), _const_spec((1, RKV_W)),
              _const_spec((1, LORA_W)), _const_spec((1, B_WIDTH)),
              _const_spec((LORA_AD - LORA_WD, B_WIDTH)), _const_spec((1, B_WIDTH)),
              _const_spec((LORA_GD - LORA_AD, B_WIDTH)), _const_spec((LORA_W - LORA_GD, B_WIDTH)),
              _const_spec((1, B_WIDTH)), _const_spec((1, B_WIDTH)), _const_spec((1, B_WIDTH)),
              _const_spec((PAIR_W, PAIR_W))]
    wide = jax.ShapeDtypeStruct((rows, B_WIDTH), F32)
    act = wide if sample else jax.ShapeDtypeStruct((rows, B_WIDTH), BF16)
    out_shape = [jax.ShapeDtypeStruct((rows, A_WIDTH), BF16), act, wide] + [act] * 6
    out_specs = [_rows_spec(tm, B_WIDTH)] * 9
    scratch = []
    if sample:
        out_shape += [wide, jax.ShapeDtypeStruct((rows, RKV_W), F32),
                      jax.ShapeDtypeStruct((rows, LORA_W), F32)]
        out_specs += [_rows_spec(tm, A_WIDTH), _rows_spec(tm, RKV_W), _rows_spec(tm, LORA_W)]
    else:
        out_shape += [jax.ShapeDtypeStruct((n_tiles * 8, RKV_W + LORA_W), F32)]
        out_specs += [pl.BlockSpec((8, RKV_W + LORA_W), lambda i: (i, 0))]
        scratch = [pltpu.VMEM((8, RKV_W), F32), pltpu.VMEM((8, LORA_W), F32)]
    return pl.pallas_call(
        functools.partial(_mix_kernel, sample=sample, tiles_per_seq=max(SEQ // tm, 1)),
        grid=(n_tiles,),
        in_specs=specs,
        out_specs=out_specs,
        out_shape=out_shape,
        scratch_shapes=scratch,
        compiler_params=_params(),
        name="mix_in",
    )(*args)


def _each(f, *lists):
    return [f(*xs) for xs in zip(*lists)]


SCAN_C = 64


def _mm(a, b):
    return jnp.dot(a.astype(BF16), b.astype(BF16), preferred_element_type=F32)


def _mm_nt(a, b):
    return lax.dot_general(a.astype(BF16), b.astype(BF16), (((1,), (1,)), ((), ())),
                           preferred_element_type=F32)


def _mm_tn(a, b):
    return lax.dot_general(a.astype(BF16), b.astype(BF16), (((0,), (0,)), ((), ())),
                           preferred_element_type=F32)


def _cumsum_rows(x):
    n = x.shape[0]
    row = lax.broadcasted_iota(jnp.int32, x.shape, 0)
    s = 1
    while s < n:
        x = x + jnp.where(row >= s, pltpu.roll(x, s, axis=0), 0.0)
        s *= 2
    return x


INV_BASE = 8


def _unit_lower_inverse(ns, row, col):
    f0 = jnp.zeros((), F32)
    same = lambda s: (row // s) == (col // s)
    eye = jnp.where(row == col, 1.0, f0)
    ps = _each(lambda n: jnp.where(same(INV_BASE), n, f0), ns)
    ts = _each(lambda p: eye + p, ps)
    s = 2
    while s < INV_BASE:
        ps = _each(lambda p: _mm(p, p), ps)
        yield
        ts = _each(lambda t, p: t + _mm(t, p), ts, ps)
        yield
        s *= 2
    s = INV_BASE
    while s < SCAN_C:
        level = same(2 * s) & jnp.logical_not(same(s))
        ws = _each(lambda n, t: _mm(jnp.where(level, n, f0), t), ns, ts)
        yield
        ts = _each(lambda t, w: t + _mm(t, w), ts, ws)
        yield
        s *= 2
    return ts


def _chunk_pairs(s0s, rs, lws, ks, vs, kks, bs):
    c = SCAN_C
    f0 = jnp.zeros((), F32)
    row = lax.broadcasted_iota(jnp.int32, (2 * c, PAIR_W), 0)
    col = lax.broadcasted_iota(jnp.int32, (2 * c, PAIR_W), 1)
    top, lft = row < c, col < HEAD
    same_head = top == lft
    strict = (row % c) > (col % HEAD)
    row_c = lax.broadcasted_iota(jnp.int32, (c, PAIR_W), 0)
    col_c = lax.broadcasted_iota(jnp.int32, (c, PAIR_W), 1)
    lft_c = col_c < HEAD
    strict_c = row_c > (col_c % HEAD)
    incl_c = row_c >= (col_c % HEAD)

    def prep(r, lw, k, v, kk, b):
        cum = _cumsum_rows(lw)
        end = cum[c - 1:c, :]
        a_t = -kk * jnp.exp(cum - lw)
        r_t = r * jnp.exp(cum)
        einv = jnp.exp(-cum)
        eend = jnp.exp(end - cum)
        return dict(
            x0=jnp.concatenate([a_t, r_t], axis=0),
            bk=jnp.concatenate([b * einv, k * einv], axis=0),
            bk_e=jnp.concatenate([b * eend, k * eend], axis=0),
            w_end=jnp.exp(end), v=v,
            v_l=jnp.where(lft_c, v, f0), v_r=jnp.where(lft_c, f0, v))

    fs = _each(prep, rs, lws, ks, vs, kks, bs)
    yield
    def grams(f, s0):
        bk = f["bk"]
        g = _mm_nt(f["x0"], jnp.concatenate(
            [jnp.where(lft, bk, f0), jnp.where(lft, f0, bk), s0], axis=0))
        g1 = pltpu.roll(g[:, PAIR_W:2 * PAIR_W], HEAD, axis=1)
        return (g[:, :PAIR_W], jnp.concatenate([g1[c:], g1[:c]], axis=0),
                g[:, 2 * PAIR_W:])

    g0s, g1s, pqs = zip(*_each(grams, fs, s0s))
    yield

    def rhs(f, g0, g1, pq):
        ak = jnp.where(strict_c, jnp.where(lft_c, g1[c:], g0[:c]), f0)
        x = pq[:c] + _mm(ak, jnp.concatenate([f["v_r"], f["v_l"]], axis=0))
        return jnp.concatenate([jnp.where(lft_c, x, f0), jnp.where(lft_c, f0, x)], axis=0)

    ys = _each(rhs, fs, g0s, g1s, pqs)
    yield
    ns = _each(lambda g0, g1: jnp.where(strict & same_head, jnp.where(top, g0, g1), f0),
               g0s, g1s)
    ts = yield from _unit_lower_inverse(ns, row, col)
    ys = _each(_mm, ts, ys)
    yield

    def out(f, g0, g1, pq, y):
        lhs = jnp.concatenate([jnp.where(incl_c, g0[c:], f0), jnp.where(incl_c, g1[:c], f0)],
                              axis=1)
        return pq[c:] + _mm(lhs, jnp.concatenate([y[:c], f["v_l"], f["v_r"], y[c:]], axis=0))

    def state(f, s0, y):
        upd = _mm_tn(jnp.concatenate([y[:c] + y[c:], f["v"]], axis=0), f["bk_e"])
        return s0 * f["w_end"] + jnp.where(same_head, upd, f0)

    outs = _each(out, fs, g0s, g1s, pqs, ys)
    yield
    return outs, _each(state, fs, s0s, ys)


SCAN_BATCHES = 4
XA_EVERY = 3


def _run_with(main, side, *, every):
    n = 0
    while True:
        if n % every == 0:
            next(side, None)
        n += 1
        try:
            next(main)
        except StopIteration as stop:
            for _ in side:
                pass
            return stop.value


def _scan_prompt_kernel(r_ref, w_ref, k_ref, v_ref, kk_ref, b_ref, xq_ref, xk_ref, xv_ref,
                        o_ref, sout_ref, xo_ref, s_ref):
    t_blk = pl.program_id(1)

    @pl.when(t_blk == 0)
    def _():
        s_ref[...] = jnp.zeros_like(s_ref)

    chains = [(j, p) for j in range(SCAN_BATCHES) for p in range(PAIRS)]
    lanes = lambda p: slice(p * PAIR_W, (p + 1) * PAIR_W)
    take = lambda ref: [ref[j, :, lanes(p)] for j, p in chains]
    os_, ss = _run_with(
        _chunk_pairs([s_ref[j, p] for j, p in chains], take(r_ref), take(w_ref), take(k_ref),
                     take(v_ref), take(kk_ref), take(b_ref)),
        _xa_attend(xq_ref, xk_ref, xv_ref, xo_ref), every=XA_EVERY)
    for (j, p), o, s_new in zip(chains, os_, ss):
        o_ref[j, :, lanes(p)] = o
        s_ref[j, p] = s_new

    @pl.when(t_blk == pl.num_programs(1) - 1)
    def _():
        sout_ref[...] = s_ref[...]


def _scan_prompt(r, lw, k, v, kk, b, xq, xk, xv, *, batch, seq):
    n_t = seq // SCAN_C
    nb = SCAN_BATCHES
    steps = (batch // nb) * n_t
    n_s = xq.shape[0]
    assert n_s % steps == 0, (n_s, steps)
    per = n_s // steps
    spec = pl.BlockSpec((nb, SCAN_C, B_WIDTH), lambda bi, ti: (bi, ti, 0))
    sspec = pl.BlockSpec((nb, PAIRS, PAIR_W, PAIR_W), lambda bi, ti: (bi, 0, 0, 0))
    step = lambda bi, ti: (bi * n_t + ti, 0, 0)
    qspec = pl.BlockSpec((per, MEM_ROWS, 128), step)
    mspec = pl.BlockSpec((per, N_MEM * MEM_ROWS, 128), step)
    o, s, xo = pl.pallas_call(
        _scan_prompt_kernel,
        grid=(batch // nb, n_t),
        in_specs=[spec] * 6 + [qspec, mspec, mspec],
        out_specs=[spec, sspec, qspec],
        out_shape=[jax.ShapeDtypeStruct((batch, seq, B_WIDTH), F32),
                   jax.ShapeDtypeStruct((batch, PAIRS, PAIR_W, PAIR_W), F32),
                   jax.ShapeDtypeStruct(xq.shape, F32)],
        scratch_shapes=[pltpu.VMEM((nb, PAIRS, PAIR_W, PAIR_W), F32)],
        compiler_params=_params(2),
        name="scan_prompt",
    )(*[x.reshape(batch, seq, B_WIDTH) for x in (r, lw, k, v, kk, b)], xq, xk, xv)
    return o.reshape(batch * seq, B_WIDTH), s, xo


SAMPLE_HEADS = 2


def _scan_sample_kernel(s_ref, r_ref, w_ref, k_ref, v_ref, kk_ref, b_ref, o_ref, sout_ref,
                        t_ref, ot_ref):
    h = pl.program_id(0)

    @pl.when(h == 0)
    def _():
        for i, ref in enumerate((r_ref, w_ref, k_ref, v_ref, kk_ref, b_ref)):
            t_ref[i] = ref[...].T

    for hh in range(s_ref.shape[0]):
        base = pl.multiple_of((h * s_ref.shape[0] + hh) * HEAD, HEAD)
        keys = pl.ds(base, HEAD)
        r, w, k, kk, b = [t_ref[i, keys, :] for i in (0, 1, 2, 4, 5)]

        def body(v8, carry, hh=hh, base=base, r=r, w=w, k=k, kk=kk, b=b):
            rows = pl.ds(pl.multiple_of(base + v8 * 8, 8), 8)
            v_rows = t_ref[3, rows, :]
            outs = []
            for j in range(8):
                vi = v8 * 8 + j
                s = s_ref[hh, vi]
                sa = jnp.sum(s * kk, axis=0, keepdims=True)
                s = s * w - sa * b + v_rows[j:j + 1, :] * k
                sout_ref[hh, vi] = s
                outs.append(jnp.sum(s * r, axis=0, keepdims=True))
            ot_ref[rows, :] = jnp.concatenate(outs, axis=0)
            return carry

        lax.fori_loop(0, HEAD // 8, body, 0)

    @pl.when(h == pl.num_programs(0) - 1)
    def _():
        o_ref[...] = ot_ref[...].T


def _scan_sample(state_t, r, w, k, v, kk, b):
    rows = r.shape[0]
    sspec = pl.BlockSpec((SAMPLE_HEADS, HEAD, HEAD, rows), lambda h: (h, 0, 0, 0))
    spec = _const_spec((rows, B_WIDTH))
    return pl.pallas_call(
        _scan_sample_kernel,
        grid=(HEADS // SAMPLE_HEADS,),
        in_specs=[sspec] + [spec] * 6,
        out_specs=[pl.BlockSpec((rows, B_WIDTH), lambda h: (0, 0)), sspec],
        out_shape=[jax.ShapeDtypeStruct((rows, B_WIDTH), F32),
                   jax.ShapeDtypeStruct(state_t.shape, F32)],
        scratch_shapes=[pltpu.VMEM((6, B_WIDTH, rows), F32), pltpu.VMEM((B_WIDTH, rows), F32)],
        compiler_params=_params(),
        name="scan_sample",
    )(state_t, r, w, k, v, kk, b)


def _softmax_rows(s):
    e = jnp.exp(s - jnp.max(s, axis=-1, keepdims=True))
    return e * (1.0 / jnp.sum(e, axis=-1, keepdims=True))


POST_ROWS = 1024


def _post_kernel(*refs, attend):
    it = iter(refs)
    (x_ref, ya_ref, o_ref, g_ref, bonus_ref, gng_ref, gnb_ref, ones_ref, wo_ref, lnx_ref,
     wq_ref) = [next(it) for _ in range(11)]
    if attend:
        mk_ref, mv_ref = next(it), next(it)
    x2_ref, out_ref = next(it), next(it)

    ones_bd = ones_ref[...]
    o = o_ref[...]
    mu = _seg_sum(o, ones_bd) * (1.0 / HEAD)
    d = o - mu
    var = _seg_sum(d * d, ones_bd) * (1.0 / HEAD)
    on = d * lax.rsqrt(var + GN_EPS) * gng_ref[...] + gnb_ref[...]
    yb = (on + bonus_ref[...]) * g_ref[...]
    y = jnp.concatenate([ya_ref[...], yb.astype(BF16)], axis=1)
    x2 = x_ref[...] + jnp.dot(y, wo_ref[...], preferred_element_type=F32)
    x2_ref[...] = x2
    q = _dot(_rms(x2, lnx_ref[...]), wq_ref[...])
    if not attend:
        out_ref[...] = q
        return
    qb = q.astype(BF16)
    heads = [slice(h * XA_DIM, (h + 1) * XA_DIM) for h in range(XA_HEADS)]
    ss = [lax.dot_general(qb[:, sl], mk_ref[0, :, sl], (((1,), (1,)), ((), ())),
                          preferred_element_type=F32) * (XA_DIM ** -0.5) for sl in heads]
    ps = [_softmax_rows(s) for s in ss]
    for sl, p in zip(heads, ps):
        out_ref[:, sl] = _dot(p, mv_ref[0, :, sl]).astype(BF16)


def _post_mix(x, ya, o, g, bonus, p, *, tm, mk=None, mv=None):
    rows = x.shape[0]
    attend = mk is not None
    args = [x, ya, o, g, bonus, p["gn_g"], p["gn_b"], p["ones_bd"], p["w_out"], p["ln_xattn"],
            p["xa_q"]]
    specs = [_rows_spec(tm, D_MODEL)] + [_rows_spec(tm, B_WIDTH)] * 4 + [
        _const_spec((1, B_WIDTH)), _const_spec((1, B_WIDTH)), _const_spec((PAIR_W, PAIR_W)),
        _const_spec((A_WIDTH + B_WIDTH, D_MODEL)),
        _const_spec((1, D_MODEL)), _const_spec((D_MODEL, D_MODEL))]
    if attend:
        tiles_per_seq = SEQ // tm
        mspec = pl.BlockSpec((1, N_MEM, D_MODEL), lambda i: (i // tiles_per_seq, 0, 0))
        args += [mk, mv]
        specs += [mspec, mspec]
    return pl.pallas_call(
        functools.partial(_post_kernel, attend=attend),
        grid=(rows // tm,),
        in_specs=specs,
        out_specs=[_rows_spec(tm, D_MODEL)] * 2,
        out_shape=[jax.ShapeDtypeStruct((rows, D_MODEL), F32),
                   jax.ShapeDtypeStruct((rows, D_MODEL), BF16 if attend else F32)],
        compiler_params=_params(),
        name="post_mix",
    )(*args)


MEM_ROWS = XA_HEADS * (XA_DIM // 128)


def _lane_allreduce(x, op):
    shift = MEM_ROWS
    while shift < 128:
        x = op(x, pltpu.roll(x, shift, axis=1))
        shift *= 2
    return x


def _xa_attend(q_ref, k_ref, v_ref, o_ref):
    f0 = jnp.zeros((), F32)
    n_blk = N_MEM * MEM_ROWS // 128
    sub = lax.broadcasted_iota(jnp.int32, (MEM_ROWS, 128), 0)
    lane = lax.broadcasted_iota(jnp.int32, (MEM_ROWS, 128), 1)
    diag = sub == (lane % MEM_ROWS)
    li = lax.broadcasted_iota(jnp.int32, (128, 128), 0)
    lj = lax.broadcasted_iota(jnp.int32, (128, 128), 1)
    comb = jnp.where((li // MEM_ROWS == lj // MEM_ROWS) & (li % XA_HEADS == lj % XA_HEADS),
                     1.0, 0.0).astype(BF16)
    samples = list(range(q_ref.shape[0]))
    scs = [_mm_nt(q_ref[j], k_ref[j]) for j in samples]

    def partial(sc):
        return jnp.concatenate(
            [jnp.sum(jnp.where(diag, sc[:, t * 128:(t + 1) * 128], f0), axis=0, keepdims=True)
             for t in range(n_blk)], axis=0)

    def scores(part):
        hi = part.astype(BF16)
        lo = (part - hi.astype(F32)).astype(BF16)
        return (jnp.dot(hi, comb, preferred_element_type=F32)
                + jnp.dot(lo, comb, preferred_element_type=F32)) * (XA_DIM ** -0.5)

    def softmax(s):
        mx = _lane_allreduce(jnp.broadcast_to(jnp.max(s, axis=0, keepdims=True), (MEM_ROWS, 128)),
                             jnp.maximum)
        e = jnp.exp(s - mx[0:1, :])
        den = _lane_allreduce(jnp.broadcast_to(jnp.sum(e, axis=0, keepdims=True), (MEM_ROWS, 128)),
                              jnp.add)
        p = e / den[0:1, :]
        return jnp.concatenate(
            [jnp.where(diag, jnp.broadcast_to(p[t:t + 1, :], (MEM_ROWS, 128)), f0)
             for t in range(n_blk)], axis=1)

    yield
    parts = _each(partial, scs)
    yield
    ss = _each(scores, parts)
    yield
    p_rows = _each(softmax, ss)
    yield
    for j, p in zip(samples, p_rows):
        o_ref[j] = _mm(p, v_ref[j])


def _pad_lora(x, axis=-1):
    x = jnp.moveaxis(x, axis, -1)
    wd = x[..., :DECAY_LORA]
    ad = x[..., DECAY_LORA:DECAY_LORA + AAA_LORA]
    gd = x[..., DECAY_LORA + AAA_LORA:]
    z = lambda n: jnp.zeros(x.shape[:-1] + (n,), x.dtype)
    out = jnp.concatenate([wd, z(LORA_AD - DECAY_LORA), ad, z(LORA_GD - LORA_AD - AAA_LORA),
                           gd, z(LORA_W - LORA_GD - GATE_LORA)], axis=-1)
    return jnp.moveaxis(out, -1, axis)


def _unpad_shift(zm, zl):
    return jnp.concatenate([zm, zl[..., LORA_WD:LORA_WD + DECAY_LORA],
                            zl[..., LORA_AD:LORA_AD + AAA_LORA],
                            zl[..., LORA_GD:LORA_GD + GATE_LORA]], axis=-1)


def _pad_rows(w, n):
    return jnp.pad(w, ((0, n - w.shape[0]), (0, 0)))


def _mem_rows(x):
    b = x.shape[0]
    return x.reshape(b, N_MEM, XA_HEADS, XA_DIM // 128, 128).transpose(0, 1, 3, 2, 4).reshape(
        b, N_MEM * MEM_ROWS, 128)


def _from_mem_rows(x, b):
    return x.reshape(b, N_MEM, XA_DIM // 128, XA_HEADS, 128).transpose(0, 1, 3, 2, 4).reshape(
        b, N_MEM, XA_HEADS, XA_DIM)


def _head_rows(x):
    b = x.shape[0]
    return x.reshape(b, XA_HEADS, XA_DIM // 128, 128).transpose(0, 2, 1, 3).reshape(b, MEM_ROWS, 128)


def _from_head_rows(x):
    b = x.shape[0]
    return x.reshape(b, XA_DIM // 128, XA_HEADS, 128).transpose(0, 2, 1, 3).reshape(b, D_MODEL)


def kernel(x_prompt, x_sample, state_rwkv, state_shift, cache_mem_k, cache_mem_v, mem_prompt, ln_ffn1, ffn1_gate, ffn1_up, ffn1_down, ln_mix, w_in, w_out, sgu_w, sgu_b, sgu_ln_g, sgu_ln_b, rwkv_mu, rwkv_w0, rwkv_w2, rwkv_a0, rwkv_a2, rwkv_g2, rwkv_k_k, rwkv_k_a, rwkv_r_k, rwkv_gn_g, rwkv_gn_b, ln_xattn, mem_norm, xa_q, xa_k, xa_v, xa_o, ln_ffn2, ffn2_gate, ffn2_up, ffn2_down, final_norm):
    assert ln_ffn1.shape[0] == 1, "single layer"
    bp, seq, _ = x_prompt.shape
    bs = x_sample.shape[0]
    row = lambda a: a.reshape(1, -1).astype(F32)
    bf = lambda a: a.astype(BF16)
    l = 0
    head_id = jnp.arange(PAIR_W) // HEAD
    tril = jnp.tril(jnp.ones((CHUNK, CHUNK), dtype=bool))
    wmask = jnp.where(tril[None], sgu_w[l], 0)
    p = {
        "ln_mix": row(ln_mix[l]),
        "w_lora": bf(_pad_lora(w_in[l].T[MAIN_W:], axis=0)),
        "sgu_wcat": bf(wmask.transpose(1, 0, 2).reshape(CHUNK, A_GROUPS * CHUNK)),
        "sgu_bias": jnp.repeat(sgu_b[l].T, A_GROUP_DIM, axis=1),
        "sgu_w00": row(jnp.repeat(sgu_w[l][:, 0, 0], A_GROUP_DIM)),
        "sgu_b0": row(jnp.repeat(sgu_b[l][:, 0], A_GROUP_DIM)),
        "sgu_ln_g": row(sgu_ln_g[l]), "sgu_ln_b": row(sgu_ln_b[l]),
        "mu_main": row(rwkv_mu[l][:RKV_W]),
        "mu_lora": row(_pad_lora(rwkv_mu[l][RKV_W:])),
        "w0": row(rwkv_w0[l]), "w2": bf(_pad_rows(rwkv_w2[l], LORA_AD - LORA_WD)),
        "a0": row(rwkv_a0[l]), "a2": bf(_pad_rows(rwkv_a2[l], LORA_GD - LORA_AD)),
        "g2": bf(_pad_rows(rwkv_g2[l], LORA_W - LORA_GD)),
        "k_k": row(rwkv_k_k[l]), "k_a": row(rwkv_k_a[l]), "r_k": row(rwkv_r_k[l]),
        "ones_bd": (head_id[:, None] == head_id[None, :]).astype(BF16),
        "gn_g": row(rwkv_gn_g[l]), "gn_b": row(rwkv_gn_b[l]),
        "ln_xattn": row(ln_xattn[l]),
    }
    fnorm = row(final_norm)

    mk, mv, mkb, mvb, (wg1, wu1, wd1) = _memkv(
        mem_prompt.reshape(bp * N_MEM, D_MODEL), row(mem_norm[l]), xa_k[l], xa_v[l], tm=MEMKV_ROWS,
        cast=(ffn1_gate[l], ffn1_up[l], ffn1_down[l]))
    ffn1 = (row(ln_ffn1[l]), wg1, wu1, wd1)

    xp = x_prompt.reshape(bp * seq, D_MODEL)
    xs = x_sample.reshape(bs, D_MODEL)
    x1, x1s, (wg2, wu2, wd2, p["xa_q"], xa_o_b, p["w_out"], p["w_main"]) = _ffn(
        xp, xs, *ffn1, tm=FFN_ROWS,
        cast=(ffn2_gate[l], ffn2_up[l], ffn2_down[l], xa_q[l], xa_o[l], w_out[l],
              (w_in[l].T, MAIN_W)))
    ffn2 = (row(ln_ffn2[l]), wg2, wu2, wd2)

    sh = state_shift[l].reshape(bs, B_PROJ)
    (ya_s, r_s, w_s, k_s, v_s, kk_s, b_s, g_s, bonus_s, va_s, zm_s, zl_s) = _mix_in(
        x1s, p, tm=bs, sample=True, shift_main=sh[:, :RKV_W], shift_lora=_pad_lora(sh[:, RKV_W:]))
    o_s, state_t = _scan_sample(jnp.transpose(state_rwkv[l], (1, 2, 3, 0)),
                                r_s, w_s, k_s, v_s, kk_s, b_s)
    state_s = jnp.transpose(state_t, (3, 0, 1, 2))
    x2s, q_s = _post_mix(x1s, ya_s, o_s, g_s, bonus_s, p, tm=bs)

    ya, r, w, k, v, kk, b, g, bonus, zlast = _mix_in(x1, p, tm=MIX_ROWS, sample=False)
    o, s_bd, attn_rows = _scan_prompt(r, w, k, v, kk, b, _head_rows(q_s),
                                      _mem_rows(cache_mem_k[l]), _mem_rows(cache_mem_v[l]),
                                      batch=bp, seq=seq)
    state_p = jnp.stack([s_bd[:, :, :HEAD, :HEAD], s_bd[:, :, HEAD:, HEAD:]],
                        axis=2).reshape(bp, HEADS, HEAD, HEAD)
    x2, attn = _post_mix(x1, ya, o, g, bonus, p, tm=POST_ROWS,
                         mk=mkb.reshape(bp, N_MEM, D_MODEL), mv=mvb.reshape(bp, N_MEM, D_MODEL))
    y_prompt, y_sample, _ = _ffn(x2, x2s, *ffn2, tm=FFN_ROWS, attn=attn,
                                 attns=_from_head_rows(attn_rows), wo=xa_o_b, final_norm=fnorm)
    tiles_per_seq = seq // MIX_ROWS
    zl_rows = zlast.reshape(bp, tiles_per_seq, 8, RKV_W + LORA_W)[:, -1, 0]
    shift_p = _unpad_shift(zl_rows[:, :RKV_W], zl_rows[:, RKV_W:])

    return (y_prompt.reshape(bp, seq, D_MODEL),
            y_sample.reshape(bs, 1, D_MODEL),
            state_p[None],
            shift_p.reshape(1, bp, 1, B_PROJ),
            _from_mem_rows(mk, bp)[None],
            _from_mem_rows(mv, bp)[None],
            state_s[None],
            _unpad_shift(zm_s, zl_s).reshape(1, bs, 1, B_PROJ),
            va_s.reshape(1, bs, 1, A_WIDTH))
```
